```python
import math
import jax
import jax.numpy as jnp
from jax import lax
import numpy as np

D_MODEL = 4096
BATCH = 4
SEQ = 4096
DEPTH = 1

D_MIX = D_MODEL
MLSTM_WIDTH = D_MIX // 2
MLSTM_HEADS = 4
MLSTM_V_DIM = MLSTM_WIDTH // MLSTM_HEADS
MLSTM_QK_DIM = MLSTM_V_DIM // 2
MLSTM_QK_WIDTH = MLSTM_HEADS * MLSTM_QK_DIM
MLSTM_CHUNK = 64
CONV_WIDTH = 4
NSA_WIDTH = D_MIX - MLSTM_WIDTH
NSA_HEAD_DIM = 128
NSA_HEADS = NSA_WIDTH // NSA_HEAD_DIM
NSA_KV_GROUPS = 4
NSA_HPG = NSA_HEADS // NSA_KV_GROUPS
NSA_KV_WIDTH = NSA_KV_GROUPS * NSA_HEAD_DIM
CMP_BLOCK = 32
CMP_STRIDE = 16
SLC_BLOCK = 64
SLC_TOP_N = 16
WINDOW = 512
QUERY_BLOCK = 64
FORCE_BONUS = 1000.0
REL_BUCKETS = 32
REL_MAX_EXACT = REL_BUCKETS // 2
REL_MAX_DISTANCE = 1024
RMS_EPS = 1e-6
NEG_LOGIT = -1e30

SPLIT_SIZES = (
    MLSTM_QK_WIDTH, MLSTM_QK_WIDTH, MLSTM_WIDTH, MLSTM_WIDTH, MLSTM_WIDTH,
    MLSTM_HEADS, MLSTM_HEADS,
    NSA_WIDTH,
    NSA_KV_WIDTH, NSA_KV_WIDTH,
    NSA_KV_WIDTH, NSA_KV_WIDTH,
    NSA_KV_WIDTH, NSA_KV_WIDTH,
    3 * NSA_HEADS,
    NSA_WIDTH,
)
IN_COLS = sum(SPLIT_SIZES)

kernel_name = "hymba_mlstm_nsa_layer"


def _rmsnorm(x, gain):
    xf = x.astype(jnp.float32)
    y = xf * lax.rsqrt(jnp.mean(xf * xf, axis=-1, keepdims=True) + RMS_EPS)
    return (y * gain.astype(jnp.float32)).astype(x.dtype)


def _rel_bucket(dist):
    n = jnp.maximum(dist, 0)
    nf = jnp.maximum(n, REL_MAX_EXACT).astype(jnp.float32)
    large = REL_MAX_EXACT + (jnp.log(nf / REL_MAX_EXACT) / math.log(REL_MAX_DISTANCE / REL_MAX_EXACT)
                             * (REL_BUCKETS - REL_MAX_EXACT)).astype(jnp.int32)
    large = jnp.minimum(large, REL_BUCKETS - 1)
    return jnp.where(n < REL_MAX_EXACT, n, large)


def _causal_conv(u, w):
    c = u.shape[-1]
    return lax.conv_general_dilated(
        u, w[:, None, :].astype(u.dtype), window_strides=(1,), padding=[(CONV_WIDTH - 1, 0)],
        dimension_numbers=("NWC", "WIO", "NWC"), feature_group_count=c)


def _mlstm(q, k, v, i_pre, f_pre):
    b_, t_, nh, dqk = q.shape
    dv = v.shape[-1]
    L = MLSTM_CHUNK
    nc = t_ // L
    k = k * (dqk ** -0.5)
    log_f = jax.nn.log_sigmoid(f_pre)

    def to_chunks(a):
        return a.reshape(b_, nc, L, nh, a.shape[-1]).transpose(1, 0, 3, 2, 4)

    def gate_chunks(a):
        return a.reshape(b_, nc, L, nh).transpose(1, 0, 3, 2)

    causal = jnp.tril(jnp.ones((L, L), dtype=bool))

    def step(carry, inp):
        c_st, n_st, m_st = carry
        qc, kc, vc, ic, lfc = inp
        bcum = jnp.cumsum(lfc, axis=-1)
        g = bcum[..., -1]
        dlog = jnp.where(causal, bcum[..., :, None] - bcum[..., None, :] + ic[..., None, :], -jnp.inf)
        m_inter = bcum + m_st[..., None]
        m_t = jnp.maximum(m_inter, jnp.max(dlog, axis=-1))
        s = jnp.einsum('bhld,bhsd->bhls', qc, kc) * jnp.exp(dlog - m_t[..., None])
        inter = jnp.exp(m_inter - m_t)
        num = jnp.einsum('bhls,bhsv->bhlv', s, vc) + inter[..., None] * jnp.einsum('bhld,bhdv->bhlv', qc, c_st)
        qn = jnp.sum(s, axis=-1) + inter * jnp.einsum('bhld,bhd->bhl', qc, n_st)
        h = num / jnp.maximum(jnp.abs(qn), jnp.exp(-m_t))[..., None]
        w_log = g[..., None] - bcum + ic
        m_next = jnp.maximum(g + m_st, jnp.max(w_log, axis=-1))
        wts = jnp.exp(w_log - m_next[..., None])
        keep = jnp.exp(g + m_st - m_next)
        c_new = keep[..., None, None] * c_st + jnp.einsum('bhl,bhld,bhlv->bhdv', wts, kc, vc)
        n_new = keep[..., None] * n_st + jnp.einsum('bhl,bhld->bhd', wts, kc)
        return (c_new, n_new, m_next), h

    init = (jnp.zeros((b_, nh, dqk, dv), jnp.float32),
            jnp.zeros((b_, nh, dqk), jnp.float32),
            jnp.zeros((b_, nh), jnp.float32))
    xs = (to_chunks(q), to_chunks(k), to_chunks(v), gate_chunks(i_pre), gate_chunks(log_f))
    _, hs = lax.scan(step, init, xs)
    return hs.transpose(1, 0, 3, 2, 4).reshape(b_, t_, nh, dv)


def _compress(blocks, pos, w1, w2):
    hid = jax.nn.silu(jnp.einsum('bnlgd,lde->bnge', blocks + pos[None, None, :, None, :], w1))
    return jnp.einsum('bnge,ef->bngf', hid, w2)


def _nsa(q, ck, cv, sk, sv, wk, wv, gate_pre,
         cmp_k_pos, cmp_k_w1, cmp_k_w2, cmp_v_pos, cmp_v_w1, cmp_v_w2, rel_bias):
    b_, t_, _ = q.shape
    G, HPG, DH = NSA_KV_GROUPS, NSA_HPG, NSA_HEAD_DIM
    scale = DH ** -0.5
    q = q.reshape(b_, t_, G, HPG, DH)
    ck, cv, sk, sv, wk, wv = [a.reshape(b_, t_, G, DH) for a in (ck, cv, sk, sv, wk, wv)]
    tpos = jnp.arange(t_, dtype=jnp.int32)

    n_cmp = (t_ - CMP_BLOCK) // CMP_STRIDE + 1
    blk_idx = np.arange(n_cmp)[:, None] * CMP_STRIDE + np.arange(CMP_BLOCK)[None, :]
    k_cmp = _compress(ck[:, blk_idx], cmp_k_pos, cmp_k_w1, cmp_k_w2)
    v_cmp = _compress(cv[:, blk_idx], cmp_v_pos, cmp_v_w1, cmp_v_w2)
    blk_end = jnp.asarray(np.arange(n_cmp) * CMP_STRIDE + CMP_BLOCK - 1, dtype=jnp.int32)
    dist_c = tpos[:, None] - blk_end[None, :]
    mask_c = dist_c >= 0
    bias_c = rel_bias[_rel_bucket(dist_c)].transpose(2, 0, 1).reshape(G, HPG, t_, n_cmp)
    logit_c = jnp.einsum('btghd,bngd->bghtn', q, k_cmp, preferred_element_type=jnp.float32) * scale + bias_c
    p_c = jax.nn.softmax(jnp.where(mask_c, logit_c, NEG_LOGIT), axis=-1) * mask_c
    o_cmp = jnp.einsum('bghtn,bngd->btghd', p_c.astype(v_cmp.dtype), v_cmp)

    n_slc = t_ // SLC_BLOCK
    cs = np.arange(n_cmp) * CMP_STRIDE
    ss = np.arange(n_slc) * SLC_BLOCK
    cover = np.clip(np.minimum(cs[:, None] + CMP_BLOCK, ss[None, :] + SLC_BLOCK)
                    - np.maximum(cs[:, None], ss[None, :]), 0, None) / CMP_BLOCK
    cover = jnp.asarray(cover, dtype=jnp.float32)
    p_slc = jnp.einsum('bghtn,nj->bgtj', p_c, cover)
    cur = tpos // SLC_BLOCK
    jb = jnp.arange(n_slc, dtype=jnp.int32)
    valid = jb[None, :] <= cur[:, None]
    forced = (jb[None, :] == 0) | (jb[None, :] == cur[:, None]) | (jb[None, :] == cur[:, None] - 1)
    score = jnp.where(valid, p_slc + jnp.where(forced, FORCE_BONUS, 0.0), -1.0)
    n_sel = min(SLC_TOP_N, n_slc)
    _, sel_idx = lax.top_k(score, n_sel)

    ks = sk.reshape(b_, n_slc, SLC_BLOCK, G, DH).transpose(0, 3, 1, 2, 4)
    vs = sv.reshape(b_, n_slc, SLC_BLOCK, G, DH).transpose(0, 3, 1, 2, 4)
    wk_pad = jnp.pad(wk, ((0, 0), (WINDOW, 0), (0, 0), (0, 0)))
    wv_pad = jnp.pad(wv, ((0, 0), (WINDOW, 0), (0, 0), (0, 0)))
    rb = rel_bias.T.reshape(G, HPG, REL_BUCKETS)
    bi = jnp.arange(b_)[:, None, None, None]
    gi = jnp.arange(G)[None, :, None, None]
    gi5 = jnp.arange(G)[None, :, None, None, None]
    hi5 = jnp.arange(HPG)[None, None, :, None, None]
    in_blk = jnp.arange(SLC_BLOCK, dtype=jnp.int32)
    win_off = jnp.arange(QUERY_BLOCK + WINDOW, dtype=jnp.int32)

    def block_fn(qb):
        t0 = qb * QUERY_BLOCK
        q_blk = lax.dynamic_slice_in_dim(q, t0, QUERY_BLOCK, axis=1)
        t_blk = t0 + jnp.arange(QUERY_BLOCK, dtype=jnp.int32)
        sel = lax.dynamic_slice_in_dim(sel_idx, t0, QUERY_BLOCK, axis=2)
        k_sel = ks[bi, gi, sel].reshape(b_, G, QUERY_BLOCK, n_sel * SLC_BLOCK, DH)
        v_sel = vs[bi, gi, sel].reshape(b_, G, QUERY_BLOCK, n_sel * SLC_BLOCK, DH)
        s_pos = (sel[..., None] * SLC_BLOCK + in_blk).reshape(b_, G, QUERY_BLOCK, n_sel * SLC_BLOCK)
        dist_s = t_blk[None, None, :, None] - s_pos
        bias_s = rb[gi5, hi5, _rel_bucket(dist_s)[:, :, None]]
        logit_s = jnp.einsum('bqghd,bgqkd->bghqk', q_blk, k_sel, preferred_element_type=jnp.float32) * scale + bias_s
        p_s = jax.nn.softmax(jnp.where((dist_s >= 0)[:, :, None], logit_s, NEG_LOGIT), axis=-1)
        o_s = jnp.einsum('bghqk,bgqkd->bqghd', p_s.astype(v_sel.dtype), v_sel)
        k_win = lax.dynamic_slice_in_dim(wk_pad, t0, QUERY_BLOCK + WINDOW, axis=1)
        v_win = lax.dynamic_slice_in_dim(wv_pad, t0, QUERY_BLOCK + WINDOW, axis=1)
        w_pos = t0 - WINDOW + win_off
        dist_w = t_blk[:, None] - w_pos[None, :]
        mask_w = (dist_w >= 0) & (dist_w < WINDOW) & (w_pos[None, :] >= 0)
        bias_w = rel_bias[_rel_bucket(dist_w)].transpose(2, 0, 1).reshape(G, HPG, QUERY_BLOCK, QUERY_BLOCK + WINDOW)
        logit_w = jnp.einsum('bqghd,bkgd->bghqk', q_blk, k_win, preferred_element_type=jnp.float32) * scale + bias_w
        p_w = jax.nn.softmax(jnp.where(mask_w, logit_w, NEG_LOGIT), axis=-1)
        o_w = jnp.einsum('bghqk,bkgd->bqghd', p_w.astype(v_win.dtype), v_win)
        return o_s, o_w

    o_slc, o_win = lax.map(block_fn, jnp.arange(t_ // QUERY_BLOCK, dtype=jnp.int32))
    o_slc = o_slc.transpose(1, 0, 2, 3, 4, 5).reshape(b_, t_, G, HPG, DH)
    o_win = o_win.transpose(1, 0, 2, 3, 4, 5).reshape(b_, t_, G, HPG, DH)

    gates = jax.nn.sigmoid(gate_pre.astype(jnp.float32)).reshape(b_, t_, G, HPG, 3)
    o = gates[..., 0:1] * o_cmp + gates[..., 1:2] * o_slc + gates[..., 2:3] * o_win
    return o.reshape(b_, t_, NSA_WIDTH)


def setup_inputs(seed: int = 0) -> dict:
    key = jax.random.key(seed)
    ks = jax.random.split(key, 16)
    nrm = jax.random.normal
    f32 = jnp.float32
    return {
        "x": nrm(ks[0], (BATCH, SEQ, D_MODEL), f32),
        "norm_gain": 1.0 + 0.02 * nrm(ks[1], (D_MODEL,), f32),
        "w_in": nrm(ks[2], (D_MODEL, IN_COLS), f32) * D_MODEL ** -0.5,
        "w_conv": nrm(ks[3], (CONV_WIDTH, 2 * MLSTM_QK_WIDTH), f32) * CONV_WIDTH ** -0.5,
        "b_igate": 0.1 * nrm(ks[4], (MLSTM_HEADS,), f32),
        "b_fgate": 3.0 + 0.5 * nrm(ks[5], (MLSTM_HEADS,), f32),
        "mlstm_norm_gain": 1.0 + 0.02 * nrm(ks[6], (MLSTM_WIDTH,), f32),
        "cmp_k_pos": 0.1 * nrm(ks[7], (CMP_BLOCK, NSA_HEAD_DIM), f32),
        "cmp_k_w1": nrm(ks[8], (CMP_BLOCK, NSA_HEAD_DIM, NSA_HEAD_DIM), f32) * (CMP_BLOCK * NSA_HEAD_DIM) ** -0.5,
        "cmp_k_w2": nrm(ks[9], (NSA_HEAD_DIM, NSA_HEAD_DIM), f32) * NSA_HEAD_DIM ** -0.5,
        "cmp_v_pos": 0.1 * nrm(ks[10], (CMP_BLOCK, NSA_HEAD_DIM), f32),
        "cmp_v_w1": nrm(ks[11], (CMP_BLOCK, NSA_HEAD_DIM, NSA_HEAD_DIM), f32) * (CMP_BLOCK * NSA_HEAD_DIM) ** -0.5,
        "cmp_v_w2": nrm(ks[12], (NSA_HEAD_DIM, NSA_HEAD_DIM), f32) * NSA_HEAD_DIM ** -0.5,
        "rel_bias": 0.5 * nrm(ks[13], (REL_BUCKETS, NSA_HEADS), f32),
        "w_out": nrm(ks[14], (D_MIX, D_MODEL), f32) * D_MIX ** -0.5,
        "final_norm_gain": 1.0 + 0.02 * nrm(ks[15], (D_MODEL,), f32),
    }


def reference(x, norm_gain, w_in, w_conv, b_igate, b_fgate, mlstm_norm_gain,
              cmp_k_pos, cmp_k_w1, cmp_k_w2, cmp_v_pos, cmp_v_w1, cmp_v_w2,
              rel_bias, w_out, final_norm_gain):
    b_, t_, _ = x.shape
    f32 = jnp.float32
    split_points = [int(s) for s in np.cumsum(SPLIT_SIZES)[:-1]]
    for _layer in range(DEPTH):
        h = _rmsnorm(x, norm_gain)
        proj = h @ w_in
        (m_q, m_k, m_v, m_o, m_z, m_i, m_f, a_q, ck, cv, sk, sv, wk, wv, a_gate, a_z) = \
            jnp.split(proj, split_points, axis=-1)

        qk = jax.nn.silu(_causal_conv(jnp.concatenate([m_q, m_k], axis=-1), w_conv))
        mq, mk = jnp.split(qk.astype(f32), 2, axis=-1)
        mq = mq.reshape(b_, t_, MLSTM_HEADS, MLSTM_QK_DIM)
        mk = mk.reshape(b_, t_, MLSTM_HEADS, MLSTM_QK_DIM)
        mv = m_v.astype(f32).reshape(b_, t_, MLSTM_HEADS, MLSTM_V_DIM)
        i_pre = m_i.astype(f32) + b_igate.astype(f32)
        f_pre = m_f.astype(f32) + b_fgate.astype(f32)
        h_m = _mlstm(mq, mk, mv, i_pre, f_pre)
        h_m = jax.nn.sigmoid(m_o.astype(f32)).reshape(b_, t_, MLSTM_HEADS, MLSTM_V_DIM) * h_m
        h_m = h_m * lax.rsqrt(jnp.mean(h_m * h_m, axis=-1, keepdims=True) + RMS_EPS)
        h_m = h_m * mlstm_norm_gain.astype(f32).reshape(MLSTM_HEADS, MLSTM_V_DIM)
        y_m = h_m.reshape(b_, t_, MLSTM_WIDTH) * jax.nn.silu(m_z.astype(f32))

        o_a = _nsa(a_q, ck, cv, sk, sv, wk, wv, a_gate,
                   cmp_k_pos, cmp_k_w1, cmp_k_w2, cmp_v_pos, cmp_v_w1, cmp_v_w2, rel_bias)
        y_a = o_a.astype(f32) * jax.nn.silu(a_z.astype(f32))

        mix = jnp.concatenate([y_m, y_a], axis=-1).astype(x.dtype)
        x = x + mix @ w_out
    return _rmsnorm(x, final_norm_gain)
```

```python
import functools
import math

import jax
import jax.numpy as jnp
import numpy as np
from jax import lax
from jax.experimental import pallas as pl
from jax.experimental.pallas import tpu as pltpu

F32 = jnp.float32
BF16 = jnp.bfloat16

D_MODEL = 4096
D_MIX = D_MODEL
MLSTM_WIDTH = D_MIX // 2
MLSTM_HEADS = 4
MLSTM_V_DIM = MLSTM_WIDTH // MLSTM_HEADS
MLSTM_QK_DIM = MLSTM_V_DIM // 2
MLSTM_QK_WIDTH = MLSTM_HEADS * MLSTM_QK_DIM
CONV_WIDTH = 4
NSA_WIDTH = D_MIX - MLSTM_WIDTH
NSA_HEAD_DIM = 128
NSA_HEADS = NSA_WIDTH // NSA_HEAD_DIM
NSA_KV_GROUPS = 4
NSA_HPG = NSA_HEADS // NSA_KV_GROUPS
NSA_KV_WIDTH = NSA_KV_GROUPS * NSA_HEAD_DIM
CMP_BLOCK = 32
CMP_STRIDE = 16
SLC_BLOCK = 64
SLC_TOP_N = 16
WINDOW = 512
FORCE_BONUS = 1000.0
REL_BUCKETS = 32
REL_MAX_EXACT = REL_BUCKETS // 2
REL_MAX_DISTANCE = 1024
RMS_EPS = 1e-6
NEG_LOGIT = -1e30

LANES = 128
VMEM_LIMIT_BYTES = 56 * 1024 * 1024

COL_MQ = 0
COL_MK = COL_MQ + MLSTM_QK_WIDTH
COL_MV = COL_MK + MLSTM_QK_WIDTH
COL_MO = COL_MV + MLSTM_WIDTH
COL_MZ = COL_MO + MLSTM_WIDTH
COL_AQ = COL_MZ + MLSTM_WIDTH
COL_CK = COL_AQ + NSA_WIDTH
COL_CV = COL_CK + NSA_KV_WIDTH
COL_SK = COL_CV + NSA_KV_WIDTH
COL_SV = COL_SK + NSA_KV_WIDTH
COL_WK = COL_SV + NSA_KV_WIDTH
COL_WV = COL_WK + NSA_KV_WIDTH
COL_AZ = COL_WV + NSA_KV_WIDTH
MAIN_COLS = COL_AZ + NSA_WIDTH
GATE_COLS = LANES
GATE_I = 0
GATE_F = MLSTM_HEADS
GATE_NSA = 2 * MLSTM_HEADS

INPROJ_TM = 512
INPROJ_TN = 1024
OUTPROJ_TM = 512
OUTPROJ_TN = 512
MLSTM_L = 256
ATT_TQ = 256
ATT_TK = 256
MASK_BIG = 2.0 ** 100


def _cparams(sem):
    return pltpu.CompilerParams(dimension_semantics=sem, vmem_limit_bytes=VMEM_LIMIT_BYTES)


def _sigmoid(x):
    return 1.0 / (1.0 + jnp.exp(-x))


def _silu(x):
    return x * _sigmoid(x)


def _dot(a, b):
    return jnp.dot(a, b, preferred_element_type=F32)


def _dot_nt(a, b):
    return lax.dot_general(a, b, (((1,), (1,)), ((), ())), preferred_element_type=F32)


def _inproj_kernel(x_ref, gain_ref, w_ref, wg_ref, o_ref, og_ref, h_ref):
    @pl.when(pl.program_id(1) == 0)
    def _():
        x = x_ref[...]
        ms = jnp.mean(x * x, axis=-1, keepdims=True)
        h = (x * lax.rsqrt(ms + RMS_EPS) * gain_ref[...]).astype(BF16)
        h_ref[...] = h
        og_ref[...] = _dot(h, wg_ref[...])

    o_ref[...] = _dot(h_ref[...], w_ref[...]).astype(BF16)


def _in_proj(x2d, gain, w_main, w_gate):
    n, d = x2d.shape
    tm, tn = min(INPROJ_TM, n), INPROJ_TN
    return pl.pallas_call(
        _inproj_kernel,
        grid=(n // tm, MAIN_COLS // tn),
        in_specs=[
            pl.BlockSpec((tm, d), lambda i, j: (i, 0)),
            pl.BlockSpec((1, d), lambda i, j: (0, 0)),
            pl.BlockSpec((d, tn), lambda i, j: (0, j)),
            pl.BlockSpec((d, GATE_COLS), lambda i, j: (0, 0)),
        ],
        out_specs=[
            pl.BlockSpec((tm, tn), lambda i, j: (i, j)),
            pl.BlockSpec((tm, GATE_COLS), lambda i, j: (i, 0)),
        ],
        out_shape=[
            jax.ShapeDtypeStruct((n, MAIN_COLS), BF16),
            jax.ShapeDtypeStruct((n, GATE_COLS), F32),
        ],
        scratch_shapes=[pltpu.VMEM((tm, d), BF16)],
        compiler_params=_cparams(("parallel", "arbitrary")),
        name="in_proj",
    )(x2d, gain, w_main, w_gate)


def _mlstm_kernel(bi_ref, bf_ref, q_ref, k_ref, v_ref, o_ref, z_ref, g_ref, wq_ref, wk_ref, ng_ref,
                  y_ref, qext, kext, c_st, n_st, m_st):
    L = MLSTM_L
    HIST = 8
    hd = pl.program_id(1)

    @pl.when(pl.program_id(2) == 0)
    def _():
        qext[0:HIST, :] = jnp.zeros((HIST, MLSTM_QK_DIM), F32)
        kext[0:HIST, :] = jnp.zeros((HIST, MLSTM_QK_DIM), F32)
        c_st[...] = jnp.zeros_like(c_st)
        n_st[...] = jnp.zeros_like(n_st)
        m_st[...] = jnp.zeros_like(m_st)

    qext[HIST:HIST + L, :] = q_ref[...].astype(F32)
    kext[HIST:HIST + L, :] = k_ref[...].astype(F32)

    def conv_silu(ext, w_ref):
        w = w_ref[...]
        y = ext[pl.ds(HIST, L), :] * w[CONV_WIDTH - 1:CONV_WIDTH, :]
        for s in range(1, CONV_WIDTH):
            y = y + ext[pl.ds(HIST - s, L), :] * w[CONV_WIDTH - 1 - s:CONV_WIDTH - s, :]
        return _silu(y)

    qc = conv_silu(qext, wq_ref)
    kc = conv_silu(kext, wk_ref) * (MLSTM_QK_DIM ** -0.5)
    qext[0:HIST, :] = qext[L:L + HIST, :]
    kext[0:HIST, :] = kext[L:L + HIST, :]

    g = g_ref[0, 0]
    i_row = g[0:1, :] + bi_ref[hd]
    f_row = g[1:2, :] + bf_ref[hd]
    lf_row = jnp.minimum(f_row, 0.0) - jnp.log(1.0 + jnp.exp(-jnp.abs(f_row)))

    rr = lax.broadcasted_iota(jnp.int32, (L, L), 0)
    cc = lax.broadcasted_iota(jnp.int32, (L, L), 1)
    upper = (rr <= cc).astype(F32)
    bcum_row = jnp.dot(jnp.broadcast_to(lf_row, (8, L)), upper, preferred_element_type=F32,
                       precision=lax.Precision.HIGHEST)[0:1, :]
    bcum_col = jnp.sum(jnp.where(rr == cc, bcum_row, 0.0), axis=1, keepdims=True)
    gsum = bcum_row[:, L - 1:L]
    m_prev = m_st[...]

    dlog = jnp.where(rr >= cc, bcum_col - bcum_row + i_row, -jnp.inf)
    m_inter = bcum_col + m_prev
    m_t = jnp.maximum(m_inter, jnp.max(dlog, axis=1, keepdims=True))
    dmat = jnp.exp(dlog - m_t)
    inter = jnp.exp(m_inter - m_t)

    qb = qc.astype(BF16)
    kct = kc.T
    vb = v_ref[...]
    s = _dot(qb, kct.astype(BF16)) * dmat
    c_prev = c_st[...]
    n_prev = n_st[...]
    num = _dot(s.astype(BF16), vb) + inter * _dot(qb, c_prev.astype(BF16))
    qn = jnp.sum(s, axis=1, keepdims=True) + inter * jnp.sum(qc * n_prev, axis=1, keepdims=True)
    hh = num / jnp.maximum(jnp.abs(qn), jnp.exp(-m_t))

    wlog = gsum - bcum_row + i_row
    m_next = jnp.maximum(gsum + m_prev, jnp.max(wlog, axis=1, keepdims=True))
    wts = jnp.exp(wlog - m_next)
    keep = jnp.exp(gsum + m_prev - m_next)
    c_st[...] = keep * c_prev + _dot((kct * wts).astype(BF16), vb)
    n_st[...] = keep * n_prev + _dot(jnp.broadcast_to(wts, (8, L)).astype(BF16), kc.astype(BF16))[0:1, :]
    m_st[...] = m_next

    hm = _sigmoid(o_ref[...].astype(F32)) * hh
    hm = hm * lax.rsqrt(jnp.mean(hm * hm, axis=-1, keepdims=True) + RMS_EPS)
    hm = hm * ng_ref[...]
    y_ref[...] = (hm * _silu(z_ref[...].astype(F32))).astype(BF16)


def _mlstm(proj, g_rows, w_conv, b_igate, b_fgate, norm_gain, bsz, t):
    L = MLSTM_L
    nc = t // L
    dqk, dv = MLSTM_QK_DIM, MLSTM_V_DIM
    row = lambda b, h, c: b * nc + c
    smem = pl.BlockSpec(memory_space=pltpu.SMEM)
    return pl.pallas_call(
        _mlstm_kernel,
        grid=(bsz, MLSTM_HEADS, nc),
        in_specs=[
            smem, smem,
            pl.BlockSpec((L, dqk), lambda b, h, c: (row(b, h, c), COL_MQ // dqk + h)),
            pl.BlockSpec((L, dqk), lambda b, h, c: (row(b, h, c), COL_MK // dqk + h)),
            pl.BlockSpec((L, dv), lambda b, h, c: (row(b, h, c), COL_MV // dv + h)),
            pl.BlockSpec((L, dv), lambda b, h, c: (row(b, h, c), COL_MO // dv + h)),
            pl.BlockSpec((L, dv), lambda b, h, c: (row(b, h, c), COL_MZ // dv + h)),
            pl.BlockSpec((1, 1, 2, L), lambda b, h, c: (b, h, 0, c)),
            pl.BlockSpec((CONV_WIDTH, dqk), lambda b, h, c: (0, h)),
            pl.BlockSpec((CONV_WIDTH, dqk), lambda b, h, c: (0, MLSTM_HEADS + h)),
            pl.BlockSpec((1, dv), lambda b, h, c: (0, h)),
        ],
        out_specs=pl.BlockSpec((L, dv), lambda b, h, c: (row(b, h, c), h)),
        out_shape=jax.ShapeDtypeStruct((bsz * t, MLSTM_WIDTH), BF16),
        scratch_shapes=[
            pltpu.VMEM((L + 8, dqk), F32),
            pltpu.VMEM((L + 8, dqk), F32),
            pltpu.VMEM((dqk, dv), F32),
            pltpu.VMEM((1, dqk), F32),
            pltpu.VMEM((1, 1), F32),
        ],
        compiler_params=_cparams(("parallel", "parallel", "arbitrary")),
        name="mlstm",
    )(b_igate, b_fgate, proj, proj, proj, proj, proj, g_rows, w_conv, w_conv, norm_gain)


def _compress_kernel(ck_ref, cv_ref, w1k_ref, w2k_ref, pk_ref, w1v_ref, w2v_ref, pv_ref,
                     ok_ref, ov_ref, xf, xcat):
    t = ck_ref.shape[0]
    nb = t // CMP_STRIDE
    dh = NSA_HEAD_DIM

    def one(src_ref, w1_ref, w2_ref, pos_ref, out_ref):
        xf[...] = src_ref[...].astype(F32)
        for l in range(CMP_STRIDE):
            xcat[:, l * dh:(l + 1) * dh] = xf[pl.ds(l, nb, stride=CMP_STRIDE), :].astype(BF16)
        w1 = w1_ref[...]
        ab = _dot(xcat[...], w1)
        pp = _dot(pos_ref[...], w1)
        pos_term = pp[0:1, 0:dh] + pp[1:2, dh:2 * dh]
        second = pltpu.roll(ab[:, dh:2 * dh], nb - 1, 0)
        hid = _silu(ab[:, 0:dh] + second + pos_term)
        out_ref[0, 0] = _dot(hid.astype(BF16), w2_ref[...]).astype(BF16)

    one(ck_ref, w1k_ref, w2k_ref, pk_ref, ok_ref)
    one(cv_ref, w1v_ref, w2v_ref, pv_ref, ov_ref)


def _compress(proj, w1k, w2k, pk, w1v, w2v, pv, bsz, t):
    g_, dh = NSA_KV_GROUPS, NSA_HEAD_DIM
    nb = t // CMP_STRIDE
    full = lambda a: pl.BlockSpec(a.shape, lambda b, g: (0,) * a.ndim)
    out_spec = pl.BlockSpec((1, 1, nb, dh), lambda b, g: (b, g, 0, 0))
    out_sd = jax.ShapeDtypeStruct((bsz, g_, nb, dh), BF16)
    return pl.pallas_call(
        _compress_kernel,
        grid=(bsz, g_),
        in_specs=[
            pl.BlockSpec((t, dh), lambda b, g: (b, COL_CK // dh + g)),
            pl.BlockSpec((t, dh), lambda b, g: (b, COL_CV // dh + g)),
            full(w1k), full(w2k), full(pk), full(w1v), full(w2v), full(pv),
        ],
        out_specs=[out_spec, out_spec],
        out_shape=[out_sd, out_sd],
        scratch_shapes=[pltpu.VMEM((t, dh), F32), pltpu.VMEM((nb, CMP_STRIDE * dh), BF16)],
        compiler_params=_cparams(("parallel", "parallel")),
        name="compress",
    )(proj, proj, w1k, w2k, pk, w1v, w2v, pv)


def _cmp_attn_kernel(q_ref, kc_ref, vc_ref, bias_ref, cov_ref, gate_ref, oc_ref, sel_ref, *, n_sel):
    tq = q_ref.shape[0]
    dh = NSA_HEAD_DIM
    n_slc = cov_ref.shape[0]
    t0 = pl.program_id(1) * tq
    g = pl.program_id(0)
    scale = dh ** -0.5
    kc = kc_ref[0, 0]
    vc = vc_ref[0, 0]
    gates = gate_ref[...]

    p_sum = None
    for hh in range(NSA_HPG):
        q = (q_ref[:, hh * dh:(hh + 1) * dh].astype(F32) * scale).astype(BF16)
        bias = bias_ref[0, hh]
        logit = _dot_nt(q, kc) + bias
        m = jnp.max(logit, axis=1, keepdims=True)
        e = jnp.exp(logit - m)
        p = jnp.where(bias > 0.5 * NEG_LOGIT, e, 0.0) / jnp.sum(e, axis=1, keepdims=True)
        p_sum = p if p_sum is None else p_sum + p
        o = _dot(p.astype(BF16), vc)
        oc_ref[:, hh * dh:(hh + 1) * dh] = (_gate_col(gates, g, hh, 0) * o).astype(BF16)

    cov = cov_ref[...]
    p_hi = p_sum.astype(BF16)
    r1 = p_sum - p_hi.astype(F32)
    p_mid = r1.astype(BF16)
    p_lo = (r1 - p_mid.astype(F32)).astype(BF16)
    st = _dot_nt(cov, p_hi) + _dot_nt(cov, p_mid) + _dot_nt(cov, p_lo)

    jb = lax.broadcasted_iota(jnp.int32, (n_slc, tq), 0)
    cur = (t0 + lax.broadcasted_iota(jnp.int32, (n_slc, tq), 1)) // SLC_BLOCK
    valid = jb <= cur
    forced = (jb == 0) | (jb == cur) | (jb == cur - 1)
    score = jnp.where(valid, st + jnp.where(forced, FORCE_BONUS, 0.0), -1.0)
    rank = jnp.zeros((n_slc, tq), F32)
    for j2 in range(n_slc):
        row = score[j2:j2 + 1, :]
        beats = (row > score) | ((row == score) & (j2 < jb))
        rank = rank + jnp.where(beats, 1.0, 0.0)
    sel = valid & (rank < n_sel)
    sel_ref[0, 0] = jnp.where(sel, 0.0, -MASK_BIG)


def _cmp_attn(proj, gates, k_cmp, v_cmp, bias_c, cover_t, bsz, t):
    g_, dh, tq = NSA_KV_GROUPS, NSA_HEAD_DIM, ATT_TQ
    nt = t // tq
    nb = t // CMP_STRIDE
    n_slc = t // SLC_BLOCK
    gw = NSA_HPG * dh
    return pl.pallas_call(
        functools.partial(_cmp_attn_kernel, n_sel=min(SLC_TOP_N, n_slc)),
        grid=(g_, nt, bsz),
        in_specs=[
            pl.BlockSpec((tq, gw), lambda g, i, b: (b * nt + i, COL_AQ // gw + g)),
            pl.BlockSpec((1, 1, nb, dh), lambda g, i, b: (b, g, 0, 0)),
            pl.BlockSpec((1, 1, nb, dh), lambda g, i, b: (b, g, 0, 0)),
            pl.BlockSpec((1, NSA_HPG, tq, nb), lambda g, i, b: (g, 0, i, 0)),
            pl.BlockSpec((n_slc, nb), lambda g, i, b: (0, 0)),
            pl.BlockSpec((tq, GATE_COLS), lambda g, i, b: (b * nt + i, 0)),
        ],
        out_specs=[
            pl.BlockSpec((tq, gw), lambda g, i, b: (b * nt + i, g)),
            pl.BlockSpec((1, 1, n_slc, tq), lambda g, i, b: (b, g, 0, i)),
        ],
        out_shape=[
            jax.ShapeDtypeStruct((bsz * t, NSA_WIDTH), BF16),
            jax.ShapeDtypeStruct((bsz, g_, n_slc, t), F32),
        ],
        compiler_params=_cparams(("parallel", "parallel", "parallel")),
        name="cmp_attn",
    )(proj, k_cmp, v_cmp, bias_c, cover_t, gates)


def _flash_init(m_ref, l_ref, acc_ref):
    m_ref[...] = jnp.full(m_ref.shape, NEG_LOGIT, F32)
    l_ref[...] = jnp.zeros_like(l_ref)
    acc_ref[...] = jnp.zeros_like(acc_ref)


def _flash_step(q, k_tile, v_tile, bias, m_ref, l_ref, acc_ref):
    s = _dot_nt(q, k_tile) + bias
    m_prev = m_ref[...]
    m_new = jnp.maximum(m_prev, jnp.max(s, axis=1, keepdims=True))
    alpha = jnp.exp(m_prev - m_new)
    p = jnp.exp(s - m_new)
    l_ref[...] = alpha * l_ref[...] + jnp.sum(p, axis=1, keepdims=True)
    acc_ref[...] = alpha * acc_ref[...] + _dot(p.astype(BF16), v_tile)
    m_ref[...] = m_new


def _gate_col(gates, g, hh, branch):
    out = None
    for gg in range(NSA_KV_GROUPS):
        col = GATE_NSA + 3 * (gg * NSA_HPG + hh) + branch
        c = gates[:, col:col + 1]
        out = c if out is None else jnp.where(g == gg, c, out)
    return _sigmoid(out)


def _slc_attn_kernel(q_ref, sb_ref, k_ref, v_ref, bias_ref, far_ref, gate_ref, os_ref,
                     kaug, qaug, m_ref, l_ref, acc_ref, *, n_near):
    tq, tk, dh = ATT_TQ, ATT_TK, NSA_HEAD_DIM
    t = k_ref.shape[0]
    g = pl.program_id(1)
    i = pl.program_id(2)
    scale = dh ** -0.5

    @pl.when(i == 0)
    def _():
        kaug[:, 0:dh] = k_ref[...]
        blk = lax.broadcasted_iota(jnp.int32, (t, LANES), 0) // SLC_BLOCK
        lane = lax.broadcasted_iota(jnp.int32, (t, LANES), 1)
        kaug[:, dh:dh + LANES] = jnp.where(blk == lane, 1.0, 0.0).astype(BF16)

    sb = sb_ref[0, 0]
    for hh in range(NSA_HPG):
        qaug[hh * tq:(hh + 1) * tq, 0:dh] = (q_ref[:, hh * dh:(hh + 1) * dh].astype(F32) * scale).astype(BF16)
        qaug[hh * tq:(hh + 1) * tq, dh:dh + LANES] = sb
    _flash_init(m_ref, l_ref, acc_ref)

    n_far = jnp.maximum(i - (n_near - 1), 0)

    def far_body(j, c):
        off = pl.multiple_of(j * tk, tk)
        _flash_step(qaug[...], kaug[pl.ds(off, tk), :], v_ref[pl.ds(off, tk), :], far_ref[0],
                    m_ref, l_ref, acc_ref)
        return c

    lax.fori_loop(0, n_far, far_body, 0)

    def near_body(j, c):
        off = pl.multiple_of(j * tk, tk)
        _flash_step(qaug[...], kaug[pl.ds(off, tk), :], v_ref[pl.ds(off, tk), :], bias_ref[0, i - j],
                    m_ref, l_ref, acc_ref)
        return c

    lax.fori_loop(n_far, i + 1, near_body, 0)

    gates = gate_ref[...]
    inv_l = 1.0 / l_ref[...]
    for hh in range(NSA_HPG):
        rows = slice(hh * tq, (hh + 1) * tq)
        o = acc_ref[rows, :] * inv_l[rows, :] * _gate_col(gates, g, hh, 1)
        os_ref[:, hh * dh:(hh + 1) * dh] = o.astype(BF16)


def _slc_attn(proj, gates, selbias, bias_s, far_s, bsz, t):
    g_, dh, tq, tk = NSA_KV_GROUPS, NSA_HEAD_DIM, ATT_TQ, ATT_TK
    nt = t // tq
    gw = NSA_HPG * dh
    r = NSA_HPG * tq
    n_near = bias_s.shape[1]
    return pl.pallas_call(
        functools.partial(_slc_attn_kernel, n_near=n_near),
        grid=(bsz, g_, nt),
        in_specs=[
            pl.BlockSpec((tq, gw), lambda b, g, i: (b * nt + i, COL_AQ // gw + g)),
            pl.BlockSpec((1, 1, tq, LANES), lambda b, g, i: (b, g, i, 0)),
            pl.BlockSpec((t, dh), lambda b, g, i: (b, COL_SK // dh + g)),
            pl.BlockSpec((t, dh), lambda b, g, i: (b, COL_SV // dh + g)),
            pl.BlockSpec((1, n_near, r, tk), lambda b, g, i: (g, 0, 0, 0)),
            pl.BlockSpec((1, r, 1), lambda b, g, i: (g, 0, 0)),
            pl.BlockSpec((tq, GATE_COLS), lambda b, g, i: (b * nt + i, 0)),
        ],
        out_specs=pl.BlockSpec((tq, gw), lambda b, g, i: (b * nt + i, g)),
        out_shape=jax.ShapeDtypeStruct((bsz * t, NSA_WIDTH), BF16),
        scratch_shapes=[
            pltpu.VMEM((t, dh + LANES), BF16),
            pltpu.VMEM((r, dh + LANES), BF16),
            pltpu.VMEM((r, 1), F32),
            pltpu.VMEM((r, 1), F32),
            pltpu.VMEM((r, dh), F32),
        ],
        compiler_params=_cparams(("parallel", "parallel", "arbitrary")),
        name="slc_attn",
    )(proj, selbias, proj, proj, bias_s, far_s, gates)


def _win_attn_kernel(q_ref, k_ref, v_ref, bias_ref, gate_ref, oc_ref, os_ref, z_ref, ya_ref,
                     qs, m_ref, l_ref, acc_ref, *, n_near):
    tq, tk, dh = ATT_TQ, ATT_TK, NSA_HEAD_DIM
    g = pl.program_id(1)
    i = pl.program_id(2)
    scale = dh ** -0.5
    for hh in range(NSA_HPG):
        qs[hh * tq:(hh + 1) * tq, :] = (q_ref[:, hh * dh:(hh + 1) * dh].astype(F32) * scale).astype(BF16)
    _flash_init(m_ref, l_ref, acc_ref)

    def body(d, c):
        off = pl.multiple_of((i - d) * tk, tk)
        _flash_step(qs[...], k_ref[pl.ds(off, tk), :], v_ref[pl.ds(off, tk), :], bias_ref[0, d],
                    m_ref, l_ref, acc_ref)
        return c

    lax.fori_loop(0, jnp.minimum(i + 1, n_near), body, 0)

    gates = gate_ref[...]
    inv_l = 1.0 / l_ref[...]
    for hh in range(NSA_HPG):
        rows = slice(hh * tq, (hh + 1) * tq)
        cols = slice(hh * dh, (hh + 1) * dh)
        o = acc_ref[rows, :] * inv_l[rows, :] * _gate_col(gates, g, hh, 2)
        o = o + oc_ref[:, cols].astype(F32) + os_ref[:, cols].astype(F32)
        ya_ref[:, cols] = (o * _silu(z_ref[:, cols].astype(F32))).astype(BF16)


def _win_attn(proj, gates, o_cmp, o_slc, bias_w, bsz, t):
    g_, dh, tq, tk = NSA_KV_GROUPS, NSA_HEAD_DIM, ATT_TQ, ATT_TK
    nt = t // tq
    gw = NSA_HPG * dh
    r = NSA_HPG * tq
    n_near = bias_w.shape[1]
    return pl.pallas_call(
        functools.partial(_win_attn_kernel, n_near=n_near),
        grid=(bsz, g_, nt),
        in_specs=[
            pl.BlockSpec((tq, gw), lambda b, g, i: (b * nt + i, COL_AQ // gw + g)),
            pl.BlockSpec((t, dh), lambda b, g, i: (b, COL_WK // dh + g)),
            pl.BlockSpec((t, dh), lambda b, g, i: (b, COL_WV // dh + g)),
            pl.BlockSpec((1, n_near, r, tk), lambda b, g, i: (g, 0, 0, 0)),
            pl.BlockSpec((tq, GATE_COLS), lambda b, g, i: (b * nt + i, 0)),
            pl.BlockSpec((tq, gw), lambda b, g, i: (b * nt + i, g)),
            pl.BlockSpec((tq, gw), lambda b, g, i: (b * nt + i, g)),
            pl.BlockSpec((tq, gw), lambda b, g, i: (b * nt + i, COL_AZ // gw + g)),
        ],
        out_specs=pl.BlockSpec((tq, gw), lambda b, g, i: (b * nt + i, g)),
        out_shape=jax.ShapeDtypeStruct((bsz * t, NSA_WIDTH), BF16),
        scratch_shapes=[
            pltpu.VMEM((r, dh), BF16),
            pltpu.VMEM((r, 1), F32),
            pltpu.VMEM((r, 1), F32),
            pltpu.VMEM((r, dh), F32),
        ],
        compiler_params=_cparams(("parallel", "parallel", "parallel")),
        name="win_attn",
    )(proj, proj, proj, bias_w, gates, o_cmp, o_slc, proj)


def _outproj_kernel(ym_ref, ya_ref, x_ref, w1_ref, w2_ref, gain_ref, o_ref, rows):
    j = pl.program_id(1)
    nj = rows.shape[0]
    tn = rows.shape[2]
    rows[j] = x_ref[...] + _dot(ym_ref[...], w1_ref[...]) + _dot(ya_ref[...], w2_ref[...])

    @pl.when(j == nj - 1)
    def _():
        ss = None
        for jj in range(nj):
            y = rows[jj]
            part = jnp.sum(y * y, axis=-1, keepdims=True)
            ss = part if ss is None else ss + part
        inv = lax.rsqrt(ss / (nj * tn) + RMS_EPS)
        for jj in range(nj):
            o_ref[:, jj * tn:(jj + 1) * tn] = rows[jj] * inv * gain_ref[:, jj * tn:(jj + 1) * tn]


def _out_proj(y_m, y_a, x2d, w_out, gain):
    n, d = x2d.shape
    tm, tn = min(OUTPROJ_TM, n), OUTPROJ_TN
    nj = d // tn
    return pl.pallas_call(
        _outproj_kernel,
        grid=(n // tm, nj),
        in_specs=[
            pl.BlockSpec((tm, MLSTM_WIDTH), lambda i, j: (i, 0)),
            pl.BlockSpec((tm, NSA_WIDTH), lambda i, j: (i, 0)),
            pl.BlockSpec((tm, tn), lambda i, j: (i, j)),
            pl.BlockSpec((MLSTM_WIDTH, tn), lambda i, j: (0, j)),
            pl.BlockSpec((NSA_WIDTH, tn), lambda i, j: (MLSTM_WIDTH // NSA_WIDTH, j)),
            pl.BlockSpec((1, d), lambda i, j: (0, 0)),
        ],
        out_specs=pl.BlockSpec((tm, d), lambda i, j: (i, 0)),
        out_shape=jax.ShapeDtypeStruct((n, d), F32),
        scratch_shapes=[pltpu.VMEM((nj, tm, tn), F32)],
        compiler_params=_cparams(("parallel", "arbitrary")),
        name="out_proj",
    )(y_m, y_a, x2d, w_out, w_out, gain)


def _rel_bucket(dist):
    n = jnp.maximum(dist, 0)
    nf = jnp.maximum(n, REL_MAX_EXACT).astype(jnp.float32)
    large = REL_MAX_EXACT + (jnp.log(nf / REL_MAX_EXACT) / math.log(REL_MAX_DISTANCE / REL_MAX_EXACT)
                             * (REL_BUCKETS - REL_MAX_EXACT)).astype(jnp.int32)
    large = jnp.minimum(large, REL_BUCKETS - 1)
    return jnp.where(n < REL_MAX_EXACT, n, large)


def _toeplitz_tiles(by_dist, n_off, tq, tk, lo, hi):
    d = np.arange(n_off)[:, None, None] * tq + np.arange(tq)[None, :, None] - np.arange(tk)[None, None, :]
    ok = (d >= lo) & (d < hi)
    vals = jnp.where(ok[None], by_dist[:, np.clip(d, 0, by_dist.shape[1] - 1)], NEG_LOGIT)
    vals = vals.reshape(NSA_KV_GROUPS, NSA_HPG, n_off, tq, tk).transpose(0, 2, 1, 3, 4)
    return vals.reshape(NSA_KV_GROUPS, n_off, NSA_HPG * tq, tk)


def _bias_tables(rel_bias, t):
    tq, tk = ATT_TQ, ATT_TK
    dmax = REL_MAX_DISTANCE + tq + tk
    by_dist = rel_bias.astype(F32)[_rel_bucket(jnp.arange(dmax, dtype=jnp.int32))].T
    nb = t // CMP_STRIDE
    n_cmp = (t - CMP_BLOCK) // CMP_STRIDE + 1
    dc = np.arange(t)[:, None] - (np.arange(nb)[None, :] * CMP_STRIDE + CMP_BLOCK - 1)
    okc = (dc >= 0) & (np.arange(nb)[None, :] < n_cmp)
    bias_c = jnp.where(okc[None], by_dist[:, np.clip(dc, 0, dmax - 1)], NEG_LOGIT)
    bias_c = bias_c.reshape(NSA_KV_GROUPS, NSA_HPG, t, nb)
    n_near_s = min(-(-(REL_MAX_DISTANCE + tk - 1) // tq), t // tq)
    bias_s = _toeplitz_tiles(by_dist, n_near_s, tq, tk, 0, 1 << 30)
    far = rel_bias.astype(F32)[REL_BUCKETS - 1].reshape(NSA_KV_GROUPS, NSA_HPG, 1)
    far_s = jnp.broadcast_to(far[:, :, None, :], (NSA_KV_GROUPS, NSA_HPG, tq, 1)).reshape(NSA_KV_GROUPS, NSA_HPG * tq, 1)
    n_near_w = min(-(-(WINDOW + tk - 1) // tq), t // tq)
    bias_w = _toeplitz_tiles(by_dist, n_near_w, tq, tk, 0, WINDOW)
    return bias_c, bias_s, far_s, bias_w


def _cover_t(t):
    nb = t // CMP_STRIDE
    n_cmp = (t - CMP_BLOCK) // CMP_STRIDE + 1
    n_slc = t // SLC_BLOCK
    cs = np.arange(nb) * CMP_STRIDE
    ss = np.arange(n_slc) * SLC_BLOCK
    cover = np.clip(np.minimum(cs[:, None] + CMP_BLOCK, ss[None, :] + SLC_BLOCK)
                    - np.maximum(cs[:, None], ss[None, :]), 0, None) / CMP_BLOCK
    cover[n_cmp:] = 0.0
    return jnp.asarray(cover.T, dtype=BF16)


def kernel(x, norm_gain, w_in, w_conv, b_igate, b_fgate, mlstm_norm_gain, cmp_k_pos, cmp_k_w1, cmp_k_w2,
           cmp_v_pos, cmp_v_w1, cmp_v_w2, rel_bias, w_out, final_norm_gain):
    bsz, t, d = x.shape
    assert d == D_MODEL and t % MLSTM_L == 0 and t % ATT_TQ == 0 and (t // CMP_STRIDE) % LANES == 0
    n = bsz * t
    x2d = x.reshape(n, d)

    o_i = COL_MZ + MLSTM_WIDTH
    o_aq = o_i + 2 * MLSTM_HEADS
    o_gate = o_aq + NSA_WIDTH + 6 * NSA_KV_WIDTH
    o_az = o_gate + 3 * NSA_HEADS
    w_main = jnp.concatenate([w_in[:, :o_i], w_in[:, o_aq:o_gate], w_in[:, o_az:]], axis=1).astype(BF16)
    w_gate = jnp.concatenate([w_in[:, o_i:o_aq], w_in[:, o_gate:o_az],
                              jnp.zeros((d, GATE_COLS - 2 * MLSTM_HEADS - 3 * NSA_HEADS), w_in.dtype)], axis=1).astype(BF16)

    proj, gates = _in_proj(x2d, norm_gain.reshape(1, d).astype(F32), w_main, w_gate)

    g_rows = gates[:, :2 * MLSTM_HEADS].reshape(bsz, t, 2, MLSTM_HEADS).transpose(0, 3, 2, 1)
    y_m = _mlstm(proj, g_rows, w_conv.astype(F32), b_igate.astype(F32), b_fgate.astype(F32),
                 mlstm_norm_gain.reshape(1, MLSTM_WIDTH).astype(F32), bsz, t)

    dh = NSA_HEAD_DIM
    half = CMP_BLOCK // 2

    def w1cat(w1):
        return jnp.concatenate([w1[:half].reshape(half * dh, dh), w1[half:].reshape(half * dh, dh)], axis=1).astype(BF16)

    k_cmp, v_cmp = _compress(
        proj, w1cat(cmp_k_w1), cmp_k_w2.astype(BF16), cmp_k_pos.reshape(2, half * dh).astype(BF16),
        w1cat(cmp_v_w1), cmp_v_w2.astype(BF16), cmp_v_pos.reshape(2, half * dh).astype(BF16), bsz, t)

    bias_c, bias_s, far_s, bias_w = _bias_tables(rel_bias, t)
    o_cmp, sel_t = _cmp_attn(proj, gates, k_cmp, v_cmp, bias_c, _cover_t(t), bsz, t)
    n_slc = t // SLC_BLOCK
    selbias = jnp.pad(sel_t.transpose(0, 1, 3, 2), ((0, 0), (0, 0), (0, 0), (0, LANES - n_slc))).astype(BF16)
    o_slc = _slc_attn(proj, gates, selbias, bias_s, far_s, bsz, t)
    y_a = _win_attn(proj, gates, o_cmp, o_slc, bias_w, bsz, t)

    out = _out_proj(y_m, y_a, x2d, w_out.astype(BF16), final_norm_gain.reshape(1, d).astype(F32))
    return out.reshape(bsz, t, d)
```

```python
import functools
import math

import jax
import jax.numpy as jnp
import numpy as np
from jax import lax
from jax.experimental import pallas as pl
from jax.experimental.pallas import tpu as pltpu

F32 = jnp.float32
BF16 = jnp.bfloat16

D_MODEL = 4096
D_MIX = D_MODEL
MLSTM_WIDTH = D_MIX // 2
MLSTM_HEADS = 4
MLSTM_V_DIM = MLSTM_WIDTH // MLSTM_HEADS
MLSTM_QK_DIM = MLSTM_V_DIM // 2
MLSTM_QK_WIDTH = MLSTM_HEADS * MLSTM_QK_DIM
CONV_WIDTH = 4
NSA_WIDTH = D_MIX - MLSTM_WIDTH
NSA_HEAD_DIM = 128
NSA_HEADS = NSA_WIDTH // NSA_HEAD_DIM
NSA_KV_GROUPS = 4
NSA_HPG = NSA_HEADS // NSA_KV_GROUPS
NSA_KV_WIDTH = NSA_KV_GROUPS * NSA_HEAD_DIM
CMP_BLOCK = 32
CMP_STRIDE = 16
SLC_BLOCK = 64
SLC_TOP_N = 16
WINDOW = 512
FORCE_BONUS = 1000.0
REL_BUCKETS = 32
REL_MAX_EXACT = REL_BUCKETS // 2
REL_MAX_DISTANCE = 1024
RMS_EPS = 1e-6
NEG_LOGIT = -1e30

LANES = 128
VMEM_LIMIT_BYTES = 56 * 1024 * 1024

COL_MQ = 0
COL_MK = COL_MQ + MLSTM_QK_WIDTH
COL_MV = COL_MK + MLSTM_QK_WIDTH
COL_MO = COL_MV + MLSTM_WIDTH
COL_MZ = COL_MO + MLSTM_WIDTH
COL_AQ = COL_MZ + MLSTM_WIDTH
COL_CK = COL_AQ + NSA_WIDTH
COL_CV = COL_CK + NSA_KV_WIDTH
COL_SK = COL_CV + NSA_KV_WIDTH
COL_SV = COL_SK + NSA_KV_WIDTH
COL_WK = COL_SV + NSA_KV_WIDTH
COL_WV = COL_WK + NSA_KV_WIDTH
COL_AZ = COL_WV + NSA_KV_WIDTH
MAIN_COLS = COL_AZ + NSA_WIDTH
GATE_COLS = LANES
GATE_I = 0
GATE_F = MLSTM_HEADS
GATE_NSA = 2 * MLSTM_HEADS

INPROJ_TM = 512
INPROJ_TN = 1024
OUTPROJ_TM = 512
OUTPROJ_TN = 512
MLSTM_L = 256
ATT_TQ = 256
ATT_TK = 256
MASK_BIG = 2.0 ** 100


def _cparams(sem):
    return pltpu.CompilerParams(dimension_semantics=sem, vmem_limit_bytes=VMEM_LIMIT_BYTES)


def _sigmoid(x):
    return 1.0 / (1.0 + jnp.exp(-x))


def _silu(x):
    return x * _sigmoid(x)


def _dot(a, b):
    return jnp.dot(a, b, preferred_element_type=F32)


def _dot_nt(a, b):
    return lax.dot_general(a, b, (((1,), (1,)), ((), ())), preferred_element_type=F32)


def _inproj_kernel(x_ref, gain_ref, w_ref, wg_ref, o_ref, og_ref, h_ref):
    @pl.when(pl.program_id(1) == 0)
    def _():
        x = x_ref[...]
        ms = jnp.mean(x * x, axis=-1, keepdims=True)
        h = (x * lax.rsqrt(ms + RMS_EPS) * gain_ref[...]).astype(BF16)
        h_ref[...] = h
        og_ref[...] = _dot(h, wg_ref[...])

    o_ref[...] = _dot(h_ref[...], w_ref[...]).astype(BF16)


def _in_proj(x2d, gain, w_main, w_gate):
    n, d = x2d.shape
    tm, tn = min(INPROJ_TM, n), INPROJ_TN
    return pl.pallas_call(
        _inproj_kernel,
        grid=(n // tm, MAIN_COLS // tn),
        in_specs=[
            pl.BlockSpec((tm, d), lambda i, j: (i, 0)),
            pl.BlockSpec((1, d), lambda i, j: (0, 0)),
            pl.BlockSpec((d, tn), lambda i, j: (0, j)),
            pl.BlockSpec((d, GATE_COLS), lambda i, j: (0, 0)),
        ],
        out_specs=[
            pl.BlockSpec((tm, tn), lambda i, j: (i, j)),
            pl.BlockSpec((tm, GATE_COLS), lambda i, j: (i, 0)),
        ],
        out_shape=[
            jax.ShapeDtypeStruct((n, MAIN_COLS), BF16),
            jax.ShapeDtypeStruct((n, GATE_COLS), F32),
        ],
        scratch_shapes=[pltpu.VMEM((tm, d), BF16)],
        compiler_params=_cparams(("parallel", "arbitrary")),
        name="in_proj",
    )(x2d, gain, w_main, w_gate)


def _mlstm_kernel(bi_ref, bf_ref, q_ref, k_ref, v_ref, o_ref, z_ref, g_ref, wq_ref, wk_ref, ng_ref,
                  y_ref, qext, kext, c_st, n_st, m_st):
    L = MLSTM_L
    HIST = 8
    hd = pl.program_id(1)

    @pl.when(pl.program_id(2) == 0)
    def _():
        qext[0:HIST, :] = jnp.zeros((HIST, MLSTM_QK_DIM), F32)
        kext[0:HIST, :] = jnp.zeros((HIST, MLSTM_QK_DIM), F32)
        c_st[...] = jnp.zeros_like(c_st)
        n_st[...] = jnp.zeros_like(n_st)
        m_st[...] = jnp.zeros_like(m_st)

    qext[HIST:HIST + L, :] = q_ref[...].astype(F32)
    kext[HIST:HIST + L, :] = k_ref[...].astype(F32)

    def conv_silu(ext, w_ref):
        w = w_ref[...]
        y = ext[pl.ds(HIST, L), :] * w[CONV_WIDTH - 1:CONV_WIDTH, :]
        for s in range(1, CONV_WIDTH):
            y = y + ext[pl.ds(HIST - s, L), :] * w[CONV_WIDTH - 1 - s:CONV_WIDTH - s, :]
        return _silu(y)

    qc = conv_silu(qext, wq_ref)
    kc = conv_silu(kext, wk_ref) * (MLSTM_QK_DIM ** -0.5)
    qext[0:HIST, :] = qext[L:L + HIST, :]
    kext[0:HIST, :] = kext[L:L + HIST, :]

    g = g_ref[0, 0]
    i_row = g[0:1, :] + bi_ref[hd]
    f_row = g[1:2, :] + bf_ref[hd]
    lf_row = jnp.minimum(f_row, 0.0) - jnp.log(1.0 + jnp.exp(-jnp.abs(f_row)))

    rr = lax.broadcasted_iota(jnp.int32, (L, L), 0)
    cc = lax.broadcasted_iota(jnp.int32, (L, L), 1)
    upper = (rr <= cc).astype(F32)
    bcum_row = jnp.dot(jnp.broadcast_to(lf_row, (8, L)), upper, preferred_element_type=F32,
                       precision=lax.Precision.HIGHEST)[0:1, :]
    bcum_col = jnp.sum(jnp.where(rr == cc, bcum_row, 0.0), axis=1, keepdims=True)
    gsum = bcum_row[:, L - 1:L]
    m_prev = m_st[...]

    dlog = jnp.where(rr >= cc, bcum_col - bcum_row + i_row, -jnp.inf)
    m_inter = bcum_col + m_prev
    m_t = jnp.maximum(m_inter, jnp.max(dlog, axis=1, keepdims=True))
    dmat = jnp.exp(dlog - m_t)
    inter = jnp.exp(m_inter - m_t)

    qb = qc.astype(BF16)
    kct = kc.T
    vb = v_ref[...]
    s = _dot(qb, kct.astype(BF16)) * dmat
    c_prev = c_st[...]
    n_prev = n_st[...]
    num = _dot(s.astype(BF16), vb) + inter * _dot(qb, c_prev.astype(BF16))
    qn = jnp.sum(s, axis=1, keepdims=True) + inter * jnp.sum(qc * n_prev, axis=1, keepdims=True)
    hh = num / jnp.maximum(jnp.abs(qn), jnp.exp(-m_t))

    wlog = gsum - bcum_row + i_row
    m_next = jnp.maximum(gsum + m_prev, jnp.max(wlog, axis=1, keepdims=True))
    wts = jnp.exp(wlog - m_next)
    keep = jnp.exp(gsum + m_prev - m_next)
    c_st[...] = keep * c_prev + _dot((kct * wts).astype(BF16), vb)
    n_st[...] = keep * n_prev + _dot(jnp.broadcast_to(wts, (8, L)).astype(BF16), kc.astype(BF16))[0:1, :]
    m_st[...] = m_next

    hm = _sigmoid(o_ref[...].astype(F32)) * hh
    hm = hm * lax.rsqrt(jnp.mean(hm * hm, axis=-1, keepdims=True) + RMS_EPS)
    hm = hm * ng_ref[...]
    y_ref[...] = (hm * _silu(z_ref[...].astype(F32))).astype(BF16)


def _mlstm(proj, g_rows, w_conv, b_igate, b_fgate, norm_gain, bsz, t):
    L = MLSTM_L
    nc = t // L
    dqk, dv = MLSTM_QK_DIM, MLSTM_V_DIM
    row = lambda b, h, c: b * nc + c
    smem = pl.BlockSpec(memory_space=pltpu.SMEM)
    return pl.pallas_call(
        _mlstm_kernel,
        grid=(bsz, MLSTM_HEADS, nc),
        in_specs=[
            smem, smem,
            pl.BlockSpec((L, dqk), lambda b, h, c: (row(b, h, c), COL_MQ // dqk + h)),
            pl.BlockSpec((L, dqk), lambda b, h, c: (row(b, h, c), COL_MK // dqk + h)),
            pl.BlockSpec((L, dv), lambda b, h, c: (row(b, h, c), COL_MV // dv + h)),
            pl.BlockSpec((L, dv), lambda b, h, c: (row(b, h, c), COL_MO // dv + h)),
            pl.BlockSpec((L, dv), lambda b, h, c: (row(b, h, c), COL_MZ // dv + h)),
            pl.BlockSpec((1, 1, 2, L), lambda b, h, c: (b, h, 0, c)),
            pl.BlockSpec((CONV_WIDTH, dqk), lambda b, h, c: (0, h)),
            pl.BlockSpec((CONV_WIDTH, dqk), lambda b, h, c: (0, MLSTM_HEADS + h)),
            pl.BlockSpec((1, dv), lambda b, h, c: (0, h)),
        ],
        out_specs=pl.BlockSpec((L, dv), lambda b, h, c: (row(b, h, c), h)),
        out_shape=jax.ShapeDtypeStruct((bsz * t, MLSTM_WIDTH), BF16),
        scratch_shapes=[
            pltpu.VMEM((L + 8, dqk), F32),
            pltpu.VMEM((L + 8, dqk), F32),
            pltpu.VMEM((dqk, dv), F32),
            pltpu.VMEM((1, dqk), F32),
            pltpu.VMEM((1, 1), F32),
        ],
        compiler_params=_cparams(("parallel", "parallel", "arbitrary")),
        name="mlstm",
    )(b_igate, b_fgate, proj, proj, proj, proj, proj, g_rows, w_conv, w_conv, norm_gain)


def _compress_kernel(ck_ref, cv_ref, w1k_ref, w2k_ref, pk_ref, w1v_ref, w2v_ref, pv_ref,
                     ok_ref, ov_ref, xf, xcat):
    t = ck_ref.shape[0]
    nb = t // CMP_STRIDE
    dh = NSA_HEAD_DIM

    def one(src_ref, w1_ref, w2_ref, pos_ref, out_ref):
        xf[...] = src_ref[...].astype(F32)
        for l in range(CMP_STRIDE):
            xcat[:, l * dh:(l + 1) * dh] = xf[pl.ds(l, nb, stride=CMP_STRIDE), :].astype(BF16)
        w1 = w1_ref[...]
        ab = _dot(xcat[...], w1)
        pp = _dot(pos_ref[...], w1)
        pos_term = pp[0:1, 0:dh] + pp[1:2, dh:2 * dh]
        second = pltpu.roll(ab[:, dh:2 * dh], nb - 1, 0)
        hid = _silu(ab[:, 0:dh] + second + pos_term)
        out_ref[0, 0] = _dot(hid.astype(BF16), w2_ref[...]).astype(BF16)

    one(ck_ref, w1k_ref, w2k_ref, pk_ref, ok_ref)
    one(cv_ref, w1v_ref, w2v_ref, pv_ref, ov_ref)


def _compress(proj, w1k, w2k, pk, w1v, w2v, pv, bsz, t):
    g_, dh = NSA_KV_GROUPS, NSA_HEAD_DIM
    nb = t // CMP_STRIDE
    full = lambda a: pl.BlockSpec(a.shape, lambda b, g: (0,) * a.ndim)
    out_spec = pl.BlockSpec((1, 1, nb, dh), lambda b, g: (b, g, 0, 0))
    out_sd = jax.ShapeDtypeStruct((bsz, g_, nb, dh), BF16)
    return pl.pallas_call(
        _compress_kernel,
        grid=(bsz, g_),
        in_specs=[
            pl.BlockSpec((t, dh), lambda b, g: (b, COL_CK // dh + g)),
            pl.BlockSpec((t, dh), lambda b, g: (b, COL_CV // dh + g)),
            full(w1k), full(w2k), full(pk), full(w1v), full(w2v), full(pv),
        ],
        out_specs=[out_spec, out_spec],
        out_shape=[out_sd, out_sd],
        scratch_shapes=[pltpu.VMEM((t, dh), F32), pltpu.VMEM((nb, CMP_STRIDE * dh), BF16)],
        compiler_params=_cparams(("parallel", "parallel")),
        name="compress",
    )(proj, proj, w1k, w2k, pk, w1v, w2v, pv)


def _cmp_attn_kernel(q_ref, kc_ref, vc_ref, bias_ref, cov_ref, gate_ref, oc_ref, sel_ref, *, n_sel):
    tq = q_ref.shape[0]
    dh = NSA_HEAD_DIM
    n_slc = cov_ref.shape[0]
    t0 = pl.program_id(1) * tq
    g = pl.program_id(0)
    scale = dh ** -0.5
    kc = kc_ref[0, 0]
    vc = vc_ref[0, 0]
    gates = gate_ref[...]

    p_sum = None
    for hh in range(NSA_HPG):
        q = (q_ref[:, hh * dh:(hh + 1) * dh].astype(F32) * scale).astype(BF16)
        bias = jnp.concatenate([bias_ref[0, hh, c] for c in range(bias_ref.shape[2])], axis=1)
        logit = _dot_nt(q, kc) + bias
        m = jnp.max(logit, axis=1, keepdims=True)
        e = jnp.exp(logit - m)
        p = jnp.where(bias > 0.5 * NEG_LOGIT, e, 0.0) / jnp.sum(e, axis=1, keepdims=True)
        p_sum = p if p_sum is None else p_sum + p
        o = _dot(p.astype(BF16), vc)
        oc_ref[:, hh * dh:(hh + 1) * dh] = (_gate_col(gates, g, hh, 0) * o).astype(BF16)

    cov = cov_ref[...]
    p_hi = p_sum.astype(BF16)
    r1 = p_sum - p_hi.astype(F32)
    p_mid = r1.astype(BF16)
    p_lo = (r1 - p_mid.astype(F32)).astype(BF16)
    st = _dot_nt(cov, p_hi) + _dot_nt(cov, p_mid) + _dot_nt(cov, p_lo)

    jb = lax.broadcasted_iota(jnp.int32, (n_slc, tq), 0)
    cur = (t0 + lax.broadcasted_iota(jnp.int32, (n_slc, tq), 1)) // SLC_BLOCK
    valid = jb <= cur
    forced = (jb == 0) | (jb == cur) | (jb == cur - 1)
    score = jnp.where(valid, st + jnp.where(forced, FORCE_BONUS, 0.0), -1.0)
    rank = jnp.zeros((n_slc, tq), F32)
    for j2 in range(n_slc):
        row = score[j2:j2 + 1, :]
        beats = (row > score) | ((row == score) & (j2 < jb))
        rank = rank + jnp.where(beats, 1.0, 0.0)
    sel = valid & (rank < n_sel)
    sel_ref[0, 0] = jnp.where(sel, 0.0, -MASK_BIG)


def _cmp_attn(proj, gates, k_cmp, v_cmp, bias_c, cover_t, bsz, t):
    g_, dh, tq = NSA_KV_GROUPS, NSA_HEAD_DIM, ATT_TQ
    nt = t // tq
    nb = t // CMP_STRIDE
    n_slc = t // SLC_BLOCK
    gw = NSA_HPG * dh
    return pl.pallas_call(
        functools.partial(_cmp_attn_kernel, n_sel=min(SLC_TOP_N, n_slc)),
        grid=(g_, nt, bsz),
        in_specs=[
            pl.BlockSpec((tq, gw), lambda g, i, b: (b * nt + i, COL_AQ // gw + g)),
            pl.BlockSpec((1, 1, nb, dh), lambda g, i, b: (b, g, 0, 0)),
            pl.BlockSpec((1, 1, nb, dh), lambda g, i, b: (b, g, 0, 0)),
            pl.BlockSpec((1, NSA_HPG, nb // LANES, tq, LANES), lambda g, i, b: (g, 0, 0, i, 0)),
            pl.BlockSpec((n_slc, nb), lambda g, i, b: (0, 0)),
            pl.BlockSpec((tq, GATE_COLS), lambda g, i, b: (b * nt + i, 0)),
        ],
        out_specs=[
            pl.BlockSpec((tq, gw), lambda g, i, b: (b * nt + i, g)),
            pl.BlockSpec((1, 1, n_slc, tq), lambda g, i, b: (b, g, 0, i)),
        ],
        out_shape=[
            jax.ShapeDtypeStruct((bsz * t, NSA_WIDTH), BF16),
            jax.ShapeDtypeStruct((bsz, g_, n_slc, t), F32),
        ],
        compiler_params=_cparams(("parallel", "parallel", "parallel")),
        name="cmp_attn",
    )(proj, k_cmp, v_cmp, bias_c, cover_t, gates)


def _flash_init(m_ref, acc_ref):
    m_ref[...] = jnp.full(m_ref.shape, NEG_LOGIT, F32)
    acc_ref[...] = jnp.zeros_like(acc_ref)


def _flash_step(q, k_tile, v_tile, bias, m_ref, acc_ref):
    s = _dot_nt(q, k_tile)
    if bias is not None:
        s = s + bias
    chunks = [s[:, c * LANES:(c + 1) * LANES] for c in range(s.shape[1] // LANES)]
    mx = functools.reduce(jnp.maximum, chunks)
    m_prev = m_ref[...]
    m_new = jnp.maximum(m_prev, jnp.max(mx, axis=1, keepdims=True))
    alpha = jnp.exp(m_prev - m_new)
    p = jnp.concatenate([jnp.exp(ch - m_new) for ch in chunks], axis=1)
    acc_ref[...] = jnp.concatenate([alpha, alpha], axis=1) * acc_ref[...] + _dot(p.astype(BF16), v_tile)
    m_ref[...] = m_new


def _flash_out(acc_ref, rows):
    dh = NSA_HEAD_DIM
    return acc_ref[rows, 0:dh] / acc_ref[rows, dh:2 * dh]


def _fill_vaug(vaug, v_ref):
    dh = NSA_HEAD_DIM
    vaug[:, 0:dh] = v_ref[...]
    vaug[:, dh:2 * dh] = jnp.ones((vaug.shape[0], dh), BF16)


def _gate_col(gates, g, hh, branch):
    out = None
    for gg in range(NSA_KV_GROUPS):
        col = GATE_NSA + 3 * (gg * NSA_HPG + hh) + branch
        c = gates[:, col:col + 1]
        out = c if out is None else jnp.where(g == gg, c, out)
    return _sigmoid(out)


def _slc_attn_kernel(q_ref, sb_ref, k_ref, v_ref, bias_ref, gate_ref, os_ref,
                     kaug, vaug, qaug, m_ref, acc_ref, *, n_near):
    tq, tk, dh = ATT_TQ, ATT_TK, NSA_HEAD_DIM
    t = k_ref.shape[0]
    g = pl.program_id(1)
    i = pl.program_id(2)
    scale = dh ** -0.5

    @pl.when(i == 0)
    def _():
        kaug[:, 0:dh] = k_ref[...]
        blk = lax.broadcasted_iota(jnp.int32, (t, LANES), 0) // SLC_BLOCK
        lane = lax.broadcasted_iota(jnp.int32, (t, LANES), 1)
        kaug[:, dh:dh + LANES] = jnp.where(blk == lane, 1.0, 0.0).astype(BF16)
        _fill_vaug(vaug, v_ref)

    sb = sb_ref[0, 0]
    for hh in range(NSA_HPG):
        qaug[hh * tq:(hh + 1) * tq, 0:dh] = (q_ref[:, hh * dh:(hh + 1) * dh].astype(F32) * scale).astype(BF16)
        qaug[hh * tq:(hh + 1) * tq, dh:dh + LANES] = sb
    _flash_init(m_ref, acc_ref)

    n_far = jnp.maximum(i - (n_near - 1), 0)

    def far_body(j, c):
        off = pl.multiple_of(j * tk, tk)
        _flash_step(qaug[...], kaug[pl.ds(off, tk), :], vaug[pl.ds(off, tk), :], None, m_ref, acc_ref)
        return c

    lax.fori_loop(0, n_far, far_body, 0)

    def near_body(j, c):
        off = pl.multiple_of(j * tk, tk)
        _flash_step(qaug[...], kaug[pl.ds(off, tk), :], vaug[pl.ds(off, tk), :], bias_ref[0, i - j],
                    m_ref, acc_ref)
        return c

    lax.fori_loop(n_far, i + 1, near_body, 0)

    gates = gate_ref[...]
    for hh in range(NSA_HPG):
        o = _flash_out(acc_ref, slice(hh * tq, (hh + 1) * tq)) * _gate_col(gates, g, hh, 1)
        os_ref[:, hh * dh:(hh + 1) * dh] = o.astype(BF16)


def _slc_attn(proj, gates, selbias, bias_s, bsz, t):
    g_, dh, tq, tk = NSA_KV_GROUPS, NSA_HEAD_DIM, ATT_TQ, ATT_TK
    nt = t // tq
    gw = NSA_HPG * dh
    r = NSA_HPG * tq
    n_near = bias_s.shape[1]
    return pl.pallas_call(
        functools.partial(_slc_attn_kernel, n_near=n_near),
        grid=(bsz, g_, nt),
        in_specs=[
            pl.BlockSpec((tq, gw), lambda b, g, i: (b * nt + i, COL_AQ // gw + g)),
            pl.BlockSpec((1, 1, tq, LANES), lambda b, g, i: (b, g, i, 0)),
            pl.BlockSpec((t, dh), lambda b, g, i: (b, COL_SK // dh + g)),
            pl.BlockSpec((t, dh), lambda b, g, i: (b, COL_SV // dh + g)),
            pl.BlockSpec((1, n_near, r, tk), lambda b, g, i: (g, 0, 0, 0)),
            pl.BlockSpec((tq, GATE_COLS), lambda b, g, i: (b * nt + i, 0)),
        ],
        out_specs=pl.BlockSpec((tq, gw), lambda b, g, i: (b * nt + i, g)),
        out_shape=jax.ShapeDtypeStruct((bsz * t, NSA_WIDTH), BF16),
        scratch_shapes=[
            pltpu.VMEM((t, dh + LANES), BF16),
            pltpu.VMEM((t, 2 * dh), BF16),
            pltpu.VMEM((r, dh + LANES), BF16),
            pltpu.VMEM((r, LANES), F32),
            pltpu.VMEM((r, 2 * dh), F32),
        ],
        compiler_params=_cparams(("parallel", "parallel", "arbitrary")),
        name="slc_attn",
    )(proj, selbias, proj, proj, bias_s, gates)


def _win_attn_kernel(q_ref, k_ref, v_ref, bias_ref, gate_ref, oc_ref, os_ref, z_ref, ya_ref,
                     vaug, qs, m_ref, acc_ref, *, n_near):
    tq, tk, dh = ATT_TQ, ATT_TK, NSA_HEAD_DIM
    g = pl.program_id(1)
    i = pl.program_id(2)
    scale = dh ** -0.5

    @pl.when(i == 0)
    def _():
        _fill_vaug(vaug, v_ref)

    for hh in range(NSA_HPG):
        qs[hh * tq:(hh + 1) * tq, :] = (q_ref[:, hh * dh:(hh + 1) * dh].astype(F32) * scale).astype(BF16)
    _flash_init(m_ref, acc_ref)

    def body(d, c):
        off = pl.multiple_of((i - d) * tk, tk)
        _flash_step(qs[...], k_ref[pl.ds(off, tk), :], vaug[pl.ds(off, tk), :], bias_ref[0, d], m_ref, acc_ref)
        return c

    lax.fori_loop(0, jnp.minimum(i + 1, n_near), body, 0)

    gates = gate_ref[...]
    for hh in range(NSA_HPG):
        cols = slice(hh * dh, (hh + 1) * dh)
        o = _flash_out(acc_ref, slice(hh * tq, (hh + 1) * tq)) * _gate_col(gates, g, hh, 2)
        o = o + oc_ref[:, cols].astype(F32) + os_ref[:, cols].astype(F32)
        ya_ref[:, cols] = (o * _silu(z_ref[:, cols].astype(F32))).astype(BF16)


def _win_attn(proj, gates, o_cmp, o_slc, bias_w, bsz, t):
    g_, dh, tq, tk = NSA_KV_GROUPS, NSA_HEAD_DIM, ATT_TQ, ATT_TK
    nt = t // tq
    gw = NSA_HPG * dh
    r = NSA_HPG * tq
    n_near = bias_w.shape[1]
    return pl.pallas_call(
        functools.partial(_win_attn_kernel, n_near=n_near),
        grid=(bsz, g_, nt),
        in_specs=[
            pl.BlockSpec((tq, gw), lambda b, g, i: (b * nt + i, COL_AQ // gw + g)),
            pl.BlockSpec((t, dh), lambda b, g, i: (b, COL_WK // dh + g)),
            pl.BlockSpec((t, dh), lambda b, g, i: (b, COL_WV // dh + g)),
            pl.BlockSpec((1, n_near, r, tk), lambda b, g, i: (g, 0, 0, 0)),
            pl.BlockSpec((tq, GATE_COLS), lambda b, g, i: (b * nt + i, 0)),
            pl.BlockSpec((tq, gw), lambda b, g, i: (b * nt + i, g)),
            pl.BlockSpec((tq, gw), lambda b, g, i: (b * nt + i, g)),
            pl.BlockSpec((tq, gw), lambda b, g, i: (b * nt + i, COL_AZ // gw + g)),
        ],
        out_specs=pl.BlockSpec((tq, gw), lambda b, g, i: (b * nt + i, g)),
        out_shape=jax.ShapeDtypeStruct((bsz * t, NSA_WIDTH), BF16),
        scratch_shapes=[
            pltpu.VMEM((t, 2 * dh), BF16),
            pltpu.VMEM((r, dh), BF16),
            pltpu.VMEM((r, LANES), F32),
            pltpu.VMEM((r, 2 * dh), F32),
        ],
        compiler_params=_cparams(("parallel", "parallel", "arbitrary")),
        name="win_attn",
    )(proj, proj, proj, bias_w, gates, o_cmp, o_slc, proj)


def _outproj_kernel(ym_ref, ya_ref, x_ref, w1_ref, w2_ref, gain_ref, o_ref, rows):
    j = pl.program_id(1)
    nj = rows.shape[0]
    tn = rows.shape[2]
    rows[j] = x_ref[...] + _dot(ym_ref[...], w1_ref[...]) + _dot(ya_ref[...], w2_ref[...])

    @pl.when(j == nj - 1)
    def _():
        ss = None
        for jj in range(nj):
            y = rows[jj]
            part = jnp.sum(y * y, axis=-1, keepdims=True)
            ss = part if ss is None else ss + part
        inv = lax.rsqrt(ss / (nj * tn) + RMS_EPS)
        for jj in range(nj):
            o_ref[:, jj * tn:(jj + 1) * tn] = rows[jj] * inv * gain_ref[:, jj * tn:(jj + 1) * tn]


def _out_proj(y_m, y_a, x2d, w_out, gain):
    n, d = x2d.shape
    tm, tn = min(OUTPROJ_TM, n), OUTPROJ_TN
    nj = d // tn
    return pl.pallas_call(
        _outproj_kernel,
        grid=(n // tm, nj),
        in_specs=[
            pl.BlockSpec((tm, MLSTM_WIDTH), lambda i, j: (i, 0)),
            pl.BlockSpec((tm, NSA_WIDTH), lambda i, j: (i, 0)),
            pl.BlockSpec((tm, tn), lambda i, j: (i, j)),
            pl.BlockSpec((MLSTM_WIDTH, tn), lambda i, j: (0, j)),
            pl.BlockSpec((NSA_WIDTH, tn), lambda i, j: (MLSTM_WIDTH // NSA_WIDTH, j)),
            pl.BlockSpec((1, d), lambda i, j: (0, 0)),
        ],
        out_specs=pl.BlockSpec((tm, d), lambda i, j: (i, 0)),
        out_shape=jax.ShapeDtypeStruct((n, d), F32),
        scratch_shapes=[pltpu.VMEM((nj, tm, tn), F32)],
        compiler_params=_cparams(("parallel", "arbitrary")),
        name="out_proj",
    )(y_m, y_a, x2d, w_out, w_out, gain)


def _rel_bucket(dist):
    n = jnp.maximum(dist, 0)
    nf = jnp.maximum(n, REL_MAX_EXACT).astype(jnp.float32)
    large = REL_MAX_EXACT + (jnp.log(nf / REL_MAX_EXACT) / math.log(REL_MAX_DISTANCE / REL_MAX_EXACT)
                             * (REL_BUCKETS - REL_MAX_EXACT)).astype(jnp.int32)
    large = jnp.minimum(large, REL_BUCKETS - 1)
    return jnp.where(n < REL_MAX_EXACT, n, large)


def _toeplitz_vec(by_dist, step, base, n_rows, n_cols, lo, hi, shift=None):
    w = n_rows + n_cols
    c = np.arange(w)
    dist = np.where(c < n_cols, base - step * c, base + step * (w - c))
    ok = (dist >= lo) & (dist < hi)
    vals = by_dist[:, np.clip(dist, 0, by_dist.shape[1] - 1)]
    if shift is not None:
        vals = vals - shift
    return jnp.where(ok[None], vals, NEG_LOGIT)


def _toeplitz(w_row, n_rows, n_cols):
    x = jnp.broadcast_to(w_row, (n_rows, n_rows + n_cols))
    return pltpu.roll(x, 0, 1, stride=1, stride_axis=0)[:, 0:n_cols]


def _bias_tables_kernel(wc_ref, ws_ref, ww_ref, bc_ref, bs_ref, bw_ref):
    nb = wc_ref.shape[2] // 2
    tq, tk = bs_ref.shape[2], bs_ref.shape[3]
    for r in range(CMP_STRIDE):
        y = _toeplitz(wc_ref[0, r:r + 1, :], nb, nb)
        for c in range(nb // LANES):
            bc_ref[0, 0, c, pl.ds(r, nb, stride=CMP_STRIDE), :] = y[:, c * LANES:(c + 1) * LANES]
    for d in range(ws_ref.shape[1]):
        bs_ref[0, d] = _toeplitz(ws_ref[0, d:d + 1, :], tq, tk)
    for d in range(ww_ref.shape[1]):
        bw_ref[0, d] = _toeplitz(ww_ref[0, d:d + 1, :], tq, tk)


def _bias_tables(rel_bias, t):
    tq, tk = ATT_TQ, ATT_TK
    g_, hpg = NSA_KV_GROUPS, NSA_HPG
    rb = rel_bias.astype(F32)
    dmax = REL_MAX_DISTANCE + tq + tk
    by_dist = rb[_rel_bucket(jnp.arange(dmax, dtype=jnp.int32))].T
    far = rb[REL_BUCKETS - 1][:, None]
    big = 1 << 30
    nb = t // CMP_STRIDE
    wc = jnp.stack([_toeplitz_vec(by_dist, CMP_STRIDE, r - (CMP_BLOCK - 1), nb, nb, 0, big)
                    for r in range(CMP_STRIDE)], axis=1)
    n_s = min(-(-(REL_MAX_DISTANCE + tk - 1) // tq), t // tq)
    ws = jnp.stack([_toeplitz_vec(by_dist, 1, d * tq, tq, tk, 0, big, far) for d in range(n_s)], axis=1)
    n_w = min(-(-(WINDOW + tk - 1) // tq), t // tq)
    ww = jnp.stack([_toeplitz_vec(by_dist, 1, d * tq, tq, tk, 0, WINDOW) for d in range(n_w)], axis=1)
    nlc = nb // LANES
    return pl.pallas_call(
        _bias_tables_kernel,
        grid=(NSA_HEADS,),
        in_specs=[
            pl.BlockSpec((1, CMP_STRIDE, 2 * nb), lambda h: (h, 0, 0)),
            pl.BlockSpec((1, n_s, tq + tk), lambda h: (h, 0, 0)),
            pl.BlockSpec((1, n_w, tq + tk), lambda h: (h, 0, 0)),
        ],
        out_specs=[
            pl.BlockSpec((1, 1, nlc, t, LANES), lambda h: (h // hpg, h % hpg, 0, 0, 0)),
            pl.BlockSpec((1, n_s, tq, tk), lambda h: (h // hpg, 0, h % hpg, 0)),
            pl.BlockSpec((1, n_w, tq, tk), lambda h: (h // hpg, 0, h % hpg, 0)),
        ],
        out_shape=[
            jax.ShapeDtypeStruct((g_, hpg, nlc, t, LANES), F32),
            jax.ShapeDtypeStruct((g_, n_s, hpg * tq, tk), F32),
            jax.ShapeDtypeStruct((g_, n_w, hpg * tq, tk), F32),
        ],
        compiler_params=_cparams(("parallel",)),
        name="bias_tables",
    )(wc, ws, ww)


def _cover_t(t):
    nb = t // CMP_STRIDE
    n_cmp = (t - CMP_BLOCK) // CMP_STRIDE + 1
    n_slc = t // SLC_BLOCK
    cs = np.arange(nb) * CMP_STRIDE
    ss = np.arange(n_slc) * SLC_BLOCK
    cover = np.clip(np.minimum(cs[:, None] + CMP_BLOCK, ss[None, :] + SLC_BLOCK)
                    - np.maximum(cs[:, None], ss[None, :]), 0, None) / CMP_BLOCK
    cover[n_cmp:] = 0.0
    return jnp.asarray(cover.T, dtype=BF16)


def kernel(x, norm_gain, w_in, w_conv, b_igate, b_fgate, mlstm_norm_gain, cmp_k_pos, cmp_k_w1, cmp_k_w2,
           cmp_v_pos, cmp_v_w1, cmp_v_w2, rel_bias, w_out, final_norm_gain):
    bsz, t, d = x.shape
    assert d == D_MODEL and t % MLSTM_L == 0 and t % ATT_TQ == 0 and (t // CMP_STRIDE) % LANES == 0
    n = bsz * t
    x2d = x.reshape(n, d)

    o_i = COL_MZ + MLSTM_WIDTH
    o_aq = o_i + 2 * MLSTM_HEADS
    o_gate = o_aq + NSA_WIDTH + 6 * NSA_KV_WIDTH
    o_az = o_gate + 3 * NSA_HEADS
    w_main = jnp.concatenate([w_in[:, :o_i], w_in[:, o_aq:o_gate], w_in[:, o_az:]], axis=1).astype(BF16)
    w_gate = jnp.concatenate([w_in[:, o_i:o_aq], w_in[:, o_gate:o_az],
                              jnp.zeros((d, GATE_COLS - 2 * MLSTM_HEADS - 3 * NSA_HEADS), w_in.dtype)], axis=1).astype(BF16)

    proj, gates = _in_proj(x2d, norm_gain.reshape(1, d).astype(F32), w_main, w_gate)

    g_rows = gates[:, :2 * MLSTM_HEADS].reshape(bsz, t, 2, MLSTM_HEADS).transpose(0, 3, 2, 1)
    y_m = _mlstm(proj, g_rows, w_conv.astype(F32), b_igate.astype(F32), b_fgate.astype(F32),
                 mlstm_norm_gain.reshape(1, MLSTM_WIDTH).astype(F32), bsz, t)

    dh = NSA_HEAD_DIM
    half = CMP_BLOCK // 2

    def w1cat(w1):
        return jnp.concatenate([w1[:half].reshape(half * dh, dh), w1[half:].reshape(half * dh, dh)], axis=1).astype(BF16)

    k_cmp, v_cmp = _compress(
        proj, w1cat(cmp_k_w1), cmp_k_w2.astype(BF16), cmp_k_pos.reshape(2, half * dh).astype(BF16),
        w1cat(cmp_v_w1), cmp_v_w2.astype(BF16), cmp_v_pos.reshape(2, half * dh).astype(BF16), bsz, t)

    bias_c, bias_s, bias_w = _bias_tables(rel_bias, t)
    o_cmp, sel_t = _cmp_attn(proj, gates, k_cmp, v_cmp, bias_c, _cover_t(t), bsz, t)
    n_slc = t // SLC_BLOCK
    selbias = jnp.pad(sel_t.transpose(0, 1, 3, 2), ((0, 0), (0, 0), (0, 0), (0, LANES - n_slc))).astype(BF16)
    o_slc = _slc_attn(proj, gates, selbias, bias_s, bsz, t)
    y_a = _win_attn(proj, gates, o_cmp, o_slc, bias_w, bsz, t)

    out = _out_proj(y_m, y_a, x2d, w_out.astype(BF16), final_norm_gain.reshape(1, d).astype(F32))
    return out.reshape(bsz, t, d)
```

```python
import functools
import math

import jax
import jax.numpy as jnp
import numpy as np
from jax import lax
from jax.experimental import pallas as pl
from jax.experimental.pallas import tpu as pltpu

F32 = jnp.float32
BF16 = jnp.bfloat16

D_MODEL = 4096
D_MIX = D_MODEL
MLSTM_WIDTH = D_MIX // 2
MLSTM_HEADS = 4
MLSTM_V_DIM = MLSTM_WIDTH // MLSTM_HEADS
MLSTM_QK_DIM = MLSTM_V_DIM // 2
MLSTM_QK_WIDTH = MLSTM_HEADS * MLSTM_QK_DIM
CONV_WIDTH = 4
NSA_WIDTH = D_MIX - MLSTM_WIDTH
NSA_HEAD_DIM = 128
NSA_HEADS = NSA_WIDTH // NSA_HEAD_DIM
NSA_KV_GROUPS = 4
NSA_HPG = NSA_HEADS // NSA_KV_GROUPS
NSA_KV_WIDTH = NSA_KV_GROUPS * NSA_HEAD_DIM
CMP_BLOCK = 32
CMP_STRIDE = 16
SLC_BLOCK = 64
SLC_TOP_N = 16
WINDOW = 512
FORCE_BONUS = 1000.0
REL_BUCKETS = 32
REL_MAX_EXACT = REL_BUCKETS // 2
REL_MAX_DISTANCE = 1024
RMS_EPS = 1e-6
NEG_LOGIT = -1e30

LANES = 128
VMEM_LIMIT_BYTES = 56 * 1024 * 1024

COL_MQ = 0
COL_MK = COL_MQ + MLSTM_QK_WIDTH
COL_MV = COL_MK + MLSTM_QK_WIDTH
COL_MO = COL_MV + MLSTM_WIDTH
COL_MZ = COL_MO + MLSTM_WIDTH
COL_AQ = COL_MZ + MLSTM_WIDTH
COL_CK = COL_AQ + NSA_WIDTH
COL_CV = COL_CK + NSA_KV_WIDTH
COL_SK = COL_CV + NSA_KV_WIDTH
COL_SV = COL_SK + NSA_KV_WIDTH
COL_WK = COL_SV + NSA_KV_WIDTH
COL_WV = COL_WK + NSA_KV_WIDTH
COL_AZ = COL_WV + NSA_KV_WIDTH
MAIN_COLS = COL_AZ + NSA_WIDTH
GATE_COLS = LANES
GATE_I = 0
GATE_F = MLSTM_HEADS
GATE_NSA = 2 * MLSTM_HEADS

INPROJ_TM = 512
INPROJ_TN = 1024
OUTPROJ_TM = 512
OUTPROJ_TN = 512
MLSTM_L = 256
ATT_TQ = 256
ATT_TK = 256
MASK_BIG = 2.0 ** 100


def _cparams(sem):
    return pltpu.CompilerParams(dimension_semantics=sem, vmem_limit_bytes=VMEM_LIMIT_BYTES)


def _sigmoid(x):
    return 1.0 / (1.0 + jnp.exp(-x))


def _silu(x):
    return x * _sigmoid(x)


def _dot(a, b):
    return jnp.dot(a, b, preferred_element_type=F32)


def _inproj_kernel(x_ref, gain_ref, w_ref, wg_ref, o_ref, og_ref, h_ref):
    @pl.when(pl.program_id(1) == 0)
    def _():
        x = x_ref[...]
        ms = jnp.mean(x * x, axis=-1, keepdims=True)
        h = (x * lax.rsqrt(ms + RMS_EPS) * gain_ref[...]).astype(BF16)
        h_ref[...] = h
        og_ref[...] = _dot(h, wg_ref[...])

    o_ref[...] = _dot(h_ref[...], w_ref[...]).astype(BF16)


def _in_proj(x2d, gain, w_main, w_gate):
    n, d = x2d.shape
    tm, tn = min(INPROJ_TM, n), INPROJ_TN
    return pl.pallas_call(
        _inproj_kernel,
        grid=(n // tm, MAIN_COLS // tn),
        in_specs=[
            pl.BlockSpec((tm, d), lambda i, j: (i, 0)),
            pl.BlockSpec((1, d), lambda i, j: (0, 0)),
            pl.BlockSpec((d, tn), lambda i, j: (0, j)),
            pl.BlockSpec((d, GATE_COLS), lambda i, j: (0, 0)),
        ],
        out_specs=[
            pl.BlockSpec((tm, tn), lambda i, j: (i, j)),
            pl.BlockSpec((tm, GATE_COLS), lambda i, j: (i, 0)),
        ],
        out_shape=[
            jax.ShapeDtypeStruct((n, MAIN_COLS), BF16),
            jax.ShapeDtypeStruct((n, GATE_COLS), F32),
        ],
        scratch_shapes=[pltpu.VMEM((tm, d), BF16)],
        compiler_params=_cparams(("parallel", "arbitrary")),
        name="in_proj",
    )(x2d, gain, w_main, w_gate)


def _mlstm_kernel(bi_ref, bf_ref, q_ref, k_ref, v_ref, o_ref, z_ref, g_ref, wq_ref, wk_ref, ng_ref,
                  y_ref, qext, kext, c_st, n_st, m_st):
    L = MLSTM_L
    HIST = 8
    hd = pl.program_id(1)

    @pl.when(pl.program_id(2) == 0)
    def _():
        qext[0:HIST, :] = jnp.zeros((HIST, MLSTM_QK_DIM), F32)
        kext[0:HIST, :] = jnp.zeros((HIST, MLSTM_QK_DIM), F32)
        c_st[...] = jnp.zeros_like(c_st)
        n_st[...] = jnp.zeros_like(n_st)
        m_st[...] = jnp.zeros_like(m_st)

    qext[HIST:HIST + L, :] = q_ref[...].astype(F32)
    kext[HIST:HIST + L, :] = k_ref[...].astype(F32)

    def conv_silu(ext, w_ref):
        w = w_ref[...]
        y = ext[pl.ds(HIST, L), :] * w[CONV_WIDTH - 1:CONV_WIDTH, :]
        for s in range(1, CONV_WIDTH):
            y = y + ext[pl.ds(HIST - s, L), :] * w[CONV_WIDTH - 1 - s:CONV_WIDTH - s, :]
        return _silu(y)

    qc = conv_silu(qext, wq_ref)
    kc = conv_silu(kext, wk_ref) * (MLSTM_QK_DIM ** -0.5)
    qext[0:HIST, :] = qext[L:L + HIST, :]
    kext[0:HIST, :] = kext[L:L + HIST, :]

    g = g_ref[0, 0]
    i_row = g[0:1, :] + bi_ref[hd]
    f_row = g[1:2, :] + bf_ref[hd]
    lf_row = jnp.minimum(f_row, 0.0) - jnp.log(1.0 + jnp.exp(-jnp.abs(f_row)))

    rr = lax.broadcasted_iota(jnp.int32, (L, L), 0)
    cc = lax.broadcasted_iota(jnp.int32, (L, L), 1)
    upper = (rr <= cc).astype(F32)
    bcum_row = jnp.dot(jnp.broadcast_to(lf_row, (8, L)), upper, preferred_element_type=F32,
                       precision=lax.Precision.HIGHEST)[0:1, :]
    bcum_col = jnp.sum(jnp.where(rr == cc, bcum_row, 0.0), axis=1, keepdims=True)
    gsum = bcum_row[:, L - 1:L]
    m_prev = m_st[...]

    dlog = jnp.where(rr >= cc, bcum_col - bcum_row + i_row, -jnp.inf)
    m_inter = bcum_col + m_prev
    m_t = jnp.maximum(m_inter, jnp.max(dlog, axis=1, keepdims=True))
    dmat = jnp.exp(dlog - m_t)
    inter = jnp.exp(m_inter - m_t)

    qb = qc.astype(BF16)
    kct = kc.T
    vb = v_ref[...]
    s = _dot(qb, kct.astype(BF16)) * dmat
    c_prev = c_st[...]
    n_prev = n_st[...]
    num = _dot(s.astype(BF16), vb) + inter * _dot(qb, c_prev.astype(BF16))
    qn = jnp.sum(s, axis=1, keepdims=True) + inter * jnp.sum(qc * n_prev, axis=1, keepdims=True)
    hh = num / jnp.maximum(jnp.abs(qn), jnp.exp(-m_t))

    wlog = gsum - bcum_row + i_row
    m_next = jnp.maximum(gsum + m_prev, jnp.max(wlog, axis=1, keepdims=True))
    wts = jnp.exp(wlog - m_next)
    keep = jnp.exp(gsum + m_prev - m_next)
    c_st[...] = keep * c_prev + _dot((kct * wts).astype(BF16), vb)
    n_st[...] = keep * n_prev + _dot(jnp.broadcast_to(wts, (8, L)).astype(BF16), kc.astype(BF16))[0:1, :]
    m_st[...] = m_next

    hm = _sigmoid(o_ref[...].astype(F32)) * hh
    hm = hm * lax.rsqrt(jnp.mean(hm * hm, axis=-1, keepdims=True) + RMS_EPS)
    hm = hm * ng_ref[...]
    y_ref[...] = (hm * _silu(z_ref[...].astype(F32))).astype(BF16)


def _mlstm(proj, g_rows, w_conv, b_igate, b_fgate, norm_gain, bsz, t):
    L = MLSTM_L
    nc = t // L
    dqk, dv = MLSTM_QK_DIM, MLSTM_V_DIM
    row = lambda b, h, c: b * nc + c
    smem = pl.BlockSpec(memory_space=pltpu.SMEM)
    return pl.pallas_call(
        _mlstm_kernel,
        grid=(bsz, MLSTM_HEADS, nc),
        in_specs=[
            smem, smem,
            pl.BlockSpec((L, dqk), lambda b, h, c: (row(b, h, c), COL_MQ // dqk + h)),
            pl.BlockSpec((L, dqk), lambda b, h, c: (row(b, h, c), COL_MK // dqk + h)),
            pl.BlockSpec((L, dv), lambda b, h, c: (row(b, h, c), COL_MV // dv + h)),
            pl.BlockSpec((L, dv), lambda b, h, c: (row(b, h, c), COL_MO // dv + h)),
            pl.BlockSpec((L, dv), lambda b, h, c: (row(b, h, c), COL_MZ // dv + h)),
            pl.BlockSpec((1, 1, 2, L), lambda b, h, c: (b, h, 0, c)),
            pl.BlockSpec((CONV_WIDTH, dqk), lambda b, h, c: (0, h)),
            pl.BlockSpec((CONV_WIDTH, dqk), lambda b, h, c: (0, MLSTM_HEADS + h)),
            pl.BlockSpec((1, dv), lambda b, h, c: (0, h)),
        ],
        out_specs=pl.BlockSpec((L, dv), lambda b, h, c: (row(b, h, c), h)),
        out_shape=jax.ShapeDtypeStruct((bsz * t, MLSTM_WIDTH), BF16),
        scratch_shapes=[
            pltpu.VMEM((L + 8, dqk), F32),
            pltpu.VMEM((L + 8, dqk), F32),
            pltpu.VMEM((dqk, dv), F32),
            pltpu.VMEM((1, dqk), F32),
            pltpu.VMEM((1, 1), F32),
        ],
        compiler_params=_cparams(("parallel", "parallel", "arbitrary")),
        name="mlstm",
    )(b_igate, b_fgate, proj, proj, proj, proj, proj, g_rows, w_conv, w_conv, norm_gain)


def _compress_kernel(ck_ref, cv_ref, w1k_ref, w2k_ref, pk_ref, w1v_ref, w2v_ref, pv_ref,
                     ok_ref, ov_ref, xf, xcat):
    t = ck_ref.shape[0]
    nb = t // CMP_STRIDE
    dh = NSA_HEAD_DIM

    def one(src_ref, w1_ref, w2_ref, pos_ref):
        xf[...] = src_ref[...].astype(F32)
        for l in range(CMP_STRIDE):
            xcat[:, l * dh:(l + 1) * dh] = xf[pl.ds(l, nb, stride=CMP_STRIDE), :].astype(BF16)
        w1 = w1_ref[...]
        ab = _dot(xcat[...], w1)
        pp = _dot(pos_ref[...], w1)
        pos_term = pp[0:1, 0:dh] + pp[1:2, dh:2 * dh]
        second = pltpu.roll(ab[:, dh:2 * dh], nb - 1, 0)
        hid = _silu(ab[:, 0:dh] + second + pos_term)
        return _dot(hid.astype(BF16), w2_ref[...])

    ok_ref[0, 0] = one(ck_ref, w1k_ref, w2k_ref, pk_ref).astype(BF16)
    ov_ref[0, 0] = one(cv_ref, w1v_ref, w2v_ref, pv_ref).T.astype(BF16)


def _compress(proj, w1k, w2k, pk, w1v, w2v, pv, bsz, t):
    g_, dh = NSA_KV_GROUPS, NSA_HEAD_DIM
    nb = t // CMP_STRIDE
    full = lambda a: pl.BlockSpec(a.shape, lambda b, g: (0,) * a.ndim)
    k_spec = pl.BlockSpec((1, 1, nb, dh), lambda b, g: (b, g, 0, 0))
    vt_spec = pl.BlockSpec((1, 1, dh, nb), lambda b, g: (b, g, 0, 0))
    return pl.pallas_call(
        _compress_kernel,
        grid=(bsz, g_),
        in_specs=[
            pl.BlockSpec((t, dh), lambda b, g: (b, COL_CK // dh + g)),
            pl.BlockSpec((t, dh), lambda b, g: (b, COL_CV // dh + g)),
            full(w1k), full(w2k), full(pk), full(w1v), full(w2v), full(pv),
        ],
        out_specs=[k_spec, vt_spec],
        out_shape=[jax.ShapeDtypeStruct((bsz, g_, nb, dh), BF16), jax.ShapeDtypeStruct((bsz, g_, dh, nb), BF16)],
        scratch_shapes=[pltpu.VMEM((t, dh), F32), pltpu.VMEM((nb, CMP_STRIDE * dh), BF16)],
        compiler_params=_cparams(("parallel", "parallel")),
        name="compress",
    )(proj, proj, w1k, w2k, pk, w1v, w2v, pv)


LOG2E = math.log2(math.e)
VROWS = NSA_HEAD_DIM + 16


def _q_t(q_ref, hh):
    dh = NSA_HEAD_DIM
    return (q_ref[:, hh * dh:(hh + 1) * dh].astype(F32) * (dh ** -0.5 * LOG2E)).T.astype(BF16)


def _gate_col(gates, g, hh, branch):
    out = None
    for gg in range(NSA_KV_GROUPS):
        col = GATE_NSA + 3 * (gg * NSA_HPG + hh) + branch
        c = gates[:, col:col + 1]
        out = c if out is None else jnp.where(g == gg, c, out)
    return _sigmoid(out)


def _fill_vt(vt, v_ref):
    dh = NSA_HEAD_DIM
    vt[0:dh, :] = v_ref[...].astype(F32).T.astype(BF16)
    vt[dh:VROWS, :] = jnp.ones((VROWS - dh, vt.shape[1]), BF16)


def _out_rows(acc, gate):
    dh = NSA_HEAD_DIM
    return (acc[0:dh, :] / acc[dh:dh + 1, :]).T * gate


def _cmp_attn_kernel(q_ref, kc_ref, vct_ref, bias_ref, cov_ref, gate_ref, oc_ref, sel_ref, score_ref, *, n_sel):
    tq = q_ref.shape[0]
    dh = NSA_HEAD_DIM
    n_slc = cov_ref.shape[0]
    t0 = pl.program_id(1) * tq
    g = pl.program_id(0)
    kc = kc_ref[0, 0]
    vct = vct_ref[0, 0]
    gates = gate_ref[...]

    heads = range(NSA_HPG)
    qk = [_dot(kc, _q_t(q_ref, hh)) for hh in heads]
    ps = []
    for hh in heads:
        logit = qk[hh] + bias_ref[0, hh]
        m = jnp.max(logit, axis=0, keepdims=True)
        e = jnp.exp2(logit - m)
        inv = jnp.where(m > 0.5 * NEG_LOGIT, 1.0 / jnp.sum(e, axis=0, keepdims=True), 0.0)
        ps.append(e * inv)
    p_sum = (ps[0] + ps[1]) + (ps[2] + ps[3])
    ots = [_dot(vct, p.astype(BF16)) for p in ps]
    for hh in heads:
        oc_ref[:, hh * dh:(hh + 1) * dh] = (ots[hh].T * _gate_col(gates, g, hh, 0)).astype(BF16)

    cov = cov_ref[...]
    p_hi = p_sum.astype(BF16)
    r1 = p_sum - p_hi.astype(F32)
    p_mid = r1.astype(BF16)
    p_lo = (r1 - p_mid.astype(F32)).astype(BF16)
    st = _dot(cov, p_hi) + _dot(cov, p_mid) + _dot(cov, p_lo)

    jb = lax.broadcasted_iota(jnp.int32, (n_slc, tq), 0)
    cur = (t0 + lax.broadcasted_iota(jnp.int32, (n_slc, tq), 1)) // SLC_BLOCK
    valid = jb <= cur
    forced = (jb == 0) | (jb == cur) | (jb == cur - 1)
    score = jnp.where(valid, st + jnp.where(forced, FORCE_BONUS, 0.0), -1.0)
    sub = 8
    score_ref[...] = score
    groups = [score_ref[r:r + sub, :] for r in range(0, n_slc, sub)]
    ranks = [jnp.zeros((sub, tq), F32) for _ in groups]
    row_id = lax.broadcasted_iota(jnp.int32, (sub, tq), 0)
    for j2 in range(n_slc):
        row = score_ref[j2:j2 + 1, :]
        for gi, sc in enumerate(groups):
            r0 = gi * sub
            if r0 > j2:
                inc = jnp.where(row >= sc, 1.0, 0.0)
            elif r0 + sub - 1 <= j2:
                inc = jnp.where(row > sc, 1.0, 0.0)
            else:
                inc = jnp.where(row_id > j2 - r0, jnp.where(row >= sc, 1.0, 0.0), jnp.where(row > sc, 1.0, 0.0))
            ranks[gi] = ranks[gi] + inc
    for gi, rk in enumerate(ranks):
        score_ref[gi * sub:(gi + 1) * sub, :] = rk
    sel = valid & (score_ref[...] < n_sel)
    sel_ref[0, 0, 0:n_slc, :] = jnp.where(sel, 0.0, -MASK_BIG).astype(BF16)
    if n_slc < LANES:
        sel_ref[0, 0, n_slc:LANES, :] = jnp.zeros((LANES - n_slc, tq), BF16)


def _cmp_attn(proj, gates, k_cmp, v_cmp_t, bias_c, cover_t, bsz, t):
    g_, dh, tq = NSA_KV_GROUPS, NSA_HEAD_DIM, ATT_TQ
    nt = t // tq
    nb = t // CMP_STRIDE
    n_slc = t // SLC_BLOCK
    gw = NSA_HPG * dh
    return pl.pallas_call(
        functools.partial(_cmp_attn_kernel, n_sel=min(SLC_TOP_N, n_slc)),
        grid=(g_, nt, bsz),
        in_specs=[
            pl.BlockSpec((tq, gw), lambda g, i, b: (b * nt + i, COL_AQ // gw + g)),
            pl.BlockSpec((1, 1, nb, dh), lambda g, i, b: (b, g, 0, 0)),
            pl.BlockSpec((1, 1, dh, nb), lambda g, i, b: (b, g, 0, 0)),
            pl.BlockSpec((1, NSA_HPG, nb, tq), lambda g, i, b: (g, 0, 0, i)),
            pl.BlockSpec((n_slc, nb), lambda g, i, b: (0, 0)),
            pl.BlockSpec((tq, GATE_COLS), lambda g, i, b: (b * nt + i, 0)),
        ],
        out_specs=[
            pl.BlockSpec((tq, gw), lambda g, i, b: (b * nt + i, g)),
            pl.BlockSpec((1, 1, LANES, tq), lambda g, i, b: (b, g, 0, i)),
        ],
        out_shape=[
            jax.ShapeDtypeStruct((bsz * t, NSA_WIDTH), BF16),
            jax.ShapeDtypeStruct((bsz, g_, LANES, t), BF16),
        ],
        scratch_shapes=[pltpu.VMEM((n_slc, tq), F32)],
        compiler_params=_cparams(("parallel", "parallel", "parallel")),
        name="cmp_attn",
    )(proj, k_cmp, v_cmp_t, bias_c, cover_t, gates)


def _slc_attn_kernel(q_ref, sbt_ref, k_ref, v_ref, bias_ref, gate_ref, os_ref,
                     kaug, vt, qt, s_buf, p_buf, a_buf, m_ref, acc_ref, *, n_near):
    tq, tk, dh = ATT_TQ, ATT_TK, NSA_HEAD_DIM
    t = k_ref.shape[0]
    g = pl.program_id(1)
    i = pl.program_id(2)
    heads = range(NSA_HPG)

    @pl.when(i == 0)
    def _():
        kaug[:, 0:dh] = k_ref[...]
        blk = lax.broadcasted_iota(jnp.int32, (t, LANES), 0) // SLC_BLOCK
        lane = lax.broadcasted_iota(jnp.int32, (t, LANES), 1)
        kaug[:, dh:dh + LANES] = jnp.where(blk == lane, 1.0, 0.0).astype(BF16)
        _fill_vt(vt, v_ref)

    sbt = sbt_ref[0, 0]
    for hh in heads:
        qt[hh, 0:dh, :] = _q_t(q_ref, hh)
        qt[hh, dh:dh + LANES, :] = sbt

    def ktile(j):
        return kaug[pl.ds(pl.multiple_of(j * tk, tk), tk), :]

    def vtile(j):
        return vt[:, pl.ds(pl.multiple_of(j * tk, tk), tk)]

    m_ref[...] = jnp.full(m_ref.shape, NEG_LOGIT, F32)
    acc_ref[...] = jnp.zeros_like(acc_ref)
    p_buf[...] = jnp.zeros_like(p_buf)
    a_buf[...] = jnp.ones_like(a_buf)
    k0 = ktile(0)
    for hh in heads:
        s_buf[hh] = _dot(k0, qt[hh])

    def stage(j, bias_fn):
        v_prev = vtile(jnp.maximum(j - 1, 0))
        k_next = ktile(jnp.minimum(j + 1, i))
        pv = [_dot(v_prev, p_buf[hh]) for hh in heads]
        s_next = [_dot(k_next, qt[hh]) for hh in heads]
        for hh in heads:
            acc_ref[hh] = a_buf[hh] * acc_ref[hh] + pv[hh]
        for hh in heads:
            s = s_buf[hh]
            if bias_fn is not None:
                s = s + bias_fn(hh)
            m_prev = m_ref[hh]
            m_new = jnp.maximum(m_prev, jnp.max(s, axis=0, keepdims=True))
            a_buf[hh] = jnp.exp2(m_prev - m_new)
            p_buf[hh] = jnp.exp2(s - m_new).astype(BF16)
            m_ref[hh] = m_new
        for hh in heads:
            s_buf[hh] = s_next[hh]

    n_far = jnp.maximum(i - (n_near - 1), 0)

    def far_body(j, c):
        stage(j, None)
        return c

    lax.fori_loop(0, n_far, far_body, 0)

    def near_body(j, c):
        stage(j, lambda hh: bias_ref[0, i - j, hh])
        return c

    lax.fori_loop(n_far, i + 1, near_body, 0)

    gates = gate_ref[...]
    v_last = vtile(i)
    pv = [_dot(v_last, p_buf[hh]) for hh in heads]
    for hh in heads:
        acc = a_buf[hh] * acc_ref[hh] + pv[hh]
        os_ref[:, hh * dh:(hh + 1) * dh] = _out_rows(acc, _gate_col(gates, g, hh, 1)).astype(BF16)


def _slc_attn(proj, gates, selbias_t, bias_s, bsz, t):
    g_, dh, tq, tk = NSA_KV_GROUPS, NSA_HEAD_DIM, ATT_TQ, ATT_TK
    nt = t // tq
    gw = NSA_HPG * dh
    n_near = bias_s.shape[1]
    return pl.pallas_call(
        functools.partial(_slc_attn_kernel, n_near=n_near),
        grid=(bsz, g_, nt),
        in_specs=[
            pl.BlockSpec((tq, gw), lambda b, g, i: (b * nt + i, COL_AQ // gw + g)),
            pl.BlockSpec((1, 1, LANES, tq), lambda b, g, i: (b, g, 0, i)),
            pl.BlockSpec((t, dh), lambda b, g, i: (b, COL_SK // dh + g)),
            pl.BlockSpec((t, dh), lambda b, g, i: (b, COL_SV // dh + g)),
            pl.BlockSpec((1, n_near, NSA_HPG, tk, tq), lambda b, g, i: (g, 0, 0, 0, 0)),
            pl.BlockSpec((tq, GATE_COLS), lambda b, g, i: (b * nt + i, 0)),
        ],
        out_specs=pl.BlockSpec((tq, gw), lambda b, g, i: (b * nt + i, g)),
        out_shape=jax.ShapeDtypeStruct((bsz * t, NSA_WIDTH), BF16),
        scratch_shapes=[
            pltpu.VMEM((t, dh + LANES), BF16),
            pltpu.VMEM((VROWS, t), BF16),
            pltpu.VMEM((NSA_HPG, dh + LANES, tq), BF16),
            pltpu.VMEM((NSA_HPG, tk, tq), F32),
            pltpu.VMEM((NSA_HPG, tk, tq), BF16),
            pltpu.VMEM((NSA_HPG, 1, tq), F32),
            pltpu.VMEM((NSA_HPG, 1, tq), F32),
            pltpu.VMEM((NSA_HPG, VROWS, tq), F32),
        ],
        compiler_params=_cparams(("parallel", "parallel", "arbitrary")),
        name="slc_attn",
    )(proj, selbias_t, proj, proj, bias_s, gates)


def _win_attn_kernel(q_ref, k_ref, v_ref, bias_ref, gate_ref, oc_ref, os_ref, z_ref, ya_ref, vt, *, n_near):
    tq, tk, dh = ATT_TQ, ATT_TK, NSA_HEAD_DIM
    g = pl.program_id(1)
    i = pl.program_id(2)

    @pl.when(i == 0)
    def _():
        _fill_vt(vt, v_ref)

    offs = [pl.multiple_of(jnp.maximum(i - d, 0) * tk, tk) for d in range(n_near)]
    d_eff = [jnp.where(i >= d, d, n_near) for d in range(n_near)]
    k_tiles = [k_ref[pl.ds(off, tk), :] for off in offs]
    v_tiles = [vt[:, pl.ds(off, tk)] for off in offs]
    qts = [_q_t(q_ref, hh) for hh in range(NSA_HPG)]
    ss = [[_dot(k_tiles[d], qts[hh]) for d in range(n_near)] for hh in range(NSA_HPG)]
    gates = gate_ref[...]
    ps = []
    for hh in range(NSA_HPG):
        s = [ss[hh][d] + bias_ref[0, d_eff[d], hh] for d in range(n_near)]
        m = functools.reduce(jnp.maximum, [jnp.max(x, axis=0, keepdims=True) for x in s])
        ps.append([jnp.exp2(x - m).astype(BF16) for x in s])
    pvs = [[_dot(v_tiles[d], ps[hh][d]) for d in range(n_near)] for hh in range(NSA_HPG)]
    for hh in range(NSA_HPG):
        cols = slice(hh * dh, (hh + 1) * dh)
        o = _out_rows(functools.reduce(lambda x, y: x + y, pvs[hh]), _gate_col(gates, g, hh, 2))
        o = o + oc_ref[:, cols].astype(F32) + os_ref[:, cols].astype(F32)
        ya_ref[:, cols] = (o * _silu(z_ref[:, cols].astype(F32))).astype(BF16)


def _win_attn(proj, gates, o_cmp, o_slc, bias_w, bsz, t):
    g_, dh, tq, tk = NSA_KV_GROUPS, NSA_HEAD_DIM, ATT_TQ, ATT_TK
    nt = t // tq
    gw = NSA_HPG * dh
    n_near = bias_w.shape[1] - 1
    return pl.pallas_call(
        functools.partial(_win_attn_kernel, n_near=n_near),
        grid=(bsz, g_, nt),
        in_specs=[
            pl.BlockSpec((tq, gw), lambda b, g, i: (b * nt + i, COL_AQ // gw + g)),
            pl.BlockSpec((t, dh), lambda b, g, i: (b, COL_WK // dh + g)),
            pl.BlockSpec((t, dh), lambda b, g, i: (b, COL_WV // dh + g)),
            pl.BlockSpec((1, n_near + 1, NSA_HPG, tk, tq), lambda b, g, i: (g, 0, 0, 0, 0)),
            pl.BlockSpec((tq, GATE_COLS), lambda b, g, i: (b * nt + i, 0)),
            pl.BlockSpec((tq, gw), lambda b, g, i: (b * nt + i, g)),
            pl.BlockSpec((tq, gw), lambda b, g, i: (b * nt + i, g)),
            pl.BlockSpec((tq, gw), lambda b, g, i: (b * nt + i, COL_AZ // gw + g)),
        ],
        out_specs=pl.BlockSpec((tq, gw), lambda b, g, i: (b * nt + i, g)),
        out_shape=jax.ShapeDtypeStruct((bsz * t, NSA_WIDTH), BF16),
        scratch_shapes=[pltpu.VMEM((VROWS, t), BF16)],
        compiler_params=_cparams(("parallel", "parallel", "arbitrary")),
        name="win_attn",
    )(proj, proj, proj, bias_w, gates, o_cmp, o_slc, proj)


def _outproj_kernel(ym_ref, ya_ref, x_ref, w1_ref, w2_ref, gain_ref, o_ref, rows):
    j = pl.program_id(1)
    nj = rows.shape[0]
    tn = rows.shape[2]
    rows[j] = x_ref[...] + _dot(ym_ref[...], w1_ref[...]) + _dot(ya_ref[...], w2_ref[...])

    @pl.when(j == nj - 1)
    def _():
        ss = None
        for jj in range(nj):
            y = rows[jj]
            part = jnp.sum(y * y, axis=-1, keepdims=True)
            ss = part if ss is None else ss + part
        inv = lax.rsqrt(ss / (nj * tn) + RMS_EPS)
        for jj in range(nj):
            o_ref[:, jj * tn:(jj + 1) * tn] = rows[jj] * inv * gain_ref[:, jj * tn:(jj + 1) * tn]


def _out_proj(y_m, y_a, x2d, w_out, gain):
    n, d = x2d.shape
    tm, tn = min(OUTPROJ_TM, n), OUTPROJ_TN
    nj = d // tn
    return pl.pallas_call(
        _outproj_kernel,
        grid=(n // tm, nj),
        in_specs=[
            pl.BlockSpec((tm, MLSTM_WIDTH), lambda i, j: (i, 0)),
            pl.BlockSpec((tm, NSA_WIDTH), lambda i, j: (i, 0)),
            pl.BlockSpec((tm, tn), lambda i, j: (i, j)),
            pl.BlockSpec((MLSTM_WIDTH, tn), lambda i, j: (0, j)),
            pl.BlockSpec((NSA_WIDTH, tn), lambda i, j: (MLSTM_WIDTH // NSA_WIDTH, j)),
            pl.BlockSpec((1, d), lambda i, j: (0, 0)),
        ],
        out_specs=pl.BlockSpec((tm, d), lambda i, j: (i, 0)),
        out_shape=jax.ShapeDtypeStruct((n, d), F32),
        scratch_shapes=[pltpu.VMEM((nj, tm, tn), F32)],
        compiler_params=_cparams(("parallel", "arbitrary")),
        name="out_proj",
    )(y_m, y_a, x2d, w_out, w_out, gain)


def _rel_bucket(dist):
    n = jnp.maximum(dist, 0)
    nf = jnp.maximum(n, REL_MAX_EXACT).astype(jnp.float32)
    large = REL_MAX_EXACT + (jnp.log(nf / REL_MAX_EXACT) / math.log(REL_MAX_DISTANCE / REL_MAX_EXACT)
                             * (REL_BUCKETS - REL_MAX_EXACT)).astype(jnp.int32)
    large = jnp.minimum(large, REL_BUCKETS - 1)
    return jnp.where(n < REL_MAX_EXACT, n, large)


def _toeplitz_vec(by_dist, base, n_pos, n_neg, lo, hi, shift=None):
    w = n_pos + n_neg
    c = np.arange(w)
    dist = np.where(c < n_pos, base + c, base - (w - c))
    ok = (dist >= lo) & (dist < hi)
    vals = by_dist[:, np.clip(dist, 0, by_dist.shape[1] - 1)]
    if shift is not None:
        vals = vals - shift
    return jnp.where(ok[None], vals * LOG2E, NEG_LOGIT)


def _toeplitz_t(w_row, n_keys, n_q, key_step):
    x = jnp.broadcast_to(w_row, (n_keys, w_row.shape[1]))
    return pltpu.roll(x, 0, 1, stride=key_step, stride_axis=0)[:, 0:n_q]


def _bias_tables_kernel(wc_ref, ws_ref, ww_ref, bc_ref, bs_ref, bw_ref):
    nb, t = bc_ref.shape[2], bc_ref.shape[3]
    tk, tq = bs_ref.shape[3], bs_ref.shape[4]
    bc_ref[0, 0] = _toeplitz_t(wc_ref[0], nb, t, CMP_STRIDE)
    for d in range(ws_ref.shape[1]):
        bs_ref[0, d, 0] = _toeplitz_t(ws_ref[0, d:d + 1, :], tk, tq, 1)
    n_w = ww_ref.shape[1]
    for d in range(n_w):
        bw_ref[0, d, 0] = _toeplitz_t(ww_ref[0, d:d + 1, :], tk, tq, 1)
    bw_ref[0, n_w, 0] = jnp.full((tk, tq), NEG_LOGIT, F32)


def _bias_tables(rel_bias, t):
    tq, tk = ATT_TQ, ATT_TK
    g_, hpg = NSA_KV_GROUPS, NSA_HPG
    rb = rel_bias.astype(F32)
    dmax = REL_MAX_DISTANCE + tq + tk
    by_dist = rb[_rel_bucket(jnp.arange(dmax, dtype=jnp.int32))].T
    far = rb[REL_BUCKETS - 1][:, None]
    big = 1 << 30
    nb = t // CMP_STRIDE
    wc = _toeplitz_vec(by_dist, -(CMP_BLOCK - 1), t, t, 0, big)[:, None, :]
    n_s = min(-(-(REL_MAX_DISTANCE + tk - 1) // tq), t // tq)
    ws = jnp.stack([_toeplitz_vec(by_dist, d * tq, tq, tk, 0, big, far) for d in range(n_s)], axis=1)
    n_w = min(-(-(WINDOW + tk - 1) // tq), t // tq)
    ww = jnp.stack([_toeplitz_vec(by_dist, d * tq, tq, tk, 0, WINDOW) for d in range(n_w)], axis=1)
    return pl.pallas_call(
        _bias_tables_kernel,
        grid=(NSA_HEADS,),
        in_specs=[
            pl.BlockSpec((1, 1, 2 * t), lambda h: (h, 0, 0)),
            pl.BlockSpec((1, n_s, tq + tk), lambda h: (h, 0, 0)),
            pl.BlockSpec((1, n_w, tq + tk), lambda h: (h, 0, 0)),
        ],
        out_specs=[
            pl.BlockSpec((1, 1, nb, t), lambda h: (h // hpg, h % hpg, 0, 0)),
            pl.BlockSpec((1, n_s, 1, tk, tq), lambda h: (h // hpg, 0, h % hpg, 0, 0)),
            pl.BlockSpec((1, n_w + 1, 1, tk, tq), lambda h: (h // hpg, 0, h % hpg, 0, 0)),
        ],
        out_shape=[
            jax.ShapeDtypeStruct((g_, hpg, nb, t), F32),
            jax.ShapeDtypeStruct((g_, n_s, hpg, tk, tq), F32),
            jax.ShapeDtypeStruct((g_, n_w + 1, hpg, tk, tq), F32),
        ],
        compiler_params=_cparams(("parallel",)),
        name="bias_tables",
    )(wc, ws, ww)


def _cover_t(t):
    nb = t // CMP_STRIDE
    n_cmp = (t - CMP_BLOCK) // CMP_STRIDE + 1
    n_slc = t // SLC_BLOCK
    cs = np.arange(nb) * CMP_STRIDE
    ss = np.arange(n_slc) * SLC_BLOCK
    cover = np.clip(np.minimum(cs[:, None] + CMP_BLOCK, ss[None, :] + SLC_BLOCK)
                    - np.maximum(cs[:, None], ss[None, :]), 0, None) / CMP_BLOCK
    cover[n_cmp:] = 0.0
    return jnp.asarray(cover.T, dtype=BF16)


def kernel(x, norm_gain, w_in, w_conv, b_igate, b_fgate, mlstm_norm_gain, cmp_k_pos, cmp_k_w1, cmp_k_w2,
           cmp_v_pos, cmp_v_w1, cmp_v_w2, rel_bias, w_out, final_norm_gain):
    bsz, t, d = x.shape
    assert d == D_MODEL and t % MLSTM_L == 0 and t % ATT_TQ == 0 and (t // CMP_STRIDE) % LANES == 0
    n = bsz * t
    x2d = x.reshape(n, d)

    o_i = COL_MZ + MLSTM_WIDTH
    o_aq = o_i + 2 * MLSTM_HEADS
    o_gate = o_aq + NSA_WIDTH + 6 * NSA_KV_WIDTH
    o_az = o_gate + 3 * NSA_HEADS
    w_main = jnp.concatenate([w_in[:, :o_i].astype(BF16), w_in[:, o_aq:o_gate].astype(BF16),
                              w_in[:, o_az:].astype(BF16)], axis=1)
    w_gate = jnp.concatenate([w_in[:, o_i:o_aq], w_in[:, o_gate:o_az],
                              jnp.zeros((d, GATE_COLS - 2 * MLSTM_HEADS - 3 * NSA_HEADS), w_in.dtype)], axis=1).astype(BF16)

    proj, gates = _in_proj(x2d, norm_gain.reshape(1, d).astype(F32), w_main, w_gate)

    g_rows = gates[:, :2 * MLSTM_HEADS].reshape(bsz, t, 2, MLSTM_HEADS).transpose(0, 3, 2, 1)
    y_m = _mlstm(proj, g_rows, w_conv.astype(F32), b_igate.astype(F32), b_fgate.astype(F32),
                 mlstm_norm_gain.reshape(1, MLSTM_WIDTH).astype(F32), bsz, t)

    dh = NSA_HEAD_DIM
    half = CMP_BLOCK // 2

    def w1cat(w1):
        return jnp.concatenate([w1[:half].reshape(half * dh, dh), w1[half:].reshape(half * dh, dh)], axis=1).astype(BF16)

    k_cmp, v_cmp_t = _compress(
        proj, w1cat(cmp_k_w1), cmp_k_w2.astype(BF16), cmp_k_pos.reshape(2, half * dh).astype(BF16),
        w1cat(cmp_v_w1), cmp_v_w2.astype(BF16), cmp_v_pos.reshape(2, half * dh).astype(BF16), bsz, t)

    bias_c, bias_s, bias_w = _bias_tables(rel_bias, t)
    o_cmp, selbias_t = _cmp_attn(proj, gates, k_cmp, v_cmp_t, bias_c, _cover_t(t), bsz, t)
    o_slc = _slc_attn(proj, gates, selbias_t, bias_s, bsz, t)
    y_a = _win_attn(proj, gates, o_cmp, o_slc, bias_w, bsz, t)

    out = _out_proj(y_m, y_a, x2d, w_out.astype(BF16), final_norm_gain.reshape(1, d).astype(F32))
    return out.reshape(bsz, t, d)
```

```python
import functools
import math

import jax
import jax.numpy as jnp
import numpy as np
from jax import lax
from jax.experimental import pallas as pl
from jax.experimental.pallas import tpu as pltpu

F32 = jnp.float32
BF16 = jnp.bfloat16

D_MODEL = 4096
D_MIX = D_MODEL
MLSTM_WIDTH = D_MIX // 2
MLSTM_HEADS = 4
MLSTM_V_DIM = MLSTM_WIDTH // MLSTM_HEADS
MLSTM_QK_DIM = MLSTM_V_DIM // 2
MLSTM_QK_WIDTH = MLSTM_HEADS * MLSTM_QK_DIM
CONV_WIDTH = 4
NSA_WIDTH = D_MIX - MLSTM_WIDTH
NSA_HEAD_DIM = 128
NSA_HEADS = NSA_WIDTH // NSA_HEAD_DIM
NSA_KV_GROUPS = 4
NSA_HPG = NSA_HEADS // NSA_KV_GROUPS
NSA_KV_WIDTH = NSA_KV_GROUPS * NSA_HEAD_DIM
CMP_BLOCK = 32
CMP_STRIDE = 16
SLC_BLOCK = 64
SLC_TOP_N = 16
WINDOW = 512
FORCE_BONUS = 1000.0
REL_BUCKETS = 32
REL_MAX_EXACT = REL_BUCKETS // 2
REL_MAX_DISTANCE = 1024
RMS_EPS = 1e-6
NEG_LOGIT = -1e30

LANES = 128
VMEM_LIMIT_BYTES = 56 * 1024 * 1024

COL_MQ = 0
COL_MK = COL_MQ + MLSTM_QK_WIDTH
COL_MV = COL_MK + MLSTM_QK_WIDTH
COL_MO = COL_MV + MLSTM_WIDTH
COL_MZ = COL_MO + MLSTM_WIDTH
COL_AQ = COL_MZ + MLSTM_WIDTH
COL_CK = COL_AQ + NSA_WIDTH
COL_CV = COL_CK + NSA_KV_WIDTH
COL_SK = COL_CV + NSA_KV_WIDTH
COL_SV = COL_SK + NSA_KV_WIDTH
COL_WK = COL_SV + NSA_KV_WIDTH
COL_WV = COL_WK + NSA_KV_WIDTH
COL_AZ = COL_WV + NSA_KV_WIDTH
MAIN_COLS = COL_AZ + NSA_WIDTH
GATE_COLS = LANES
GATE_I = 0
GATE_F = MLSTM_HEADS
GATE_NSA = 2 * MLSTM_HEADS

INPROJ_TM = 512
INPROJ_TN = 1024
OUTPROJ_TM = 512
OUTPROJ_TN = 512
MLSTM_L = 256
ATT_TQ = 256
ATT_TK = 256
SLC_TQ = 512
MASK_BIG = 2.0 ** 100


def _cparams(sem):
    return pltpu.CompilerParams(dimension_semantics=sem, vmem_limit_bytes=VMEM_LIMIT_BYTES)


def _sigmoid(x):
    return 1.0 / (1.0 + jnp.exp(-x))


def _silu(x):
    return x * _sigmoid(x)


def _dot(a, b):
    return jnp.dot(a, b, preferred_element_type=F32)


def _inproj_kernel(x_ref, gain_ref, w_ref, wg_ref, o_ref, og_ref, h_ref):
    @pl.when(pl.program_id(1) == 0)
    def _():
        x = x_ref[...]
        ms = jnp.mean(x * x, axis=-1, keepdims=True)
        h = (x * lax.rsqrt(ms + RMS_EPS) * gain_ref[...]).astype(BF16)
        h_ref[...] = h
        og_ref[...] = _dot(h, wg_ref[...])

    o_ref[...] = _dot(h_ref[...], w_ref[...]).astype(BF16)


def _in_proj(x2d, gain, w_main, w_gate):
    n, d = x2d.shape
    tm, tn = min(INPROJ_TM, n), INPROJ_TN
    return pl.pallas_call(
        _inproj_kernel,
        grid=(n // tm, MAIN_COLS // tn),
        in_specs=[
            pl.BlockSpec((tm, d), lambda i, j: (i, 0)),
            pl.BlockSpec((1, d), lambda i, j: (0, 0)),
            pl.BlockSpec((d, tn), lambda i, j: (0, j)),
            pl.BlockSpec((d, GATE_COLS), lambda i, j: (0, 0)),
        ],
        out_specs=[
            pl.BlockSpec((tm, tn), lambda i, j: (i, j)),
            pl.BlockSpec((tm, GATE_COLS), lambda i, j: (i, 0)),
        ],
        out_shape=[
            jax.ShapeDtypeStruct((n, MAIN_COLS), BF16),
            jax.ShapeDtypeStruct((n, GATE_COLS), F32),
        ],
        scratch_shapes=[pltpu.VMEM((tm, d), BF16)],
        compiler_params=_cparams(("parallel", "arbitrary")),
        name="in_proj",
    )(x2d, gain, w_main, w_gate)


def _mlstm_kernel(bi_ref, bf_ref, q_ref, k_ref, v_ref, o_ref, z_ref, g_ref, wq_ref, wk_ref, ng_ref,
                  y_ref, qext, kext, c_st, n_st, m_st):
    L = MLSTM_L
    HIST = 8
    hd = pl.program_id(1)

    @pl.when(pl.program_id(2) == 0)
    def _():
        qext[0:HIST, :] = jnp.zeros((HIST, MLSTM_QK_DIM), F32)
        kext[0:HIST, :] = jnp.zeros((HIST, MLSTM_QK_DIM), F32)
        c_st[...] = jnp.zeros_like(c_st)
        n_st[...] = jnp.zeros_like(n_st)
        m_st[...] = jnp.zeros_like(m_st)

    qext[HIST:HIST + L, :] = q_ref[...].astype(F32)
    kext[HIST:HIST + L, :] = k_ref[...].astype(F32)

    def conv_silu(ext, w_ref):
        w = w_ref[...]
        y = ext[pl.ds(HIST, L), :] * w[CONV_WIDTH - 1:CONV_WIDTH, :]
        for s in range(1, CONV_WIDTH):
            y = y + ext[pl.ds(HIST - s, L), :] * w[CONV_WIDTH - 1 - s:CONV_WIDTH - s, :]
        return _silu(y)

    qc = conv_silu(qext, wq_ref)
    kc = conv_silu(kext, wk_ref) * (MLSTM_QK_DIM ** -0.5)
    qext[0:HIST, :] = qext[L:L + HIST, :]
    kext[0:HIST, :] = kext[L:L + HIST, :]

    g = g_ref[0, 0]
    i_row = g[0:1, :] + bi_ref[hd]
    f_row = g[1:2, :] + bf_ref[hd]
    lf_row = jnp.minimum(f_row, 0.0) - jnp.log(1.0 + jnp.exp(-jnp.abs(f_row)))

    rr = lax.broadcasted_iota(jnp.int32, (L, L), 0)
    cc = lax.broadcasted_iota(jnp.int32, (L, L), 1)
    upper = (rr <= cc).astype(F32)
    bcum_row = jnp.dot(jnp.broadcast_to(lf_row, (8, L)), upper, preferred_element_type=F32,
                       precision=lax.Precision.HIGHEST)[0:1, :]
    bcum_col = jnp.sum(jnp.where(rr == cc, bcum_row, 0.0), axis=1, keepdims=True)
    gsum = bcum_row[:, L - 1:L]
    m_prev = m_st[...]

    dlog = jnp.where(rr >= cc, bcum_col - bcum_row + i_row, -jnp.inf)
    m_inter = bcum_col + m_prev
    m_t = jnp.maximum(m_inter, jnp.max(dlog, axis=1, keepdims=True))
    dmat = jnp.exp(dlog - m_t)
    inter = jnp.exp(m_inter - m_t)

    qb = qc.astype(BF16)
    kct = kc.T
    vb = v_ref[...]
    s = _dot(qb, kct.astype(BF16)) * dmat
    c_prev = c_st[...]
    n_prev = n_st[...]
    num = _dot(s.astype(BF16), vb) + inter * _dot(qb, c_prev.astype(BF16))
    qn = jnp.sum(s, axis=1, keepdims=True) + inter * jnp.sum(qc * n_prev, axis=1, keepdims=True)
    hh = num / jnp.maximum(jnp.abs(qn), jnp.exp(-m_t))

    wlog = gsum - bcum_row + i_row
    m_next = jnp.maximum(gsum + m_prev, jnp.max(wlog, axis=1, keepdims=True))
    wts = jnp.exp(wlog - m_next)
    keep = jnp.exp(gsum + m_prev - m_next)
    c_st[...] = keep * c_prev + _dot((kct * wts).astype(BF16), vb)
    n_st[...] = keep * n_prev + _dot(jnp.broadcast_to(wts, (8, L)).astype(BF16), kc.astype(BF16))[0:1, :]
    m_st[...] = m_next

    hm = _sigmoid(o_ref[...].astype(F32)) * hh
    hm = hm * lax.rsqrt(jnp.mean(hm * hm, axis=-1, keepdims=True) + RMS_EPS)
    hm = hm * ng_ref[...]
    y_ref[...] = (hm * _silu(z_ref[...].astype(F32))).astype(BF16)


def _mlstm(proj, g_rows, w_conv, b_igate, b_fgate, norm_gain, bsz, t):
    L = MLSTM_L
    nc = t // L
    dqk, dv = MLSTM_QK_DIM, MLSTM_V_DIM
    row = lambda b, h, c: b * nc + c
    smem = pl.BlockSpec(memory_space=pltpu.SMEM)
    return pl.pallas_call(
        _mlstm_kernel,
        grid=(bsz, MLSTM_HEADS, nc),
        in_specs=[
            smem, smem,
            pl.BlockSpec((L, dqk), lambda b, h, c: (row(b, h, c), COL_MQ // dqk + h)),
            pl.BlockSpec((L, dqk), lambda b, h, c: (row(b, h, c), COL_MK // dqk + h)),
            pl.BlockSpec((L, dv), lambda b, h, c: (row(b, h, c), COL_MV // dv + h)),
            pl.BlockSpec((L, dv), lambda b, h, c: (row(b, h, c), COL_MO // dv + h)),
            pl.BlockSpec((L, dv), lambda b, h, c: (row(b, h, c), COL_MZ // dv + h)),
            pl.BlockSpec((1, 1, 2, L), lambda b, h, c: (b, h, 0, c)),
            pl.BlockSpec((CONV_WIDTH, dqk), lambda b, h, c: (0, h)),
            pl.BlockSpec((CONV_WIDTH, dqk), lambda b, h, c: (0, MLSTM_HEADS + h)),
            pl.BlockSpec((1, dv), lambda b, h, c: (0, h)),
        ],
        out_specs=pl.BlockSpec((L, dv), lambda b, h, c: (row(b, h, c), h)),
        out_shape=jax.ShapeDtypeStruct((bsz * t, MLSTM_WIDTH), BF16),
        scratch_shapes=[
            pltpu.VMEM((L + 8, dqk), F32),
            pltpu.VMEM((L + 8, dqk), F32),
            pltpu.VMEM((dqk, dv), F32),
            pltpu.VMEM((1, dqk), F32),
            pltpu.VMEM((1, 1), F32),
        ],
        compiler_params=_cparams(("parallel", "parallel", "arbitrary")),
        name="mlstm",
    )(b_igate, b_fgate, proj, proj, proj, proj, proj, g_rows, w_conv, w_conv, norm_gain)


def _compress_kernel(ck_ref, cv_ref, w1k_ref, w2k_ref, pk_ref, w1v_ref, w2v_ref, pv_ref,
                     ok_ref, ov_ref, xf, xcat):
    t = ck_ref.shape[0]
    nb = t // CMP_STRIDE
    dh = NSA_HEAD_DIM

    def one(src_ref, w1_ref, w2_ref, pos_ref):
        xf[...] = src_ref[...].astype(F32)
        for l in range(CMP_STRIDE):
            xcat[:, l * dh:(l + 1) * dh] = xf[pl.ds(l, nb, stride=CMP_STRIDE), :].astype(BF16)
        w1 = w1_ref[...]
        ab = _dot(xcat[...], w1)
        pp = _dot(pos_ref[...], w1)
        pos_term = pp[0:1, 0:dh] + pp[1:2, dh:2 * dh]
        second = pltpu.roll(ab[:, dh:2 * dh], nb - 1, 0)
        hid = _silu(ab[:, 0:dh] + second + pos_term)
        return _dot(hid.astype(BF16), w2_ref[...])

    ok_ref[0, 0] = one(ck_ref, w1k_ref, w2k_ref, pk_ref).astype(BF16)
    ov_ref[0, 0] = one(cv_ref, w1v_ref, w2v_ref, pv_ref).T.astype(BF16)


def _compress(proj, w1k, w2k, pk, w1v, w2v, pv, bsz, t):
    g_, dh = NSA_KV_GROUPS, NSA_HEAD_DIM
    nb = t // CMP_STRIDE
    full = lambda a: pl.BlockSpec(a.shape, lambda b, g: (0,) * a.ndim)
    k_spec = pl.BlockSpec((1, 1, nb, dh), lambda b, g: (b, g, 0, 0))
    vt_spec = pl.BlockSpec((1, 1, dh, nb), lambda b, g: (b, g, 0, 0))
    return pl.pallas_call(
        _compress_kernel,
        grid=(bsz, g_),
        in_specs=[
            pl.BlockSpec((t, dh), lambda b, g: (b, COL_CK // dh + g)),
            pl.BlockSpec((t, dh), lambda b, g: (b, COL_CV // dh + g)),
            full(w1k), full(w2k), full(pk), full(w1v), full(w2v), full(pv),
        ],
        out_specs=[k_spec, vt_spec],
        out_shape=[jax.ShapeDtypeStruct((bsz, g_, nb, dh), BF16), jax.ShapeDtypeStruct((bsz, g_, dh, nb), BF16)],
        scratch_shapes=[pltpu.VMEM((t, dh), F32), pltpu.VMEM((nb, CMP_STRIDE * dh), BF16)],
        compiler_params=_cparams(("parallel", "parallel")),
        name="compress",
    )(proj, proj, w1k, w2k, pk, w1v, w2v, pv)


LOG2E = math.log2(math.e)
VROWS = NSA_HEAD_DIM + 16


def _dot_nt(a, b):
    return lax.dot_general(a, b, (((1,), (1,)), ((), ())), preferred_element_type=F32)


def _q_scaled(q_ref, hh):
    dh = NSA_HEAD_DIM
    return (q_ref[:, hh * dh:(hh + 1) * dh].astype(F32) * (dh ** -0.5 * LOG2E)).astype(BF16)


def _q_t(q_ref, hh):
    dh = NSA_HEAD_DIM
    return (q_ref[:, hh * dh:(hh + 1) * dh].astype(F32) * (dh ** -0.5 * LOG2E)).T.astype(BF16)


def _gate_row(gt_ref, g, hh, branch):
    row = GATE_NSA + 3 * (g * NSA_HPG + hh) + branch
    return _sigmoid(gt_ref[pl.ds(row, 1), :])


def _fill_vt(vt, v_ref):
    dh = NSA_HEAD_DIM
    vt[0:dh, :] = v_ref[...].astype(F32).T.astype(BF16)
    vt[dh:VROWS, :] = jnp.ones((VROWS - dh, vt.shape[1]), BF16)


def _out_t(acc, gate_row):
    dh = NSA_HEAD_DIM
    return acc[0:dh, :] * (gate_row / acc[dh:dh + 1, :])


def _cmp_attn_kernel(q_ref, kc_ref, vct_ref, bias_ref, cov_ref, gate_ref, oc_ref, sel_ref, score_ref, *, n_sel):
    tq = q_ref.shape[0]
    dh = NSA_HEAD_DIM
    n_slc = cov_ref.shape[0]
    t0 = pl.program_id(1) * tq
    g = pl.program_id(0)
    kc = kc_ref[0, 0]
    vct = vct_ref[0, 0]
    heads = range(NSA_HPG)
    qk = [_dot_nt(kc, _q_scaled(q_ref, hh)) for hh in heads]
    ps = []
    for hh in heads:
        logit = qk[hh] + bias_ref[0, hh]
        m = jnp.max(logit, axis=0, keepdims=True)
        e = jnp.exp2(logit - m)
        inv = jnp.where(m > 0.5 * NEG_LOGIT, 1.0 / jnp.sum(e, axis=0, keepdims=True), 0.0)
        ps.append(e * inv)
    p_sum = (ps[0] + ps[1]) + (ps[2] + ps[3])
    ots = [_dot(vct, p.astype(BF16)) for p in ps]
    for hh in heads:
        oc_ref[0, hh] = (ots[hh] * _gate_row(gate_ref, g, hh, 0)).astype(BF16)

    cov = cov_ref[...]
    p_hi = p_sum.astype(BF16)
    r1 = p_sum - p_hi.astype(F32)
    p_mid = r1.astype(BF16)
    p_lo = (r1 - p_mid.astype(F32)).astype(BF16)
    st = _dot(cov, p_hi) + _dot(cov, p_mid) + _dot(cov, p_lo)

    jb = lax.broadcasted_iota(jnp.int32, (n_slc, tq), 0)
    cur = (t0 + lax.broadcasted_iota(jnp.int32, (n_slc, tq), 1)) // SLC_BLOCK
    valid = jb <= cur
    forced = (jb == 0) | (jb == cur) | (jb == cur - 1)
    score = jnp.where(valid, st + jnp.where(forced, FORCE_BONUS, 0.0), -1.0)
    sub = 8
    score_ref[...] = score
    groups = [score_ref[r:r + sub, :] for r in range(0, n_slc, sub)]
    ranks = [jnp.zeros((sub, tq), F32) for _ in groups]
    row_id = lax.broadcasted_iota(jnp.int32, (sub, tq), 0)
    for j2 in range(n_slc):
        row = score_ref[j2:j2 + 1, :]
        for gi, sc in enumerate(groups):
            r0 = gi * sub
            if r0 > j2:
                inc = jnp.where(row >= sc, 1.0, 0.0)
            elif r0 + sub - 1 <= j2:
                inc = jnp.where(row > sc, 1.0, 0.0)
            else:
                inc = jnp.where(row_id > j2 - r0, jnp.where(row >= sc, 1.0, 0.0), jnp.where(row > sc, 1.0, 0.0))
            ranks[gi] = ranks[gi] + inc
    for gi, rk in enumerate(ranks):
        score_ref[gi * sub:(gi + 1) * sub, :] = rk
    sel = valid & (score_ref[...] < n_sel)
    sel_ref[0, 0, 0:n_slc, :] = jnp.where(sel, 0.0, -MASK_BIG).astype(BF16)
    if n_slc < LANES:
        sel_ref[0, 0, n_slc:LANES, :] = jnp.zeros((LANES - n_slc, tq), BF16)


def _cmp_attn(proj, gates, k_cmp, v_cmp_t, bias_c, cover_t, bsz, t):
    g_, dh, tq = NSA_KV_GROUPS, NSA_HEAD_DIM, ATT_TQ
    nt = t // tq
    nb = t // CMP_STRIDE
    n_slc = t // SLC_BLOCK
    gw = NSA_HPG * dh
    return pl.pallas_call(
        functools.partial(_cmp_attn_kernel, n_sel=min(SLC_TOP_N, n_slc)),
        grid=(g_, nt, bsz),
        in_specs=[
            pl.BlockSpec((tq, gw), lambda g, i, b: (b * nt + i, COL_AQ // gw + g)),
            pl.BlockSpec((1, 1, nb, dh), lambda g, i, b: (b, g, 0, 0)),
            pl.BlockSpec((1, 1, dh, nb), lambda g, i, b: (b, g, 0, 0)),
            pl.BlockSpec((1, NSA_HPG, nb, tq), lambda g, i, b: (g, 0, 0, i)),
            pl.BlockSpec((n_slc, nb), lambda g, i, b: (0, 0)),
            pl.BlockSpec((GATE_COLS, tq), lambda g, i, b: (0, b * nt + i)),
        ],
        out_specs=[
            pl.BlockSpec((1, NSA_HPG, dh, tq), lambda g, i, b: (b, g, 0, i)),
            pl.BlockSpec((1, 1, LANES, tq), lambda g, i, b: (b, g, 0, i)),
        ],
        out_shape=[
            jax.ShapeDtypeStruct((bsz, NSA_HEADS, dh, t), BF16),
            jax.ShapeDtypeStruct((bsz, g_, LANES, t), BF16),
        ],
        scratch_shapes=[pltpu.VMEM((n_slc, tq), F32)],
        compiler_params=_cparams(("parallel", "parallel", "parallel")),
        name="cmp_attn",
    )(proj, k_cmp, v_cmp_t, bias_c, cover_t, gates)


def _slc_attn_kernel(q_ref, sbt_ref, k_ref, v_ref, bias_ref, gate_ref, os_ref,
                     kaug, vt, qt, s_buf, p_buf, a_buf, m_ref, acc_ref, *, n_near):
    tq, tk, dh = SLC_TQ, ATT_TK, NSA_HEAD_DIM
    r = tq // tk
    t = k_ref.shape[0]
    g = pl.program_id(1)
    i = pl.program_id(2)
    heads = range(NSA_HPG)

    @pl.when(i == 0)
    def _():
        kaug[:, 0:dh] = k_ref[...]
        blk = lax.broadcasted_iota(jnp.int32, (t, LANES), 0) // SLC_BLOCK
        lane = lax.broadcasted_iota(jnp.int32, (t, LANES), 1)
        kaug[:, dh:dh + LANES] = jnp.where(blk == lane, 1.0, 0.0).astype(BF16)
        _fill_vt(vt, v_ref)

    sbt = sbt_ref[0, 0]
    for hh in heads:
        qt[hh, 0:dh, :] = _q_t(q_ref, hh)
        qt[hh, dh:dh + LANES, :] = sbt

    def ktile(j):
        return kaug[pl.ds(pl.multiple_of(j * tk, tk), tk), :]

    def vtile(j):
        return vt[:, pl.ds(pl.multiple_of(j * tk, tk), tk)]

    m_ref[...] = jnp.full(m_ref.shape, NEG_LOGIT, F32)
    acc_ref[...] = jnp.zeros_like(acc_ref)
    p_buf[...] = jnp.zeros_like(p_buf)
    a_buf[...] = jnp.ones_like(a_buf)
    k0 = ktile(0)
    for hh in heads:
        s_buf[hh] = _dot(k0, qt[hh])

    last = (i + 1) * r - 1

    def stage(j, bias_fn):
        v_prev = vtile(jnp.maximum(j - 1, 0))
        k_next = ktile(jnp.minimum(j + 1, last))
        pv = [_dot(v_prev, p_buf[hh]) for hh in heads]
        s_next = [_dot(k_next, qt[hh]) for hh in heads]
        for hh in heads:
            acc_ref[hh] = a_buf[hh] * acc_ref[hh] + pv[hh]
        for hh in heads:
            s = s_buf[hh]
            if bias_fn is not None:
                s = s + bias_fn(hh)
            m_prev = m_ref[hh]
            m_new = jnp.maximum(m_prev, jnp.max(s, axis=0, keepdims=True))
            a_buf[hh] = jnp.exp2(m_prev - m_new)
            p_buf[hh] = jnp.exp2(s - m_new).astype(BF16)
            m_ref[hh] = m_new
        for hh in heads:
            s_buf[hh] = s_next[hh]

    n_far = jnp.maximum(i * r + r - n_near, 0)

    def far_body(j, c):
        stage(j, None)
        return c

    lax.fori_loop(0, n_far, far_body, 0)

    def near_body(j, c):
        stage(j, lambda hh: bias_ref[0, i * r + (r - 1) - j, hh])
        return c

    lax.fori_loop(n_far, last + 1, near_body, 0)

    v_last = vtile(last)
    pv = [_dot(v_last, p_buf[hh]) for hh in heads]
    for hh in heads:
        acc = a_buf[hh] * acc_ref[hh] + pv[hh]
        os_ref[0, hh] = _out_t(acc, _gate_row(gate_ref, g, hh, 1)).astype(BF16)


def _slc_attn(proj, gates, selbias_t, bias_s, bsz, t):
    g_, dh, tq, tk = NSA_KV_GROUPS, NSA_HEAD_DIM, SLC_TQ, ATT_TK
    nt = t // tq
    gw = NSA_HPG * dh
    n_near = bias_s.shape[1]
    return pl.pallas_call(
        functools.partial(_slc_attn_kernel, n_near=n_near),
        grid=(bsz, g_, nt),
        in_specs=[
            pl.BlockSpec((tq, gw), lambda b, g, i: (b * nt + i, COL_AQ // gw + g)),
            pl.BlockSpec((1, 1, LANES, tq), lambda b, g, i: (b, g, 0, i)),
            pl.BlockSpec((t, dh), lambda b, g, i: (b, COL_SK // dh + g)),
            pl.BlockSpec((t, dh), lambda b, g, i: (b, COL_SV // dh + g)),
            pl.BlockSpec((1, n_near, NSA_HPG, tk, tq), lambda b, g, i: (g, 0, 0, 0, 0)),
            pl.BlockSpec((GATE_COLS, tq), lambda b, g, i: (0, b * nt + i)),
        ],
        out_specs=pl.BlockSpec((1, NSA_HPG, dh, tq), lambda b, g, i: (b, g, 0, i)),
        out_shape=jax.ShapeDtypeStruct((bsz, NSA_HEADS, dh, t), BF16),
        scratch_shapes=[
            pltpu.VMEM((t, dh + LANES), BF16),
            pltpu.VMEM((VROWS, t), BF16),
            pltpu.VMEM((NSA_HPG, dh + LANES, tq), BF16),
            pltpu.VMEM((NSA_HPG, tk, tq), F32),
            pltpu.VMEM((NSA_HPG, tk, tq), BF16),
            pltpu.VMEM((NSA_HPG, 1, tq), F32),
            pltpu.VMEM((NSA_HPG, 1, tq), F32),
            pltpu.VMEM((NSA_HPG, VROWS, tq), F32),
        ],
        compiler_params=_cparams(("parallel", "parallel", "arbitrary")),
        name="slc_attn",
    )(proj, selbias_t, proj, proj, bias_s, gates)


def _win_attn_kernel(q_ref, k_ref, v_ref, bias_ref, gate_ref, oc_ref, os_ref, z_ref, ya_ref, vt, *, n_near):
    tq, tk, dh = ATT_TQ, ATT_TK, NSA_HEAD_DIM
    g = pl.program_id(1)
    i = pl.program_id(2)

    @pl.when(i == 0)
    def _():
        _fill_vt(vt, v_ref)

    offs = [pl.multiple_of(jnp.maximum(i - d, 0) * tk, tk) for d in range(n_near)]
    d_eff = [jnp.where(i >= d, d, n_near) for d in range(n_near)]
    k_tiles = [k_ref[pl.ds(off, tk), :] for off in offs]
    v_tiles = [vt[:, pl.ds(off, tk)] for off in offs]
    qs = [_q_scaled(q_ref, hh) for hh in range(NSA_HPG)]
    ss = [[_dot_nt(k_tiles[d], qs[hh]) for d in range(n_near)] for hh in range(NSA_HPG)]
    ps = []
    for hh in range(NSA_HPG):
        s = [ss[hh][d] + bias_ref[0, d_eff[d], hh] for d in range(n_near)]
        m = functools.reduce(jnp.maximum, [jnp.max(x, axis=0, keepdims=True) for x in s])
        ps.append([jnp.exp2(x - m).astype(BF16) for x in s])
    pvs = [[_dot(v_tiles[d], ps[hh][d]) for d in range(n_near)] for hh in range(NSA_HPG)]
    for hh in range(NSA_HPG):
        cols = slice(hh * dh, (hh + 1) * dh)
        o = _out_t(functools.reduce(lambda x, y: x + y, pvs[hh]), _gate_row(gate_ref, g, hh, 2))
        o = o + oc_ref[0, hh].astype(F32) + os_ref[0, hh].astype(F32)
        ya_ref[:, cols] = (o.T * _silu(z_ref[:, cols].astype(F32))).astype(BF16)


def _win_attn(proj, gates, o_cmp, o_slc, bias_w, bsz, t):
    g_, dh, tq, tk = NSA_KV_GROUPS, NSA_HEAD_DIM, ATT_TQ, ATT_TK
    nt = t // tq
    gw = NSA_HPG * dh
    n_near = bias_w.shape[1] - 1
    return pl.pallas_call(
        functools.partial(_win_attn_kernel, n_near=n_near),
        grid=(bsz, g_, nt),
        in_specs=[
            pl.BlockSpec((tq, gw), lambda b, g, i: (b * nt + i, COL_AQ // gw + g)),
            pl.BlockSpec((t, dh), lambda b, g, i: (b, COL_WK // dh + g)),
            pl.BlockSpec((t, dh), lambda b, g, i: (b, COL_WV // dh + g)),
            pl.BlockSpec((1, n_near + 1, NSA_HPG, tk, tq), lambda b, g, i: (g, 0, 0, 0, 0)),
            pl.BlockSpec((GATE_COLS, tq), lambda b, g, i: (0, b * nt + i)),
            pl.BlockSpec((1, NSA_HPG, dh, tq), lambda b, g, i: (b, g, 0, i)),
            pl.BlockSpec((1, NSA_HPG, dh, tq), lambda b, g, i: (b, g, 0, i)),
            pl.BlockSpec((tq, gw), lambda b, g, i: (b * nt + i, COL_AZ // gw + g)),
        ],
        out_specs=pl.BlockSpec((tq, gw), lambda b, g, i: (b * nt + i, g)),
        out_shape=jax.ShapeDtypeStruct((bsz * t, NSA_WIDTH), BF16),
        scratch_shapes=[pltpu.VMEM((VROWS, t), BF16)],
        compiler_params=_cparams(("parallel", "parallel", "arbitrary")),
        name="win_attn",
    )(proj, proj, proj, bias_w, gates, o_cmp, o_slc, proj)


def _outproj_kernel(ym_ref, ya_ref, x_ref, w1_ref, w2_ref, gain_ref, o_ref, rows):
    j = pl.program_id(1)
    nj = rows.shape[0]
    tn = rows.shape[2]
    rows[j] = x_ref[...] + _dot(ym_ref[...], w1_ref[...]) + _dot(ya_ref[...], w2_ref[...])

    @pl.when(j == nj - 1)
    def _():
        ss = None
        for jj in range(nj):
            y = rows[jj]
            part = jnp.sum(y * y, axis=-1, keepdims=True)
            ss = part if ss is None else ss + part
        inv = lax.rsqrt(ss / (nj * tn) + RMS_EPS)
        for jj in range(nj):
            o_ref[:, jj * tn:(jj + 1) * tn] = rows[jj] * inv * gain_ref[:, jj * tn:(jj + 1) * tn]


def _out_proj(y_m, y_a, x2d, w_out, gain):
    n, d = x2d.shape
    tm, tn = min(OUTPROJ_TM, n), OUTPROJ_TN
    nj = d // tn
    return pl.pallas_call(
        _outproj_kernel,
        grid=(n // tm, nj),
        in_specs=[
            pl.BlockSpec((tm, MLSTM_WIDTH), lambda i, j: (i, 0)),
            pl.BlockSpec((tm, NSA_WIDTH), lambda i, j: (i, 0)),
            pl.BlockSpec((tm, tn), lambda i, j: (i, j)),
            pl.BlockSpec((MLSTM_WIDTH, tn), lambda i, j: (0, j)),
            pl.BlockSpec((NSA_WIDTH, tn), lambda i, j: (MLSTM_WIDTH // NSA_WIDTH, j)),
            pl.BlockSpec((1, d), lambda i, j: (0, 0)),
        ],
        out_specs=pl.BlockSpec((tm, d), lambda i, j: (i, 0)),
        out_shape=jax.ShapeDtypeStruct((n, d), F32),
        scratch_shapes=[pltpu.VMEM((nj, tm, tn), F32)],
        compiler_params=_cparams(("parallel", "arbitrary")),
        name="out_proj",
    )(y_m, y_a, x2d, w_out, w_out, gain)


def _rel_bucket(dist):
    n = jnp.maximum(dist, 0)
    nf = jnp.maximum(n, REL_MAX_EXACT).astype(jnp.float32)
    large = REL_MAX_EXACT + (jnp.log(nf / REL_MAX_EXACT) / math.log(REL_MAX_DISTANCE / REL_MAX_EXACT)
                             * (REL_BUCKETS - REL_MAX_EXACT)).astype(jnp.int32)
    large = jnp.minimum(large, REL_BUCKETS - 1)
    return jnp.where(n < REL_MAX_EXACT, n, large)


def _toeplitz_vec(by_dist, base, n_pos, n_neg, lo, hi, shift=None):
    w = n_pos + n_neg
    c = np.arange(w)
    dist = np.where(c < n_pos, base + c, base - (w - c))
    ok = (dist >= lo) & (dist < hi)
    vals = by_dist[:, np.clip(dist, 0, by_dist.shape[1] - 1)]
    if shift is not None:
        vals = vals - shift
    return jnp.where(ok[None], vals * LOG2E, NEG_LOGIT)


def _toeplitz_t(w_row, n_keys, n_q, key_step):
    x = jnp.broadcast_to(w_row, (n_keys, w_row.shape[1]))
    return pltpu.roll(x, 0, 1, stride=key_step, stride_axis=0)[:, 0:n_q]


def _bias_tables_kernel(wc_ref, ws_ref, ww_ref, bc_ref, bs_ref, bw_ref):
    nb, t = bc_ref.shape[2], bc_ref.shape[3]
    bc_ref[0, 0] = _toeplitz_t(wc_ref[0], nb, t, CMP_STRIDE)
    for d in range(ws_ref.shape[1]):
        bs_ref[0, d, 0] = _toeplitz_t(ws_ref[0, d:d + 1, :], bs_ref.shape[3], bs_ref.shape[4], 1)
    tk, tq = bw_ref.shape[3], bw_ref.shape[4]
    n_w = ww_ref.shape[1]
    for d in range(n_w):
        bw_ref[0, d, 0] = _toeplitz_t(ww_ref[0, d:d + 1, :], tk, tq, 1)
    bw_ref[0, n_w, 0] = jnp.full((tk, tq), NEG_LOGIT, F32)


def _bias_tables(rel_bias, t):
    tq, tk = ATT_TQ, ATT_TK
    g_, hpg = NSA_KV_GROUPS, NSA_HPG
    rb = rel_bias.astype(F32)
    dmax = REL_MAX_DISTANCE + tq + tk
    by_dist = rb[_rel_bucket(jnp.arange(dmax, dtype=jnp.int32))].T
    far = rb[REL_BUCKETS - 1][:, None]
    big = 1 << 30
    nb = t // CMP_STRIDE
    wc = _toeplitz_vec(by_dist, -(CMP_BLOCK - 1), t, t, 0, big)[:, None, :]
    sq = min(SLC_TQ, t)
    r = sq // tk
    n_s = min(-(-(REL_MAX_DISTANCE + tk - 1) // tk) + r - 1, t // tk)
    ws = jnp.stack([_toeplitz_vec(by_dist, (d - (r - 1)) * tk, sq, tk, 0, big, far) for d in range(n_s)], axis=1)
    n_w = min(-(-(WINDOW + tk - 1) // tq), t // tq)
    ww = jnp.stack([_toeplitz_vec(by_dist, d * tq, tq, tk, 0, WINDOW) for d in range(n_w)], axis=1)
    return pl.pallas_call(
        _bias_tables_kernel,
        grid=(NSA_HEADS,),
        in_specs=[
            pl.BlockSpec((1, 1, 2 * t), lambda h: (h, 0, 0)),
            pl.BlockSpec((1, n_s, sq + tk), lambda h: (h, 0, 0)),
            pl.BlockSpec((1, n_w, tq + tk), lambda h: (h, 0, 0)),
        ],
        out_specs=[
            pl.BlockSpec((1, 1, nb, t), lambda h: (h // hpg, h % hpg, 0, 0)),
            pl.BlockSpec((1, n_s, 1, tk, sq), lambda h: (h // hpg, 0, h % hpg, 0, 0)),
            pl.BlockSpec((1, n_w + 1, 1, tk, tq), lambda h: (h // hpg, 0, h % hpg, 0, 0)),
        ],
        out_shape=[
            jax.ShapeDtypeStruct((g_, hpg, nb, t), F32),
            jax.ShapeDtypeStruct((g_, n_s, hpg, tk, sq), F32),
            jax.ShapeDtypeStruct((g_, n_w + 1, hpg, tk, tq), F32),
        ],
        compiler_params=_cparams(("parallel",)),
        name="bias_tables",
    )(wc, ws, ww)


def _cover_t(t):
    nb = t // CMP_STRIDE
    n_cmp = (t - CMP_BLOCK) // CMP_STRIDE + 1
    n_slc = t // SLC_BLOCK
    cs = np.arange(nb) * CMP_STRIDE
    ss = np.arange(n_slc) * SLC_BLOCK
    cover = np.clip(np.minimum(cs[:, None] + CMP_BLOCK, ss[None, :] + SLC_BLOCK)
                    - np.maximum(cs[:, None], ss[None, :]), 0, None) / CMP_BLOCK
    cover[n_cmp:] = 0.0
    return jnp.asarray(cover.T, dtype=BF16)


def kernel(x, norm_gain, w_in, w_conv, b_igate, b_fgate, mlstm_norm_gain, cmp_k_pos, cmp_k_w1, cmp_k_w2,
           cmp_v_pos, cmp_v_w1, cmp_v_w2, rel_bias, w_out, final_norm_gain):
    bsz, t, d = x.shape
    assert d == D_MODEL and t % MLSTM_L == 0 and t % ATT_TQ == 0 and (t // CMP_STRIDE) % LANES == 0
    n = bsz * t
    x2d = x.reshape(n, d)

    o_i = COL_MZ + MLSTM_WIDTH
    o_aq = o_i + 2 * MLSTM_HEADS
    o_gate = o_aq + NSA_WIDTH + 6 * NSA_KV_WIDTH
    o_az = o_gate + 3 * NSA_HEADS
    w_bf = lax.optimization_barrier(w_in.astype(BF16))
    w_main = jnp.concatenate([w_bf[:, :o_i], w_bf[:, o_aq:o_gate], w_bf[:, o_az:]], axis=1)
    w_gate = jnp.concatenate([w_in[:, o_i:o_aq], w_in[:, o_gate:o_az],
                              jnp.zeros((d, GATE_COLS - 2 * MLSTM_HEADS - 3 * NSA_HEADS), w_in.dtype)], axis=1).astype(BF16)

    proj, gates = _in_proj(x2d, norm_gain.reshape(1, d).astype(F32), w_main, w_gate)

    g_rows = gates[:, :2 * MLSTM_HEADS].reshape(bsz, t, 2, MLSTM_HEADS).transpose(0, 3, 2, 1)
    y_m = _mlstm(proj, g_rows, w_conv.astype(F32), b_igate.astype(F32), b_fgate.astype(F32),
                 mlstm_norm_gain.reshape(1, MLSTM_WIDTH).astype(F32), bsz, t)

    dh = NSA_HEAD_DIM
    half = CMP_BLOCK // 2

    def w1cat(w1):
        return jnp.concatenate([w1[:half].reshape(half * dh, dh), w1[half:].reshape(half * dh, dh)], axis=1).astype(BF16)

    k_cmp, v_cmp_t = _compress(
        proj, w1cat(cmp_k_w1), cmp_k_w2.astype(BF16), cmp_k_pos.reshape(2, half * dh).astype(BF16),
        w1cat(cmp_v_w1), cmp_v_w2.astype(BF16), cmp_v_pos.reshape(2, half * dh).astype(BF16), bsz, t)

    bias_c, bias_s, bias_w = _bias_tables(rel_bias, t)
    gates_t = gates.T
    o_cmp, selbias_t = _cmp_attn(proj, gates_t, k_cmp, v_cmp_t, bias_c, _cover_t(t), bsz, t)
    o_slc = _slc_attn(proj, gates_t, selbias_t, bias_s, bsz, t)
    y_a = _win_attn(proj, gates_t, o_cmp, o_slc, bias_w, bsz, t)

    out = _out_proj(y_m, y_a, x2d, w_out.astype(BF16), final_norm_gain.reshape(1, d).astype(F32))
    return out.reshape(bsz, t, d)
```

```python
import functools
import math

import jax
import jax.numpy as jnp
import numpy as np
from jax import lax
from jax.experimental import pallas as pl
from jax.experimental.pallas import tpu as pltpu

F32 = jnp.float32
BF16 = jnp.bfloat16

D_MODEL = 4096
D_MIX = D_MODEL
MLSTM_WIDTH = D_MIX // 2
MLSTM_HEADS = 4
MLSTM_V_DIM = MLSTM_WIDTH // MLSTM_HEADS
MLSTM_QK_DIM = MLSTM_V_DIM // 2
MLSTM_QK_WIDTH = MLSTM_HEADS * MLSTM_QK_DIM
CONV_WIDTH = 4
NSA_WIDTH = D_MIX - MLSTM_WIDTH
NSA_HEAD_DIM = 128
NSA_HEADS = NSA_WIDTH // NSA_HEAD_DIM
NSA_KV_GROUPS = 4
NSA_HPG = NSA_HEADS // NSA_KV_GROUPS
NSA_KV_WIDTH = NSA_KV_GROUPS * NSA_HEAD_DIM
CMP_BLOCK = 32
CMP_STRIDE = 16
SLC_BLOCK = 64
SLC_TOP_N = 16
WINDOW = 512
FORCE_BONUS = 1000.0
REL_BUCKETS = 32
REL_MAX_EXACT = REL_BUCKETS // 2
REL_MAX_DISTANCE = 1024
RMS_EPS = 1e-6
NEG_LOGIT = -1e30

LANES = 128
VMEM_LIMIT_BYTES = 56 * 1024 * 1024

COL_MQ = 0
COL_MK = COL_MQ + MLSTM_QK_WIDTH
COL_MV = COL_MK + MLSTM_QK_WIDTH
COL_MO = COL_MV + MLSTM_WIDTH
COL_MZ = COL_MO + MLSTM_WIDTH
COL_AQ = COL_MZ + MLSTM_WIDTH
COL_CK = COL_AQ + NSA_WIDTH
COL_CV = COL_CK + NSA_KV_WIDTH
COL_SK = COL_CV + NSA_KV_WIDTH
COL_SV = COL_SK + NSA_KV_WIDTH
COL_WK = COL_SV + NSA_KV_WIDTH
COL_WV = COL_WK + NSA_KV_WIDTH
COL_AZ = COL_WV + NSA_KV_WIDTH
MAIN_COLS = COL_AZ + NSA_WIDTH
GATE_COLS = LANES
GATE_I = 0
GATE_F = MLSTM_HEADS
GATE_NSA = 2 * MLSTM_HEADS

INPROJ_TM = 512
INPROJ_TN = 1024
OUTPROJ_TM = 1024
OUTPROJ_TN = 512
MLSTM_L = 256
ATT_TQ = 256
ATT_TK = 256
WPREP_TR = 256
SLC_TQ = 1024
MASK_BIG = 2.0 ** 100


def _cparams(sem):
    return pltpu.CompilerParams(dimension_semantics=sem, vmem_limit_bytes=VMEM_LIMIT_BYTES)


def _sigmoid(x):
    return 0.5 * jnp.tanh(0.5 * x) + 0.5


def _silu(x):
    return x * _sigmoid(x)


def _dot(a, b):
    return jnp.dot(a, b, preferred_element_type=F32)


W_OFF_I = COL_MZ + MLSTM_WIDTH
W_OFF_AQ = W_OFF_I + 2 * MLSTM_HEADS
W_OFF_GATE = W_OFF_AQ + NSA_WIDTH + 6 * NSA_KV_WIDTH
W_OFF_AZ = W_OFF_GATE + 3 * NSA_HEADS


def _wprep_kernel(w_ref, o_ref, g_ref):
    n_mid = W_OFF_GATE - W_OFF_AQ
    o_ref[:, 0:W_OFF_I] = w_ref[:, 0:W_OFF_I].astype(BF16)
    o_ref[:, W_OFF_I:W_OFF_I + n_mid] = w_ref[:, W_OFF_AQ:W_OFF_GATE].astype(BF16)
    o_ref[:, W_OFF_I + n_mid:MAIN_COLS] = w_ref[:, W_OFF_AZ:W_OFF_AZ + NSA_WIDTH].astype(BF16)
    n_if = W_OFF_AQ - W_OFF_I
    n_ag = W_OFF_AZ - W_OFF_GATE
    g_ref[...] = jnp.zeros_like(g_ref)
    g_ref[:, 0:n_if] = w_ref[:, W_OFF_I:W_OFF_AQ].astype(BF16)
    g_ref[:, n_if:n_if + n_ag] = w_ref[:, W_OFF_GATE:W_OFF_AZ].astype(BF16)


def _w_prep(w_in):
    d, cols = w_in.shape
    tr = WPREP_TR
    return pl.pallas_call(
        _wprep_kernel,
        grid=(d // tr,),
        in_specs=[pl.BlockSpec((tr, cols), lambda i: (i, 0))],
        out_specs=[pl.BlockSpec((tr, MAIN_COLS), lambda i: (i, 0)), pl.BlockSpec((tr, GATE_COLS), lambda i: (i, 0))],
        out_shape=[jax.ShapeDtypeStruct((d, MAIN_COLS), BF16), jax.ShapeDtypeStruct((d, GATE_COLS), BF16)],
        compiler_params=_cparams(("parallel",)),
        name="w_prep",
    )(w_in)


def _inproj_kernel(x_ref, gain_ref, w_ref, wg_ref, o_ref, og_ref, h_ref):
    @pl.when(pl.program_id(1) == 0)
    def _():
        x = x_ref[...]
        ms = jnp.mean(x * x, axis=-1, keepdims=True)
        h = (x * lax.rsqrt(ms + RMS_EPS) * gain_ref[...]).astype(BF16)
        h_ref[...] = h
        og_ref[...] = _dot(h, wg_ref[...])

    o_ref[...] = _dot(h_ref[...], w_ref[...]).astype(BF16)


def _in_proj(x2d, gain, w_main, w_gate):
    n, d = x2d.shape
    tm, tn = min(INPROJ_TM, n), INPROJ_TN
    return pl.pallas_call(
        _inproj_kernel,
        grid=(n // tm, MAIN_COLS // tn),
        in_specs=[
            pl.BlockSpec((tm, d), lambda i, j: (i, 0)),
            pl.BlockSpec((1, d), lambda i, j: (0, 0)),
            pl.BlockSpec((d, tn), lambda i, j: (0, j)),
            pl.BlockSpec((d, GATE_COLS), lambda i, j: (0, 0)),
        ],
        out_specs=[
            pl.BlockSpec((tm, tn), lambda i, j: (i, j)),
            pl.BlockSpec((tm, GATE_COLS), lambda i, j: (i, 0)),
        ],
        out_shape=[
            jax.ShapeDtypeStruct((n, MAIN_COLS), BF16),
            jax.ShapeDtypeStruct((n, GATE_COLS), F32),
        ],
        scratch_shapes=[pltpu.VMEM((tm, d), BF16)],
        compiler_params=_cparams(("parallel", "arbitrary")),
        name="in_proj",
    )(x2d, gain, w_main, w_gate)


def _mlstm_kernel(bi_ref, bf_ref, q_ref, k_ref, v_ref, o_ref, z_ref, g_ref, wq_ref, wk_ref, ng_ref,
                  y_ref, qext, kext, c_st, n_st, m_st):
    L = MLSTM_L
    HIST = 8
    hd = pl.program_id(1)

    @pl.when(pl.program_id(2) == 0)
    def _():
        qext[0:HIST, :] = jnp.zeros((HIST, MLSTM_QK_DIM), F32)
        kext[0:HIST, :] = jnp.zeros((HIST, MLSTM_QK_DIM), F32)
        c_st[...] = jnp.zeros_like(c_st)
        n_st[...] = jnp.zeros_like(n_st)
        m_st[...] = jnp.zeros_like(m_st)

    qext[HIST:HIST + L, :] = q_ref[...].astype(F32)
    kext[HIST:HIST + L, :] = k_ref[...].astype(F32)

    def conv_silu(ext, w_ref):
        w = w_ref[...]
        y = ext[pl.ds(HIST, L), :] * w[CONV_WIDTH - 1:CONV_WIDTH, :]
        for s in range(1, CONV_WIDTH):
            y = y + ext[pl.ds(HIST - s, L), :] * w[CONV_WIDTH - 1 - s:CONV_WIDTH - s, :]
        return _silu(y)

    qc = conv_silu(qext, wq_ref)
    kc = conv_silu(kext, wk_ref) * (MLSTM_QK_DIM ** -0.5)
    qext[0:HIST, :] = qext[L:L + HIST, :]
    kext[0:HIST, :] = kext[L:L + HIST, :]

    g = g_ref[0, 0]
    i_row = g[0:1, :] + bi_ref[hd]
    f_row = g[1:2, :] + bf_ref[hd]
    lf_row = jnp.minimum(f_row, 0.0) - jnp.log(1.0 + jnp.exp(-jnp.abs(f_row)))

    rr = lax.broadcasted_iota(jnp.int32, (L, L), 0)
    cc = lax.broadcasted_iota(jnp.int32, (L, L), 1)
    upper = (rr <= cc).astype(F32)
    bcum_row = jnp.dot(jnp.broadcast_to(lf_row, (8, L)), upper, preferred_element_type=F32,
                       precision=lax.Precision.HIGHEST)[0:1, :]
    bcum_col = jnp.sum(jnp.where(rr == cc, bcum_row, 0.0), axis=1, keepdims=True)
    gsum = bcum_row[:, L - 1:L]
    m_prev = m_st[...]

    dlog = jnp.where(rr >= cc, bcum_col - bcum_row + i_row, -jnp.inf)
    m_inter = bcum_col + m_prev
    m_t = jnp.maximum(m_inter, jnp.max(dlog, axis=1, keepdims=True))
    dmat = jnp.exp(dlog - m_t)
    inter = jnp.exp(m_inter - m_t)

    qb = qc.astype(BF16)
    kct = kc.T
    vb = v_ref[...]
    s = _dot(qb, kct.astype(BF16)) * dmat
    c_prev = c_st[...]
    n_prev = n_st[...]
    num = _dot(s.astype(BF16), vb) + inter * _dot(qb, c_prev.astype(BF16))
    qn = jnp.sum(s, axis=1, keepdims=True) + inter * jnp.sum(qc * n_prev, axis=1, keepdims=True)
    hh = num / jnp.maximum(jnp.abs(qn), jnp.exp(-m_t))

    wlog = gsum - bcum_row + i_row
    m_next = jnp.maximum(gsum + m_prev, jnp.max(wlog, axis=1, keepdims=True))
    wts = jnp.exp(wlog - m_next)
    keep = jnp.exp(gsum + m_prev - m_next)
    c_st[...] = keep * c_prev + _dot((kct * wts).astype(BF16), vb)
    n_st[...] = keep * n_prev + _dot(jnp.broadcast_to(wts, (8, L)).astype(BF16), kc.astype(BF16))[0:1, :]
    m_st[...] = m_next

    hm = _sigmoid(o_ref[...].astype(F32)) * hh
    hm = hm * lax.rsqrt(jnp.mean(hm * hm, axis=-1, keepdims=True) + RMS_EPS)
    hm = hm * ng_ref[...]
    y_ref[...] = (hm * _silu(z_ref[...].astype(F32))).astype(BF16)


def _mlstm(proj, g_rows, w_conv, b_igate, b_fgate, norm_gain, bsz, t):
    L = MLSTM_L
    nc = t // L
    dqk, dv = MLSTM_QK_DIM, MLSTM_V_DIM
    row = lambda b, h, c: b * nc + c
    smem = pl.BlockSpec(memory_space=pltpu.SMEM)
    return pl.pallas_call(
        _mlstm_kernel,
        grid=(bsz, MLSTM_HEADS, nc),
        in_specs=[
            smem, smem,
            pl.BlockSpec((L, dqk), lambda b, h, c: (row(b, h, c), COL_MQ // dqk + h)),
            pl.BlockSpec((L, dqk), lambda b, h, c: (row(b, h, c), COL_MK // dqk + h)),
            pl.BlockSpec((L, dv), lambda b, h, c: (row(b, h, c), COL_MV // dv + h)),
            pl.BlockSpec((L, dv), lambda b, h, c: (row(b, h, c), COL_MO // dv + h)),
            pl.BlockSpec((L, dv), lambda b, h, c: (row(b, h, c), COL_MZ // dv + h)),
            pl.BlockSpec((1, 1, 2, L), lambda b, h, c: (b, h, 0, c)),
            pl.BlockSpec((CONV_WIDTH, dqk), lambda b, h, c: (0, h)),
            pl.BlockSpec((CONV_WIDTH, dqk), lambda b, h, c: (0, MLSTM_HEADS + h)),
            pl.BlockSpec((1, dv), lambda b, h, c: (0, h)),
        ],
        out_specs=pl.BlockSpec((L, dv), lambda b, h, c: (row(b, h, c), h)),
        out_shape=jax.ShapeDtypeStruct((bsz * t, MLSTM_WIDTH), BF16),
        scratch_shapes=[
            pltpu.VMEM((L + 8, dqk), F32),
            pltpu.VMEM((L + 8, dqk), F32),
            pltpu.VMEM((dqk, dv), F32),
            pltpu.VMEM((1, dqk), F32),
            pltpu.VMEM((1, 1), F32),
        ],
        compiler_params=_cparams(("parallel", "parallel", "arbitrary")),
        name="mlstm",
    )(b_igate, b_fgate, proj, proj, proj, proj, proj, g_rows, w_conv, w_conv, norm_gain)


def _compress_kernel(ck_ref, cv_ref, w1k_ref, w2k_ref, pk_ref, w1v_ref, w2v_ref, pv_ref,
                     ok_ref, ov_ref, xf, xcat):
    t = ck_ref.shape[0]
    nb = t // CMP_STRIDE
    dh = NSA_HEAD_DIM

    def one(src_ref, w1_ref, w2_ref, pos_ref):
        xf[...] = src_ref[...].astype(F32)
        for l in range(CMP_STRIDE):
            xcat[:, l * dh:(l + 1) * dh] = xf[pl.ds(l, nb, stride=CMP_STRIDE), :].astype(BF16)
        w1 = w1_ref[...]
        ab = _dot(xcat[...], w1)
        pp = _dot(pos_ref[...], w1)
        pos_term = pp[0:1, 0:dh] + pp[1:2, dh:2 * dh]
        second = pltpu.roll(ab[:, dh:2 * dh], nb - 1, 0)
        hid = _silu(ab[:, 0:dh] + second + pos_term)
        return _dot(hid.astype(BF16), w2_ref[...])

    ok_ref[0, 0] = one(ck_ref, w1k_ref, w2k_ref, pk_ref).astype(BF16)
    ov_ref[0, 0] = one(cv_ref, w1v_ref, w2v_ref, pv_ref).T.astype(BF16)


def _compress(proj, w1k, w2k, pk, w1v, w2v, pv, bsz, t):
    g_, dh = NSA_KV_GROUPS, NSA_HEAD_DIM
    nb = t // CMP_STRIDE
    full = lambda a: pl.BlockSpec(a.shape, lambda b, g: (0,) * a.ndim)
    k_spec = pl.BlockSpec((1, 1, nb, dh), lambda b, g: (b, g, 0, 0))
    vt_spec = pl.BlockSpec((1, 1, dh, nb), lambda b, g: (b, g, 0, 0))
    return pl.pallas_call(
        _compress_kernel,
        grid=(bsz, g_),
        in_specs=[
            pl.BlockSpec((t, dh), lambda b, g: (b, COL_CK // dh + g)),
            pl.BlockSpec((t, dh), lambda b, g: (b, COL_CV // dh + g)),
            full(w1k), full(w2k), full(pk), full(w1v), full(w2v), full(pv),
        ],
        out_specs=[k_spec, vt_spec],
        out_shape=[jax.ShapeDtypeStruct((bsz, g_, nb, dh), BF16), jax.ShapeDtypeStruct((bsz, g_, dh, nb), BF16)],
        scratch_shapes=[pltpu.VMEM((t, dh), F32), pltpu.VMEM((nb, CMP_STRIDE * dh), BF16)],
        compiler_params=_cparams(("parallel", "parallel")),
        name="compress",
    )(proj, proj, w1k, w2k, pk, w1v, w2v, pv)


LOG2E = math.log2(math.e)
VROWS = NSA_HEAD_DIM + 16


def _dot_nt(a, b):
    return lax.dot_general(a, b, (((1,), (1,)), ((), ())), preferred_element_type=F32)


def _q_scaled(q_ref, hh):
    dh = NSA_HEAD_DIM
    return (q_ref[:, hh * dh:(hh + 1) * dh].astype(F32) * (dh ** -0.5 * LOG2E)).astype(BF16)


def _q_t(q_ref, hh):
    dh = NSA_HEAD_DIM
    return (q_ref[:, hh * dh:(hh + 1) * dh].astype(F32) * (dh ** -0.5 * LOG2E)).T.astype(BF16)


def _gate_row(gt_ref, g, hh, branch):
    row = GATE_NSA + 3 * (g * NSA_HPG + hh) + branch
    return _sigmoid(gt_ref[pl.ds(row, 1), :])


def _fill_vt(vt, v_ref):
    dh = NSA_HEAD_DIM
    vt[0:dh, :] = v_ref[...].astype(F32).T.astype(BF16)
    vt[dh:VROWS, :] = jnp.ones((VROWS - dh, vt.shape[1]), BF16)


def _out_t(acc, gate_row):
    dh = NSA_HEAD_DIM
    return acc[0:dh, :] * (gate_row / acc[dh:dh + 1, :])


def _cmp_attn_kernel(q_ref, kc_ref, vct_ref, bias_ref, cov_ref, gate_ref, oc_ref, sel_ref, score_ref, *, n_sel):
    tq = q_ref.shape[0]
    dh = NSA_HEAD_DIM
    n_slc = cov_ref.shape[0]
    t0 = pl.program_id(1) * tq
    g = pl.program_id(0)
    kc = kc_ref[0, 0]
    vct = vct_ref[0, 0]
    heads = range(NSA_HPG)
    qk = [_dot_nt(kc, _q_scaled(q_ref, hh)) for hh in heads]
    ps = []
    for hh in heads:
        logit = qk[hh] + bias_ref[0, hh]
        m = jnp.max(logit, axis=0, keepdims=True)
        e = jnp.exp2(logit - m)
        inv = jnp.where(m > 0.5 * NEG_LOGIT, 1.0 / jnp.sum(e, axis=0, keepdims=True), 0.0)
        ps.append(e * inv)
    p_sum = (ps[0] + ps[1]) + (ps[2] + ps[3])
    ots = [_dot(vct, p.astype(BF16)) for p in ps]
    for hh in heads:
        oc_ref[0, hh] = (ots[hh] * _gate_row(gate_ref, g, hh, 0)).astype(BF16)

    cov = cov_ref[...]
    p_hi = p_sum.astype(BF16)
    r1 = p_sum - p_hi.astype(F32)
    p_mid = r1.astype(BF16)
    p_lo = (r1 - p_mid.astype(F32)).astype(BF16)
    st = _dot(cov, p_hi) + _dot(cov, p_mid) + _dot(cov, p_lo)

    jb = lax.broadcasted_iota(jnp.int32, (n_slc, tq), 0)
    cur = (t0 + lax.broadcasted_iota(jnp.int32, (n_slc, tq), 1)) // SLC_BLOCK
    valid = jb <= cur
    forced = (jb == 0) | (jb == cur) | (jb == cur - 1)
    score = jnp.where(valid, st + jnp.where(forced, FORCE_BONUS, 0.0), -1.0)
    sub = 8
    score_ref[...] = score
    groups = [score_ref[r:r + sub, :] for r in range(0, n_slc, sub)]
    ranks = [jnp.zeros((sub, tq), F32) for _ in groups]
    row_id = lax.broadcasted_iota(jnp.int32, (sub, tq), 0)
    for j2 in range(n_slc):
        row = score_ref[j2:j2 + 1, :]
        for gi, sc in enumerate(groups):
            r0 = gi * sub
            if r0 > j2:
                inc = jnp.where(row >= sc, 1.0, 0.0)
            elif r0 + sub - 1 <= j2:
                inc = jnp.where(row > sc, 1.0, 0.0)
            else:
                inc = jnp.where(row_id > j2 - r0, jnp.where(row >= sc, 1.0, 0.0), jnp.where(row > sc, 1.0, 0.0))
            ranks[gi] = ranks[gi] + inc
    for gi, rk in enumerate(ranks):
        score_ref[gi * sub:(gi + 1) * sub, :] = rk
    sel = valid & (score_ref[...] < n_sel)
    sel_ref[0, 0, 0:n_slc, :] = jnp.where(sel, 0.0, -MASK_BIG).astype(BF16)
    if n_slc < LANES:
        sel_ref[0, 0, n_slc:LANES, :] = jnp.zeros((LANES - n_slc, tq), BF16)


def _cmp_attn(proj, gates, k_cmp, v_cmp_t, bias_c, cover_t, bsz, t):
    g_, dh, tq = NSA_KV_GROUPS, NSA_HEAD_DIM, ATT_TQ
    nt = t // tq
    nb = t // CMP_STRIDE
    n_slc = t // SLC_BLOCK
    gw = NSA_HPG * dh
    return pl.pallas_call(
        functools.partial(_cmp_attn_kernel, n_sel=min(SLC_TOP_N, n_slc)),
        grid=(g_, nt, bsz),
        in_specs=[
            pl.BlockSpec((tq, gw), lambda g, i, b: (b * nt + i, COL_AQ // gw + g)),
            pl.BlockSpec((1, 1, nb, dh), lambda g, i, b: (b, g, 0, 0)),
            pl.BlockSpec((1, 1, dh, nb), lambda g, i, b: (b, g, 0, 0)),
            pl.BlockSpec((1, NSA_HPG, nb, tq), lambda g, i, b: (g, 0, 0, i)),
            pl.BlockSpec((n_slc, nb), lambda g, i, b: (0, 0)),
            pl.BlockSpec((GATE_COLS, tq), lambda g, i, b: (0, b * nt + i)),
        ],
        out_specs=[
            pl.BlockSpec((1, NSA_HPG, dh, tq), lambda g, i, b: (b, g, 0, i)),
            pl.BlockSpec((1, 1, LANES, tq), lambda g, i, b: (b, g, 0, i)),
        ],
        out_shape=[
            jax.ShapeDtypeStruct((bsz, NSA_HEADS, dh, t), BF16),
            jax.ShapeDtypeStruct((bsz, g_, LANES, t), BF16),
        ],
        scratch_shapes=[pltpu.VMEM((n_slc, tq), F32)],
        compiler_params=_cparams(("parallel", "parallel", "parallel")),
        name="cmp_attn",
    )(proj, k_cmp, v_cmp_t, bias_c, cover_t, gates)


def _slc_attn_kernel(q_ref, sbt_ref, k_ref, v_ref, bias_ref, gate_ref, os_ref,
                     kaug, vt, qt, s_buf, p_buf, a_buf, m_ref, acc_ref, *, n_near):
    tq, tk, dh = SLC_TQ, ATT_TK, NSA_HEAD_DIM
    r = tq // tk
    t = k_ref.shape[0]
    g = pl.program_id(1)
    i = pl.program_id(2)
    heads = range(NSA_HPG)

    @pl.when(i == 0)
    def _():
        kaug[:, 0:dh] = k_ref[...]
        blk = lax.broadcasted_iota(jnp.int32, (t, LANES), 0) // SLC_BLOCK
        lane = lax.broadcasted_iota(jnp.int32, (t, LANES), 1)
        kaug[:, dh:dh + LANES] = jnp.where(blk == lane, 1.0, 0.0).astype(BF16)
        _fill_vt(vt, v_ref)

    sbt = sbt_ref[0, 0]
    for hh in heads:
        qt[hh, 0:dh, :] = _q_t(q_ref, hh)
        qt[hh, dh:dh + LANES, :] = sbt

    def ktile(j):
        return kaug[pl.ds(pl.multiple_of(j * tk, tk), tk), :]

    def vtile(j):
        return vt[:, pl.ds(pl.multiple_of(j * tk, tk), tk)]

    m_ref[...] = jnp.full(m_ref.shape, NEG_LOGIT, F32)
    acc_ref[...] = jnp.zeros_like(acc_ref)
    p_buf[...] = jnp.zeros_like(p_buf)
    a_buf[...] = jnp.ones_like(a_buf)
    k0 = ktile(0)
    for hh in heads:
        s_buf[hh] = _dot(k0, qt[hh])

    last = (i + 1) * r - 1

    def stage(j, bias_fn):
        v_prev = vtile(jnp.maximum(j - 1, 0))
        k_next = ktile(jnp.minimum(j + 1, last))
        pv = [_dot(v_prev, p_buf[hh]) for hh in heads]
        s_next = [_dot(k_next, qt[hh]) for hh in heads]
        for hh in heads:
            acc_ref[hh] = a_buf[hh] * acc_ref[hh] + pv[hh]
        for hh in heads:
            s = s_buf[hh]
            if bias_fn is not None:
                s = s + bias_fn(hh)
            m_prev = m_ref[hh]
            m_new = jnp.maximum(m_prev, jnp.max(s, axis=0, keepdims=True))
            a_buf[hh] = jnp.exp2(m_prev - m_new)
            p_buf[hh] = jnp.exp2(s - m_new).astype(BF16)
            m_ref[hh] = m_new
        for hh in heads:
            s_buf[hh] = s_next[hh]

    n_far = jnp.maximum(i * r + r - n_near, 0)

    def far_body(j, c):
        stage(j, None)
        return c

    lax.fori_loop(0, n_far, far_body, 0)

    def near_body(j, c):
        off = pl.multiple_of((i * r + (r - 1) - j) * tk, tk)
        stage(j, lambda hh: bias_ref[0, hh, :, pl.ds(off, tq)])
        return c

    lax.fori_loop(n_far, last + 1, near_body, 0)

    v_last = vtile(last)
    pv = [_dot(v_last, p_buf[hh]) for hh in heads]
    for hh in heads:
        acc = a_buf[hh] * acc_ref[hh] + pv[hh]
        os_ref[0, hh] = _out_t(acc, _gate_row(gate_ref, g, hh, 1)).astype(BF16)


def _slc_attn(proj, gates, selbias_t, bias_s, bsz, t):
    g_, dh, tq, tk = NSA_KV_GROUPS, NSA_HEAD_DIM, SLC_TQ, ATT_TK
    nt = t // tq
    gw = NSA_HPG * dh
    n_near = (bias_s.shape[3] - tq) // tk + 1
    return pl.pallas_call(
        functools.partial(_slc_attn_kernel, n_near=n_near),
        grid=(bsz, g_, nt),
        in_specs=[
            pl.BlockSpec((tq, gw), lambda b, g, i: (b * nt + i, COL_AQ // gw + g)),
            pl.BlockSpec((1, 1, LANES, tq), lambda b, g, i: (b, g, 0, i)),
            pl.BlockSpec((t, dh), lambda b, g, i: (b, COL_SK // dh + g)),
            pl.BlockSpec((t, dh), lambda b, g, i: (b, COL_SV // dh + g)),
            pl.BlockSpec((1, NSA_HPG, tk, bias_s.shape[3]), lambda b, g, i: (g, 0, 0, 0)),
            pl.BlockSpec((GATE_COLS, tq), lambda b, g, i: (0, b * nt + i)),
        ],
        out_specs=pl.BlockSpec((1, NSA_HPG, dh, tq), lambda b, g, i: (b, g, 0, i)),
        out_shape=jax.ShapeDtypeStruct((bsz, NSA_HEADS, dh, t), BF16),
        scratch_shapes=[
            pltpu.VMEM((t, dh + LANES), BF16),
            pltpu.VMEM((VROWS, t), BF16),
            pltpu.VMEM((NSA_HPG, dh + LANES, tq), BF16),
            pltpu.VMEM((NSA_HPG, tk, tq), F32),
            pltpu.VMEM((NSA_HPG, tk, tq), BF16),
            pltpu.VMEM((NSA_HPG, 1, tq), F32),
            pltpu.VMEM((NSA_HPG, 1, tq), F32),
            pltpu.VMEM((NSA_HPG, VROWS, tq), F32),
        ],
        compiler_params=_cparams(("parallel", "parallel", "arbitrary")),
        name="slc_attn",
    )(proj, selbias_t, proj, proj, bias_s, gates)


def _win_attn_kernel(q_ref, k_ref, v_ref, bias_ref, gate_ref, oc_ref, os_ref, z_ref, ya_ref, vt, *, n_near):
    tq, tk, dh = ATT_TQ, ATT_TK, NSA_HEAD_DIM
    g = pl.program_id(1)
    i = pl.program_id(2)

    @pl.when(i == 0)
    def _():
        _fill_vt(vt, v_ref)

    offs = [pl.multiple_of(jnp.maximum(i - d, 0) * tk, tk) for d in range(n_near)]
    d_eff = [jnp.where(i >= d, d, n_near) for d in range(n_near)]
    k_tiles = [k_ref[pl.ds(off, tk), :] for off in offs]
    v_tiles = [vt[:, pl.ds(off, tk)] for off in offs]
    qs = [_q_scaled(q_ref, hh) for hh in range(NSA_HPG)]
    ss = [[_dot_nt(k_tiles[d], qs[hh]) for d in range(n_near)] for hh in range(NSA_HPG)]
    ps = []
    for hh in range(NSA_HPG):
        s = [ss[hh][d] + bias_ref[0, d_eff[d], hh] for d in range(n_near)]
        m = functools.reduce(jnp.maximum, [jnp.max(x, axis=0, keepdims=True) for x in s])
        ps.append([jnp.exp2(x - m).astype(BF16) for x in s])
    pvs = [[_dot(v_tiles[d], ps[hh][d]) for d in range(n_near)] for hh in range(NSA_HPG)]
    for hh in range(NSA_HPG):
        cols = slice(hh * dh, (hh + 1) * dh)
        o = _out_t(functools.reduce(lambda x, y: x + y, pvs[hh]), _gate_row(gate_ref, g, hh, 2))
        o = o + oc_ref[0, hh].astype(F32) + os_ref[0, hh].astype(F32)
        ya_ref[:, cols] = (o.T * _silu(z_ref[:, cols].astype(F32))).astype(BF16)


def _win_attn(proj, gates, o_cmp, o_slc, bias_w, bsz, t):
    g_, dh, tq, tk = NSA_KV_GROUPS, NSA_HEAD_DIM, ATT_TQ, ATT_TK
    nt = t // tq
    gw = NSA_HPG * dh
    n_near = bias_w.shape[1] - 1
    return pl.pallas_call(
        functools.partial(_win_attn_kernel, n_near=n_near),
        grid=(bsz, g_, nt),
        in_specs=[
            pl.BlockSpec((tq, gw), lambda b, g, i: (b * nt + i, COL_AQ // gw + g)),
            pl.BlockSpec((t, dh), lambda b, g, i: (b, COL_WK // dh + g)),
            pl.BlockSpec((t, dh), lambda b, g, i: (b, COL_WV // dh + g)),
            pl.BlockSpec((1, n_near + 1, NSA_HPG, tk, tq), lambda b, g, i: (g, 0, 0, 0, 0)),
            pl.BlockSpec((GATE_COLS, tq), lambda b, g, i: (0, b * nt + i)),
            pl.BlockSpec((1, NSA_HPG, dh, tq), lambda b, g, i: (b, g, 0, i)),
            pl.BlockSpec((1, NSA_HPG, dh, tq), lambda b, g, i: (b, g, 0, i)),
            pl.BlockSpec((tq, gw), lambda b, g, i: (b * nt + i, COL_AZ // gw + g)),
        ],
        out_specs=pl.BlockSpec((tq, gw), lambda b, g, i: (b * nt + i, g)),
        out_shape=jax.ShapeDtypeStruct((bsz * t, NSA_WIDTH), BF16),
        scratch_shapes=[pltpu.VMEM((VROWS, t), BF16)],
        compiler_params=_cparams(("parallel", "parallel", "arbitrary")),
        name="win_attn",
    )(proj, proj, proj, bias_w, gates, o_cmp, o_slc, proj)


def _outproj_kernel(ym_ref, ya_ref, x_ref, w1_ref, w2_ref, gain_ref, o_ref, rows, inv_ref):
    j = pl.program_id(1)
    nj = rows.shape[0]
    tn = rows.shape[2]

    @pl.when(j < nj)
    def _():
        rows[j] = x_ref[...] + _dot(ym_ref[...], w1_ref[...]) + _dot(ya_ref[...], w2_ref[...])

    @pl.when(j == nj)
    def _():
        ss = None
        for jj in range(nj):
            y = rows[jj]
            part = jnp.sum(y * y, axis=-1, keepdims=True)
            ss = part if ss is None else ss + part
        inv_ref[...] = lax.rsqrt(ss / (nj * tn) + RMS_EPS)

    @pl.when(j >= nj)
    def _():
        o_ref[...] = rows[j - nj] * inv_ref[...] * gain_ref[...]


def _out_proj(y_m, y_a, x2d, w_out, gain):
    n, d = x2d.shape
    tm, tn = min(OUTPROJ_TM, n), OUTPROJ_TN
    nj = d // tn
    first = lambda j: jnp.minimum(j, nj - 1)
    second = lambda j: jnp.maximum(j - nj, 0)
    return pl.pallas_call(
        _outproj_kernel,
        grid=(n // tm, 2 * nj),
        in_specs=[
            pl.BlockSpec((tm, MLSTM_WIDTH), lambda i, j: (i, 0)),
            pl.BlockSpec((tm, NSA_WIDTH), lambda i, j: (i, 0)),
            pl.BlockSpec((tm, tn), lambda i, j: (i, first(j))),
            pl.BlockSpec((MLSTM_WIDTH, tn), lambda i, j: (0, first(j))),
            pl.BlockSpec((NSA_WIDTH, tn), lambda i, j: (MLSTM_WIDTH // NSA_WIDTH, first(j))),
            pl.BlockSpec((1, tn), lambda i, j: (0, second(j))),
        ],
        out_specs=pl.BlockSpec((tm, tn), lambda i, j: (i, second(j))),
        out_shape=jax.ShapeDtypeStruct((n, d), F32),
        scratch_shapes=[pltpu.VMEM((nj, tm, tn), F32), pltpu.VMEM((tm, 1), F32)],
        compiler_params=_cparams(("parallel", "arbitrary")),
        name="out_proj",
    )(y_m, y_a, x2d, w_out, w_out, gain)


def _rel_bucket(dist):
    n = jnp.maximum(dist, 0)
    nf = jnp.maximum(n, REL_MAX_EXACT).astype(jnp.float32)
    large = REL_MAX_EXACT + (jnp.log(nf / REL_MAX_EXACT) / math.log(REL_MAX_DISTANCE / REL_MAX_EXACT)
                             * (REL_BUCKETS - REL_MAX_EXACT)).astype(jnp.int32)
    large = jnp.minimum(large, REL_BUCKETS - 1)
    return jnp.where(n < REL_MAX_EXACT, n, large)


def _toeplitz_vec(by_dist, base, n_pos, n_neg, lo, hi, shift=None):
    w = n_pos + n_neg
    c = np.arange(w)
    dist = np.where(c < n_pos, base + c, base - (w - c))
    ok = (dist >= lo) & (dist < hi)
    vals = by_dist[:, np.clip(dist, 0, by_dist.shape[1] - 1)]
    if shift is not None:
        vals = vals - shift
    return jnp.where(ok[None], vals * LOG2E, NEG_LOGIT)


def _toeplitz_t(w_row, n_keys, n_q, key_step):
    x = jnp.broadcast_to(w_row, (n_keys, w_row.shape[1]))
    return pltpu.roll(x, 0, 1, stride=key_step, stride_axis=0)[:, 0:n_q]


def _bias_tables_kernel(wc_ref, ws_ref, ww_ref, bc_ref, bs_ref, bw_ref):
    nb, t = bc_ref.shape[2], bc_ref.shape[3]
    bc_ref[0, 0] = _toeplitz_t(wc_ref[0], nb, t, CMP_STRIDE)
    bs_ref[0, 0] = _toeplitz_t(ws_ref[0], bs_ref.shape[2], bs_ref.shape[3], 1)
    tk, tq = bw_ref.shape[3], bw_ref.shape[4]
    n_w = ww_ref.shape[1]
    for d in range(n_w):
        bw_ref[0, d, 0] = _toeplitz_t(ww_ref[0, d:d + 1, :], tk, tq, 1)
    bw_ref[0, n_w, 0] = jnp.full((tk, tq), NEG_LOGIT, F32)


def _bias_tables(rel_bias, t):
    tq, tk = ATT_TQ, ATT_TK
    g_, hpg = NSA_KV_GROUPS, NSA_HPG
    rb = rel_bias.astype(F32)
    dmax = REL_MAX_DISTANCE + tq + tk
    by_dist = rb[_rel_bucket(jnp.arange(dmax, dtype=jnp.int32))].T
    far = rb[REL_BUCKETS - 1][:, None]
    big = 1 << 30
    nb = t // CMP_STRIDE
    wc = _toeplitz_vec(by_dist, -(CMP_BLOCK - 1), t, t, 0, big)[:, None, :]
    sq = min(SLC_TQ, t)
    r = sq // tk
    n_s = min(-(-(REL_MAX_DISTANCE + tk - 1) // tk) + r - 1, t // tk)
    wm = (n_s - 1) * tk + sq
    ws = _toeplitz_vec(by_dist, -(r - 1) * tk, wm, tk, 0, big, far)[:, None, :]
    n_w = min(-(-(WINDOW + tk - 1) // tq), t // tq)
    ww = jnp.stack([_toeplitz_vec(by_dist, d * tq, tq, tk, 0, WINDOW) for d in range(n_w)], axis=1)
    return pl.pallas_call(
        _bias_tables_kernel,
        grid=(NSA_HEADS,),
        in_specs=[
            pl.BlockSpec((1, 1, 2 * t), lambda h: (h, 0, 0)),
            pl.BlockSpec((1, 1, wm + tk), lambda h: (h, 0, 0)),
            pl.BlockSpec((1, n_w, tq + tk), lambda h: (h, 0, 0)),
        ],
        out_specs=[
            pl.BlockSpec((1, 1, nb, t), lambda h: (h // hpg, h % hpg, 0, 0)),
            pl.BlockSpec((1, 1, tk, wm), lambda h: (h // hpg, h % hpg, 0, 0)),
            pl.BlockSpec((1, n_w + 1, 1, tk, tq), lambda h: (h // hpg, 0, h % hpg, 0, 0)),
        ],
        out_shape=[
            jax.ShapeDtypeStruct((g_, hpg, nb, t), F32),
            jax.ShapeDtypeStruct((g_, hpg, tk, wm), F32),
            jax.ShapeDtypeStruct((g_, n_w + 1, hpg, tk, tq), F32),
        ],
        compiler_params=_cparams(("parallel",)),
        name="bias_tables",
    )(wc, ws, ww)


def _cover_t(t):
    nb = t // CMP_STRIDE
    n_cmp = (t - CMP_BLOCK) // CMP_STRIDE + 1
    n_slc = t // SLC_BLOCK
    cs = np.arange(nb) * CMP_STRIDE
    ss = np.arange(n_slc) * SLC_BLOCK
    cover = np.clip(np.minimum(cs[:, None] + CMP_BLOCK, ss[None, :] + SLC_BLOCK)
                    - np.maximum(cs[:, None], ss[None, :]), 0, None) / CMP_BLOCK
    cover[n_cmp:] = 0.0
    return jnp.asarray(cover.T, dtype=BF16)


def kernel(x, norm_gain, w_in, w_conv, b_igate, b_fgate, mlstm_norm_gain, cmp_k_pos, cmp_k_w1, cmp_k_w2,
           cmp_v_pos, cmp_v_w1, cmp_v_w2, rel_bias, w_out, final_norm_gain):
    bsz, t, d = x.shape
    assert d == D_MODEL and t % MLSTM_L == 0 and t % ATT_TQ == 0 and (t // CMP_STRIDE) % LANES == 0
    n = bsz * t
    x2d = x.reshape(n, d)

    w_main, w_gate = _w_prep(w_in)
    proj, gates = _in_proj(x2d, norm_gain.reshape(1, d).astype(F32), w_main, w_gate)

    g_rows = gates[:, :2 * MLSTM_HEADS].reshape(bsz, t, 2, MLSTM_HEADS).transpose(0, 3, 2, 1)
    y_m = _mlstm(proj, g_rows, w_conv.astype(F32), b_igate.astype(F32), b_fgate.astype(F32),
                 mlstm_norm_gain.reshape(1, MLSTM_WIDTH).astype(F32), bsz, t)

    dh = NSA_HEAD_DIM
    half = CMP_BLOCK // 2

    def w1cat(w1):
        return jnp.concatenate([w1[:half].reshape(half * dh, dh), w1[half:].reshape(half * dh, dh)], axis=1).astype(BF16)

    k_cmp, v_cmp_t = _compress(
        proj, w1cat(cmp_k_w1), cmp_k_w2.astype(BF16), cmp_k_pos.reshape(2, half * dh).astype(BF16),
        w1cat(cmp_v_w1), cmp_v_w2.astype(BF16), cmp_v_pos.reshape(2, half * dh).astype(BF16), bsz, t)

    bias_c, bias_s, bias_w = _bias_tables(rel_bias, t)
    gates_t = gates.T
    o_cmp, selbias_t = _cmp_attn(proj, gates_t, k_cmp, v_cmp_t, bias_c, _cover_t(t), bsz, t)
    o_slc = _slc_attn(proj, gates_t, selbias_t, bias_s, bsz, t)
    y_a = _win_attn(proj, gates_t, o_cmp, o_slc, bias_w, bsz, t)

    out = _out_proj(y_m, y_a, x2d, w_out.astype(BF16), final_norm_gain.reshape(1, d).astype(F32))
    return out.reshape(bsz, t, d)
```

```python
import functools
import math

import jax
import jax.numpy as jnp
import numpy as np
from jax import lax
from jax.experimental import pallas as pl
from jax.experimental.pallas import tpu as pltpu

F32 = jnp.float32
BF16 = jnp.bfloat16

D_MODEL = 4096
D_MIX = D_MODEL
MLSTM_WIDTH = D_MIX // 2
MLSTM_HEADS = 4
MLSTM_V_DIM = MLSTM_WIDTH // MLSTM_HEADS
MLSTM_QK_DIM = MLSTM_V_DIM // 2
MLSTM_QK_WIDTH = MLSTM_HEADS * MLSTM_QK_DIM
CONV_WIDTH = 4
NSA_WIDTH = D_MIX - MLSTM_WIDTH
NSA_HEAD_DIM = 128
NSA_HEADS = NSA_WIDTH // NSA_HEAD_DIM
NSA_KV_GROUPS = 4
NSA_HPG = NSA_HEADS // NSA_KV_GROUPS
NSA_KV_WIDTH = NSA_KV_GROUPS * NSA_HEAD_DIM
CMP_BLOCK = 32
CMP_STRIDE = 16
SLC_BLOCK = 64
SLC_TOP_N = 16
WINDOW = 512
FORCE_BONUS = 1000.0
REL_BUCKETS = 32
REL_MAX_EXACT = REL_BUCKETS // 2
REL_MAX_DISTANCE = 1024
RMS_EPS = 1e-6
NEG_LOGIT = -1e30

LANES = 128
VMEM_LIMIT_BYTES = 56 * 1024 * 1024

COL_MQ = 0
COL_MK = COL_MQ + MLSTM_QK_WIDTH
COL_MV = COL_MK + MLSTM_QK_WIDTH
COL_MO = COL_MV + MLSTM_WIDTH
COL_MZ = COL_MO + MLSTM_WIDTH
COL_AQ = COL_MZ + MLSTM_WIDTH
COL_CK = COL_AQ + NSA_WIDTH
COL_CV = COL_CK + NSA_KV_WIDTH
COL_SK = COL_CV + NSA_KV_WIDTH
COL_SV = COL_SK + NSA_KV_WIDTH
COL_WK = COL_SV + NSA_KV_WIDTH
COL_WV = COL_WK + NSA_KV_WIDTH
COL_AZ = COL_WV + NSA_KV_WIDTH
MAIN_COLS = COL_AZ + NSA_WIDTH
GATE_COLS = LANES
GATE_I = 0
GATE_F = MLSTM_HEADS
GATE_NSA = 2 * MLSTM_HEADS

INPROJ_TM = 512
INPROJ_TN = 1024
OUTPROJ_TM = 512
OUTPROJ_TN = 512
MLSTM_L = 256
ATT_TQ = 256
ATT_TK = 256
WPREP_TR = 512
SLC_TQ = 512
CMP_TQ = 512
MASK_BIG = 2.0 ** 100


def _cparams(sem):
    return pltpu.CompilerParams(dimension_semantics=sem, vmem_limit_bytes=VMEM_LIMIT_BYTES)


def _sigmoid(x):
    return 0.5 * jnp.tanh(0.5 * x) + 0.5


def _silu(x):
    return x * _sigmoid(x)


def _dot(a, b):
    return jnp.dot(a, b, preferred_element_type=F32)


def _dot_nt(a, b):
    return lax.dot_general(a, b, (((1,), (1,)), ((), ())), preferred_element_type=F32)


W_OFF_I = COL_MZ + MLSTM_WIDTH
W_OFF_AQ = W_OFF_I + 2 * MLSTM_HEADS
W_OFF_GATE = W_OFF_AQ + NSA_WIDTH + 6 * NSA_KV_WIDTH
W_OFF_AZ = W_OFF_GATE + 3 * NSA_HEADS


def _wprep_kernel(w_ref, gi_ref, ga_ref, o_ref, g_ref):
    o_ref[...] = w_ref[...].astype(BF16)

    @pl.when(pl.program_id(0) == 0)
    def _():
        pad = jnp.zeros((GATE_COLS - gi_ref.shape[0] - ga_ref.shape[0], g_ref.shape[1]), F32)
        g_ref[...] = jnp.concatenate([gi_ref[...], ga_ref[...], pad], axis=0).astype(BF16)


def _w_prep(w_t):
    rows, d = w_t.shape
    tr = WPREP_TR
    nb_a, nb_b = W_OFF_I // tr, (W_OFF_GATE - W_OFF_AQ) // tr

    def src(m):
        skip = jnp.where(m >= nb_a, W_OFF_AQ - W_OFF_I, 0) + jnp.where(m >= nb_a + nb_b, W_OFF_AZ - W_OFF_GATE, 0)
        return pl.multiple_of(m * tr + skip, 8)

    rows_at = lambda start, size: pl.BlockSpec((pl.Element(size), pl.Element(d)), lambda m: (start, 0))
    return pl.pallas_call(
        _wprep_kernel,
        grid=(MAIN_COLS // tr,),
        in_specs=[
            pl.BlockSpec((pl.Element(tr), pl.Element(d)), lambda m: (src(m), 0)),
            rows_at(W_OFF_I, W_OFF_AQ - W_OFF_I),
            rows_at(W_OFF_GATE, W_OFF_AZ - W_OFF_GATE),
        ],
        out_specs=[pl.BlockSpec((tr, d), lambda m: (m, 0)), pl.BlockSpec((GATE_COLS, d), lambda m: (0, 0))],
        out_shape=[jax.ShapeDtypeStruct((MAIN_COLS, d), BF16), jax.ShapeDtypeStruct((GATE_COLS, d), BF16)],
        compiler_params=_cparams(("arbitrary",)),
        name="w_prep",
    )(w_t, w_t, w_t)


def _inproj_kernel(x_ref, gain_ref, w_ref, wg_ref, o_ref, og_ref, h_ref):
    @pl.when(pl.program_id(1) == 0)
    def _():
        x = x_ref[...]
        ms = jnp.mean(x * x, axis=-1, keepdims=True)
        h = (x * lax.rsqrt(ms + RMS_EPS) * gain_ref[...]).astype(BF16)
        h_ref[...] = h
        og_ref[...] = _dot_nt(h, wg_ref[...])

    o_ref[...] = _dot_nt(h_ref[...], w_ref[...]).astype(BF16)


def _in_proj(x2d, gain, w_main, w_gate):
    n, d = x2d.shape
    tm, tn = min(INPROJ_TM, n), INPROJ_TN
    return pl.pallas_call(
        _inproj_kernel,
        grid=(n // tm, MAIN_COLS // tn),
        in_specs=[
            pl.BlockSpec((tm, d), lambda i, j: (i, 0)),
            pl.BlockSpec((1, d), lambda i, j: (0, 0)),
            pl.BlockSpec((tn, d), lambda i, j: (j, 0)),
            pl.BlockSpec((GATE_COLS, d), lambda i, j: (0, 0)),
        ],
        out_specs=[
            pl.BlockSpec((tm, tn), lambda i, j: (i, j)),
            pl.BlockSpec((tm, GATE_COLS), lambda i, j: (i, 0)),
        ],
        out_shape=[
            jax.ShapeDtypeStruct((n, MAIN_COLS), BF16),
            jax.ShapeDtypeStruct((n, GATE_COLS), F32),
        ],
        scratch_shapes=[pltpu.VMEM((tm, d), BF16)],
        compiler_params=_cparams(("parallel", "arbitrary")),
        name="in_proj",
    )(x2d, gain, w_main, w_gate)


def _mlstm_kernel(bi_ref, bf_ref, q_ref, k_ref, v_ref, o_ref, z_ref, g_ref, wq_ref, wk_ref, ng_ref,
                  y_ref, qext, kext, c_st, n_st, m_st):
    L = MLSTM_L
    HIST = 8
    hd = pl.program_id(1)

    @pl.when(pl.program_id(2) == 0)
    def _():
        qext[0:HIST, :] = jnp.zeros((HIST, MLSTM_QK_DIM), F32)
        kext[0:HIST, :] = jnp.zeros((HIST, MLSTM_QK_DIM), F32)
        c_st[...] = jnp.zeros_like(c_st)
        n_st[...] = jnp.zeros_like(n_st)
        m_st[...] = jnp.zeros_like(m_st)

    qext[HIST:HIST + L, :] = q_ref[...].astype(F32)
    kext[HIST:HIST + L, :] = k_ref[...].astype(F32)

    def conv_silu(ext, w_ref):
        w = w_ref[...]
        y = ext[pl.ds(HIST, L), :] * w[CONV_WIDTH - 1:CONV_WIDTH, :]
        for s in range(1, CONV_WIDTH):
            y = y + ext[pl.ds(HIST - s, L), :] * w[CONV_WIDTH - 1 - s:CONV_WIDTH - s, :]
        return _silu(y)

    qc = conv_silu(qext, wq_ref)
    kc = conv_silu(kext, wk_ref) * (MLSTM_QK_DIM ** -0.5)
    qext[0:HIST, :] = qext[L:L + HIST, :]
    kext[0:HIST, :] = kext[L:L + HIST, :]

    g = g_ref[0, 0]
    i_row = g[0:1, :] + bi_ref[hd]
    f_row = g[1:2, :] + bf_ref[hd]
    lf_row = jnp.minimum(f_row, 0.0) - jnp.log(1.0 + jnp.exp(-jnp.abs(f_row)))

    rr = lax.broadcasted_iota(jnp.int32, (L, L), 0)
    cc = lax.broadcasted_iota(jnp.int32, (L, L), 1)
    upper = (rr <= cc).astype(F32)
    bcum_row = jnp.dot(jnp.broadcast_to(lf_row, (8, L)), upper, preferred_element_type=F32,
                       precision=lax.Precision.HIGHEST)[0:1, :]
    bcum_col = jnp.sum(jnp.where(rr == cc, bcum_row, 0.0), axis=1, keepdims=True)
    gsum = bcum_row[:, L - 1:L]
    m_prev = m_st[...]

    dlog = jnp.where(rr >= cc, bcum_col - bcum_row + i_row, -jnp.inf)
    m_inter = bcum_col + m_prev
    m_t = jnp.maximum(m_inter, jnp.max(dlog, axis=1, keepdims=True))
    dmat = jnp.exp(dlog - m_t)
    inter = jnp.exp(m_inter - m_t)

    qb = qc.astype(BF16)
    kct = kc.T
    vb = v_ref[...]
    s = _dot(qb, kct.astype(BF16)) * dmat
    c_prev = c_st[...]
    n_prev = n_st[...]
    num = _dot(s.astype(BF16), vb) + inter * _dot(qb, c_prev.astype(BF16))
    qn = jnp.sum(s, axis=1, keepdims=True) + inter * jnp.sum(qc * n_prev, axis=1, keepdims=True)
    hh = num / jnp.maximum(jnp.abs(qn), jnp.exp(-m_t))

    wlog = gsum - bcum_row + i_row
    m_next = jnp.maximum(gsum + m_prev, jnp.max(wlog, axis=1, keepdims=True))
    wts = jnp.exp(wlog - m_next)
    keep = jnp.exp(gsum + m_prev - m_next)
    c_st[...] = keep * c_prev + _dot((kct * wts).astype(BF16), vb)
    n_st[...] = keep * n_prev + _dot(jnp.broadcast_to(wts, (8, L)).astype(BF16), kc.astype(BF16))[0:1, :]
    m_st[...] = m_next

    hm = _sigmoid(o_ref[...].astype(F32)) * hh
    hm = hm * lax.rsqrt(jnp.mean(hm * hm, axis=-1, keepdims=True) + RMS_EPS)
    hm = hm * ng_ref[...]
    y_ref[...] = (hm * _silu(z_ref[...].astype(F32))).astype(BF16)


def _mlstm(proj, g_rows, w_conv, b_igate, b_fgate, norm_gain, bsz, t):
    L = MLSTM_L
    nc = t // L
    dqk, dv = MLSTM_QK_DIM, MLSTM_V_DIM
    row = lambda b, h, c: b * nc + c
    smem = pl.BlockSpec(memory_space=pltpu.SMEM)
    return pl.pallas_call(
        _mlstm_kernel,
        grid=(bsz, MLSTM_HEADS, nc),
        in_specs=[
            smem, smem,
            pl.BlockSpec((L, dqk), lambda b, h, c: (row(b, h, c), COL_MQ // dqk + h)),
            pl.BlockSpec((L, dqk), lambda b, h, c: (row(b, h, c), COL_MK // dqk + h)),
            pl.BlockSpec((L, dv), lambda b, h, c: (row(b, h, c), COL_MV // dv + h)),
            pl.BlockSpec((L, dv), lambda b, h, c: (row(b, h, c), COL_MO // dv + h)),
            pl.BlockSpec((L, dv), lambda b, h, c: (row(b, h, c), COL_MZ // dv + h)),
            pl.BlockSpec((1, 1, 2, L), lambda b, h, c: (b, h, 0, c)),
            pl.BlockSpec((CONV_WIDTH, dqk), lambda b, h, c: (0, h)),
            pl.BlockSpec((CONV_WIDTH, dqk), lambda b, h, c: (0, MLSTM_HEADS + h)),
            pl.BlockSpec((1, dv), lambda b, h, c: (0, h)),
        ],
        out_specs=pl.BlockSpec((L, dv), lambda b, h, c: (row(b, h, c), h)),
        out_shape=jax.ShapeDtypeStruct((bsz * t, MLSTM_WIDTH), BF16),
        scratch_shapes=[
            pltpu.VMEM((L + 8, dqk), F32),
            pltpu.VMEM((L + 8, dqk), F32),
            pltpu.VMEM((dqk, dv), F32),
            pltpu.VMEM((1, dqk), F32),
            pltpu.VMEM((1, 1), F32),
        ],
        compiler_params=_cparams(("parallel", "parallel", "arbitrary")),
        name="mlstm",
    )(b_igate, b_fgate, proj, proj, proj, proj, proj, g_rows, w_conv, w_conv, norm_gain)


def _compress_kernel(ck_ref, cv_ref, w1k_ref, w2k_ref, pk_ref, w1v_ref, w2v_ref, pv_ref,
                     ok_ref, ov_ref, xf, xcat):
    t = ck_ref.shape[0]
    nb = t // CMP_STRIDE
    dh = NSA_HEAD_DIM

    def one(src_ref, w1_ref, w2_ref, pos_ref):
        xf[...] = src_ref[...].astype(F32)
        for l in range(CMP_STRIDE):
            xcat[:, l * dh:(l + 1) * dh] = xf[pl.ds(l, nb, stride=CMP_STRIDE), :].astype(BF16)
        w1 = w1_ref[...]
        ab = _dot(xcat[...], w1)
        pp = _dot(pos_ref[...], w1)
        pos_term = pp[0:1, 0:dh] + pp[1:2, dh:2 * dh]
        second = pltpu.roll(ab[:, dh:2 * dh], nb - 1, 0)
        hid = _silu(ab[:, 0:dh] + second + pos_term)
        return _dot(hid.astype(BF16), w2_ref[...])

    ok_ref[0, 0] = one(ck_ref, w1k_ref, w2k_ref, pk_ref).astype(BF16)
    ov_ref[0, 0] = one(cv_ref, w1v_ref, w2v_ref, pv_ref).T.astype(BF16)


def _compress(proj, w1k, w2k, pk, w1v, w2v, pv, bsz, t):
    g_, dh = NSA_KV_GROUPS, NSA_HEAD_DIM
    nb = t // CMP_STRIDE
    full = lambda a: pl.BlockSpec(a.shape, lambda b, g: (0,) * a.ndim)
    k_spec = pl.BlockSpec((1, 1, nb, dh), lambda b, g: (b, g, 0, 0))
    vt_spec = pl.BlockSpec((1, 1, dh, nb), lambda b, g: (b, g, 0, 0))
    return pl.pallas_call(
        _compress_kernel,
        grid=(bsz, g_),
        in_specs=[
            pl.BlockSpec((t, dh), lambda b, g: (b, COL_CK // dh + g)),
            pl.BlockSpec((t, dh), lambda b, g: (b, COL_CV // dh + g)),
            full(w1k), full(w2k), full(pk), full(w1v), full(w2v), full(pv),
        ],
        out_specs=[k_spec, vt_spec],
        out_shape=[jax.ShapeDtypeStruct((bsz, g_, nb, dh), BF16), jax.ShapeDtypeStruct((bsz, g_, dh, nb), BF16)],
        scratch_shapes=[pltpu.VMEM((t, dh), F32), pltpu.VMEM((nb, CMP_STRIDE * dh), BF16)],
        compiler_params=_cparams(("parallel", "parallel")),
        name="compress",
    )(proj, proj, w1k, w2k, pk, w1v, w2v, pv)


LOG2E = math.log2(math.e)
VROWS = NSA_HEAD_DIM + 16


def _q_scaled(q_ref, hh):
    dh = NSA_HEAD_DIM
    return (q_ref[:, hh * dh:(hh + 1) * dh].astype(F32) * (dh ** -0.5 * LOG2E)).astype(BF16)


def _q_t(q_ref, hh):
    dh = NSA_HEAD_DIM
    return (q_ref[:, hh * dh:(hh + 1) * dh].astype(F32) * (dh ** -0.5 * LOG2E)).T.astype(BF16)


def _gate_row(gt_ref, g, hh, branch):
    row = GATE_NSA + 3 * (g * NSA_HPG + hh) + branch
    return _sigmoid(gt_ref[pl.ds(row, 1), :])


def _fill_vt(vt, v_ref):
    dh = NSA_HEAD_DIM
    vt[0:dh, :] = v_ref[...].astype(F32).T.astype(BF16)
    vt[dh:VROWS, :] = jnp.ones((VROWS - dh, vt.shape[1]), BF16)


def _out_t(acc, gate_row):
    dh = NSA_HEAD_DIM
    return acc[0:dh, :] * (gate_row / acc[dh:dh + 1, :])


def _cmp_attn_kernel(q_ref, kc_ref, vct_ref, bias_ref, cov_ref, gate_ref, oc_ref, sel_ref, score_ref, *, n_sel):
    tq = q_ref.shape[0]
    dh = NSA_HEAD_DIM
    n_slc = cov_ref.shape[0]
    t0 = pl.program_id(1) * tq
    g = pl.program_id(0)
    kc = kc_ref[0, 0]
    vct = vct_ref[0, 0]
    heads = range(NSA_HPG)
    qk = [_dot_nt(kc, _q_scaled(q_ref, hh)) for hh in heads]
    ps = []
    for hh in heads:
        logit = qk[hh] + bias_ref[0, hh]
        m = jnp.max(logit, axis=0, keepdims=True)
        e = jnp.exp2(logit - m)
        inv = jnp.where(m > 0.5 * NEG_LOGIT, 1.0 / jnp.sum(e, axis=0, keepdims=True), 0.0)
        ps.append(e * inv)
    p_sum = (ps[0] + ps[1]) + (ps[2] + ps[3])
    ots = [_dot(vct, p.astype(BF16)) for p in ps]
    for hh in heads:
        oc_ref[0, hh] = (ots[hh] * _gate_row(gate_ref, g, hh, 0)).astype(BF16)

    cov = cov_ref[...]
    p_hi = p_sum.astype(BF16)
    r1 = p_sum - p_hi.astype(F32)
    p_mid = r1.astype(BF16)
    p_lo = (r1 - p_mid.astype(F32)).astype(BF16)
    st = _dot(cov, p_hi) + _dot(cov, p_mid) + _dot(cov, p_lo)

    jb = lax.broadcasted_iota(jnp.int32, (n_slc, tq), 0)
    cur = (t0 + lax.broadcasted_iota(jnp.int32, (n_slc, tq), 1)) // SLC_BLOCK
    valid = jb <= cur
    forced = (jb == 0) | (jb == cur) | (jb == cur - 1)
    score = jnp.where(valid, st + jnp.where(forced, FORCE_BONUS, 0.0), -1.0)
    sub = 8
    score_ref[...] = score
    groups = [score_ref[r:r + sub, :] for r in range(0, n_slc, sub)]
    ranks = [jnp.zeros((sub, tq), F32) for _ in groups]
    row_id = lax.broadcasted_iota(jnp.int32, (sub, tq), 0)
    for j2 in range(n_slc):
        row = score_ref[j2:j2 + 1, :]
        for gi, sc in enumerate(groups):
            r0 = gi * sub
            if r0 > j2:
                inc = jnp.where(row >= sc, 1.0, 0.0)
            elif r0 + sub - 1 <= j2:
                inc = jnp.where(row > sc, 1.0, 0.0)
            else:
                inc = jnp.where(row_id > j2 - r0, jnp.where(row >= sc, 1.0, 0.0), jnp.where(row > sc, 1.0, 0.0))
            ranks[gi] = ranks[gi] + inc
    for gi, rk in enumerate(ranks):
        score_ref[gi * sub:(gi + 1) * sub, :] = rk
    sel = valid & (score_ref[...] < n_sel)
    sel_ref[0, 0, 0:n_slc, :] = jnp.where(sel, 0.0, -MASK_BIG).astype(BF16)
    if n_slc < LANES:
        sel_ref[0, 0, n_slc:LANES, :] = jnp.zeros((LANES - n_slc, tq), BF16)


def _cmp_attn(proj, gates, k_cmp, v_cmp_t, bias_c, cover_t, bsz, t):
    g_, dh, tq = NSA_KV_GROUPS, NSA_HEAD_DIM, CMP_TQ
    nt = t // tq
    nb = t // CMP_STRIDE
    n_slc = t // SLC_BLOCK
    gw = NSA_HPG * dh
    return pl.pallas_call(
        functools.partial(_cmp_attn_kernel, n_sel=min(SLC_TOP_N, n_slc)),
        grid=(g_, nt, bsz),
        in_specs=[
            pl.BlockSpec((tq, gw), lambda g, i, b: (b * nt + i, COL_AQ // gw + g)),
            pl.BlockSpec((1, 1, nb, dh), lambda g, i, b: (b, g, 0, 0)),
            pl.BlockSpec((1, 1, dh, nb), lambda g, i, b: (b, g, 0, 0)),
            pl.BlockSpec((1, NSA_HPG, nb, tq), lambda g, i, b: (g, 0, 0, i)),
            pl.BlockSpec((n_slc, nb), lambda g, i, b: (0, 0)),
            pl.BlockSpec((GATE_COLS, tq), lambda g, i, b: (0, b * nt + i)),
        ],
        out_specs=[
            pl.BlockSpec((1, NSA_HPG, dh, tq), lambda g, i, b: (b, g, 0, i)),
            pl.BlockSpec((1, 1, LANES, tq), lambda g, i, b: (b, g, 0, i)),
        ],
        out_shape=[
            jax.ShapeDtypeStruct((bsz, NSA_HEADS, dh, t), BF16),
            jax.ShapeDtypeStruct((bsz, g_, LANES, t), BF16),
        ],
        scratch_shapes=[pltpu.VMEM((n_slc, tq), F32)],
        compiler_params=_cparams(("parallel", "parallel", "parallel")),
        name="cmp_attn",
    )(proj, k_cmp, v_cmp_t, bias_c, cover_t, gates)


def _slc_attn_kernel(q_ref, sbt_ref, k_ref, v_ref, bias_ref, gate_ref, os_ref,
                     kaug, vt, qt, s_buf, p_buf, a_buf, m_ref, acc_ref, *, n_near):
    tq, tk, dh = SLC_TQ, ATT_TK, NSA_HEAD_DIM
    r = tq // tk
    t = k_ref.shape[0]
    g = pl.program_id(1)
    i = pl.program_id(2)
    heads = range(NSA_HPG)

    @pl.when(i == 0)
    def _():
        kaug[:, 0:dh] = k_ref[...]
        blk = lax.broadcasted_iota(jnp.int32, (t, LANES), 0) // SLC_BLOCK
        lane = lax.broadcasted_iota(jnp.int32, (t, LANES), 1)
        kaug[:, dh:dh + LANES] = jnp.where(blk == lane, 1.0, 0.0).astype(BF16)
        _fill_vt(vt, v_ref)

    sbt = sbt_ref[0, 0]
    for hh in heads:
        qt[hh, 0:dh, :] = _q_t(q_ref, hh)
        qt[hh, dh:dh + LANES, :] = sbt

    def ktile(j):
        return kaug[pl.ds(pl.multiple_of(j * tk, tk), tk), :]

    def vtile(j):
        return vt[:, pl.ds(pl.multiple_of(j * tk, tk), tk)]

    m_ref[...] = jnp.full(m_ref.shape, NEG_LOGIT, F32)
    acc_ref[...] = jnp.zeros_like(acc_ref)
    p_buf[...] = jnp.zeros_like(p_buf)
    a_buf[...] = jnp.ones_like(a_buf)
    k0 = ktile(0)
    for hh in heads:
        s_buf[hh] = _dot(k0, qt[hh])

    last = (i + 1) * r - 1

    def stage(j, bias_fn):
        v_prev = vtile(jnp.maximum(j - 1, 0))
        k_next = ktile(jnp.minimum(j + 1, last))
        pv = [_dot(v_prev, p_buf[hh]) for hh in heads]
        s_next = [_dot(k_next, qt[hh]) for hh in heads]
        for hh in heads:
            acc_ref[hh] = a_buf[hh] * acc_ref[hh] + pv[hh]
        for hh in heads:
            s = s_buf[hh]
            if bias_fn is not None:
                s = s + bias_fn(hh)
            m_prev = m_ref[hh]
            m_new = jnp.maximum(m_prev, jnp.max(s, axis=0, keepdims=True))
            a_buf[hh] = jnp.exp2(m_prev - m_new)
            p_buf[hh] = jnp.exp2(s - m_new).astype(BF16)
            m_ref[hh] = m_new
        for hh in heads:
            s_buf[hh] = s_next[hh]

    n_far = jnp.maximum(i * r + r - n_near, 0)

    def far_body(j, c):
        stage(j, None)
        return c

    lax.fori_loop(0, n_far, far_body, 0)

    def near_body(j, c):
        off = pl.multiple_of((i * r + (r - 1) - j) * tk, tk)
        stage(j, lambda hh: bias_ref[0, hh, :, pl.ds(off, tq)])
        return c

    lax.fori_loop(n_far, last + 1, near_body, 0)

    v_last = vtile(last)
    pv = [_dot(v_last, p_buf[hh]) for hh in heads]
    for hh in heads:
        acc = a_buf[hh] * acc_ref[hh] + pv[hh]
        os_ref[0, hh] = _out_t(acc, _gate_row(gate_ref, g, hh, 1)).astype(BF16)


def _slc_attn(proj, gates, selbias_t, bias_s, bsz, t):
    g_, dh, tq, tk = NSA_KV_GROUPS, NSA_HEAD_DIM, SLC_TQ, ATT_TK
    nt = t // tq
    gw = NSA_HPG * dh
    n_near = (bias_s.shape[3] - tq) // tk + 1
    return pl.pallas_call(
        functools.partial(_slc_attn_kernel, n_near=n_near),
        grid=(bsz, g_, nt),
        in_specs=[
            pl.BlockSpec((tq, gw), lambda b, g, i: (b * nt + i, COL_AQ // gw + g)),
            pl.BlockSpec((1, 1, LANES, tq), lambda b, g, i: (b, g, 0, i)),
            pl.BlockSpec((t, dh), lambda b, g, i: (b, COL_SK // dh + g)),
            pl.BlockSpec((t, dh), lambda b, g, i: (b, COL_SV // dh + g)),
            pl.BlockSpec((1, NSA_HPG, tk, bias_s.shape[3]), lambda b, g, i: (g, 0, 0, 0)),
            pl.BlockSpec((GATE_COLS, tq), lambda b, g, i: (0, b * nt + i)),
        ],
        out_specs=pl.BlockSpec((1, NSA_HPG, dh, tq), lambda b, g, i: (b, g, 0, i)),
        out_shape=jax.ShapeDtypeStruct((bsz, NSA_HEADS, dh, t), BF16),
        scratch_shapes=[
            pltpu.VMEM((t, dh + LANES), BF16),
            pltpu.VMEM((VROWS, t), BF16),
            pltpu.VMEM((NSA_HPG, dh + LANES, tq), BF16),
            pltpu.VMEM((NSA_HPG, tk, tq), F32),
            pltpu.VMEM((NSA_HPG, tk, tq), BF16),
            pltpu.VMEM((NSA_HPG, 1, tq), F32),
            pltpu.VMEM((NSA_HPG, 1, tq), F32),
            pltpu.VMEM((NSA_HPG, VROWS, tq), F32),
        ],
        compiler_params=_cparams(("parallel", "parallel", "arbitrary")),
        name="slc_attn",
    )(proj, selbias_t, proj, proj, bias_s, gates)


def _win_attn_kernel(q_ref, k_ref, v_ref, bias_ref, gate_ref, oc_ref, os_ref, z_ref, ya_ref, vt, *, n_near):
    tq, tk, dh = ATT_TQ, ATT_TK, NSA_HEAD_DIM
    g = pl.program_id(1)
    i = pl.program_id(2)

    @pl.when(i == 0)
    def _():
        _fill_vt(vt, v_ref)

    offs = [pl.multiple_of(jnp.maximum(i - d, 0) * tk, tk) for d in range(n_near)]
    d_eff = [jnp.where(i >= d, d, n_near) for d in range(n_near)]
    k_tiles = [k_ref[pl.ds(off, tk), :] for off in offs]
    v_tiles = [vt[:, pl.ds(off, tk)] for off in offs]
    qs = [_q_scaled(q_ref, hh) for hh in range(NSA_HPG)]
    ss = [[_dot_nt(k_tiles[d], qs[hh]) for d in range(n_near)] for hh in range(NSA_HPG)]
    ps = []
    for hh in range(NSA_HPG):
        s = [ss[hh][d] + bias_ref[0, d_eff[d], hh] for d in range(n_near)]
        m = functools.reduce(jnp.maximum, [jnp.max(x, axis=0, keepdims=True) for x in s])
        ps.append([jnp.exp2(x - m).astype(BF16) for x in s])
    pvs = [[_dot(v_tiles[d], ps[hh][d]) for d in range(n_near)] for hh in range(NSA_HPG)]
    for hh in range(NSA_HPG):
        cols = slice(hh * dh, (hh + 1) * dh)
        o = _out_t(functools.reduce(lambda x, y: x + y, pvs[hh]), _gate_row(gate_ref, g, hh, 2))
        o = o + oc_ref[0, hh].astype(F32) + os_ref[0, hh].astype(F32)
        ya_ref[:, cols] = (o.T * _silu(z_ref[:, cols].astype(F32))).astype(BF16)


def _win_attn(proj, gates, o_cmp, o_slc, bias_w, bsz, t):
    g_, dh, tq, tk = NSA_KV_GROUPS, NSA_HEAD_DIM, ATT_TQ, ATT_TK
    nt = t // tq
    gw = NSA_HPG * dh
    n_near = bias_w.shape[1] - 1
    return pl.pallas_call(
        functools.partial(_win_attn_kernel, n_near=n_near),
        grid=(bsz, g_, nt),
        in_specs=[
            pl.BlockSpec((tq, gw), lambda b, g, i: (b * nt + i, COL_AQ // gw + g)),
            pl.BlockSpec((t, dh), lambda b, g, i: (b, COL_WK // dh + g)),
            pl.BlockSpec((t, dh), lambda b, g, i: (b, COL_WV // dh + g)),
            pl.BlockSpec((1, n_near + 1, NSA_HPG, tk, tq), lambda b, g, i: (g, 0, 0, 0, 0)),
            pl.BlockSpec((GATE_COLS, tq), lambda b, g, i: (0, b * nt + i)),
            pl.BlockSpec((1, NSA_HPG, dh, tq), lambda b, g, i: (b, g, 0, i)),
            pl.BlockSpec((1, NSA_HPG, dh, tq), lambda b, g, i: (b, g, 0, i)),
            pl.BlockSpec((tq, gw), lambda b, g, i: (b * nt + i, COL_AZ // gw + g)),
        ],
        out_specs=pl.BlockSpec((tq, gw), lambda b, g, i: (b * nt + i, g)),
        out_shape=jax.ShapeDtypeStruct((bsz * t, NSA_WIDTH), BF16),
        scratch_shapes=[pltpu.VMEM((VROWS, t), BF16)],
        compiler_params=_cparams(("parallel", "parallel", "arbitrary")),
        name="win_attn",
    )(proj, proj, proj, bias_w, gates, o_cmp, o_slc, proj)


def _outproj_kernel(ym_ref, ya_ref, x_ref, w1_ref, w2_ref, gain_ref, o_ref, rows):
    j = pl.program_id(1)
    nj = rows.shape[0]
    tn = rows.shape[2]
    rows[j] = x_ref[...] + _dot(ym_ref[...], w1_ref[...]) + _dot(ya_ref[...], w2_ref[...])

    @pl.when(j == nj - 1)
    def _():
        ss = None
        for jj in range(nj):
            y = rows[jj]
            part = jnp.sum(y * y, axis=-1, keepdims=True)
            ss = part if ss is None else ss + part
        inv = lax.rsqrt(ss / (nj * tn) + RMS_EPS)
        for jj in range(nj):
            o_ref[:, jj * tn:(jj + 1) * tn] = rows[jj] * inv * gain_ref[:, jj * tn:(jj + 1) * tn]


def _out_proj(y_m, y_a, x2d, w_out, gain):
    n, d = x2d.shape
    tm, tn = min(OUTPROJ_TM, n), OUTPROJ_TN
    nj = d // tn
    return pl.pallas_call(
        _outproj_kernel,
        grid=(n // tm, nj),
        in_specs=[
            pl.BlockSpec((tm, MLSTM_WIDTH), lambda i, j: (i, 0)),
            pl.BlockSpec((tm, NSA_WIDTH), lambda i, j: (i, 0)),
            pl.BlockSpec((tm, tn), lambda i, j: (i, j)),
            pl.BlockSpec((MLSTM_WIDTH, tn), lambda i, j: (0, j)),
            pl.BlockSpec((NSA_WIDTH, tn), lambda i, j: (MLSTM_WIDTH // NSA_WIDTH, j)),
            pl.BlockSpec((1, d), lambda i, j: (0, 0)),
        ],
        out_specs=pl.BlockSpec((tm, d), lambda i, j: (i, 0)),
        out_shape=jax.ShapeDtypeStruct((n, d), F32),
        scratch_shapes=[pltpu.VMEM((nj, tm, tn), F32)],
        compiler_params=_cparams(("parallel", "arbitrary")),
        name="out_proj",
    )(y_m, y_a, x2d, w_out, w_out, gain)


def _rel_bucket(dist):
    n = jnp.maximum(dist, 0)
    nf = jnp.maximum(n, REL_MAX_EXACT).astype(jnp.float32)
    large = REL_MAX_EXACT + (jnp.log(nf / REL_MAX_EXACT) / math.log(REL_MAX_DISTANCE / REL_MAX_EXACT)
                             * (REL_BUCKETS - REL_MAX_EXACT)).astype(jnp.int32)
    large = jnp.minimum(large, REL_BUCKETS - 1)
    return jnp.where(n < REL_MAX_EXACT, n, large)


def _toeplitz_vec(by_dist, base, n_pos, n_neg, lo, hi, shift=None):
    w = n_pos + n_neg
    c = np.arange(w)
    dist = np.where(c < n_pos, base + c, base - (w - c))
    ok = (dist >= lo) & (dist < hi)
    vals = by_dist[:, np.clip(dist, 0, by_dist.shape[1] - 1)]
    if shift is not None:
        vals = vals - shift
    return jnp.where(ok[None], vals * LOG2E, NEG_LOGIT)


def _toeplitz_t(w_row, n_keys, n_q, key_step):
    x = jnp.broadcast_to(w_row, (n_keys, w_row.shape[1]))
    return pltpu.roll(x, 0, 1, stride=key_step, stride_axis=0)[:, 0:n_q]


def _bias_tables_kernel(wc_ref, ws_ref, ww_ref, bc_ref, bs_ref, bw_ref):
    nb, t = bc_ref.shape[2], bc_ref.shape[3]
    bc_ref[0, 0] = _toeplitz_t(wc_ref[0], nb, t, CMP_STRIDE)
    bs_ref[0, 0] = _toeplitz_t(ws_ref[0], bs_ref.shape[2], bs_ref.shape[3], 1)
    tk, tq = bw_ref.shape[3], bw_ref.shape[4]
    n_w = ww_ref.shape[1]
    for d in range(n_w):
        bw_ref[0, d, 0] = _toeplitz_t(ww_ref[0, d:d + 1, :], tk, tq, 1)
    bw_ref[0, n_w, 0] = jnp.full((tk, tq), NEG_LOGIT, F32)


def _bias_tables(rel_bias, t):
    tq, tk = ATT_TQ, ATT_TK
    g_, hpg = NSA_KV_GROUPS, NSA_HPG
    rb = rel_bias.astype(F32)
    dmax = REL_MAX_DISTANCE + tq + tk
    by_dist = rb[_rel_bucket(jnp.arange(dmax, dtype=jnp.int32))].T
    far = rb[REL_BUCKETS - 1][:, None]
    big = 1 << 30
    nb = t // CMP_STRIDE
    wc = _toeplitz_vec(by_dist, -(CMP_BLOCK - 1), t, t, 0, big)[:, None, :]
    sq = min(SLC_TQ, t)
    r = sq // tk
    n_s = min(-(-(REL_MAX_DISTANCE + tk - 1) // tk) + r - 1, t // tk)
    wm = (n_s - 1) * tk + sq
    ws = _toeplitz_vec(by_dist, -(r - 1) * tk, wm, tk, 0, big, far)[:, None, :]
    n_w = min(-(-(WINDOW + tk - 1) // tq), t // tq)
    ww = jnp.stack([_toeplitz_vec(by_dist, d * tq, tq, tk, 0, WINDOW) for d in range(n_w)], axis=1)
    return pl.pallas_call(
        _bias_tables_kernel,
        grid=(NSA_HEADS,),
        in_specs=[
            pl.BlockSpec((1, 1, 2 * t), lambda h: (h, 0, 0)),
            pl.BlockSpec((1, 1, wm + tk), lambda h: (h, 0, 0)),
            pl.BlockSpec((1, n_w, tq + tk), lambda h: (h, 0, 0)),
        ],
        out_specs=[
            pl.BlockSpec((1, 1, nb, t), lambda h: (h // hpg, h % hpg, 0, 0)),
            pl.BlockSpec((1, 1, tk, wm), lambda h: (h // hpg, h % hpg, 0, 0)),
            pl.BlockSpec((1, n_w + 1, 1, tk, tq), lambda h: (h // hpg, 0, h % hpg, 0, 0)),
        ],
        out_shape=[
            jax.ShapeDtypeStruct((g_, hpg, nb, t), F32),
            jax.ShapeDtypeStruct((g_, hpg, tk, wm), F32),
            jax.ShapeDtypeStruct((g_, n_w + 1, hpg, tk, tq), F32),
        ],
        compiler_params=_cparams(("parallel",)),
        name="bias_tables",
    )(wc, ws, ww)


def _cover_t(t):
    nb = t // CMP_STRIDE
    n_cmp = (t - CMP_BLOCK) // CMP_STRIDE + 1
    n_slc = t // SLC_BLOCK
    cs = np.arange(nb) * CMP_STRIDE
    ss = np.arange(n_slc) * SLC_BLOCK
    cover = np.clip(np.minimum(cs[:, None] + CMP_BLOCK, ss[None, :] + SLC_BLOCK)
                    - np.maximum(cs[:, None], ss[None, :]), 0, None) / CMP_BLOCK
    cover[n_cmp:] = 0.0
    return jnp.asarray(cover.T, dtype=BF16)


def kernel(x, norm_gain, w_in, w_conv, b_igate, b_fgate, mlstm_norm_gain, cmp_k_pos, cmp_k_w1, cmp_k_w2,
           cmp_v_pos, cmp_v_w1, cmp_v_w2, rel_bias, w_out, final_norm_gain):
    bsz, t, d = x.shape
    assert d == D_MODEL and t % MLSTM_L == 0 and t % ATT_TQ == 0 and (t // CMP_STRIDE) % LANES == 0
    n = bsz * t
    x2d = x.reshape(n, d)

    w_main, w_gate = _w_prep(w_in.T)
    proj, gates = _in_proj(x2d, norm_gain.reshape(1, d).astype(F32), w_main, w_gate)

    g_rows = gates[:, :2 * MLSTM_HEADS].reshape(bsz, t, 2, MLSTM_HEADS).transpose(0, 3, 2, 1)
    y_m = _mlstm(proj, g_rows, w_conv.astype(F32), b_igate.astype(F32), b_fgate.astype(F32),
                 mlstm_norm_gain.reshape(1, MLSTM_WIDTH).astype(F32), bsz, t)

    dh = NSA_HEAD_DIM
    half = CMP_BLOCK // 2

    def w1cat(w1):
        return jnp.concatenate([w1[:half].reshape(half * dh, dh), w1[half:].reshape(half * dh, dh)], axis=1).astype(BF16)

    k_cmp, v_cmp_t = _compress(
        proj, w1cat(cmp_k_w1), cmp_k_w2.astype(BF16), cmp_k_pos.reshape(2, half * dh).astype(BF16),
        w1cat(cmp_v_w1), cmp_v_w2.astype(BF16), cmp_v_pos.reshape(2, half * dh).astype(BF16), bsz, t)

    bias_c, bias_s, bias_w = _bias_tables(rel_bias, t)
    gates_t = gates.T
    o_cmp, selbias_t = _cmp_attn(proj, gates_t, k_cmp, v_cmp_t, bias_c, _cover_t(t), bsz, t)
    o_slc = _slc_attn(proj, gates_t, selbias_t, bias_s, bsz, t)
    y_a = _win_attn(proj, gates_t, o_cmp, o_slc, bias_w, bsz, t)

    out = _out_proj(y_m, y_a, x2d, w_out.astype(BF16), final_norm_gain.reshape(1, d).astype(F32))
    return out.reshape(bsz, t, d)
```

```python
import functools
import math

import jax
import jax.numpy as jnp
import numpy as np
from jax import lax
from jax.experimental import pallas as pl
from jax.experimental.pallas import tpu as pltpu

F32 = jnp.float32
BF16 = jnp.bfloat16

D_MODEL = 4096
D_MIX = D_MODEL
MLSTM_WIDTH = D_MIX // 2
MLSTM_HEADS = 4
MLSTM_V_DIM = MLSTM_WIDTH // MLSTM_HEADS
MLSTM_QK_DIM = MLSTM_V_DIM // 2
MLSTM_QK_WIDTH = MLSTM_HEADS * MLSTM_QK_DIM
CONV_WIDTH = 4
NSA_WIDTH = D_MIX - MLSTM_WIDTH
NSA_HEAD_DIM = 128
NSA_HEADS = NSA_WIDTH // NSA_HEAD_DIM
NSA_KV_GROUPS = 4
NSA_HPG = NSA_HEADS // NSA_KV_GROUPS
NSA_KV_WIDTH = NSA_KV_GROUPS * NSA_HEAD_DIM
CMP_BLOCK = 32
CMP_STRIDE = 16
SLC_BLOCK = 64
SLC_TOP_N = 16
WINDOW = 512
FORCE_BONUS = 1000.0
REL_BUCKETS = 32
REL_MAX_EXACT = REL_BUCKETS // 2
REL_MAX_DISTANCE = 1024
RMS_EPS = 1e-6
NEG_LOGIT = -1e30

LANES = 128
VMEM_LIMIT_BYTES = 56 * 1024 * 1024

COL_MQ = 0
COL_MK = COL_MQ + MLSTM_QK_WIDTH
COL_MV = COL_MK + MLSTM_QK_WIDTH
COL_MO = COL_MV + MLSTM_WIDTH
COL_MZ = COL_MO + MLSTM_WIDTH
COL_AQ = COL_MZ + MLSTM_WIDTH
COL_CK = COL_AQ + NSA_WIDTH
COL_CV = COL_CK + NSA_KV_WIDTH
COL_SK = COL_CV + NSA_KV_WIDTH
COL_SV = COL_SK + NSA_KV_WIDTH
COL_WK = COL_SV + NSA_KV_WIDTH
COL_WV = COL_WK + NSA_KV_WIDTH
COL_AZ = COL_WV + NSA_KV_WIDTH
MAIN_COLS = COL_AZ + NSA_WIDTH
GATE_COLS = LANES
GATE_I = 0
GATE_F = MLSTM_HEADS
GATE_NSA = 2 * MLSTM_HEADS

INPROJ_TM = 512
INPROJ_TN = 1536
OUTPROJ_TM = 512
OUTPROJ_TN = 1024
MLSTM_L = 256
ATT_TQ = 256
ATT_TK = 256
WPREP_TR = 512
SLC_TQ = 512
CMP_TQ = 512
MASK_BIG = 2.0 ** 100


def _cparams(sem):
    return pltpu.CompilerParams(dimension_semantics=sem, vmem_limit_bytes=VMEM_LIMIT_BYTES)


def _sigmoid(x):
    return 0.5 * jnp.tanh(0.5 * x) + 0.5


def _silu(x):
    return x * _sigmoid(x)


def _dot(a, b):
    return jnp.dot(a, b, preferred_element_type=F32)


def _dot_nt(a, b):
    return lax.dot_general(a, b, (((1,), (1,)), ((), ())), preferred_element_type=F32)


W_OFF_I = COL_MZ + MLSTM_WIDTH
W_OFF_AQ = W_OFF_I + 2 * MLSTM_HEADS
W_OFF_GATE = W_OFF_AQ + NSA_WIDTH + 6 * NSA_KV_WIDTH
W_OFF_AZ = W_OFF_GATE + 3 * NSA_HEADS


def _wprep_kernel(w_ref, gi_ref, ga_ref, o_ref, g_ref):
    o_ref[...] = w_ref[...].astype(BF16)

    @pl.when(pl.program_id(0) == 0)
    def _():
        pad = jnp.zeros((GATE_COLS - gi_ref.shape[0] - ga_ref.shape[0], g_ref.shape[1]), F32)
        g_ref[...] = jnp.concatenate([gi_ref[...], ga_ref[...], pad], axis=0).astype(BF16)


def _w_prep(w_t):
    rows, d = w_t.shape
    tr = WPREP_TR
    nb_a, nb_b = W_OFF_I // tr, (W_OFF_GATE - W_OFF_AQ) // tr

    def src(m):
        skip = jnp.where(m >= nb_a, W_OFF_AQ - W_OFF_I, 0) + jnp.where(m >= nb_a + nb_b, W_OFF_AZ - W_OFF_GATE, 0)
        return pl.multiple_of(m * tr + skip, 8)

    rows_at = lambda start, size: pl.BlockSpec((pl.Element(size), pl.Element(d)), lambda m: (start, 0))
    return pl.pallas_call(
        _wprep_kernel,
        grid=(MAIN_COLS // tr,),
        in_specs=[
            pl.BlockSpec((pl.Element(tr), pl.Element(d)), lambda m: (src(m), 0)),
            rows_at(W_OFF_I, W_OFF_AQ - W_OFF_I),
            rows_at(W_OFF_GATE, W_OFF_AZ - W_OFF_GATE),
        ],
        out_specs=[pl.BlockSpec((tr, d), lambda m: (m, 0)), pl.BlockSpec((GATE_COLS, d), lambda m: (0, 0))],
        out_shape=[jax.ShapeDtypeStruct((MAIN_COLS, d), BF16), jax.ShapeDtypeStruct((GATE_COLS, d), BF16)],
        compiler_params=_cparams(("arbitrary",)),
        name="w_prep",
    )(w_t, w_t, w_t)


def _inproj_kernel(x_ref, gain_ref, w_ref, wg_ref, o_ref, og_ref, h_ref):
    @pl.when(pl.program_id(1) == 0)
    def _():
        x = x_ref[...]
        ms = jnp.mean(x * x, axis=-1, keepdims=True)
        h = (x * lax.rsqrt(ms + RMS_EPS) * gain_ref[...]).astype(BF16)
        h_ref[...] = h
        og_ref[...] = _dot_nt(h, wg_ref[...])

    o_ref[...] = _dot_nt(h_ref[...], w_ref[...]).astype(BF16)


def _in_proj(x2d, gain, w_main, w_gate):
    n, d = x2d.shape
    tm, tn = min(INPROJ_TM, n), INPROJ_TN
    return pl.pallas_call(
        _inproj_kernel,
        grid=(n // tm, MAIN_COLS // tn),
        in_specs=[
            pl.BlockSpec((tm, d), lambda i, j: (i, 0)),
            pl.BlockSpec((1, d), lambda i, j: (0, 0)),
            pl.BlockSpec((tn, d), lambda i, j: (j, 0)),
            pl.BlockSpec((GATE_COLS, d), lambda i, j: (0, 0)),
        ],
        out_specs=[
            pl.BlockSpec((tm, tn), lambda i, j: (i, j)),
            pl.BlockSpec((tm, GATE_COLS), lambda i, j: (i, 0)),
        ],
        out_shape=[
            jax.ShapeDtypeStruct((n, MAIN_COLS), BF16),
            jax.ShapeDtypeStruct((n, GATE_COLS), F32),
        ],
        scratch_shapes=[pltpu.VMEM((tm, d), BF16)],
        compiler_params=_cparams(("parallel", "arbitrary")),
        name="in_proj",
    )(x2d, gain, w_main, w_gate)


def _mlstm_kernel(bi_ref, bf_ref, q_ref, k_ref, v_ref, o_ref, z_ref, g_ref, wq_ref, wk_ref, ng_ref,
                  y_ref, qext, kext, c_st, n_st, m_st):
    L = MLSTM_L
    HIST = 8
    hd = pl.program_id(1)

    @pl.when(pl.program_id(2) == 0)
    def _():
        qext[0:HIST, :] = jnp.zeros((HIST, MLSTM_QK_DIM), F32)
        kext[0:HIST, :] = jnp.zeros((HIST, MLSTM_QK_DIM), F32)
        c_st[...] = jnp.zeros_like(c_st)
        n_st[...] = jnp.zeros_like(n_st)
        m_st[...] = jnp.zeros_like(m_st)

    qext[HIST:HIST + L, :] = q_ref[...].astype(F32)
    kext[HIST:HIST + L, :] = k_ref[...].astype(F32)

    def conv_silu(ext, w_ref):
        w = w_ref[...]
        y = ext[pl.ds(HIST, L), :] * w[CONV_WIDTH - 1:CONV_WIDTH, :]
        for s in range(1, CONV_WIDTH):
            y = y + ext[pl.ds(HIST - s, L), :] * w[CONV_WIDTH - 1 - s:CONV_WIDTH - s, :]
        return _silu(y)

    qc = conv_silu(qext, wq_ref)
    kc = conv_silu(kext, wk_ref) * (MLSTM_QK_DIM ** -0.5)
    qext[0:HIST, :] = qext[L:L + HIST, :]
    kext[0:HIST, :] = kext[L:L + HIST, :]

    g = g_ref[0, 0]
    i_row = g[0:1, :] + bi_ref[hd]
    f_row = g[1:2, :] + bf_ref[hd]
    lf_row = jnp.minimum(f_row, 0.0) - jnp.log(1.0 + jnp.exp(-jnp.abs(f_row)))

    rr = lax.broadcasted_iota(jnp.int32, (L, L), 0)
    cc = lax.broadcasted_iota(jnp.int32, (L, L), 1)
    upper = (rr <= cc).astype(F32)
    bcum_row = jnp.dot(jnp.broadcast_to(lf_row, (8, L)), upper, preferred_element_type=F32,
                       precision=lax.Precision.HIGHEST)[0:1, :]
    bcum_col = jnp.sum(jnp.where(rr == cc, bcum_row, 0.0), axis=1, keepdims=True)
    gsum = bcum_row[:, L - 1:L]
    m_prev = m_st[...]

    dlog = jnp.where(rr >= cc, bcum_col - bcum_row + i_row, -jnp.inf)
    m_inter = bcum_col + m_prev
    m_t = jnp.maximum(m_inter, jnp.max(dlog, axis=1, keepdims=True))
    dmat = jnp.exp(dlog - m_t)
    inter = jnp.exp(m_inter - m_t)

    qb = qc.astype(BF16)
    kct = kc.T
    vb = v_ref[...]
    s = _dot(qb, kct.astype(BF16)) * dmat
    c_prev = c_st[...]
    n_prev = n_st[...]
    num = _dot(s.astype(BF16), vb) + inter * _dot(qb, c_prev.astype(BF16))
    qn = jnp.sum(s, axis=1, keepdims=True) + inter * jnp.sum(qc * n_prev, axis=1, keepdims=True)
    hh = num / jnp.maximum(jnp.abs(qn), jnp.exp(-m_t))

    wlog = gsum - bcum_row + i_row
    m_next = jnp.maximum(gsum + m_prev, jnp.max(wlog, axis=1, keepdims=True))
    wts = jnp.exp(wlog - m_next)
    keep = jnp.exp(gsum + m_prev - m_next)
    c_st[...] = keep * c_prev + _dot((kct * wts).astype(BF16), vb)
    n_st[...] = keep * n_prev + _dot(jnp.broadcast_to(wts, (8, L)).astype(BF16), kc.astype(BF16))[0:1, :]
    m_st[...] = m_next

    hm = _sigmoid(o_ref[...].astype(F32)) * hh
    hm = hm * lax.rsqrt(jnp.mean(hm * hm, axis=-1, keepdims=True) + RMS_EPS)
    hm = hm * ng_ref[...]
    y_ref[...] = (hm * _silu(z_ref[...].astype(F32))).astype(BF16)


def _mlstm(proj, g_rows, w_conv, b_igate, b_fgate, norm_gain, bsz, t):
    L = MLSTM_L
    nc = t // L
    dqk, dv = MLSTM_QK_DIM, MLSTM_V_DIM
    row = lambda b, h, c: b * nc + c
    smem = pl.BlockSpec(memory_space=pltpu.SMEM)
    return pl.pallas_call(
        _mlstm_kernel,
        grid=(bsz, MLSTM_HEADS, nc),
        in_specs=[
            smem, smem,
            pl.BlockSpec((L, dqk), lambda b, h, c: (row(b, h, c), COL_MQ // dqk + h)),
            pl.BlockSpec((L, dqk), lambda b, h, c: (row(b, h, c), COL_MK // dqk + h)),
            pl.BlockSpec((L, dv), lambda b, h, c: (row(b, h, c), COL_MV // dv + h)),
            pl.BlockSpec((L, dv), lambda b, h, c: (row(b, h, c), COL_MO // dv + h)),
            pl.BlockSpec((L, dv), lambda b, h, c: (row(b, h, c), COL_MZ // dv + h)),
            pl.BlockSpec((1, 1, 2, L), lambda b, h, c: (b, h, 0, c)),
            pl.BlockSpec((CONV_WIDTH, dqk), lambda b, h, c: (0, h)),
            pl.BlockSpec((CONV_WIDTH, dqk), lambda b, h, c: (0, MLSTM_HEADS + h)),
            pl.BlockSpec((1, dv), lambda b, h, c: (0, h)),
        ],
        out_specs=pl.BlockSpec((L, dv), lambda b, h, c: (row(b, h, c), h)),
        out_shape=jax.ShapeDtypeStruct((bsz * t, MLSTM_WIDTH), BF16),
        scratch_shapes=[
            pltpu.VMEM((L + 8, dqk), F32),
            pltpu.VMEM((L + 8, dqk), F32),
            pltpu.VMEM((dqk, dv), F32),
            pltpu.VMEM((1, dqk), F32),
            pltpu.VMEM((1, 1), F32),
        ],
        compiler_params=_cparams(("parallel", "parallel", "arbitrary")),
        name="mlstm",
    )(b_igate, b_fgate, proj, proj, proj, proj, proj, g_rows, w_conv, w_conv, norm_gain)


def _compress_kernel(ck_ref, cv_ref, w1k_ref, w2k_ref, pk_ref, w1v_ref, w2v_ref, pv_ref,
                     ok_ref, ov_ref, xf, xcat):
    t = ck_ref.shape[0]
    nb = t // CMP_STRIDE
    dh = NSA_HEAD_DIM

    def one(src_ref, w1_ref, w2_ref, pos_ref):
        xf[...] = src_ref[...].astype(F32)
        for l in range(CMP_STRIDE):
            xcat[:, l * dh:(l + 1) * dh] = xf[pl.ds(l, nb, stride=CMP_STRIDE), :].astype(BF16)
        w1 = w1_ref[...]
        ab = _dot(xcat[...], w1)
        pp = _dot(pos_ref[...], w1)
        pos_term = pp[0:1, 0:dh] + pp[1:2, dh:2 * dh]
        second = pltpu.roll(ab[:, dh:2 * dh], nb - 1, 0)
        hid = _silu(ab[:, 0:dh] + second + pos_term)
        return _dot(hid.astype(BF16), w2_ref[...])

    ok_ref[0, 0] = one(ck_ref, w1k_ref, w2k_ref, pk_ref).astype(BF16)
    ov_ref[0, 0] = one(cv_ref, w1v_ref, w2v_ref, pv_ref).T.astype(BF16)


def _compress(proj, w1k, w2k, pk, w1v, w2v, pv, bsz, t):
    g_, dh = NSA_KV_GROUPS, NSA_HEAD_DIM
    nb = t // CMP_STRIDE
    full = lambda a: pl.BlockSpec(a.shape, lambda b, g: (0,) * a.ndim)
    k_spec = pl.BlockSpec((1, 1, nb, dh), lambda b, g: (b, g, 0, 0))
    vt_spec = pl.BlockSpec((1, 1, dh, nb), lambda b, g: (b, g, 0, 0))
    return pl.pallas_call(
        _compress_kernel,
        grid=(bsz, g_),
        in_specs=[
            pl.BlockSpec((t, dh), lambda b, g: (b, COL_CK // dh + g)),
            pl.BlockSpec((t, dh), lambda b, g: (b, COL_CV // dh + g)),
            full(w1k), full(w2k), full(pk), full(w1v), full(w2v), full(pv),
        ],
        out_specs=[k_spec, vt_spec],
        out_shape=[jax.ShapeDtypeStruct((bsz, g_, nb, dh), BF16), jax.ShapeDtypeStruct((bsz, g_, dh, nb), BF16)],
        scratch_shapes=[pltpu.VMEM((t, dh), F32), pltpu.VMEM((nb, CMP_STRIDE * dh), BF16)],
        compiler_params=_cparams(("parallel", "parallel")),
        name="compress",
    )(proj, proj, w1k, w2k, pk, w1v, w2v, pv)


LOG2E = math.log2(math.e)
VROWS = NSA_HEAD_DIM + 16


def _q_scaled(q_ref, hh):
    dh = NSA_HEAD_DIM
    return (q_ref[:, hh * dh:(hh + 1) * dh].astype(F32) * (dh ** -0.5 * LOG2E)).astype(BF16)


def _q_t(q_ref, hh):
    dh = NSA_HEAD_DIM
    return (q_ref[:, hh * dh:(hh + 1) * dh].astype(F32) * (dh ** -0.5 * LOG2E)).T.astype(BF16)


def _gate_row(gt_ref, g, hh, branch):
    row = GATE_NSA + 3 * (g * NSA_HPG + hh) + branch
    return _sigmoid(gt_ref[pl.ds(row, 1), :])


def _fill_vt(vt, v_ref):
    dh = NSA_HEAD_DIM
    vt[0:dh, :] = v_ref[...].astype(F32).T.astype(BF16)
    vt[dh:VROWS, :] = jnp.ones((VROWS - dh, vt.shape[1]), BF16)


def _out_t(acc, gate_row):
    dh = NSA_HEAD_DIM
    return acc[0:dh, :] * (gate_row / acc[dh:dh + 1, :])


def _cmp_attn_kernel(q_ref, kc_ref, vct_ref, bias_ref, cov_ref, gate_ref, oc_ref, sel_ref, score_ref, *, n_sel):
    tq = q_ref.shape[0]
    dh = NSA_HEAD_DIM
    n_slc = cov_ref.shape[0]
    t0 = pl.program_id(1) * tq
    g = pl.program_id(0)
    kc = kc_ref[0, 0]
    vct = vct_ref[0, 0]
    heads = range(NSA_HPG)
    qk = [_dot_nt(kc, _q_scaled(q_ref, hh)) for hh in heads]
    ps = []
    for hh in heads:
        logit = qk[hh] + bias_ref[0, hh]
        m = jnp.max(logit, axis=0, keepdims=True)
        e = jnp.exp2(logit - m)
        inv = jnp.where(m > 0.5 * NEG_LOGIT, 1.0 / jnp.sum(e, axis=0, keepdims=True), 0.0)
        ps.append(e * inv)
    p_sum = (ps[0] + ps[1]) + (ps[2] + ps[3])
    ots = [_dot(vct, p.astype(BF16)) for p in ps]
    for hh in heads:
        oc_ref[0, hh] = (ots[hh] * _gate_row(gate_ref, g, hh, 0)).astype(BF16)

    cov = cov_ref[...]
    p_hi = p_sum.astype(BF16)
    r1 = p_sum - p_hi.astype(F32)
    p_mid = r1.astype(BF16)
    p_lo = (r1 - p_mid.astype(F32)).astype(BF16)
    st = _dot(cov, p_hi) + _dot(cov, p_mid) + _dot(cov, p_lo)

    jb = lax.broadcasted_iota(jnp.int32, (n_slc, tq), 0)
    cur = (t0 + lax.broadcasted_iota(jnp.int32, (n_slc, tq), 1)) // SLC_BLOCK
    valid = jb <= cur
    forced = (jb == 0) | (jb == cur) | (jb == cur - 1)
    score = jnp.where(valid, st + jnp.where(forced, FORCE_BONUS, 0.0), -1.0)
    sub = 8
    score_ref[...] = score
    groups = [score_ref[r:r + sub, :] for r in range(0, n_slc, sub)]
    ranks = [jnp.zeros((sub, tq), F32) for _ in groups]
    row_id = lax.broadcasted_iota(jnp.int32, (sub, tq), 0)
    for j2 in range(n_slc):
        row = score_ref[j2:j2 + 1, :]
        for gi, sc in enumerate(groups):
            r0 = gi * sub
            if r0 > j2:
                inc = jnp.where(row >= sc, 1.0, 0.0)
            elif r0 + sub - 1 <= j2:
                inc = jnp.where(row > sc, 1.0, 0.0)
            else:
                inc = jnp.where(row_id > j2 - r0, jnp.where(row >= sc, 1.0, 0.0), jnp.where(row > sc, 1.0, 0.0))
            ranks[gi] = ranks[gi] + inc
    for gi, rk in enumerate(ranks):
        score_ref[gi * sub:(gi + 1) * sub, :] = rk
    sel = valid & (score_ref[...] < n_sel)
    sel_ref[0, 0, 0:n_slc, :] = jnp.where(sel, 0.0, -MASK_BIG).astype(BF16)
    if n_slc < LANES:
        sel_ref[0, 0, n_slc:LANES, :] = jnp.zeros((LANES - n_slc, tq), BF16)


def _cmp_attn(proj, gates, k_cmp, v_cmp_t, bias_c, cover_t, bsz, t):
    g_, dh, tq = NSA_KV_GROUPS, NSA_HEAD_DIM, CMP_TQ
    nt = t // tq
    nb = t // CMP_STRIDE
    n_slc = t // SLC_BLOCK
    gw = NSA_HPG * dh
    return pl.pallas_call(
        functools.partial(_cmp_attn_kernel, n_sel=min(SLC_TOP_N, n_slc)),
        grid=(g_, nt, bsz),
        in_specs=[
            pl.BlockSpec((tq, gw), lambda g, i, b: (b * nt + i, COL_AQ // gw + g)),
            pl.BlockSpec((1, 1, nb, dh), lambda g, i, b: (b, g, 0, 0)),
            pl.BlockSpec((1, 1, dh, nb), lambda g, i, b: (b, g, 0, 0)),
            pl.BlockSpec((1, NSA_HPG, nb, tq), lambda g, i, b: (g, 0, 0, i)),
            pl.BlockSpec((n_slc, nb), lambda g, i, b: (0, 0)),
            pl.BlockSpec((GATE_COLS, tq), lambda g, i, b: (0, b * nt + i)),
        ],
        out_specs=[
            pl.BlockSpec((1, NSA_HPG, dh, tq), lambda g, i, b: (b, g, 0, i)),
            pl.BlockSpec((1, 1, LANES, tq), lambda g, i, b: (b, g, 0, i)),
        ],
        out_shape=[
            jax.ShapeDtypeStruct((bsz, NSA_HEADS, dh, t), BF16),
            jax.ShapeDtypeStruct((bsz, g_, LANES, t), BF16),
        ],
        scratch_shapes=[pltpu.VMEM((n_slc, tq), F32)],
        compiler_params=_cparams(("parallel", "parallel", "parallel")),
        name="cmp_attn",
    )(proj, k_cmp, v_cmp_t, bias_c, cover_t, gates)


def _slc_attn_kernel(q_ref, sbt_ref, k_ref, v_ref, bias_ref, gate_ref, os_ref,
                     kaug, vt, qt, s_buf, p_buf, a_buf, m_ref, acc_ref, *, n_near):
    tq, tk, dh = SLC_TQ, ATT_TK, NSA_HEAD_DIM
    r = tq // tk
    t = k_ref.shape[0]
    g = pl.program_id(1)
    i = pl.program_id(2)
    heads = range(NSA_HPG)

    @pl.when(i == 0)
    def _():
        kaug[:, 0:dh] = k_ref[...]
        blk = lax.broadcasted_iota(jnp.int32, (t, LANES), 0) // SLC_BLOCK
        lane = lax.broadcasted_iota(jnp.int32, (t, LANES), 1)
        kaug[:, dh:dh + LANES] = jnp.where(blk == lane, 1.0, 0.0).astype(BF16)
        _fill_vt(vt, v_ref)

    sbt = sbt_ref[0, 0]
    for hh in heads:
        qt[hh, 0:dh, :] = _q_t(q_ref, hh)
        qt[hh, dh:dh + LANES, :] = sbt

    def ktile(j):
        return kaug[pl.ds(pl.multiple_of(j * tk, tk), tk), :]

    def vtile(j):
        return vt[:, pl.ds(pl.multiple_of(j * tk, tk), tk)]

    m_ref[...] = jnp.full(m_ref.shape, NEG_LOGIT, F32)
    acc_ref[...] = jnp.zeros_like(acc_ref)
    p_buf[...] = jnp.zeros_like(p_buf)
    a_buf[...] = jnp.ones_like(a_buf)
    k0 = ktile(0)
    for hh in heads:
        s_buf[hh] = _dot(k0, qt[hh])

    last = (i + 1) * r - 1

    def stage(j, bias_fn):
        v_prev = vtile(jnp.maximum(j - 1, 0))
        k_next = ktile(jnp.minimum(j + 1, last))
        pv = [_dot(v_prev, p_buf[hh]) for hh in heads]
        s_next = [_dot(k_next, qt[hh]) for hh in heads]
        for hh in heads:
            acc_ref[hh] = a_buf[hh] * acc_ref[hh] + pv[hh]
        for hh in heads:
            s = s_buf[hh]
            if bias_fn is not None:
                s = s + bias_fn(hh)
            m_prev = m_ref[hh]
            m_new = jnp.maximum(m_prev, jnp.max(s, axis=0, keepdims=True))
            a_buf[hh] = jnp.exp2(m_prev - m_new)
            p_buf[hh] = jnp.exp2(s - m_new).astype(BF16)
            m_ref[hh] = m_new
        for hh in heads:
            s_buf[hh] = s_next[hh]

    n_far = jnp.maximum(i * r + r - n_near, 0)

    def far_body(j, c):
        stage(j, None)
        return c

    lax.fori_loop(0, n_far, far_body, 0)

    def near_body(j, c):
        off = pl.multiple_of((i * r + (r - 1) - j) * tk, tk)
        stage(j, lambda hh: bias_ref[0, hh, :, pl.ds(off, tq)])
        return c

    lax.fori_loop(n_far, last + 1, near_body, 0)

    v_last = vtile(last)
    pv = [_dot(v_last, p_buf[hh]) for hh in heads]
    for hh in heads:
        acc = a_buf[hh] * acc_ref[hh] + pv[hh]
        os_ref[0, hh] = _out_t(acc, _gate_row(gate_ref, g, hh, 1)).astype(BF16)


def _slc_attn(proj, gates, selbias_t, bias_s, bsz, t):
    g_, dh, tq, tk = NSA_KV_GROUPS, NSA_HEAD_DIM, SLC_TQ, ATT_TK
    nt = t // tq
    gw = NSA_HPG * dh
    n_near = (bias_s.shape[3] - tq) // tk + 1
    return pl.pallas_call(
        functools.partial(_slc_attn_kernel, n_near=n_near),
        grid=(bsz, g_, nt),
        in_specs=[
            pl.BlockSpec((tq, gw), lambda b, g, i: (b * nt + i, COL_AQ // gw + g)),
            pl.BlockSpec((1, 1, LANES, tq), lambda b, g, i: (b, g, 0, i)),
            pl.BlockSpec((t, dh), lambda b, g, i: (b, COL_SK // dh + g)),
            pl.BlockSpec((t, dh), lambda b, g, i: (b, COL_SV // dh + g)),
            pl.BlockSpec((1, NSA_HPG, tk, bias_s.shape[3]), lambda b, g, i: (g, 0, 0, 0)),
            pl.BlockSpec((GATE_COLS, tq), lambda b, g, i: (0, b * nt + i)),
        ],
        out_specs=pl.BlockSpec((1, NSA_HPG, dh, tq), lambda b, g, i: (b, g, 0, i)),
        out_shape=jax.ShapeDtypeStruct((bsz, NSA_HEADS, dh, t), BF16),
        scratch_shapes=[
            pltpu.VMEM((t, dh + LANES), BF16),
            pltpu.VMEM((VROWS, t), BF16),
            pltpu.VMEM((NSA_HPG, dh + LANES, tq), BF16),
            pltpu.VMEM((NSA_HPG, tk, tq), F32),
            pltpu.VMEM((NSA_HPG, tk, tq), BF16),
            pltpu.VMEM((NSA_HPG, 1, tq), F32),
            pltpu.VMEM((NSA_HPG, 1, tq), F32),
            pltpu.VMEM((NSA_HPG, VROWS, tq), F32),
        ],
        compiler_params=_cparams(("parallel", "parallel", "arbitrary")),
        name="slc_attn",
    )(proj, selbias_t, proj, proj, bias_s, gates)


def _win_attn_kernel(q_ref, k_ref, v_ref, bias_ref, gate_ref, oc_ref, os_ref, z_ref, ya_ref, vt, *, n_near):
    tq, tk, dh = ATT_TQ, ATT_TK, NSA_HEAD_DIM
    g = pl.program_id(1)
    i = pl.program_id(2)

    @pl.when(i == 0)
    def _():
        _fill_vt(vt, v_ref)

    offs = [pl.multiple_of(jnp.maximum(i - d, 0) * tk, tk) for d in range(n_near)]
    d_eff = [jnp.where(i >= d, d, n_near) for d in range(n_near)]
    k_tiles = [k_ref[pl.ds(off, tk), :] for off in offs]
    v_tiles = [vt[:, pl.ds(off, tk)] for off in offs]
    qs = [_q_scaled(q_ref, hh) for hh in range(NSA_HPG)]
    ss = [[_dot_nt(k_tiles[d], qs[hh]) for d in range(n_near)] for hh in range(NSA_HPG)]
    ps = []
    for hh in range(NSA_HPG):
        s = [ss[hh][d] + bias_ref[0, d_eff[d], hh] for d in range(n_near)]
        m = functools.reduce(jnp.maximum, [jnp.max(x, axis=0, keepdims=True) for x in s])
        ps.append([jnp.exp2(x - m).astype(BF16) for x in s])
    pvs = [[_dot(v_tiles[d], ps[hh][d]) for d in range(n_near)] for hh in range(NSA_HPG)]
    for hh in range(NSA_HPG):
        cols = slice(hh * dh, (hh + 1) * dh)
        o = _out_t(functools.reduce(lambda x, y: x + y, pvs[hh]), _gate_row(gate_ref, g, hh, 2))
        o = o + oc_ref[0, hh].astype(F32) + os_ref[0, hh].astype(F32)
        ya_ref[:, cols] = (o.T * _silu(z_ref[:, cols].astype(F32))).astype(BF16)


def _win_attn(proj, gates, o_cmp, o_slc, bias_w, bsz, t):
    g_, dh, tq, tk = NSA_KV_GROUPS, NSA_HEAD_DIM, ATT_TQ, ATT_TK
    nt = t // tq
    gw = NSA_HPG * dh
    n_near = bias_w.shape[1] - 1
    return pl.pallas_call(
        functools.partial(_win_attn_kernel, n_near=n_near),
        grid=(bsz, g_, nt),
        in_specs=[
            pl.BlockSpec((tq, gw), lambda b, g, i: (b * nt + i, COL_AQ // gw + g)),
            pl.BlockSpec((t, dh), lambda b, g, i: (b, COL_WK // dh + g)),
            pl.BlockSpec((t, dh), lambda b, g, i: (b, COL_WV // dh + g)),
            pl.BlockSpec((1, n_near + 1, NSA_HPG, tk, tq), lambda b, g, i: (g, 0, 0, 0, 0)),
            pl.BlockSpec((GATE_COLS, tq), lambda b, g, i: (0, b * nt + i)),
            pl.BlockSpec((1, NSA_HPG, dh, tq), lambda b, g, i: (b, g, 0, i)),
            pl.BlockSpec((1, NSA_HPG, dh, tq), lambda b, g, i: (b, g, 0, i)),
            pl.BlockSpec((tq, gw), lambda b, g, i: (b * nt + i, COL_AZ // gw + g)),
        ],
        out_specs=pl.BlockSpec((tq, gw), lambda b, g, i: (b * nt + i, g)),
        out_shape=jax.ShapeDtypeStruct((bsz * t, NSA_WIDTH), BF16),
        scratch_shapes=[pltpu.VMEM((VROWS, t), BF16)],
        compiler_params=_cparams(("parallel", "parallel", "arbitrary")),
        name="win_attn",
    )(proj, proj, proj, bias_w, gates, o_cmp, o_slc, proj)


def _outproj_kernel(ym_ref, ya_ref, x_ref, w1_ref, w2_ref, gain_ref, o_ref, rows):
    j = pl.program_id(1)
    nj = rows.shape[0]
    tn = rows.shape[2]
    rows[j] = x_ref[...] + _dot(ym_ref[...], w1_ref[...]) + _dot(ya_ref[...], w2_ref[...])

    @pl.when(j == nj - 1)
    def _():
        ss = None
        for jj in range(nj):
            y = rows[jj]
            part = jnp.sum(y * y, axis=-1, keepdims=True)
            ss = part if ss is None else ss + part
        inv = lax.rsqrt(ss / (nj * tn) + RMS_EPS)
        for jj in range(nj):
            o_ref[:, jj * tn:(jj + 1) * tn] = rows[jj] * inv * gain_ref[:, jj * tn:(jj + 1) * tn]


def _out_proj(y_m, y_a, x2d, w_out, gain):
    n, d = x2d.shape
    tm, tn = min(OUTPROJ_TM, n), OUTPROJ_TN
    nj = d // tn
    return pl.pallas_call(
        _outproj_kernel,
        grid=(n // tm, nj),
        in_specs=[
            pl.BlockSpec((tm, MLSTM_WIDTH), lambda i, j: (i, 0)),
            pl.BlockSpec((tm, NSA_WIDTH), lambda i, j: (i, 0)),
            pl.BlockSpec((tm, tn), lambda i, j: (i, j)),
            pl.BlockSpec((MLSTM_WIDTH, tn), lambda i, j: (0, j)),
            pl.BlockSpec((NSA_WIDTH, tn), lambda i, j: (MLSTM_WIDTH // NSA_WIDTH, j)),
            pl.BlockSpec((1, d), lambda i, j: (0, 0)),
        ],
        out_specs=pl.BlockSpec((tm, d), lambda i, j: (i, 0)),
        out_shape=jax.ShapeDtypeStruct((n, d), F32),
        scratch_shapes=[pltpu.VMEM((nj, tm, tn), F32)],
        compiler_params=_cparams(("parallel", "arbitrary")),
        name="out_proj",
    )(y_m, y_a, x2d, w_out, w_out, gain)


def _rel_bucket(dist):
    n = jnp.maximum(dist, 0)
    nf = jnp.maximum(n, REL_MAX_EXACT).astype(jnp.float32)
    large = REL_MAX_EXACT + (jnp.log(nf / REL_MAX_EXACT) / math.log(REL_MAX_DISTANCE / REL_MAX_EXACT)
                             * (REL_BUCKETS - REL_MAX_EXACT)).astype(jnp.int32)
    large = jnp.minimum(large, REL_BUCKETS - 1)
    return jnp.where(n < REL_MAX_EXACT, n, large)


def _toeplitz_vec(by_dist, base, n_pos, n_neg, lo, hi, shift=None):
    w = n_pos + n_neg
    c = np.arange(w)
    dist = np.where(c < n_pos, base + c, base - (w - c))
    ok = (dist >= lo) & (dist < hi)
    vals = by_dist[:, np.clip(dist, 0, by_dist.shape[1] - 1)]
    if shift is not None:
        vals = vals - shift
    return jnp.where(ok[None], vals * LOG2E, NEG_LOGIT)


def _toeplitz_t(w_row, n_keys, n_q, key_step):
    x = jnp.broadcast_to(w_row, (n_keys, w_row.shape[1]))
    return pltpu.roll(x, 0, 1, stride=key_step, stride_axis=0)[:, 0:n_q]


def _bias_tables_kernel(wc_ref, ws_ref, ww_ref, bc_ref, bs_ref, bw_ref):
    nb, t = bc_ref.shape[2], bc_ref.shape[3]
    bc_ref[0, 0] = _toeplitz_t(wc_ref[0], nb, t, CMP_STRIDE)
    bs_ref[0, 0] = _toeplitz_t(ws_ref[0], bs_ref.shape[2], bs_ref.shape[3], 1)
    tk, tq = bw_ref.shape[3], bw_ref.shape[4]
    n_w = ww_ref.shape[1]
    for d in range(n_w):
        bw_ref[0, d, 0] = _toeplitz_t(ww_ref[0, d:d + 1, :], tk, tq, 1)
    bw_ref[0, n_w, 0] = jnp.full((tk, tq), NEG_LOGIT, F32)


def _bias_tables(rel_bias, t):
    tq, tk = ATT_TQ, ATT_TK
    g_, hpg = NSA_KV_GROUPS, NSA_HPG
    rb = rel_bias.astype(F32)
    dmax = REL_MAX_DISTANCE + tq + tk
    by_dist = rb[_rel_bucket(jnp.arange(dmax, dtype=jnp.int32))].T
    far = rb[REL_BUCKETS - 1][:, None]
    big = 1 << 30
    nb = t // CMP_STRIDE
    wc = _toeplitz_vec(by_dist, -(CMP_BLOCK - 1), t, t, 0, big)[:, None, :]
    sq = min(SLC_TQ, t)
    r = sq // tk
    n_s = min(-(-(REL_MAX_DISTANCE + tk - 1) // tk) + r - 1, t // tk)
    wm = (n_s - 1) * tk + sq
    ws = _toeplitz_vec(by_dist, -(r - 1) * tk, wm, tk, 0, big, far)[:, None, :]
    n_w = min(-(-(WINDOW + tk - 1) // tq), t // tq)
    ww = jnp.stack([_toeplitz_vec(by_dist, d * tq, tq, tk, 0, WINDOW) for d in range(n_w)], axis=1)
    return pl.pallas_call(
        _bias_tables_kernel,
        grid=(NSA_HEADS,),
        in_specs=[
            pl.BlockSpec((1, 1, 2 * t), lambda h: (h, 0, 0)),
            pl.BlockSpec((1, 1, wm + tk), lambda h: (h, 0, 0)),
            pl.BlockSpec((1, n_w, tq + tk), lambda h: (h, 0, 0)),
        ],
        out_specs=[
            pl.BlockSpec((1, 1, nb, t), lambda h: (h // hpg, h % hpg, 0, 0)),
            pl.BlockSpec((1, 1, tk, wm), lambda h: (h // hpg, h % hpg, 0, 0)),
            pl.BlockSpec((1, n_w + 1, 1, tk, tq), lambda h: (h // hpg, 0, h % hpg, 0, 0)),
        ],
        out_shape=[
            jax.ShapeDtypeStruct((g_, hpg, nb, t), F32),
            jax.ShapeDtypeStruct((g_, hpg, tk, wm), F32),
            jax.ShapeDtypeStruct((g_, n_w + 1, hpg, tk, tq), F32),
        ],
        compiler_params=_cparams(("parallel",)),
        name="bias_tables",
    )(wc, ws, ww)


def _cover_t(t):
    nb = t // CMP_STRIDE
    n_cmp = (t - CMP_BLOCK) // CMP_STRIDE + 1
    n_slc = t // SLC_BLOCK
    cs = np.arange(nb) * CMP_STRIDE
    ss = np.arange(n_slc) * SLC_BLOCK
    cover = np.clip(np.minimum(cs[:, None] + CMP_BLOCK, ss[None, :] + SLC_BLOCK)
                    - np.maximum(cs[:, None], ss[None, :]), 0, None) / CMP_BLOCK
    cover[n_cmp:] = 0.0
    return jnp.asarray(cover.T, dtype=BF16)


def kernel(x, norm_gain, w_in, w_conv, b_igate, b_fgate, mlstm_norm_gain, cmp_k_pos, cmp_k_w1, cmp_k_w2,
           cmp_v_pos, cmp_v_w1, cmp_v_w2, rel_bias, w_out, final_norm_gain):
    bsz, t, d = x.shape
    assert d == D_MODEL and t % MLSTM_L == 0 and t % ATT_TQ == 0 and (t // CMP_STRIDE) % LANES == 0
    n = bsz * t
    x2d = x.reshape(n, d)

    w_main, w_gate = _w_prep(w_in.T)
    proj, gates = _in_proj(x2d, norm_gain.reshape(1, d).astype(F32), w_main, w_gate)

    g_rows = gates[:, :2 * MLSTM_HEADS].reshape(bsz, t, 2, MLSTM_HEADS).transpose(0, 3, 2, 1)
    y_m = _mlstm(proj, g_rows, w_conv.astype(F32), b_igate.astype(F32), b_fgate.astype(F32),
                 mlstm_norm_gain.reshape(1, MLSTM_WIDTH).astype(F32), bsz, t)

    dh = NSA_HEAD_DIM
    half = CMP_BLOCK // 2

    def w1cat(w1):
        return jnp.concatenate([w1[:half].reshape(half * dh, dh), w1[half:].reshape(half * dh, dh)], axis=1).astype(BF16)

    k_cmp, v_cmp_t = _compress(
        proj, w1cat(cmp_k_w1), cmp_k_w2.astype(BF16), cmp_k_pos.reshape(2, half * dh).astype(BF16),
        w1cat(cmp_v_w1), cmp_v_w2.astype(BF16), cmp_v_pos.reshape(2, half * dh).astype(BF16), bsz, t)

    bias_c, bias_s, bias_w = _bias_tables(rel_bias, t)
    gates_t = gates.T
    o_cmp, selbias_t = _cmp_attn(proj, gates_t, k_cmp, v_cmp_t, bias_c, _cover_t(t), bsz, t)
    o_slc = _slc_attn(proj, gates_t, selbias_t, bias_s, bsz, t)
    y_a = _win_attn(proj, gates_t, o_cmp, o_slc, bias_w, bsz, t)

    out = _out_proj(y_m, y_a, x2d, w_out.astype(BF16), final_norm_gain.reshape(1, d).astype(F32))
    return out.reshape(bsz, t, d)
```

```python
import functools
import math

import jax
import jax.numpy as jnp
import numpy as np
from jax import lax
from jax.experimental import pallas as pl
from jax.experimental.pallas import tpu as pltpu

F32 = jnp.float32
BF16 = jnp.bfloat16

D_MODEL = 4096
D_MIX = D_MODEL
MLSTM_WIDTH = D_MIX // 2
MLSTM_HEADS = 4
MLSTM_V_DIM = MLSTM_WIDTH // MLSTM_HEADS
MLSTM_QK_DIM = MLSTM_V_DIM // 2
MLSTM_QK_WIDTH = MLSTM_HEADS * MLSTM_QK_DIM
CONV_WIDTH = 4
NSA_WIDTH = D_MIX - MLSTM_WIDTH
NSA_HEAD_DIM = 128
NSA_HEADS = NSA_WIDTH // NSA_HEAD_DIM
NSA_KV_GROUPS = 4
NSA_HPG = NSA_HEADS // NSA_KV_GROUPS
NSA_KV_WIDTH = NSA_KV_GROUPS * NSA_HEAD_DIM
CMP_BLOCK = 32
CMP_STRIDE = 16
SLC_BLOCK = 64
SLC_TOP_N = 16
WINDOW = 512
FORCE_BONUS = 1000.0
REL_BUCKETS = 32
REL_MAX_EXACT = REL_BUCKETS // 2
REL_MAX_DISTANCE = 1024
RMS_EPS = 1e-6
NEG_LOGIT = -1e30

LANES = 128
VMEM_LIMIT_BYTES = 56 * 1024 * 1024

COL_MQ = 0
COL_MK = COL_MQ + MLSTM_QK_WIDTH
COL_MV = COL_MK + MLSTM_QK_WIDTH
COL_MO = COL_MV + MLSTM_WIDTH
COL_MZ = COL_MO + MLSTM_WIDTH
COL_AQ = COL_MZ + MLSTM_WIDTH
COL_CK = COL_AQ + NSA_WIDTH
COL_CV = COL_CK + NSA_KV_WIDTH
COL_SK = COL_CV + NSA_KV_WIDTH
COL_SV = COL_SK + NSA_KV_WIDTH
COL_WK = COL_SV + NSA_KV_WIDTH
COL_WV = COL_WK + NSA_KV_WIDTH
COL_AZ = COL_WV + NSA_KV_WIDTH
MAIN_COLS = COL_AZ + NSA_WIDTH
GATE_COLS = LANES
GATE_I = 0
GATE_F = MLSTM_HEADS
GATE_NSA = 2 * MLSTM_HEADS

INPROJ_TM = 512
INPROJ_TN = 1536
OUTPROJ_TM = 512
OUTPROJ_TN = 1024
MLSTM_L = 256
ATT_TQ = 256
ATT_TK = 256
WPREP_TR = 512
SLC_TQ = 512
CMP_TQ = 512
MASK_BIG = 2.0 ** 100


def _cparams(sem):
    return pltpu.CompilerParams(dimension_semantics=sem, vmem_limit_bytes=VMEM_LIMIT_BYTES)


def _sigmoid(x):
    return 0.5 * jnp.tanh(0.5 * x) + 0.5


def _silu(x):
    return x * _sigmoid(x)


def _dot(a, b):
    return jnp.dot(a, b, preferred_element_type=F32)


def _dot_nt(a, b):
    return lax.dot_general(a, b, (((1,), (1,)), ((), ())), preferred_element_type=F32)


W_OFF_I = COL_MZ + MLSTM_WIDTH
W_OFF_AQ = W_OFF_I + 2 * MLSTM_HEADS
W_OFF_GATE = W_OFF_AQ + NSA_WIDTH + 6 * NSA_KV_WIDTH
W_OFF_AZ = W_OFF_GATE + 3 * NSA_HEADS


def _wprep_kernel(w_ref, gi_ref, ga_ref, o_ref, g_ref):
    o_ref[...] = w_ref[...].astype(BF16)

    @pl.when(pl.program_id(0) == 0)
    def _():
        pad = jnp.zeros((GATE_COLS - gi_ref.shape[0] - ga_ref.shape[0], g_ref.shape[1]), F32)
        g_ref[...] = jnp.concatenate([gi_ref[...], ga_ref[...], pad], axis=0).astype(BF16)


def _w_prep(w_t):
    rows, d = w_t.shape
    tr = WPREP_TR
    nb_a, nb_b = W_OFF_I // tr, (W_OFF_GATE - W_OFF_AQ) // tr

    def src(m):
        skip = jnp.where(m >= nb_a, W_OFF_AQ - W_OFF_I, 0) + jnp.where(m >= nb_a + nb_b, W_OFF_AZ - W_OFF_GATE, 0)
        return pl.multiple_of(m * tr + skip, 8)

    rows_at = lambda start, size: pl.BlockSpec((pl.Element(size), pl.Element(d)), lambda m: (start, 0))
    return pl.pallas_call(
        _wprep_kernel,
        grid=(MAIN_COLS // tr,),
        in_specs=[
            pl.BlockSpec((pl.Element(tr), pl.Element(d)), lambda m: (src(m), 0)),
            rows_at(W_OFF_I, W_OFF_AQ - W_OFF_I),
            rows_at(W_OFF_GATE, W_OFF_AZ - W_OFF_GATE),
        ],
        out_specs=[pl.BlockSpec((tr, d), lambda m: (m, 0)), pl.BlockSpec((GATE_COLS, d), lambda m: (0, 0))],
        out_shape=[jax.ShapeDtypeStruct((MAIN_COLS, d), BF16), jax.ShapeDtypeStruct((GATE_COLS, d), BF16)],
        compiler_params=_cparams(("arbitrary",)),
        name="w_prep",
    )(w_t, w_t, w_t)


def _inproj_kernel(x_ref, gain_ref, w_ref, wg_ref, o_ref, og_ref, h_ref):
    @pl.when(pl.program_id(1) == 0)
    def _():
        x = x_ref[...]
        ms = jnp.mean(x * x, axis=-1, keepdims=True)
        h = (x * lax.rsqrt(ms + RMS_EPS) * gain_ref[...]).astype(BF16)
        h_ref[...] = h
        og_ref[...] = _dot_nt(h, wg_ref[...])

    o_ref[...] = _dot_nt(h_ref[...], w_ref[...]).astype(BF16)


def _in_proj(x2d, gain, w_main, w_gate):
    n, d = x2d.shape
    tm, tn = min(INPROJ_TM, n), INPROJ_TN
    return pl.pallas_call(
        _inproj_kernel,
        grid=(n // tm, MAIN_COLS // tn),
        in_specs=[
            pl.BlockSpec((tm, d), lambda i, j: (i, 0)),
            pl.BlockSpec((1, d), lambda i, j: (0, 0)),
            pl.BlockSpec((tn, d), lambda i, j: (j, 0)),
            pl.BlockSpec((GATE_COLS, d), lambda i, j: (0, 0)),
        ],
        out_specs=[
            pl.BlockSpec((tm, tn), lambda i, j: (i, j)),
            pl.BlockSpec((tm, GATE_COLS), lambda i, j: (i, 0)),
        ],
        out_shape=[
            jax.ShapeDtypeStruct((n, MAIN_COLS), BF16),
            jax.ShapeDtypeStruct((n, GATE_COLS), F32),
        ],
        scratch_shapes=[pltpu.VMEM((tm, d), BF16)],
        compiler_params=_cparams(("parallel", "arbitrary")),
        name="in_proj",
    )(x2d, gain, w_main, w_gate)


def _mlstm_kernel(bi_ref, bf_ref, q_ref, k_ref, v_ref, o_ref, z_ref, g_ref, wq_ref, wk_ref, ng_ref,
                  y_ref, qext, kext, c_st, n_st, m_st):
    L = MLSTM_L
    HIST = 8
    dqk, dv = MLSTM_QK_DIM, MLSTM_V_DIM

    @pl.when(pl.program_id(1) == 0)
    def _():
        qext[:, 0:HIST, :] = jnp.zeros((MLSTM_HEADS, HIST, dqk), F32)
        kext[:, 0:HIST, :] = jnp.zeros((MLSTM_HEADS, HIST, dqk), F32)
        c_st[...] = jnp.zeros_like(c_st)
        n_st[...] = jnp.zeros_like(n_st)
        m_st[...] = jnp.zeros_like(m_st)

    rr = lax.broadcasted_iota(jnp.int32, (L, L), 0)
    cc = lax.broadcasted_iota(jnp.int32, (L, L), 1)
    upper = (rr <= cc).astype(F32)

    for hd in range(MLSTM_HEADS):
        qcols = slice(hd * dqk, (hd + 1) * dqk)
        vcols = slice(hd * dv, (hd + 1) * dv)
        qext[hd, HIST:HIST + L, :] = q_ref[:, qcols].astype(F32)
        kext[hd, HIST:HIST + L, :] = k_ref[:, qcols].astype(F32)

        def conv_silu(ext, w_ref):
            w = w_ref[:, qcols]
            y = ext[hd, pl.ds(HIST, L), :] * w[CONV_WIDTH - 1:CONV_WIDTH, :]
            for s in range(1, CONV_WIDTH):
                y = y + ext[hd, pl.ds(HIST - s, L), :] * w[CONV_WIDTH - 1 - s:CONV_WIDTH - s, :]
            return _silu(y)

        qc = conv_silu(qext, wq_ref)
        kc = conv_silu(kext, wk_ref) * (dqk ** -0.5)
        qext[hd, 0:HIST, :] = qext[hd, L:L + HIST, :]
        kext[hd, 0:HIST, :] = kext[hd, L:L + HIST, :]

        g = g_ref[0, hd]
        i_row = g[0:1, :] + bi_ref[hd]
        f_row = g[1:2, :] + bf_ref[hd]
        lf_row = jnp.minimum(f_row, 0.0) - jnp.log(1.0 + jnp.exp(-jnp.abs(f_row)))

        bcum_row = jnp.dot(jnp.broadcast_to(lf_row, (8, L)), upper, preferred_element_type=F32,
                           precision=lax.Precision.HIGHEST)[0:1, :]
        bcum_col = jnp.sum(jnp.where(rr == cc, bcum_row, 0.0), axis=1, keepdims=True)
        gsum = bcum_row[:, L - 1:L]
        m_prev = m_st[hd]

        dlog = jnp.where(rr >= cc, bcum_col - bcum_row + i_row, -jnp.inf)
        m_inter = bcum_col + m_prev
        m_t = jnp.maximum(m_inter, jnp.max(dlog, axis=1, keepdims=True))
        dmat = jnp.exp(dlog - m_t)
        inter = jnp.exp(m_inter - m_t)

        qb = qc.astype(BF16)
        kct = kc.T
        vb = v_ref[:, vcols]
        s = _dot(qb, kct.astype(BF16)) * dmat
        c_prev = c_st[hd]
        n_prev = n_st[hd]
        num = _dot(s.astype(BF16), vb) + inter * _dot(qb, c_prev.astype(BF16))
        qn = jnp.sum(s, axis=1, keepdims=True) + inter * jnp.sum(qc * n_prev, axis=1, keepdims=True)
        hh = num / jnp.maximum(jnp.abs(qn), jnp.exp(-m_t))

        wlog = gsum - bcum_row + i_row
        m_next = jnp.maximum(gsum + m_prev, jnp.max(wlog, axis=1, keepdims=True))
        wts = jnp.exp(wlog - m_next)
        keep = jnp.exp(gsum + m_prev - m_next)
        c_st[hd] = keep * c_prev + _dot((kct * wts).astype(BF16), vb)
        n_st[hd] = keep * n_prev + _dot(jnp.broadcast_to(wts, (8, L)).astype(BF16), kc.astype(BF16))[0:1, :]
        m_st[hd] = m_next

        hm = _sigmoid(o_ref[:, vcols].astype(F32)) * hh
        hm = hm * lax.rsqrt(jnp.mean(hm * hm, axis=-1, keepdims=True) + RMS_EPS)
        hm = hm * ng_ref[:, vcols]
        y_ref[:, vcols] = (hm * _silu(z_ref[:, vcols].astype(F32))).astype(BF16)


def _mlstm(proj, g_rows, w_conv, b_igate, b_fgate, norm_gain, bsz, t):
    L = MLSTM_L
    nc = t // L
    nh, dqk, dv = MLSTM_HEADS, MLSTM_QK_DIM, MLSTM_V_DIM
    row = lambda b, c: b * nc + c
    smem = pl.BlockSpec(memory_space=pltpu.SMEM)
    return pl.pallas_call(
        _mlstm_kernel,
        grid=(bsz, nc),
        in_specs=[
            smem, smem,
            pl.BlockSpec((L, MLSTM_QK_WIDTH), lambda b, c: (row(b, c), COL_MQ // MLSTM_QK_WIDTH)),
            pl.BlockSpec((L, MLSTM_QK_WIDTH), lambda b, c: (row(b, c), COL_MK // MLSTM_QK_WIDTH)),
            pl.BlockSpec((L, MLSTM_WIDTH), lambda b, c: (row(b, c), COL_MV // MLSTM_WIDTH)),
            pl.BlockSpec((L, MLSTM_WIDTH), lambda b, c: (row(b, c), COL_MO // MLSTM_WIDTH)),
            pl.BlockSpec((L, MLSTM_WIDTH), lambda b, c: (row(b, c), COL_MZ // MLSTM_WIDTH)),
            pl.BlockSpec((1, nh, 2, L), lambda b, c: (b, 0, 0, c)),
            pl.BlockSpec((CONV_WIDTH, MLSTM_QK_WIDTH), lambda b, c: (0, 0)),
            pl.BlockSpec((CONV_WIDTH, MLSTM_QK_WIDTH), lambda b, c: (0, 1)),
            pl.BlockSpec((1, MLSTM_WIDTH), lambda b, c: (0, 0)),
        ],
        out_specs=pl.BlockSpec((L, MLSTM_WIDTH), lambda b, c: (row(b, c), 0)),
        out_shape=jax.ShapeDtypeStruct((bsz * t, MLSTM_WIDTH), BF16),
        scratch_shapes=[
            pltpu.VMEM((nh, L + 8, dqk), F32),
            pltpu.VMEM((nh, L + 8, dqk), F32),
            pltpu.VMEM((nh, dqk, dv), F32),
            pltpu.VMEM((nh, 1, dqk), F32),
            pltpu.VMEM((nh, 1, 1), F32),
        ],
        compiler_params=_cparams(("parallel", "arbitrary")),
        name="mlstm",
    )(b_igate, b_fgate, proj, proj, proj, proj, proj, g_rows, w_conv, w_conv, norm_gain)


def _compress_kernel(ck_ref, cv_ref, w1k_ref, w2k_ref, pk_ref, w1v_ref, w2v_ref, pv_ref,
                     ok_ref, ov_ref, xf, xcat):
    t = ck_ref.shape[0]
    nb = t // CMP_STRIDE
    dh = NSA_HEAD_DIM

    def one(src_ref, w1_ref, w2_ref, pos_ref):
        xf[...] = src_ref[...].astype(F32)
        for l in range(CMP_STRIDE):
            xcat[:, l * dh:(l + 1) * dh] = xf[pl.ds(l, nb, stride=CMP_STRIDE), :].astype(BF16)
        w1 = w1_ref[...]
        ab = _dot(xcat[...], w1)
        pp = _dot(pos_ref[...], w1)
        pos_term = pp[0:1, 0:dh] + pp[1:2, dh:2 * dh]
        second = pltpu.roll(ab[:, dh:2 * dh], nb - 1, 0)
        hid = _silu(ab[:, 0:dh] + second + pos_term)
        return _dot(hid.astype(BF16), w2_ref[...])

    ok_ref[0, 0] = one(ck_ref, w1k_ref, w2k_ref, pk_ref).astype(BF16)
    ov_ref[0, 0] = one(cv_ref, w1v_ref, w2v_ref, pv_ref).T.astype(BF16)


def _compress(proj, w1k, w2k, pk, w1v, w2v, pv, bsz, t):
    g_, dh = NSA_KV_GROUPS, NSA_HEAD_DIM
    nb = t // CMP_STRIDE
    full = lambda a: pl.BlockSpec(a.shape, lambda b, g: (0,) * a.ndim)
    k_spec = pl.BlockSpec((1, 1, nb, dh), lambda b, g: (b, g, 0, 0))
    vt_spec = pl.BlockSpec((1, 1, dh, nb), lambda b, g: (b, g, 0, 0))
    return pl.pallas_call(
        _compress_kernel,
        grid=(bsz, g_),
        in_specs=[
            pl.BlockSpec((t, dh), lambda b, g: (b, COL_CK // dh + g)),
            pl.BlockSpec((t, dh), lambda b, g: (b, COL_CV // dh + g)),
            full(w1k), full(w2k), full(pk), full(w1v), full(w2v), full(pv),
        ],
        out_specs=[k_spec, vt_spec],
        out_shape=[jax.ShapeDtypeStruct((bsz, g_, nb, dh), BF16), jax.ShapeDtypeStruct((bsz, g_, dh, nb), BF16)],
        scratch_shapes=[pltpu.VMEM((t, dh), F32), pltpu.VMEM((nb, CMP_STRIDE * dh), BF16)],
        compiler_params=_cparams(("parallel", "parallel")),
        name="compress",
    )(proj, proj, w1k, w2k, pk, w1v, w2v, pv)


LOG2E = math.log2(math.e)
VROWS = NSA_HEAD_DIM + 16


def _q_scaled(q_ref, hh):
    dh = NSA_HEAD_DIM
    return (q_ref[:, hh * dh:(hh + 1) * dh].astype(F32) * (dh ** -0.5 * LOG2E)).astype(BF16)


def _q_t(q_ref, hh):
    dh = NSA_HEAD_DIM
    return (q_ref[:, hh * dh:(hh + 1) * dh].astype(F32) * (dh ** -0.5 * LOG2E)).T.astype(BF16)


def _gate_row(gt_ref, g, hh, branch):
    row = GATE_NSA + 3 * (g * NSA_HPG + hh) + branch
    return _sigmoid(gt_ref[pl.ds(row, 1), :])


def _fill_vt(vt, v_ref):
    dh = NSA_HEAD_DIM
    vt[0:dh, :] = v_ref[...].astype(F32).T.astype(BF16)
    vt[dh:VROWS, :] = jnp.ones((VROWS - dh, vt.shape[1]), BF16)


def _out_t(acc, gate_row):
    dh = NSA_HEAD_DIM
    return acc[0:dh, :] * (gate_row / acc[dh:dh + 1, :])


def _cmp_attn_kernel(q_ref, kc_ref, vct_ref, bias_ref, cov_ref, gate_ref, oc_ref, sel_ref, score_ref, *, n_sel):
    tq = q_ref.shape[0]
    dh = NSA_HEAD_DIM
    n_slc = cov_ref.shape[0]
    t0 = pl.program_id(1) * tq
    g = pl.program_id(0)
    kc = kc_ref[0, 0]
    vct = vct_ref[0, 0]
    heads = range(NSA_HPG)
    qk = [_dot_nt(kc, _q_scaled(q_ref, hh)) for hh in heads]
    ps = []
    for hh in heads:
        logit = qk[hh] + bias_ref[0, hh]
        m = jnp.max(logit, axis=0, keepdims=True)
        e = jnp.exp2(logit - m)
        inv = jnp.where(m > 0.5 * NEG_LOGIT, 1.0 / jnp.sum(e, axis=0, keepdims=True), 0.0)
        ps.append(e * inv)
    p_sum = (ps[0] + ps[1]) + (ps[2] + ps[3])
    ots = [_dot(vct, p.astype(BF16)) for p in ps]
    for hh in heads:
        oc_ref[0, hh] = (ots[hh] * _gate_row(gate_ref, g, hh, 0)).astype(BF16)

    cov = cov_ref[...]
    p_hi = p_sum.astype(BF16)
    r1 = p_sum - p_hi.astype(F32)
    p_mid = r1.astype(BF16)
    p_lo = (r1 - p_mid.astype(F32)).astype(BF16)
    st = _dot(cov, p_hi) + _dot(cov, p_mid) + _dot(cov, p_lo)

    jb = lax.broadcasted_iota(jnp.int32, (n_slc, tq), 0)
    cur = (t0 + lax.broadcasted_iota(jnp.int32, (n_slc, tq), 1)) // SLC_BLOCK
    valid = jb <= cur
    forced = (jb == 0) | (jb == cur) | (jb == cur - 1)
    score = jnp.where(valid, st + jnp.where(forced, FORCE_BONUS, 0.0), -1.0)
    sub = 8
    score_ref[...] = score
    groups = [score_ref[r:r + sub, :] for r in range(0, n_slc, sub)]
    ranks = [jnp.zeros((sub, tq), F32) for _ in groups]
    row_id = lax.broadcasted_iota(jnp.int32, (sub, tq), 0)
    for j2 in range(n_slc):
        row = score_ref[j2:j2 + 1, :]
        for gi, sc in enumerate(groups):
            r0 = gi * sub
            if r0 > j2:
                inc = jnp.where(row >= sc, 1.0, 0.0)
            elif r0 + sub - 1 <= j2:
                inc = jnp.where(row > sc, 1.0, 0.0)
            else:
                inc = jnp.where(row_id > j2 - r0, jnp.where(row >= sc, 1.0, 0.0), jnp.where(row > sc, 1.0, 0.0))
            ranks[gi] = ranks[gi] + inc
    for gi, rk in enumerate(ranks):
        score_ref[gi * sub:(gi + 1) * sub, :] = rk
    sel = valid & (score_ref[...] < n_sel)
    sel_ref[0, 0, 0:n_slc, :] = jnp.where(sel, 0.0, -MASK_BIG).astype(BF16)
    if n_slc < LANES:
        sel_ref[0, 0, n_slc:LANES, :] = jnp.zeros((LANES - n_slc, tq), BF16)


def _cmp_attn(proj, gates, k_cmp, v_cmp_t, bias_c, cover_t, bsz, t):
    g_, dh, tq = NSA_KV_GROUPS, NSA_HEAD_DIM, CMP_TQ
    nt = t // tq
    nb = t // CMP_STRIDE
    n_slc = t // SLC_BLOCK
    gw = NSA_HPG * dh
    return pl.pallas_call(
        functools.partial(_cmp_attn_kernel, n_sel=min(SLC_TOP_N, n_slc)),
        grid=(g_, nt, bsz),
        in_specs=[
            pl.BlockSpec((tq, gw), lambda g, i, b: (b * nt + i, COL_AQ // gw + g)),
            pl.BlockSpec((1, 1, nb, dh), lambda g, i, b: (b, g, 0, 0)),
            pl.BlockSpec((1, 1, dh, nb), lambda g, i, b: (b, g, 0, 0)),
            pl.BlockSpec((1, NSA_HPG, nb, tq), lambda g, i, b: (g, 0, 0, i)),
            pl.BlockSpec((n_slc, nb), lambda g, i, b: (0, 0)),
            pl.BlockSpec((GATE_COLS, tq), lambda g, i, b: (0, b * nt + i)),
        ],
        out_specs=[
            pl.BlockSpec((1, NSA_HPG, dh, tq), lambda g, i, b: (b, g, 0, i)),
            pl.BlockSpec((1, 1, LANES, tq), lambda g, i, b: (b, g, 0, i)),
        ],
        out_shape=[
            jax.ShapeDtypeStruct((bsz, NSA_HEADS, dh, t), BF16),
            jax.ShapeDtypeStruct((bsz, g_, LANES, t), BF16),
        ],
        scratch_shapes=[pltpu.VMEM((n_slc, tq), F32)],
        compiler_params=_cparams(("parallel", "parallel", "parallel")),
        name="cmp_attn",
    )(proj, k_cmp, v_cmp_t, bias_c, cover_t, gates)


def _slc_attn_kernel(q_ref, sbt_ref, k_ref, v_ref, bias_ref, gate_ref, os_ref,
                     kaug, vt, qt, s_buf, p_buf, a_buf, m_ref, acc_ref, *, n_near):
    tq, tk, dh = SLC_TQ, ATT_TK, NSA_HEAD_DIM
    r = tq // tk
    t = k_ref.shape[0]
    g = pl.program_id(1)
    i = pl.program_id(2)
    heads = range(NSA_HPG)

    @pl.when(i == 0)
    def _():
        kaug[:, 0:dh] = k_ref[...]
        blk = lax.broadcasted_iota(jnp.int32, (t, LANES), 0) // SLC_BLOCK
        lane = lax.broadcasted_iota(jnp.int32, (t, LANES), 1)
        kaug[:, dh:dh + LANES] = jnp.where(blk == lane, 1.0, 0.0).astype(BF16)
        _fill_vt(vt, v_ref)

    sbt = sbt_ref[0, 0]
    for hh in heads:
        qt[hh, 0:dh, :] = _q_t(q_ref, hh)
        qt[hh, dh:dh + LANES, :] = sbt

    def ktile(j):
        return kaug[pl.ds(pl.multiple_of(j * tk, tk), tk), :]

    def vtile(j):
        return vt[:, pl.ds(pl.multiple_of(j * tk, tk), tk)]

    m_ref[...] = jnp.full(m_ref.shape, NEG_LOGIT, F32)
    acc_ref[...] = jnp.zeros_like(acc_ref)
    p_buf[...] = jnp.zeros_like(p_buf)
    a_buf[...] = jnp.ones_like(a_buf)
    k0 = ktile(0)
    for hh in heads:
        s_buf[hh] = _dot(k0, qt[hh])

    last = (i + 1) * r - 1

    def stage(j, bias_fn):
        v_prev = vtile(jnp.maximum(j - 1, 0))
        k_next = ktile(jnp.minimum(j + 1, last))
        pv = [_dot(v_prev, p_buf[hh]) for hh in heads]
        s_next = [_dot(k_next, qt[hh]) for hh in heads]
        for hh in heads:
            acc_ref[hh] = a_buf[hh] * acc_ref[hh] + pv[hh]
        for hh in heads:
            s = s_buf[hh]
            if bias_fn is not None:
                s = s + bias_fn(hh)
            m_prev = m_ref[hh]
            m_new = jnp.maximum(m_prev, jnp.max(s, axis=0, keepdims=True))
            a_buf[hh] = jnp.exp2(m_prev - m_new)
            p_buf[hh] = jnp.exp2(s - m_new).astype(BF16)
            m_ref[hh] = m_new
        for hh in heads:
            s_buf[hh] = s_next[hh]

    n_far = jnp.maximum(i * r + r - n_near, 0)

    def far_body(j, c):
        stage(j, None)
        return c

    lax.fori_loop(0, n_far, far_body, 0)

    def near_body(j, c):
        off = pl.multiple_of((i * r + (r - 1) - j) * tk, tk)
        stage(j, lambda hh: bias_ref[0, hh, :, pl.ds(off, tq)])
        return c

    lax.fori_loop(n_far, last + 1, near_body, 0)

    v_last = vtile(last)
    pv = [_dot(v_last, p_buf[hh]) for hh in heads]
    for hh in heads:
        acc = a_buf[hh] * acc_ref[hh] + pv[hh]
        os_ref[0, hh] = _out_t(acc, _gate_row(gate_ref, g, hh, 1)).astype(BF16)


def _slc_attn(proj, gates, selbias_t, bias_s, bsz, t):
    g_, dh, tq, tk = NSA_KV_GROUPS, NSA_HEAD_DIM, SLC_TQ, ATT_TK
    nt = t // tq
    gw = NSA_HPG * dh
    n_near = (bias_s.shape[3] - tq) // tk + 1
    return pl.pallas_call(
        functools.partial(_slc_attn_kernel, n_near=n_near),
        grid=(bsz, g_, nt),
        in_specs=[
            pl.BlockSpec((tq, gw), lambda b, g, i: (b * nt + i, COL_AQ // gw + g)),
            pl.BlockSpec((1, 1, LANES, tq), lambda b, g, i: (b, g, 0, i)),
            pl.BlockSpec((t, dh), lambda b, g, i: (b, COL_SK // dh + g)),
            pl.BlockSpec((t, dh), lambda b, g, i: (b, COL_SV // dh + g)),
            pl.BlockSpec((1, NSA_HPG, tk, bias_s.shape[3]), lambda b, g, i: (g, 0, 0, 0)),
            pl.BlockSpec((GATE_COLS, tq), lambda b, g, i: (0, b * nt + i)),
        ],
        out_specs=pl.BlockSpec((1, NSA_HPG, dh, tq), lambda b, g, i: (b, g, 0, i)),
        out_shape=jax.ShapeDtypeStruct((bsz, NSA_HEADS, dh, t), BF16),
        scratch_shapes=[
            pltpu.VMEM((t, dh + LANES), BF16),
            pltpu.VMEM((VROWS, t), BF16),
            pltpu.VMEM((NSA_HPG, dh + LANES, tq), BF16),
            pltpu.VMEM((NSA_HPG, tk, tq), F32),
            pltpu.VMEM((NSA_HPG, tk, tq), BF16),
            pltpu.VMEM((NSA_HPG, 1, tq), F32),
            pltpu.VMEM((NSA_HPG, 1, tq), F32),
            pltpu.VMEM((NSA_HPG, VROWS, tq), F32),
        ],
        compiler_params=_cparams(("parallel", "parallel", "arbitrary")),
        name="slc_attn",
    )(proj, selbias_t, proj, proj, bias_s, gates)


def _win_attn_kernel(q_ref, k_ref, v_ref, bias_ref, gate_ref, oc_ref, os_ref, z_ref, ya_ref, vt, *, n_near):
    tq, tk, dh = ATT_TQ, ATT_TK, NSA_HEAD_DIM
    g = pl.program_id(1)
    i = pl.program_id(2)

    @pl.when(i == 0)
    def _():
        _fill_vt(vt, v_ref)

    offs = [pl.multiple_of(jnp.maximum(i - d, 0) * tk, tk) for d in range(n_near)]
    d_eff = [jnp.where(i >= d, d, n_near) for d in range(n_near)]
    k_tiles = [k_ref[pl.ds(off, tk), :] for off in offs]
    v_tiles = [vt[:, pl.ds(off, tk)] for off in offs]
    qs = [_q_scaled(q_ref, hh) for hh in range(NSA_HPG)]
    ss = [[_dot_nt(k_tiles[d], qs[hh]) for d in range(n_near)] for hh in range(NSA_HPG)]
    ps = []
    for hh in range(NSA_HPG):
        s = [ss[hh][d] + bias_ref[0, d_eff[d], hh] for d in range(n_near)]
        m = functools.reduce(jnp.maximum, [jnp.max(x, axis=0, keepdims=True) for x in s])
        ps.append([jnp.exp2(x - m).astype(BF16) for x in s])
    pvs = [[_dot(v_tiles[d], ps[hh][d]) for d in range(n_near)] for hh in range(NSA_HPG)]
    for hh in range(NSA_HPG):
        cols = slice(hh * dh, (hh + 1) * dh)
        o = _out_t(functools.reduce(lambda x, y: x + y, pvs[hh]), _gate_row(gate_ref, g, hh, 2))
        o = o + oc_ref[0, hh].astype(F32) + os_ref[0, hh].astype(F32)
        ya_ref[:, cols] = (o.T * _silu(z_ref[:, cols].astype(F32))).astype(BF16)


def _win_attn(proj, gates, o_cmp, o_slc, bias_w, bsz, t):
    g_, dh, tq, tk = NSA_KV_GROUPS, NSA_HEAD_DIM, ATT_TQ, ATT_TK
    nt = t // tq
    gw = NSA_HPG * dh
    n_near = bias_w.shape[1] - 1
    return pl.pallas_call(
        functools.partial(_win_attn_kernel, n_near=n_near),
        grid=(bsz, g_, nt),
        in_specs=[
            pl.BlockSpec((tq, gw), lambda b, g, i: (b * nt + i, COL_AQ // gw + g)),
            pl.BlockSpec((t, dh), lambda b, g, i: (b, COL_WK // dh + g)),
            pl.BlockSpec((t, dh), lambda b, g, i: (b, COL_WV // dh + g)),
            pl.BlockSpec((1, n_near + 1, NSA_HPG, tk, tq), lambda b, g, i: (g, 0, 0, 0, 0)),
            pl.BlockSpec((GATE_COLS, tq), lambda b, g, i: (0, b * nt + i)),
            pl.BlockSpec((1, NSA_HPG, dh, tq), lambda b, g, i: (b, g, 0, i)),
            pl.BlockSpec((1, NSA_HPG, dh, tq), lambda b, g, i: (b, g, 0, i)),
            pl.BlockSpec((tq, gw), lambda b, g, i: (b * nt + i, COL_AZ // gw + g)),
        ],
        out_specs=pl.BlockSpec((tq, gw), lambda b, g, i: (b * nt + i, g)),
        out_shape=jax.ShapeDtypeStruct((bsz * t, NSA_WIDTH), BF16),
        scratch_shapes=[pltpu.VMEM((VROWS, t), BF16)],
        compiler_params=_cparams(("parallel", "parallel", "arbitrary")),
        name="win_attn",
    )(proj, proj, proj, bias_w, gates, o_cmp, o_slc, proj)


def _outproj_kernel(ym_ref, ya_ref, x_ref, w1_ref, w2_ref, gain_ref, o_ref, rows):
    j = pl.program_id(1)
    nj = rows.shape[0]
    tn = rows.shape[2]
    rows[j] = x_ref[...] + _dot(ym_ref[...], w1_ref[...]) + _dot(ya_ref[...], w2_ref[...])

    @pl.when(j == nj - 1)
    def _():
        ss = None
        for jj in range(nj):
            y = rows[jj]
            part = jnp.sum(y * y, axis=-1, keepdims=True)
            ss = part if ss is None else ss + part
        inv = lax.rsqrt(ss / (nj * tn) + RMS_EPS)
        for jj in range(nj):
            o_ref[:, jj * tn:(jj + 1) * tn] = rows[jj] * inv * gain_ref[:, jj * tn:(jj + 1) * tn]


def _out_proj(y_m, y_a, x2d, w_out, gain):
    n, d = x2d.shape
    tm, tn = min(OUTPROJ_TM, n), OUTPROJ_TN
    nj = d // tn
    return pl.pallas_call(
        _outproj_kernel,
        grid=(n // tm, nj),
        in_specs=[
            pl.BlockSpec((tm, MLSTM_WIDTH), lambda i, j: (i, 0)),
            pl.BlockSpec((tm, NSA_WIDTH), lambda i, j: (i, 0)),
            pl.BlockSpec((tm, tn), lambda i, j: (i, j)),
            pl.BlockSpec((MLSTM_WIDTH, tn), lambda i, j: (0, j)),
            pl.BlockSpec((NSA_WIDTH, tn), lambda i, j: (MLSTM_WIDTH // NSA_WIDTH, j)),
            pl.BlockSpec((1, d), lambda i, j: (0, 0)),
        ],
        out_specs=pl.BlockSpec((tm, d), lambda i, j: (i, 0)),
        out_shape=jax.ShapeDtypeStruct((n, d), F32),
        scratch_shapes=[pltpu.VMEM((nj, tm, tn), F32)],
        compiler_params=_cparams(("parallel", "arbitrary")),
        name="out_proj",
    )(y_m, y_a, x2d, w_out, w_out, gain)


def _rel_bucket(dist):
    n = jnp.maximum(dist, 0)
    nf = jnp.maximum(n, REL_MAX_EXACT).astype(jnp.float32)
    large = REL_MAX_EXACT + (jnp.log(nf / REL_MAX_EXACT) / math.log(REL_MAX_DISTANCE / REL_MAX_EXACT)
                             * (REL_BUCKETS - REL_MAX_EXACT)).astype(jnp.int32)
    large = jnp.minimum(large, REL_BUCKETS - 1)
    return jnp.where(n < REL_MAX_EXACT, n, large)


def _toeplitz_vec(by_dist, base, n_pos, n_neg, lo, hi, shift=None):
    w = n_pos + n_neg
    c = np.arange(w)
    dist = np.where(c < n_pos, base + c, base - (w - c))
    ok = (dist >= lo) & (dist < hi)
    vals = by_dist[:, np.clip(dist, 0, by_dist.shape[1] - 1)]
    if shift is not None:
        vals = vals - shift
    return jnp.where(ok[None], vals * LOG2E, NEG_LOGIT)


def _toeplitz_t(w_row, n_keys, n_q, key_step):
    x = jnp.broadcast_to(w_row, (n_keys, w_row.shape[1]))
    return pltpu.roll(x, 0, 1, stride=key_step, stride_axis=0)[:, 0:n_q]


def _bias_tables_kernel(wc_ref, ws_ref, ww_ref, bc_ref, bs_ref, bw_ref):
    nb, t = bc_ref.shape[2], bc_ref.shape[3]
    bc_ref[0, 0] = _toeplitz_t(wc_ref[0], nb, t, CMP_STRIDE)
    bs_ref[0, 0] = _toeplitz_t(ws_ref[0], bs_ref.shape[2], bs_ref.shape[3], 1)
    tk, tq = bw_ref.shape[3], bw_ref.shape[4]
    n_w = ww_ref.shape[1]
    for d in range(n_w):
        bw_ref[0, d, 0] = _toeplitz_t(ww_ref[0, d:d + 1, :], tk, tq, 1)
    bw_ref[0, n_w, 0] = jnp.full((tk, tq), NEG_LOGIT, F32)


def _bias_tables(rel_bias, t):
    tq, tk = ATT_TQ, ATT_TK
    g_, hpg = NSA_KV_GROUPS, NSA_HPG
    rb = rel_bias.astype(F32)
    dmax = REL_MAX_DISTANCE + tq + tk
    by_dist = rb[_rel_bucket(jnp.arange(dmax, dtype=jnp.int32))].T
    far = rb[REL_BUCKETS - 1][:, None]
    big = 1 << 30
    nb = t // CMP_STRIDE
    wc = _toeplitz_vec(by_dist, -(CMP_BLOCK - 1), t, t, 0, big)[:, None, :]
    sq = min(SLC_TQ, t)
    r = sq // tk
    n_s = min(-(-(REL_MAX_DISTANCE + tk - 1) // tk) + r - 1, t // tk)
    wm = (n_s - 1) * tk + sq
    ws = _toeplitz_vec(by_dist, -(r - 1) * tk, wm, tk, 0, big, far)[:, None, :]
    n_w = min(-(-(WINDOW + tk - 1) // tq), t // tq)
    ww = jnp.stack([_toeplitz_vec(by_dist, d * tq, tq, tk, 0, WINDOW) for d in range(n_w)], axis=1)
    return pl.pallas_call(
        _bias_tables_kernel,
        grid=(NSA_HEADS,),
        in_specs=[
            pl.BlockSpec((1, 1, 2 * t), lambda h: (h, 0, 0)),
            pl.BlockSpec((1, 1, wm + tk), lambda h: (h, 0, 0)),
            pl.BlockSpec((1, n_w, tq + tk), lambda h: (h, 0, 0)),
        ],
        out_specs=[
            pl.BlockSpec((1, 1, nb, t), lambda h: (h // hpg, h % hpg, 0, 0)),
            pl.BlockSpec((1, 1, tk, wm), lambda h: (h // hpg, h % hpg, 0, 0)),
            pl.BlockSpec((1, n_w + 1, 1, tk, tq), lambda h: (h // hpg, 0, h % hpg, 0, 0)),
        ],
        out_shape=[
            jax.ShapeDtypeStruct((g_, hpg, nb, t), F32),
            jax.ShapeDtypeStruct((g_, hpg, tk, wm), F32),
            jax.ShapeDtypeStruct((g_, n_w + 1, hpg, tk, tq), F32),
        ],
        compiler_params=_cparams(("parallel",)),
        name="bias_tables",
    )(wc, ws, ww)


def _cover_t(t):
    nb = t // CMP_STRIDE
    n_cmp = (t - CMP_BLOCK) // CMP_STRIDE + 1
    n_slc = t // SLC_BLOCK
    cs = np.arange(nb) * CMP_STRIDE
    ss = np.arange(n_slc) * SLC_BLOCK
    cover = np.clip(np.minimum(cs[:, None] + CMP_BLOCK, ss[None, :] + SLC_BLOCK)
                    - np.maximum(cs[:, None], ss[None, :]), 0, None) / CMP_BLOCK
    cover[n_cmp:] = 0.0
    return jnp.asarray(cover.T, dtype=BF16)


def kernel(x, norm_gain, w_in, w_conv, b_igate, b_fgate, mlstm_norm_gain, cmp_k_pos, cmp_k_w1, cmp_k_w2,
           cmp_v_pos, cmp_v_w1, cmp_v_w2, rel_bias, w_out, final_norm_gain):
    bsz, t, d = x.shape
    assert d == D_MODEL and t % MLSTM_L == 0 and t % ATT_TQ == 0 and (t // CMP_STRIDE) % LANES == 0
    n = bsz * t
    x2d = x.reshape(n, d)

    w_main, w_gate = _w_prep(w_in.T)
    proj, gates = _in_proj(x2d, norm_gain.reshape(1, d).astype(F32), w_main, w_gate)

    g_rows = gates[:, :2 * MLSTM_HEADS].reshape(bsz, t, 2, MLSTM_HEADS).transpose(0, 3, 2, 1)
    y_m = _mlstm(proj, g_rows, w_conv.astype(F32), b_igate.astype(F32), b_fgate.astype(F32),
                 mlstm_norm_gain.reshape(1, MLSTM_WIDTH).astype(F32), bsz, t)

    dh = NSA_HEAD_DIM
    half = CMP_BLOCK // 2

    def w1cat(w1):
        return jnp.concatenate([w1[:half].reshape(half * dh, dh), w1[half:].reshape(half * dh, dh)], axis=1).astype(BF16)

    k_cmp, v_cmp_t = _compress(
        proj, w1cat(cmp_k_w1), cmp_k_w2.astype(BF16), cmp_k_pos.reshape(2, half * dh).astype(BF16),
        w1cat(cmp_v_w1), cmp_v_w2.astype(BF16), cmp_v_pos.reshape(2, half * dh).astype(BF16), bsz, t)

    bias_c, bias_s, bias_w = _bias_tables(rel_bias, t)
    gates_t = gates.T
    o_cmp, selbias_t = _cmp_attn(proj, gates_t, k_cmp, v_cmp_t, bias_c, _cover_t(t), bsz, t)
    o_slc = _slc_attn(proj, gates_t, selbias_t, bias_s, bsz, t)
    y_a = _win_attn(proj, gates_t, o_cmp, o_slc, bias_w, bsz, t)

    out = _out_proj(y_m, y_a, x2d, w_out.astype(BF16), final_norm_gain.reshape(1, d).astype(F32))
    return out.reshape(bsz, t, d)
```

```python
import functools
import math

import jax
import jax.numpy as jnp
import numpy as np
from jax import lax
from jax.experimental import pallas as pl
from jax.experimental.pallas import tpu as pltpu

F32 = jnp.float32
BF16 = jnp.bfloat16

D_MODEL = 4096
D_MIX = D_MODEL
MLSTM_WIDTH = D_MIX // 2
MLSTM_HEADS = 4
MLSTM_V_DIM = MLSTM_WIDTH // MLSTM_HEADS
MLSTM_QK_DIM = MLSTM_V_DIM // 2
MLSTM_QK_WIDTH = MLSTM_HEADS * MLSTM_QK_DIM
CONV_WIDTH = 4
NSA_WIDTH = D_MIX - MLSTM_WIDTH
NSA_HEAD_DIM = 128
NSA_HEADS = NSA_WIDTH // NSA_HEAD_DIM
NSA_KV_GROUPS = 4
NSA_HPG = NSA_HEADS // NSA_KV_GROUPS
NSA_KV_WIDTH = NSA_KV_GROUPS * NSA_HEAD_DIM
CMP_BLOCK = 32
CMP_STRIDE = 16
SLC_BLOCK = 64
SLC_TOP_N = 16
WINDOW = 512
FORCE_BONUS = 1000.0
REL_BUCKETS = 32
REL_MAX_EXACT = REL_BUCKETS // 2
REL_MAX_DISTANCE = 1024
RMS_EPS = 1e-6
NEG_LOGIT = -1e30

LANES = 128
VMEM_LIMIT_BYTES = 56 * 1024 * 1024

COL_MQ = 0
COL_MK = COL_MQ + MLSTM_QK_WIDTH
COL_MV = COL_MK + MLSTM_QK_WIDTH
COL_MO = COL_MV + MLSTM_WIDTH
COL_MZ = COL_MO + MLSTM_WIDTH
COL_AQ = COL_MZ + MLSTM_WIDTH
COL_CK = COL_AQ + NSA_WIDTH
COL_CV = COL_CK + NSA_KV_WIDTH
COL_SK = COL_CV + NSA_KV_WIDTH
COL_SV = COL_SK + NSA_KV_WIDTH
COL_WK = COL_SV + NSA_KV_WIDTH
COL_WV = COL_WK + NSA_KV_WIDTH
COL_AZ = COL_WV + NSA_KV_WIDTH
MAIN_COLS = COL_AZ + NSA_WIDTH
GATE_COLS = LANES
GATE_I = 0
GATE_F = MLSTM_HEADS
GATE_NSA = 2 * MLSTM_HEADS

INPROJ_TM = 512
INPROJ_TN = 1536
OUTPROJ_TM = 512
OUTPROJ_TN = 1024
MLSTM_L = 256
ATT_TQ = 256
ATT_TK = 256
WPREP_TR = 512
SLC_TQ = 512
SLC_TK = 512
CMP_TQ = 512
WIN_TQ = 512
MASK_BIG = 2.0 ** 100


def _cparams(sem):
    return pltpu.CompilerParams(dimension_semantics=sem, vmem_limit_bytes=VMEM_LIMIT_BYTES)


def _sigmoid(x):
    return 0.5 * jnp.tanh(0.5 * x) + 0.5


def _silu(x):
    return x * _sigmoid(x)


def _dot(a, b):
    return jnp.dot(a, b, preferred_element_type=F32)


def _dot_nt(a, b):
    return lax.dot_general(a, b, (((1,), (1,)), ((), ())), preferred_element_type=F32)


W_OFF_I = COL_MZ + MLSTM_WIDTH
W_OFF_AQ = W_OFF_I + 2 * MLSTM_HEADS
W_OFF_GATE = W_OFF_AQ + NSA_WIDTH + 6 * NSA_KV_WIDTH
W_OFF_AZ = W_OFF_GATE + 3 * NSA_HEADS


def _wprep_kernel(w_ref, gi_ref, ga_ref, o_ref, g_ref):
    o_ref[...] = w_ref[...].astype(BF16)

    @pl.when(pl.program_id(0) == 0)
    def _():
        pad = jnp.zeros((GATE_COLS - gi_ref.shape[0] - ga_ref.shape[0], g_ref.shape[1]), F32)
        g_ref[...] = jnp.concatenate([gi_ref[...], ga_ref[...], pad], axis=0).astype(BF16)


def _w_prep(w_t):
    rows, d = w_t.shape
    tr = WPREP_TR
    nb_a, nb_b = W_OFF_I // tr, (W_OFF_GATE - W_OFF_AQ) // tr

    def src(m):
        skip = jnp.where(m >= nb_a, W_OFF_AQ - W_OFF_I, 0) + jnp.where(m >= nb_a + nb_b, W_OFF_AZ - W_OFF_GATE, 0)
        return pl.multiple_of(m * tr + skip, 8)

    rows_at = lambda start, size: pl.BlockSpec((pl.Element(size), pl.Element(d)), lambda m: (start, 0))
    return pl.pallas_call(
        _wprep_kernel,
        grid=(MAIN_COLS // tr,),
        in_specs=[
            pl.BlockSpec((pl.Element(tr), pl.Element(d)), lambda m: (src(m), 0)),
            rows_at(W_OFF_I, W_OFF_AQ - W_OFF_I),
            rows_at(W_OFF_GATE, W_OFF_AZ - W_OFF_GATE),
        ],
        out_specs=[pl.BlockSpec((tr, d), lambda m: (m, 0)), pl.BlockSpec((GATE_COLS, d), lambda m: (0, 0))],
        out_shape=[jax.ShapeDtypeStruct((MAIN_COLS, d), BF16), jax.ShapeDtypeStruct((GATE_COLS, d), BF16)],
        compiler_params=_cparams(("arbitrary",)),
        name="w_prep",
    )(w_t, w_t, w_t)


def _inproj_kernel(x_ref, gain_ref, w_ref, wg_ref, o_ref, og_ref, h_ref):
    @pl.when(pl.program_id(1) == 0)
    def _():
        x = x_ref[...]
        ms = jnp.mean(x * x, axis=-1, keepdims=True)
        h = (x * lax.rsqrt(ms + RMS_EPS) * gain_ref[...]).astype(BF16)
        h_ref[...] = h
        og_ref[...] = _dot_nt(h, wg_ref[...])

    o_ref[...] = _dot_nt(h_ref[...], w_ref[...]).astype(BF16)


def _in_proj(x2d, gain, w_main, w_gate):
    n, d = x2d.shape
    tm, tn = min(INPROJ_TM, n), INPROJ_TN
    return pl.pallas_call(
        _inproj_kernel,
        grid=(n // tm, MAIN_COLS // tn),
        in_specs=[
            pl.BlockSpec((tm, d), lambda i, j: (i, 0)),
            pl.BlockSpec((1, d), lambda i, j: (0, 0)),
            pl.BlockSpec((tn, d), lambda i, j: (j, 0)),
            pl.BlockSpec((GATE_COLS, d), lambda i, j: (0, 0)),
        ],
        out_specs=[
            pl.BlockSpec((tm, tn), lambda i, j: (i, j)),
            pl.BlockSpec((tm, GATE_COLS), lambda i, j: (i, 0)),
        ],
        out_shape=[
            jax.ShapeDtypeStruct((n, MAIN_COLS), BF16),
            jax.ShapeDtypeStruct((n, GATE_COLS), F32),
        ],
        scratch_shapes=[pltpu.VMEM((tm, d), BF16)],
        compiler_params=_cparams(("parallel", "arbitrary")),
        name="in_proj",
    )(x2d, gain, w_main, w_gate)


def _mlstm_kernel(bi_ref, bf_ref, q_ref, k_ref, v_ref, o_ref, z_ref, g_ref, wq_ref, wk_ref, ng_ref,
                  y_ref, qext, kext, c_st, n_st, m_st):
    L = MLSTM_L
    HIST = 8
    dqk, dv = MLSTM_QK_DIM, MLSTM_V_DIM

    @pl.when(pl.program_id(1) == 0)
    def _():
        qext[:, 0:HIST, :] = jnp.zeros((MLSTM_HEADS, HIST, dqk), F32)
        kext[:, 0:HIST, :] = jnp.zeros((MLSTM_HEADS, HIST, dqk), F32)
        c_st[...] = jnp.zeros_like(c_st)
        n_st[...] = jnp.zeros_like(n_st)
        m_st[...] = jnp.zeros_like(m_st)

    rr = lax.broadcasted_iota(jnp.int32, (L, L), 0)
    cc = lax.broadcasted_iota(jnp.int32, (L, L), 1)
    upper = (rr <= cc).astype(F32)

    for hd in range(MLSTM_HEADS):
        qcols = slice(hd * dqk, (hd + 1) * dqk)
        vcols = slice(hd * dv, (hd + 1) * dv)
        qext[hd, HIST:HIST + L, :] = q_ref[:, qcols].astype(F32)
        kext[hd, HIST:HIST + L, :] = k_ref[:, qcols].astype(F32)

        def conv_silu(ext, w_ref):
            w = w_ref[:, qcols]
            y = ext[hd, pl.ds(HIST, L), :] * w[CONV_WIDTH - 1:CONV_WIDTH, :]
            for s in range(1, CONV_WIDTH):
                y = y + ext[hd, pl.ds(HIST - s, L), :] * w[CONV_WIDTH - 1 - s:CONV_WIDTH - s, :]
            return _silu(y)

        qc = conv_silu(qext, wq_ref)
        kc = conv_silu(kext, wk_ref) * (dqk ** -0.5)
        qext[hd, 0:HIST, :] = qext[hd, L:L + HIST, :]
        kext[hd, 0:HIST, :] = kext[hd, L:L + HIST, :]

        g = g_ref[0, hd]
        i_row = g[0:1, :] + bi_ref[hd]
        f_row = g[1:2, :] + bf_ref[hd]
        lf_row = jnp.minimum(f_row, 0.0) - jnp.log(1.0 + jnp.exp(-jnp.abs(f_row)))

        bcum_row = jnp.dot(jnp.broadcast_to(lf_row, (8, L)), upper, preferred_element_type=F32,
                           precision=lax.Precision.HIGHEST)[0:1, :]
        bcum_col = jnp.sum(jnp.where(rr == cc, bcum_row, 0.0), axis=1, keepdims=True)
        gsum = bcum_row[:, L - 1:L]
        m_prev = m_st[hd]

        dlog = jnp.where(rr >= cc, bcum_col - bcum_row + i_row, -jnp.inf)
        m_inter = bcum_col + m_prev
        m_t = jnp.maximum(m_inter, jnp.max(dlog, axis=1, keepdims=True))
        dmat = jnp.exp(dlog - m_t)
        inter = jnp.exp(m_inter - m_t)

        qb = qc.astype(BF16)
        kct = kc.T
        vb = v_ref[:, vcols]
        s = _dot(qb, kct.astype(BF16)) * dmat
        c_prev = c_st[hd]
        n_prev = n_st[hd]
        num = _dot(s.astype(BF16), vb) + inter * _dot(qb, c_prev.astype(BF16))
        qn = jnp.sum(s, axis=1, keepdims=True) + inter * jnp.sum(qc * n_prev, axis=1, keepdims=True)
        hh = num / jnp.maximum(jnp.abs(qn), jnp.exp(-m_t))

        wlog = gsum - bcum_row + i_row
        m_next = jnp.maximum(gsum + m_prev, jnp.max(wlog, axis=1, keepdims=True))
        wts = jnp.exp(wlog - m_next)
        keep = jnp.exp(gsum + m_prev - m_next)
        c_st[hd] = keep * c_prev + _dot((kct * wts).astype(BF16), vb)
        n_st[hd] = keep * n_prev + _dot(jnp.broadcast_to(wts, (8, L)).astype(BF16), kc.astype(BF16))[0:1, :]
        m_st[hd] = m_next

        hm = _sigmoid(o_ref[:, vcols].astype(F32)) * hh
        hm = hm * lax.rsqrt(jnp.mean(hm * hm, axis=-1, keepdims=True) + RMS_EPS)
        hm = hm * ng_ref[:, vcols]
        y_ref[:, vcols] = (hm * _silu(z_ref[:, vcols].astype(F32))).astype(BF16)


def _mlstm(proj, g_rows, w_conv, b_igate, b_fgate, norm_gain, bsz, t):
    L = MLSTM_L
    nc = t // L
    nh, dqk, dv = MLSTM_HEADS, MLSTM_QK_DIM, MLSTM_V_DIM
    row = lambda b, c: b * nc + c
    smem = pl.BlockSpec(memory_space=pltpu.SMEM)
    return pl.pallas_call(
        _mlstm_kernel,
        grid=(bsz, nc),
        in_specs=[
            smem, smem,
            pl.BlockSpec((L, MLSTM_QK_WIDTH), lambda b, c: (row(b, c), COL_MQ // MLSTM_QK_WIDTH)),
            pl.BlockSpec((L, MLSTM_QK_WIDTH), lambda b, c: (row(b, c), COL_MK // MLSTM_QK_WIDTH)),
            pl.BlockSpec((L, MLSTM_WIDTH), lambda b, c: (row(b, c), COL_MV // MLSTM_WIDTH)),
            pl.BlockSpec((L, MLSTM_WIDTH), lambda b, c: (row(b, c), COL_MO // MLSTM_WIDTH)),
            pl.BlockSpec((L, MLSTM_WIDTH), lambda b, c: (row(b, c), COL_MZ // MLSTM_WIDTH)),
            pl.BlockSpec((1, nh, 2, L), lambda b, c: (b, 0, 0, c)),
            pl.BlockSpec((CONV_WIDTH, MLSTM_QK_WIDTH), lambda b, c: (0, 0)),
            pl.BlockSpec((CONV_WIDTH, MLSTM_QK_WIDTH), lambda b, c: (0, 1)),
            pl.BlockSpec((1, MLSTM_WIDTH), lambda b, c: (0, 0)),
        ],
        out_specs=pl.BlockSpec((L, MLSTM_WIDTH), lambda b, c: (row(b, c), 0)),
        out_shape=jax.ShapeDtypeStruct((bsz * t, MLSTM_WIDTH), BF16),
        scratch_shapes=[
            pltpu.VMEM((nh, L + 8, dqk), F32),
            pltpu.VMEM((nh, L + 8, dqk), F32),
            pltpu.VMEM((nh, dqk, dv), F32),
            pltpu.VMEM((nh, 1, dqk), F32),
            pltpu.VMEM((nh, 1, 1), F32),
        ],
        compiler_params=_cparams(("parallel", "arbitrary")),
        name="mlstm",
    )(b_igate, b_fgate, proj, proj, proj, proj, proj, g_rows, w_conv, w_conv, norm_gain)


def _compress_kernel(ck_ref, cv_ref, w1k_ref, w2k_ref, pk_ref, w1v_ref, w2v_ref, pv_ref,
                     ok_ref, ov_ref, xf, xcat):
    t = ck_ref.shape[0]
    nb = t // CMP_STRIDE
    dh = NSA_HEAD_DIM

    def one(src_ref, w1_ref, w2_ref, pos_ref):
        xf[...] = src_ref[...].astype(F32)
        for l in range(CMP_STRIDE):
            xcat[:, l * dh:(l + 1) * dh] = xf[pl.ds(l, nb, stride=CMP_STRIDE), :].astype(BF16)
        w1 = w1_ref[...]
        ab = _dot(xcat[...], w1)
        pp = _dot(pos_ref[...], w1)
        pos_term = pp[0:1, 0:dh] + pp[1:2, dh:2 * dh]
        second = pltpu.roll(ab[:, dh:2 * dh], nb - 1, 0)
        hid = _silu(ab[:, 0:dh] + second + pos_term)
        return _dot(hid.astype(BF16), w2_ref[...])

    ok_ref[0, 0] = one(ck_ref, w1k_ref, w2k_ref, pk_ref).astype(BF16)
    ov_ref[0, 0] = one(cv_ref, w1v_ref, w2v_ref, pv_ref).T.astype(BF16)


def _compress(proj, w1k, w2k, pk, w1v, w2v, pv, bsz, t):
    g_, dh = NSA_KV_GROUPS, NSA_HEAD_DIM
    nb = t // CMP_STRIDE
    full = lambda a: pl.BlockSpec(a.shape, lambda b, g: (0,) * a.ndim)
    k_spec = pl.BlockSpec((1, 1, nb, dh), lambda b, g: (b, g, 0, 0))
    vt_spec = pl.BlockSpec((1, 1, dh, nb), lambda b, g: (b, g, 0, 0))
    return pl.pallas_call(
        _compress_kernel,
        grid=(bsz, g_),
        in_specs=[
            pl.BlockSpec((t, dh), lambda b, g: (b, COL_CK // dh + g)),
            pl.BlockSpec((t, dh), lambda b, g: (b, COL_CV // dh + g)),
            full(w1k), full(w2k), full(pk), full(w1v), full(w2v), full(pv),
        ],
        out_specs=[k_spec, vt_spec],
        out_shape=[jax.ShapeDtypeStruct((bsz, g_, nb, dh), BF16), jax.ShapeDtypeStruct((bsz, g_, dh, nb), BF16)],
        scratch_shapes=[pltpu.VMEM((t, dh), F32), pltpu.VMEM((nb, CMP_STRIDE * dh), BF16)],
        compiler_params=_cparams(("parallel", "parallel")),
        name="compress",
    )(proj, proj, w1k, w2k, pk, w1v, w2v, pv)


LOG2E = math.log2(math.e)
VROWS = NSA_HEAD_DIM + 16


def _q_scaled(q_ref, hh):
    dh = NSA_HEAD_DIM
    return (q_ref[:, hh * dh:(hh + 1) * dh].astype(F32) * (dh ** -0.5 * LOG2E)).astype(BF16)


def _q_t(q_ref, hh):
    dh = NSA_HEAD_DIM
    return (q_ref[:, hh * dh:(hh + 1) * dh].astype(F32) * (dh ** -0.5 * LOG2E)).T.astype(BF16)


def _gate_row(gt_ref, g, hh, branch):
    row = GATE_NSA + 3 * (g * NSA_HPG + hh) + branch
    return _sigmoid(gt_ref[pl.ds(row, 1), :])


def _fill_vt(vt, v_ref):
    dh = NSA_HEAD_DIM
    vt[0:dh, :] = v_ref[...].astype(F32).T.astype(BF16)
    vt[dh:VROWS, :] = jnp.ones((VROWS - dh, vt.shape[1]), BF16)


def _out_t(acc, gate_row):
    dh = NSA_HEAD_DIM
    return acc[0:dh, :] * (gate_row / acc[dh:dh + 1, :])


def _cmp_attn_kernel(q_ref, kc_ref, vct_ref, bias_ref, cov_ref, gate_ref, oc_ref, sel_ref, score_ref, *, n_sel):
    tq = q_ref.shape[0]
    dh = NSA_HEAD_DIM
    n_slc = cov_ref.shape[0]
    t0 = pl.program_id(1) * tq
    g = pl.program_id(0)
    kc = kc_ref[0, 0]
    vct = vct_ref[0, 0]
    heads = range(NSA_HPG)
    qk = [_dot_nt(kc, _q_scaled(q_ref, hh)) for hh in heads]
    ps = []
    for hh in heads:
        logit = qk[hh] + bias_ref[0, hh]
        m = jnp.max(logit, axis=0, keepdims=True)
        e = jnp.exp2(logit - m)
        inv = jnp.where(m > 0.5 * NEG_LOGIT, 1.0 / jnp.sum(e, axis=0, keepdims=True), 0.0)
        ps.append(e * inv)
    p_sum = (ps[0] + ps[1]) + (ps[2] + ps[3])
    ots = [_dot(vct, p.astype(BF16)) for p in ps]
    for hh in heads:
        oc_ref[0, hh] = (ots[hh] * _gate_row(gate_ref, g, hh, 0)).astype(BF16)

    cov = cov_ref[...]
    p_hi = p_sum.astype(BF16)
    r1 = p_sum - p_hi.astype(F32)
    p_mid = r1.astype(BF16)
    p_lo = (r1 - p_mid.astype(F32)).astype(BF16)
    st = _dot(cov, p_hi) + _dot(cov, p_mid) + _dot(cov, p_lo)

    jb = lax.broadcasted_iota(jnp.int32, (n_slc, tq), 0)
    cur = (t0 + lax.broadcasted_iota(jnp.int32, (n_slc, tq), 1)) // SLC_BLOCK
    valid = jb <= cur
    forced = (jb == 0) | (jb == cur) | (jb == cur - 1)
    score = jnp.where(valid, st + jnp.where(forced, FORCE_BONUS, 0.0), -1.0)
    sub = 8
    score_ref[...] = score
    groups = [score_ref[r:r + sub, :] for r in range(0, n_slc, sub)]
    ranks = [jnp.zeros((sub, tq), F32) for _ in groups]
    row_id = lax.broadcasted_iota(jnp.int32, (sub, tq), 0)
    for j2 in range(n_slc):
        row = score_ref[j2:j2 + 1, :]
        for gi, sc in enumerate(groups):
            r0 = gi * sub
            if r0 > j2:
                inc = jnp.where(row >= sc, 1.0, 0.0)
            elif r0 + sub - 1 <= j2:
                inc = jnp.where(row > sc, 1.0, 0.0)
            else:
                inc = jnp.where(row_id > j2 - r0, jnp.where(row >= sc, 1.0, 0.0), jnp.where(row > sc, 1.0, 0.0))
            ranks[gi] = ranks[gi] + inc
    for gi, rk in enumerate(ranks):
        score_ref[gi * sub:(gi + 1) * sub, :] = rk
    sel = valid & (score_ref[...] < n_sel)
    sel_ref[0, 0, 0:n_slc, :] = jnp.where(sel, 0.0, -MASK_BIG).astype(BF16)
    if n_slc < LANES:
        sel_ref[0, 0, n_slc:LANES, :] = jnp.zeros((LANES - n_slc, tq), BF16)


def _cmp_attn(proj, gates, k_cmp, v_cmp_t, bias_c, cover_t, bsz, t):
    g_, dh, tq = NSA_KV_GROUPS, NSA_HEAD_DIM, CMP_TQ
    nt = t // tq
    nb = t // CMP_STRIDE
    n_slc = t // SLC_BLOCK
    gw = NSA_HPG * dh
    return pl.pallas_call(
        functools.partial(_cmp_attn_kernel, n_sel=min(SLC_TOP_N, n_slc)),
        grid=(g_, nt, bsz),
        in_specs=[
            pl.BlockSpec((tq, gw), lambda g, i, b: (b * nt + i, COL_AQ // gw + g)),
            pl.BlockSpec((1, 1, nb, dh), lambda g, i, b: (b, g, 0, 0)),
            pl.BlockSpec((1, 1, dh, nb), lambda g, i, b: (b, g, 0, 0)),
            pl.BlockSpec((1, NSA_HPG, nb, tq), lambda g, i, b: (g, 0, 0, i)),
            pl.BlockSpec((n_slc, nb), lambda g, i, b: (0, 0)),
            pl.BlockSpec((GATE_COLS, tq), lambda g, i, b: (0, b * nt + i)),
        ],
        out_specs=[
            pl.BlockSpec((1, NSA_HPG, dh, tq), lambda g, i, b: (b, g, 0, i)),
            pl.BlockSpec((1, 1, LANES, tq), lambda g, i, b: (b, g, 0, i)),
        ],
        out_shape=[
            jax.ShapeDtypeStruct((bsz, NSA_HEADS, dh, t), BF16),
            jax.ShapeDtypeStruct((bsz, g_, LANES, t), BF16),
        ],
        scratch_shapes=[pltpu.VMEM((n_slc, tq), F32)],
        compiler_params=_cparams(("parallel", "parallel", "parallel")),
        name="cmp_attn",
    )(proj, k_cmp, v_cmp_t, bias_c, cover_t, gates)


def _slc_attn_kernel(q_ref, sbt_ref, k_ref, v_ref, bias_ref, gate_ref, os_ref,
                     kaug, vt, qt, s_buf, p_buf, a_buf, m_ref, acc_ref, *, n_near):
    tq, tk, dh = SLC_TQ, SLC_TK, NSA_HEAD_DIM
    r = tq // tk
    t = k_ref.shape[0]
    g = pl.program_id(1)
    i = pl.program_id(2)
    heads = range(NSA_HPG)

    @pl.when(i == 0)
    def _():
        kaug[:, 0:dh] = k_ref[...]
        blk = lax.broadcasted_iota(jnp.int32, (t, LANES), 0) // SLC_BLOCK
        lane = lax.broadcasted_iota(jnp.int32, (t, LANES), 1)
        kaug[:, dh:dh + LANES] = jnp.where(blk == lane, 1.0, 0.0).astype(BF16)
        _fill_vt(vt, v_ref)

    sbt = sbt_ref[0, 0]
    for hh in heads:
        qt[hh, 0:dh, :] = _q_t(q_ref, hh)
        qt[hh, dh:dh + LANES, :] = sbt

    def ktile(j):
        return kaug[pl.ds(pl.multiple_of(j * tk, tk), tk), :]

    def vtile(j):
        return vt[:, pl.ds(pl.multiple_of(j * tk, tk), tk)]

    m_ref[...] = jnp.full(m_ref.shape, NEG_LOGIT, F32)
    acc_ref[...] = jnp.zeros_like(acc_ref)
    p_buf[...] = jnp.zeros_like(p_buf)
    a_buf[...] = jnp.ones_like(a_buf)
    k0 = ktile(0)
    for hh in heads:
        s_buf[hh] = _dot(k0, qt[hh])

    last = (i + 1) * r - 1

    def stage(j, bias_fn):
        v_prev = vtile(jnp.maximum(j - 1, 0))
        k_next = ktile(jnp.minimum(j + 1, last))
        pv = [_dot(v_prev, p_buf[hh]) for hh in heads]
        s_next = [_dot(k_next, qt[hh]) for hh in heads]
        for hh in heads:
            acc_ref[hh] = a_buf[hh] * acc_ref[hh] + pv[hh]
        for hh in heads:
            s = s_buf[hh]
            if bias_fn is not None:
                s = s + bias_fn(hh)
            m_prev = m_ref[hh]
            m_new = jnp.maximum(m_prev, jnp.max(s, axis=0, keepdims=True))
            a_buf[hh] = jnp.exp2(m_prev - m_new)
            p_buf[hh] = jnp.exp2(s - m_new).astype(BF16)
            m_ref[hh] = m_new
        for hh in heads:
            s_buf[hh] = s_next[hh]

    n_far = jnp.maximum(i * r + r - n_near, 0)

    def far_body(j, c):
        stage(j, None)
        return c

    lax.fori_loop(0, n_far, far_body, 0)

    def near_body(j, c):
        off = pl.multiple_of((i * r + (r - 1) - j) * tk, tk)
        stage(j, lambda hh: bias_ref[0, hh, :, pl.ds(off, tq)])
        return c

    lax.fori_loop(n_far, last + 1, near_body, 0)

    v_last = vtile(last)
    pv = [_dot(v_last, p_buf[hh]) for hh in heads]
    for hh in heads:
        acc = a_buf[hh] * acc_ref[hh] + pv[hh]
        os_ref[0, hh] = _out_t(acc, _gate_row(gate_ref, g, hh, 1)).astype(BF16)


def _slc_attn(proj, gates, selbias_t, bias_s, bsz, t):
    g_, dh, tq, tk = NSA_KV_GROUPS, NSA_HEAD_DIM, SLC_TQ, SLC_TK
    nt = t // tq
    gw = NSA_HPG * dh
    n_near = (bias_s.shape[3] - tq) // tk + 1
    return pl.pallas_call(
        functools.partial(_slc_attn_kernel, n_near=n_near),
        grid=(bsz, g_, nt),
        in_specs=[
            pl.BlockSpec((tq, gw), lambda b, g, i: (b * nt + i, COL_AQ // gw + g)),
            pl.BlockSpec((1, 1, LANES, tq), lambda b, g, i: (b, g, 0, i)),
            pl.BlockSpec((t, dh), lambda b, g, i: (b, COL_SK // dh + g)),
            pl.BlockSpec((t, dh), lambda b, g, i: (b, COL_SV // dh + g)),
            pl.BlockSpec((1, NSA_HPG, tk, bias_s.shape[3]), lambda b, g, i: (g, 0, 0, 0)),
            pl.BlockSpec((GATE_COLS, tq), lambda b, g, i: (0, b * nt + i)),
        ],
        out_specs=pl.BlockSpec((1, NSA_HPG, dh, tq), lambda b, g, i: (b, g, 0, i)),
        out_shape=jax.ShapeDtypeStruct((bsz, NSA_HEADS, dh, t), BF16),
        scratch_shapes=[
            pltpu.VMEM((t, dh + LANES), BF16),
            pltpu.VMEM((VROWS, t), BF16),
            pltpu.VMEM((NSA_HPG, dh + LANES, tq), BF16),
            pltpu.VMEM((NSA_HPG, tk, tq), F32),
            pltpu.VMEM((NSA_HPG, tk, tq), BF16),
            pltpu.VMEM((NSA_HPG, 1, tq), F32),
            pltpu.VMEM((NSA_HPG, 1, tq), F32),
            pltpu.VMEM((NSA_HPG, VROWS, tq), F32),
        ],
        compiler_params=_cparams(("parallel", "parallel", "arbitrary")),
        name="slc_attn",
    )(proj, selbias_t, proj, proj, bias_s, gates)


def _win_attn_kernel(q_ref, k_ref, v_ref, bias_ref, gate_ref, oc_ref, os_ref, z_ref, ya_ref, vt, *, n_tiles):
    tq, tk, dh = WIN_TQ, ATT_TK, NSA_HEAD_DIM
    r = tq // tk
    g = pl.program_id(1)
    i = pl.program_id(2)
    heads = range(NSA_HPG)

    @pl.when(i == 0)
    def _():
        _fill_vt(vt, v_ref)

    js = [i * r + (r - 1) - d for d in range(n_tiles)]
    offs = [pl.multiple_of(jnp.maximum(j, 0) * tk, tk) for j in js]
    boffs = [pl.multiple_of(jnp.where(js[d] >= 0, d, n_tiles) * tk, tk) for d in range(n_tiles)]
    k_tiles = [k_ref[pl.ds(off, tk), :] for off in offs]
    v_tiles = [vt[:, pl.ds(off, tk)] for off in offs]
    qs = [_q_scaled(q_ref, hh) for hh in heads]
    ss = [[_dot_nt(k_tiles[d], qs[hh]) for d in range(n_tiles)] for hh in heads]
    ps = []
    for hh in heads:
        s = [ss[hh][d] + bias_ref[0, hh, :, pl.ds(boffs[d], tq)] for d in range(n_tiles)]
        m = functools.reduce(jnp.maximum, [jnp.max(x, axis=0, keepdims=True) for x in s])
        ps.append([jnp.exp2(x - m).astype(BF16) for x in s])
    pvs = [[_dot(v_tiles[d], ps[hh][d]) for d in range(n_tiles)] for hh in heads]
    for hh in heads:
        cols = slice(hh * dh, (hh + 1) * dh)
        o = _out_t(functools.reduce(lambda x, y: x + y, pvs[hh]), _gate_row(gate_ref, g, hh, 2))
        o = o + oc_ref[0, hh].astype(F32) + os_ref[0, hh].astype(F32)
        ya_ref[:, cols] = (o.T * _silu(z_ref[:, cols].astype(F32))).astype(BF16)


def _win_attn(proj, gates, o_cmp, o_slc, bias_w, bsz, t):
    g_, dh, tq, tk = NSA_KV_GROUPS, NSA_HEAD_DIM, WIN_TQ, ATT_TK
    nt = t // tq
    gw = NSA_HPG * dh
    n_tiles = (bias_w.shape[3] - tq) // tk
    return pl.pallas_call(
        functools.partial(_win_attn_kernel, n_tiles=n_tiles),
        grid=(bsz, g_, nt),
        in_specs=[
            pl.BlockSpec((tq, gw), lambda b, g, i: (b * nt + i, COL_AQ // gw + g)),
            pl.BlockSpec((t, dh), lambda b, g, i: (b, COL_WK // dh + g)),
            pl.BlockSpec((t, dh), lambda b, g, i: (b, COL_WV // dh + g)),
            pl.BlockSpec((1, NSA_HPG, tk, bias_w.shape[3]), lambda b, g, i: (g, 0, 0, 0)),
            pl.BlockSpec((GATE_COLS, tq), lambda b, g, i: (0, b * nt + i)),
            pl.BlockSpec((1, NSA_HPG, dh, tq), lambda b, g, i: (b, g, 0, i)),
            pl.BlockSpec((1, NSA_HPG, dh, tq), lambda b, g, i: (b, g, 0, i)),
            pl.BlockSpec((tq, gw), lambda b, g, i: (b * nt + i, COL_AZ // gw + g)),
        ],
        out_specs=pl.BlockSpec((tq, gw), lambda b, g, i: (b * nt + i, g)),
        out_shape=jax.ShapeDtypeStruct((bsz * t, NSA_WIDTH), BF16),
        scratch_shapes=[pltpu.VMEM((VROWS, t), BF16)],
        compiler_params=_cparams(("parallel", "parallel", "arbitrary")),
        name="win_attn",
    )(proj, proj, proj, bias_w, gates, o_cmp, o_slc, proj)


def _outproj_kernel(ym_ref, ya_ref, x_ref, w1_ref, w2_ref, gain_ref, o_ref, rows):
    j = pl.program_id(1)
    nj = rows.shape[0]
    tn = rows.shape[2]
    rows[j] = x_ref[...] + _dot(ym_ref[...], w1_ref[...]) + _dot(ya_ref[...], w2_ref[...])

    @pl.when(j == nj - 1)
    def _():
        ss = None
        for jj in range(nj):
            y = rows[jj]
            part = jnp.sum(y * y, axis=-1, keepdims=True)
            ss = part if ss is None else ss + part
        inv = lax.rsqrt(ss / (nj * tn) + RMS_EPS)
        for jj in range(nj):
            o_ref[:, jj * tn:(jj + 1) * tn] = rows[jj] * inv * gain_ref[:, jj * tn:(jj + 1) * tn]


def _out_proj(y_m, y_a, x2d, w_out, gain):
    n, d = x2d.shape
    tm, tn = min(OUTPROJ_TM, n), OUTPROJ_TN
    nj = d // tn
    return pl.pallas_call(
        _outproj_kernel,
        grid=(n // tm, nj),
        in_specs=[
            pl.BlockSpec((tm, MLSTM_WIDTH), lambda i, j: (i, 0)),
            pl.BlockSpec((tm, NSA_WIDTH), lambda i, j: (i, 0)),
            pl.BlockSpec((tm, tn), lambda i, j: (i, j)),
            pl.BlockSpec((MLSTM_WIDTH, tn), lambda i, j: (0, j)),
            pl.BlockSpec((NSA_WIDTH, tn), lambda i, j: (MLSTM_WIDTH // NSA_WIDTH, j)),
            pl.BlockSpec((1, d), lambda i, j: (0, 0)),
        ],
        out_specs=pl.BlockSpec((tm, d), lambda i, j: (i, 0)),
        out_shape=jax.ShapeDtypeStruct((n, d), F32),
        scratch_shapes=[pltpu.VMEM((nj, tm, tn), F32)],
        compiler_params=_cparams(("parallel", "arbitrary")),
        name="out_proj",
    )(y_m, y_a, x2d, w_out, w_out, gain)


def _rel_bucket(dist):
    n = jnp.maximum(dist, 0)
    nf = jnp.maximum(n, REL_MAX_EXACT).astype(jnp.float32)
    large = REL_MAX_EXACT + (jnp.log(nf / REL_MAX_EXACT) / math.log(REL_MAX_DISTANCE / REL_MAX_EXACT)
                             * (REL_BUCKETS - REL_MAX_EXACT)).astype(jnp.int32)
    large = jnp.minimum(large, REL_BUCKETS - 1)
    return jnp.where(n < REL_MAX_EXACT, n, large)


def _toeplitz_vec(by_dist, base, n_pos, n_neg, lo, hi, shift=None):
    w = n_pos + n_neg
    c = np.arange(w)
    dist = np.where(c < n_pos, base + c, base - (w - c))
    ok = (dist >= lo) & (dist < hi)
    vals = by_dist[:, np.clip(dist, 0, by_dist.shape[1] - 1)]
    if shift is not None:
        vals = vals - shift
    return jnp.where(ok[None], vals * LOG2E, NEG_LOGIT)


def _toeplitz_t(w_row, n_keys, n_q, key_step):
    x = jnp.broadcast_to(w_row, (n_keys, w_row.shape[1]))
    return pltpu.roll(x, 0, 1, stride=key_step, stride_axis=0)[:, 0:n_q]


def _bias_tables_kernel(wc_ref, ws_ref, ww_ref, bc_ref, bs_ref, bw_ref):
    nb, t = bc_ref.shape[2], bc_ref.shape[3]
    bc_ref[0, 0] = _toeplitz_t(wc_ref[0], nb, t, CMP_STRIDE)
    bs_ref[0, 0] = _toeplitz_t(ws_ref[0], bs_ref.shape[2], bs_ref.shape[3], 1)
    bw_ref[0, 0] = _toeplitz_t(ww_ref[0], bw_ref.shape[2], bw_ref.shape[3], 1)


def _bias_tables(rel_bias, t):
    tq, tk = ATT_TQ, ATT_TK
    g_, hpg = NSA_KV_GROUPS, NSA_HPG
    rb = rel_bias.astype(F32)
    dmax = REL_MAX_DISTANCE + 2 * max(SLC_TQ, SLC_TK, WIN_TQ, tq, tk)
    by_dist = rb[_rel_bucket(jnp.arange(dmax, dtype=jnp.int32))].T
    far = rb[REL_BUCKETS - 1][:, None]
    big = 1 << 30
    nb = t // CMP_STRIDE
    wc = _toeplitz_vec(by_dist, -(CMP_BLOCK - 1), t, t, 0, big)[:, None, :]
    sq, sk = min(SLC_TQ, t), SLC_TK
    r = sq // sk
    n_s = min(-(-(REL_MAX_DISTANCE + sk - 1) // sk) + r - 1, t // sk)
    wm = (n_s - 1) * sk + sq
    ws = _toeplitz_vec(by_dist, -(r - 1) * sk, wm, sk, 0, big, far)[:, None, :]
    wq = min(WIN_TQ, t)
    rw = wq // tk
    n_w = rw + (WINDOW - 1 + tk - 1) // tk
    wmw = n_w * tk + wq
    ww = _toeplitz_vec(by_dist, -(rw - 1) * tk, wmw, tk, 0, WINDOW)[:, None, :]
    return pl.pallas_call(
        _bias_tables_kernel,
        grid=(NSA_HEADS,),
        in_specs=[
            pl.BlockSpec((1, 1, 2 * t), lambda h: (h, 0, 0)),
            pl.BlockSpec((1, 1, wm + sk), lambda h: (h, 0, 0)),
            pl.BlockSpec((1, 1, wmw + tk), lambda h: (h, 0, 0)),
        ],
        out_specs=[
            pl.BlockSpec((1, 1, nb, t), lambda h: (h // hpg, h % hpg, 0, 0)),
            pl.BlockSpec((1, 1, sk, wm), lambda h: (h // hpg, h % hpg, 0, 0)),
            pl.BlockSpec((1, 1, tk, wmw), lambda h: (h // hpg, h % hpg, 0, 0)),
        ],
        out_shape=[
            jax.ShapeDtypeStruct((g_, hpg, nb, t), F32),
            jax.ShapeDtypeStruct((g_, hpg, sk, wm), F32),
            jax.ShapeDtypeStruct((g_, hpg, tk, wmw), F32),
        ],
        compiler_params=_cparams(("parallel",)),
        name="bias_tables",
    )(wc, ws, ww)


def _cover_t(t):
    nb = t // CMP_STRIDE
    n_cmp = (t - CMP_BLOCK) // CMP_STRIDE + 1
    n_slc = t // SLC_BLOCK
    cs = np.arange(nb) * CMP_STRIDE
    ss = np.arange(n_slc) * SLC_BLOCK
    cover = np.clip(np.minimum(cs[:, None] + CMP_BLOCK, ss[None, :] + SLC_BLOCK)
                    - np.maximum(cs[:, None], ss[None, :]), 0, None) / CMP_BLOCK
    cover[n_cmp:] = 0.0
    return jnp.asarray(cover.T, dtype=BF16)


def kernel(x, norm_gain, w_in, w_conv, b_igate, b_fgate, mlstm_norm_gain, cmp_k_pos, cmp_k_w1, cmp_k_w2,
           cmp_v_pos, cmp_v_w1, cmp_v_w2, rel_bias, w_out, final_norm_gain):
    bsz, t, d = x.shape
    assert d == D_MODEL and t % MLSTM_L == 0 and t % ATT_TQ == 0 and (t // CMP_STRIDE) % LANES == 0
    n = bsz * t
    x2d = x.reshape(n, d)

    w_main, w_gate = _w_prep(w_in.T)
    proj, gates = _in_proj(x2d, norm_gain.reshape(1, d).astype(F32), w_main, w_gate)

    g_rows = gates[:, :2 * MLSTM_HEADS].reshape(bsz, t, 2, MLSTM_HEADS).transpose(0, 3, 2, 1)
    y_m = _mlstm(proj, g_rows, w_conv.astype(F32), b_igate.astype(F32), b_fgate.astype(F32),
                 mlstm_norm_gain.reshape(1, MLSTM_WIDTH).astype(F32), bsz, t)

    dh = NSA_HEAD_DIM
    half = CMP_BLOCK // 2

    def w1cat(w1):
        return jnp.concatenate([w1[:half].reshape(half * dh, dh), w1[half:].reshape(half * dh, dh)], axis=1).astype(BF16)

    k_cmp, v_cmp_t = _compress(
        proj, w1cat(cmp_k_w1), cmp_k_w2.astype(BF16), cmp_k_pos.reshape(2, half * dh).astype(BF16),
        w1cat(cmp_v_w1), cmp_v_w2.astype(BF16), cmp_v_pos.reshape(2, half * dh).astype(BF16), bsz, t)

    bias_c, bias_s, bias_w = _bias_tables(rel_bias, t)
    gates_t = gates.T
    o_cmp, selbias_t = _cmp_attn(proj, gates_t, k_cmp, v_cmp_t, bias_c, _cover_t(t), bsz, t)
    o_slc = _slc_attn(proj, gates_t, selbias_t, bias_s, bsz, t)
    y_a = _win_attn(proj, gates_t, o_cmp, o_slc, bias_w, bsz, t)

    out = _out_proj(y_m, y_a, x2d, w_out.astype(BF16), final_norm_gain.reshape(1, d).astype(F32))
    return out.reshape(bsz, t, d)
```

```python
import functools
import math

import jax
import jax.numpy as jnp
import numpy as np
from jax import lax
from jax.experimental import pallas as pl
from jax.experimental.pallas import tpu as pltpu

F32 = jnp.float32
BF16 = jnp.bfloat16

D_MODEL = 4096
D_MIX = D_MODEL
MLSTM_WIDTH = D_MIX // 2
MLSTM_HEADS = 4
MLSTM_V_DIM = MLSTM_WIDTH // MLSTM_HEADS
MLSTM_QK_DIM = MLSTM_V_DIM // 2
MLSTM_QK_WIDTH = MLSTM_HEADS * MLSTM_QK_DIM
CONV_WIDTH = 4
NSA_WIDTH = D_MIX - MLSTM_WIDTH
NSA_HEAD_DIM = 128
NSA_HEADS = NSA_WIDTH // NSA_HEAD_DIM
NSA_KV_GROUPS = 4
NSA_HPG = NSA_HEADS // NSA_KV_GROUPS
NSA_KV_WIDTH = NSA_KV_GROUPS * NSA_HEAD_DIM
CMP_BLOCK = 32
CMP_STRIDE = 16
SLC_BLOCK = 64
SLC_TOP_N = 16
WINDOW = 512
FORCE_BONUS = 1000.0
REL_BUCKETS = 32
REL_MAX_EXACT = REL_BUCKETS // 2
REL_MAX_DISTANCE = 1024
RMS_EPS = 1e-6
NEG_LOGIT = -1e30

LANES = 128
VMEM_LIMIT_BYTES = 56 * 1024 * 1024

COL_MQ = 0
COL_MK = COL_MQ + MLSTM_QK_WIDTH
COL_MV = COL_MK + MLSTM_QK_WIDTH
COL_MO = COL_MV + MLSTM_WIDTH
COL_MZ = COL_MO + MLSTM_WIDTH
COL_AQ = COL_MZ + MLSTM_WIDTH
COL_CK = COL_AQ + NSA_WIDTH
COL_CV = COL_CK + NSA_KV_WIDTH
COL_SK = COL_CV + NSA_KV_WIDTH
COL_SV = COL_SK + NSA_KV_WIDTH
COL_WK = COL_SV + NSA_KV_WIDTH
COL_WV = COL_WK + NSA_KV_WIDTH
COL_AZ = COL_WV + NSA_KV_WIDTH
MAIN_COLS = COL_AZ + NSA_WIDTH
GATE_COLS = LANES
GATE_I = 0
GATE_F = MLSTM_HEADS
GATE_NSA = 2 * MLSTM_HEADS

INPROJ_TM = 512
INPROJ_TN = 1536
INPROJ_W_STREAMS = 3
OUTPROJ_TM = 512
OUTPROJ_TN = 1024
OUTPROJ_W_STREAMS = 2
MLSTM_L = 256
ATT_TQ = 256
ATT_TK = 256
WPREP_TR = 512
SLC_TQ = 512
SLC_TK = 512
CMP_TQ = 512
WIN_TQ = 256
MASK_BIG = 2.0 ** 100


def _cparams(sem):
    return pltpu.CompilerParams(dimension_semantics=sem, vmem_limit_bytes=VMEM_LIMIT_BYTES)


def _sigmoid(x):
    return 0.5 * jnp.tanh(0.5 * x) + 0.5


def _silu(x):
    return x * _sigmoid(x)


def _dot(a, b):
    return jnp.dot(a, b, preferred_element_type=F32)


def _dot_nt(a, b):
    return lax.dot_general(a, b, (((1,), (1,)), ((), ())), preferred_element_type=F32)


W_OFF_I = COL_MZ + MLSTM_WIDTH
W_OFF_AQ = W_OFF_I + 2 * MLSTM_HEADS
W_OFF_GATE = W_OFF_AQ + NSA_WIDTH + 6 * NSA_KV_WIDTH
W_OFF_AZ = W_OFF_GATE + 3 * NSA_HEADS


def _wprep_kernel(w_ref, gi_ref, ga_ref, o_ref, g_ref):
    o_ref[...] = w_ref[...].astype(BF16)

    @pl.when(pl.program_id(0) == 0)
    def _():
        pad = jnp.zeros((GATE_COLS - gi_ref.shape[0] - ga_ref.shape[0], g_ref.shape[1]), F32)
        g_ref[...] = jnp.concatenate([gi_ref[...], ga_ref[...], pad], axis=0).astype(BF16)


def _w_prep(w_t):
    rows, d = w_t.shape
    tr = WPREP_TR
    nb_a, nb_b = W_OFF_I // tr, (W_OFF_GATE - W_OFF_AQ) // tr

    def src(m):
        skip = jnp.where(m >= nb_a, W_OFF_AQ - W_OFF_I, 0) + jnp.where(m >= nb_a + nb_b, W_OFF_AZ - W_OFF_GATE, 0)
        return pl.multiple_of(m * tr + skip, 8)

    rows_at = lambda start, size: pl.BlockSpec((pl.Element(size), pl.Element(d)), lambda m: (start, 0))
    return pl.pallas_call(
        _wprep_kernel,
        grid=(MAIN_COLS // tr,),
        in_specs=[
            pl.BlockSpec((pl.Element(tr), pl.Element(d)), lambda m: (src(m), 0)),
            rows_at(W_OFF_I, W_OFF_AQ - W_OFF_I),
            rows_at(W_OFF_GATE, W_OFF_AZ - W_OFF_GATE),
        ],
        out_specs=[pl.BlockSpec((tr, d), lambda m: (m, 0)), pl.BlockSpec((GATE_COLS, d), lambda m: (0, 0))],
        out_shape=[jax.ShapeDtypeStruct((MAIN_COLS, d), BF16), jax.ShapeDtypeStruct((GATE_COLS, d), BF16)],
        compiler_params=_cparams(("arbitrary",)),
        name="w_prep",
    )(w_t, w_t, w_t)


def _inproj_kernel(x_ref, gain_ref, *refs):
    w_refs, (wg_ref, o_ref, og_ref, h_ref) = refs[:INPROJ_W_STREAMS], refs[INPROJ_W_STREAMS:]

    @pl.when(pl.program_id(1) == 0)
    def _():
        x = x_ref[...]
        ms = jnp.mean(x * x, axis=-1, keepdims=True)
        h = (x * lax.rsqrt(ms + RMS_EPS) * gain_ref[...]).astype(BF16)
        h_ref[...] = h
        og_ref[...] = _dot_nt(h, wg_ref[...])

    ts = o_ref.shape[1] // INPROJ_W_STREAMS
    for s, w_ref in enumerate(w_refs):
        o_ref[:, s * ts:(s + 1) * ts] = _dot_nt(h_ref[...], w_ref[...]).astype(BF16)


def _in_proj(x2d, gain, w_main, w_gate):
    n, d = x2d.shape
    tm, tn, ns = min(INPROJ_TM, n), INPROJ_TN, INPROJ_W_STREAMS
    w_spec = lambda s: pl.BlockSpec((tn // ns, d), lambda i, j: (ns * j + s, 0))
    return pl.pallas_call(
        _inproj_kernel,
        grid=(n // tm, MAIN_COLS // tn),
        in_specs=[
            pl.BlockSpec((tm, d), lambda i, j: (i, 0)),
            pl.BlockSpec((1, d), lambda i, j: (0, 0)),
            *[w_spec(s) for s in range(ns)],
            pl.BlockSpec((GATE_COLS, d), lambda i, j: (0, 0)),
        ],
        out_specs=[
            pl.BlockSpec((tm, tn), lambda i, j: (i, j)),
            pl.BlockSpec((tm, GATE_COLS), lambda i, j: (i, 0)),
        ],
        out_shape=[
            jax.ShapeDtypeStruct((n, MAIN_COLS), BF16),
            jax.ShapeDtypeStruct((n, GATE_COLS), F32),
        ],
        scratch_shapes=[pltpu.VMEM((tm, d), BF16)],
        compiler_params=_cparams(("parallel", "arbitrary")),
        name="in_proj",
    )(x2d, gain, *([w_main] * ns), w_gate)


def _mlstm_kernel(bi_ref, bf_ref, q_ref, k_ref, v_ref, o_ref, z_ref, g_ref, wq_ref, wk_ref, ng_ref,
                  y_ref, qext, kext, c_st, n_st, m_st):
    L = MLSTM_L
    HIST = 8
    dqk, dv = MLSTM_QK_DIM, MLSTM_V_DIM

    @pl.when(pl.program_id(1) == 0)
    def _():
        qext[:, 0:HIST, :] = jnp.zeros((MLSTM_HEADS, HIST, dqk), F32)
        kext[:, 0:HIST, :] = jnp.zeros((MLSTM_HEADS, HIST, dqk), F32)
        c_st[...] = jnp.zeros_like(c_st)
        n_st[...] = jnp.zeros_like(n_st)
        m_st[...] = jnp.zeros_like(m_st)

    rr = lax.broadcasted_iota(jnp.int32, (L, L), 0)
    cc = lax.broadcasted_iota(jnp.int32, (L, L), 1)
    upper = (rr <= cc).astype(F32)

    for hd in range(MLSTM_HEADS):
        qcols = slice(hd * dqk, (hd + 1) * dqk)
        vcols = slice(hd * dv, (hd + 1) * dv)
        qext[hd, HIST:HIST + L, :] = q_ref[:, qcols].astype(F32)
        kext[hd, HIST:HIST + L, :] = k_ref[:, qcols].astype(F32)

        def conv_silu(ext, w_ref):
            w = w_ref[:, qcols]
            y = ext[hd, pl.ds(HIST, L), :] * w[CONV_WIDTH - 1:CONV_WIDTH, :]
            for s in range(1, CONV_WIDTH):
                y = y + ext[hd, pl.ds(HIST - s, L), :] * w[CONV_WIDTH - 1 - s:CONV_WIDTH - s, :]
            return _silu(y)

        qc = conv_silu(qext, wq_ref)
        kc = conv_silu(kext, wk_ref) * (dqk ** -0.5)
        qext[hd, 0:HIST, :] = qext[hd, L:L + HIST, :]
        kext[hd, 0:HIST, :] = kext[hd, L:L + HIST, :]

        g = g_ref[0, hd]
        i_row = g[0:1, :] + bi_ref[hd]
        f_row = g[1:2, :] + bf_ref[hd]
        lf_row = jnp.minimum(f_row, 0.0) - jnp.log(1.0 + jnp.exp(-jnp.abs(f_row)))

        bcum_row = jnp.dot(jnp.broadcast_to(lf_row, (8, L)), upper, preferred_element_type=F32,
                           precision=lax.Precision.HIGHEST)[0:1, :]
        bcum_col = jnp.sum(jnp.where(rr == cc, bcum_row, 0.0), axis=1, keepdims=True)
        gsum = bcum_row[:, L - 1:L]
        m_prev = m_st[hd]

        dlog = jnp.where(rr >= cc, bcum_col - bcum_row + i_row, -jnp.inf)
        m_inter = bcum_col + m_prev
        m_t = jnp.maximum(m_inter, jnp.max(dlog, axis=1, keepdims=True))
        dmat = jnp.exp(dlog - m_t)
        inter = jnp.exp(m_inter - m_t)

        qb = qc.astype(BF16)
        kct = kc.T
        vb = v_ref[:, vcols]
        s = _dot(qb, kct.astype(BF16)) * dmat
        c_prev = c_st[hd]
        n_prev = n_st[hd]
        num = _dot(s.astype(BF16), vb) + inter * _dot(qb, c_prev.astype(BF16))
        qn = jnp.sum(s, axis=1, keepdims=True) + inter * jnp.sum(qc * n_prev, axis=1, keepdims=True)
        hh = num / jnp.maximum(jnp.abs(qn), jnp.exp(-m_t))

        wlog = gsum - bcum_row + i_row
        m_next = jnp.maximum(gsum + m_prev, jnp.max(wlog, axis=1, keepdims=True))
        wts = jnp.exp(wlog - m_next)
        keep = jnp.exp(gsum + m_prev - m_next)
        c_st[hd] = keep * c_prev + _dot((kct * wts).astype(BF16), vb)
        n_st[hd] = keep * n_prev + _dot(jnp.broadcast_to(wts, (8, L)).astype(BF16), kc.astype(BF16))[0:1, :]
        m_st[hd] = m_next

        hm = _sigmoid(o_ref[:, vcols].astype(F32)) * hh
        hm = hm * lax.rsqrt(jnp.mean(hm * hm, axis=-1, keepdims=True) + RMS_EPS)
        hm = hm * ng_ref[:, vcols]
        y_ref[:, vcols] = (hm * _silu(z_ref[:, vcols].astype(F32))).astype(BF16)


def _mlstm(proj, g_rows, w_conv, b_igate, b_fgate, norm_gain, bsz, t):
    L = MLSTM_L
    nc = t // L
    nh, dqk, dv = MLSTM_HEADS, MLSTM_QK_DIM, MLSTM_V_DIM
    row = lambda b, c: b * nc + c
    smem = pl.BlockSpec(memory_space=pltpu.SMEM)
    return pl.pallas_call(
        _mlstm_kernel,
        grid=(bsz, nc),
        in_specs=[
            smem, smem,
            pl.BlockSpec((L, MLSTM_QK_WIDTH), lambda b, c: (row(b, c), COL_MQ // MLSTM_QK_WIDTH)),
            pl.BlockSpec((L, MLSTM_QK_WIDTH), lambda b, c: (row(b, c), COL_MK // MLSTM_QK_WIDTH)),
            pl.BlockSpec((L, MLSTM_WIDTH), lambda b, c: (row(b, c), COL_MV // MLSTM_WIDTH)),
            pl.BlockSpec((L, MLSTM_WIDTH), lambda b, c: (row(b, c), COL_MO // MLSTM_WIDTH)),
            pl.BlockSpec((L, MLSTM_WIDTH), lambda b, c: (row(b, c), COL_MZ // MLSTM_WIDTH)),
            pl.BlockSpec((1, nh, 2, L), lambda b, c: (b, 0, 0, c)),
            pl.BlockSpec((CONV_WIDTH, MLSTM_QK_WIDTH), lambda b, c: (0, 0)),
            pl.BlockSpec((CONV_WIDTH, MLSTM_QK_WIDTH), lambda b, c: (0, 1)),
            pl.BlockSpec((1, MLSTM_WIDTH), lambda b, c: (0, 0)),
        ],
        out_specs=pl.BlockSpec((L, MLSTM_WIDTH), lambda b, c: (row(b, c), 0)),
        out_shape=jax.ShapeDtypeStruct((bsz * t, MLSTM_WIDTH), BF16),
        scratch_shapes=[
            pltpu.VMEM((nh, L + 8, dqk), F32),
            pltpu.VMEM((nh, L + 8, dqk), F32),
            pltpu.VMEM((nh, dqk, dv), F32),
            pltpu.VMEM((nh, 1, dqk), F32),
            pltpu.VMEM((nh, 1, 1), F32),
        ],
        compiler_params=_cparams(("parallel", "arbitrary")),
        name="mlstm",
    )(b_igate, b_fgate, proj, proj, proj, proj, proj, g_rows, w_conv, w_conv, norm_gain)


def _compress_kernel(ck_ref, cv_ref, w1k_ref, w2k_ref, pk_ref, w1v_ref, w2v_ref, pv_ref,
                     ok_ref, ov_ref, xf, xcat):
    t = ck_ref.shape[0]
    nb = t // CMP_STRIDE
    dh = NSA_HEAD_DIM

    def one(src_ref, w1_ref, w2_ref, pos_ref):
        xf[...] = src_ref[...].astype(F32)
        for l in range(CMP_STRIDE):
            xcat[:, l * dh:(l + 1) * dh] = xf[pl.ds(l, nb, stride=CMP_STRIDE), :].astype(BF16)
        w1 = w1_ref[...]
        ab = _dot(xcat[...], w1)
        pp = _dot(pos_ref[...], w1)
        pos_term = pp[0:1, 0:dh] + pp[1:2, dh:2 * dh]
        second = pltpu.roll(ab[:, dh:2 * dh], nb - 1, 0)
        hid = _silu(ab[:, 0:dh] + second + pos_term)
        return _dot(hid.astype(BF16), w2_ref[...])

    ok_ref[0, 0] = one(ck_ref, w1k_ref, w2k_ref, pk_ref).astype(BF16)
    ov_ref[0, 0] = one(cv_ref, w1v_ref, w2v_ref, pv_ref).T.astype(BF16)


def _compress(proj, w1k, w2k, pk, w1v, w2v, pv, bsz, t):
    g_, dh = NSA_KV_GROUPS, NSA_HEAD_DIM
    nb = t // CMP_STRIDE
    full = lambda a: pl.BlockSpec(a.shape, lambda b, g: (0,) * a.ndim)
    k_spec = pl.BlockSpec((1, 1, nb, dh), lambda b, g: (b, g, 0, 0))
    vt_spec = pl.BlockSpec((1, 1, dh, nb), lambda b, g: (b, g, 0, 0))
    return pl.pallas_call(
        _compress_kernel,
        grid=(bsz, g_),
        in_specs=[
            pl.BlockSpec((t, dh), lambda b, g: (b, COL_CK // dh + g)),
            pl.BlockSpec((t, dh), lambda b, g: (b, COL_CV // dh + g)),
            full(w1k), full(w2k), full(pk), full(w1v), full(w2v), full(pv),
        ],
        out_specs=[k_spec, vt_spec],
        out_shape=[jax.ShapeDtypeStruct((bsz, g_, nb, dh), BF16), jax.ShapeDtypeStruct((bsz, g_, dh, nb), BF16)],
        scratch_shapes=[pltpu.VMEM((t, dh), F32), pltpu.VMEM((nb, CMP_STRIDE * dh), BF16)],
        compiler_params=_cparams(("parallel", "parallel")),
        name="compress",
    )(proj, proj, w1k, w2k, pk, w1v, w2v, pv)


LOG2E = math.log2(math.e)
VROWS = NSA_HEAD_DIM + 16


def _q_scaled(q_ref, hh):
    dh = NSA_HEAD_DIM
    return (q_ref[:, hh * dh:(hh + 1) * dh].astype(F32) * (dh ** -0.5 * LOG2E)).astype(BF16)


def _q_t(q_ref, hh):
    dh = NSA_HEAD_DIM
    return (q_ref[:, hh * dh:(hh + 1) * dh].astype(F32) * (dh ** -0.5 * LOG2E)).T.astype(BF16)


def _gate_row(gt_ref, g, hh, branch):
    row = GATE_NSA + 3 * (g * NSA_HPG + hh) + branch
    return _sigmoid(gt_ref[pl.ds(row, 1), :])


def _fill_vt(vt, v_ref):
    dh = NSA_HEAD_DIM
    vt[0:dh, :] = v_ref[...].astype(F32).T.astype(BF16)
    vt[dh:VROWS, :] = jnp.ones((VROWS - dh, vt.shape[1]), BF16)


def _out_t(acc, gate_row):
    dh = NSA_HEAD_DIM
    return acc[0:dh, :] * (gate_row / acc[dh:dh + 1, :])


def _cmp_attn_kernel(q_ref, kc_ref, vct_ref, bias_ref, cov_ref, gate_ref, oc_ref, sel_ref, score_ref, *, n_sel):
    tq = q_ref.shape[0]
    dh = NSA_HEAD_DIM
    n_slc = cov_ref.shape[0]
    t0 = pl.program_id(1) * tq
    g = pl.program_id(0)
    kc = kc_ref[0, 0]
    vct = vct_ref[0, 0]
    heads = range(NSA_HPG)
    qk = [_dot_nt(kc, _q_scaled(q_ref, hh)) for hh in heads]
    ps = []
    for hh in heads:
        logit = qk[hh] + bias_ref[0, hh]
        m = jnp.max(logit, axis=0, keepdims=True)
        e = jnp.exp2(logit - m)
        inv = jnp.where(m > 0.5 * NEG_LOGIT, 1.0 / jnp.sum(e, axis=0, keepdims=True), 0.0)
        ps.append(e * inv)
    p_sum = (ps[0] + ps[1]) + (ps[2] + ps[3])
    ots = [_dot(vct, p.astype(BF16)) for p in ps]
    for hh in heads:
        oc_ref[0, hh] = (ots[hh] * _gate_row(gate_ref, g, hh, 0)).astype(BF16)

    cov = cov_ref[...]
    p_hi = p_sum.astype(BF16)
    r1 = p_sum - p_hi.astype(F32)
    p_mid = r1.astype(BF16)
    p_lo = (r1 - p_mid.astype(F32)).astype(BF16)
    st = _dot(cov, p_hi) + _dot(cov, p_mid) + _dot(cov, p_lo)

    jb = lax.broadcasted_iota(jnp.int32, (n_slc, tq), 0)
    cur = (t0 + lax.broadcasted_iota(jnp.int32, (n_slc, tq), 1)) // SLC_BLOCK
    valid = jb <= cur
    forced = (jb == 0) | (jb == cur) | (jb == cur - 1)
    score = jnp.where(valid, st + jnp.where(forced, FORCE_BONUS, 0.0), -1.0)
    sub = 8
    score_ref[...] = score
    groups = [score_ref[r:r + sub, :] for r in range(0, n_slc, sub)]
    ranks = [jnp.zeros((sub, tq), F32) for _ in groups]
    row_id = lax.broadcasted_iota(jnp.int32, (sub, tq), 0)
    for j2 in range(n_slc):
        row = score_ref[j2:j2 + 1, :]
        for gi, sc in enumerate(groups):
            r0 = gi * sub
            if r0 > j2:
                inc = jnp.where(row >= sc, 1.0, 0.0)
            elif r0 + sub - 1 <= j2:
                inc = jnp.where(row > sc, 1.0, 0.0)
            else:
                inc = jnp.where(row_id > j2 - r0, jnp.where(row >= sc, 1.0, 0.0), jnp.where(row > sc, 1.0, 0.0))
            ranks[gi] = ranks[gi] + inc
    for gi, rk in enumerate(ranks):
        score_ref[gi * sub:(gi + 1) * sub, :] = rk
    sel = valid & (score_ref[...] < n_sel)
    sel_ref[0, 0, 0:n_slc, :] = jnp.where(sel, 0.0, -MASK_BIG).astype(BF16)
    if n_slc < LANES:
        sel_ref[0, 0, n_slc:LANES, :] = jnp.zeros((LANES - n_slc, tq), BF16)


def _cmp_attn(proj, gates, k_cmp, v_cmp_t, bias_c, cover_t, bsz, t):
    g_, dh, tq = NSA_KV_GROUPS, NSA_HEAD_DIM, CMP_TQ
    nt = t // tq
    nb = t // CMP_STRIDE
    n_slc = t // SLC_BLOCK
    gw = NSA_HPG * dh
    return pl.pallas_call(
        functools.partial(_cmp_attn_kernel, n_sel=min(SLC_TOP_N, n_slc)),
        grid=(g_, nt, bsz),
        in_specs=[
            pl.BlockSpec((tq, gw), lambda g, i, b: (b * nt + i, COL_AQ // gw + g)),
            pl.BlockSpec((1, 1, nb, dh), lambda g, i, b: (b, g, 0, 0)),
            pl.BlockSpec((1, 1, dh, nb), lambda g, i, b: (b, g, 0, 0)),
            pl.BlockSpec((1, NSA_HPG, nb, tq), lambda g, i, b: (g, 0, 0, i)),
            pl.BlockSpec((n_slc, nb), lambda g, i, b: (0, 0)),
            pl.BlockSpec((GATE_COLS, tq), lambda g, i, b: (0, b * nt + i)),
        ],
        out_specs=[
            pl.BlockSpec((1, NSA_HPG, dh, tq), lambda g, i, b: (b, g, 0, i)),
            pl.BlockSpec((1, 1, LANES, tq), lambda g, i, b: (b, g, 0, i)),
        ],
        out_shape=[
            jax.ShapeDtypeStruct((bsz, NSA_HEADS, dh, t), BF16),
            jax.ShapeDtypeStruct((bsz, g_, LANES, t), BF16),
        ],
        scratch_shapes=[pltpu.VMEM((n_slc, tq), F32)],
        compiler_params=_cparams(("parallel", "parallel", "parallel")),
        name="cmp_attn",
    )(proj, k_cmp, v_cmp_t, bias_c, cover_t, gates)


def _slc_attn_kernel(q_ref, sbt_ref, k_ref, v_ref, bias_ref, gate_ref, os_ref,
                     kaug, vt, qt, s_buf, p_buf, a_buf, m_ref, acc_ref, *, n_near):
    tq, tk, dh = SLC_TQ, SLC_TK, NSA_HEAD_DIM
    r = tq // tk
    t = k_ref.shape[0]
    g = pl.program_id(1)
    i = pl.program_id(2)
    heads = range(NSA_HPG)

    @pl.when(i == 0)
    def _():
        kaug[:, 0:dh] = k_ref[...]
        blk = lax.broadcasted_iota(jnp.int32, (t, LANES), 0) // SLC_BLOCK
        lane = lax.broadcasted_iota(jnp.int32, (t, LANES), 1)
        kaug[:, dh:dh + LANES] = jnp.where(blk == lane, 1.0, 0.0).astype(BF16)
        _fill_vt(vt, v_ref)

    sbt = sbt_ref[0, 0]
    for hh in heads:
        qt[hh, 0:dh, :] = _q_t(q_ref, hh)
        qt[hh, dh:dh + LANES, :] = sbt

    def ktile(j):
        return kaug[pl.ds(pl.multiple_of(j * tk, tk), tk), :]

    def vtile(j):
        return vt[:, pl.ds(pl.multiple_of(j * tk, tk), tk)]

    m_ref[...] = jnp.full(m_ref.shape, NEG_LOGIT, F32)
    acc_ref[...] = jnp.zeros_like(acc_ref)
    p_buf[...] = jnp.zeros_like(p_buf)
    a_buf[...] = jnp.ones_like(a_buf)
    k0 = ktile(0)
    for hh in heads:
        s_buf[hh] = _dot(k0, qt[hh])

    last = (i + 1) * r - 1

    def stage(j, bias_fn):
        v_prev = vtile(jnp.maximum(j - 1, 0))
        k_next = ktile(jnp.minimum(j + 1, last))
        pv = [_dot(v_prev, p_buf[hh]) for hh in heads]
        s_next = [_dot(k_next, qt[hh]) for hh in heads]
        for hh in heads:
            acc_ref[hh] = a_buf[hh] * acc_ref[hh] + pv[hh]
        for hh in heads:
            s = s_buf[hh]
            if bias_fn is not None:
                s = s + bias_fn(hh)
            m_prev = m_ref[hh]
            m_new = jnp.maximum(m_prev, jnp.max(s, axis=0, keepdims=True))
            a_buf[hh] = jnp.exp2(m_prev - m_new)
            p_buf[hh] = jnp.exp2(s - m_new).astype(BF16)
            m_ref[hh] = m_new
        for hh in heads:
            s_buf[hh] = s_next[hh]

    n_far = jnp.maximum(i * r + r - n_near, 0)

    def far_body(j, c):
        stage(j, None)
        return c

    lax.fori_loop(0, n_far, far_body, 0)

    def near_body(j, c):
        off = pl.multiple_of((i * r + (r - 1) - j) * tk, tk)
        stage(j, lambda hh: bias_ref[0, hh, :, pl.ds(off, tq)])
        return c

    lax.fori_loop(n_far, last + 1, near_body, 0)

    v_last = vtile(last)
    pv = [_dot(v_last, p_buf[hh]) for hh in heads]
    for hh in heads:
        acc = a_buf[hh] * acc_ref[hh] + pv[hh]
        os_ref[0, hh] = _out_t(acc, _gate_row(gate_ref, g, hh, 1)).astype(BF16)


def _slc_attn(proj, gates, selbias_t, bias_s, bsz, t):
    g_, dh, tq, tk = NSA_KV_GROUPS, NSA_HEAD_DIM, SLC_TQ, SLC_TK
    nt = t // tq
    gw = NSA_HPG * dh
    n_near = (bias_s.shape[3] - tq) // tk + 1
    return pl.pallas_call(
        functools.partial(_slc_attn_kernel, n_near=n_near),
        grid=(bsz, g_, nt),
        in_specs=[
            pl.BlockSpec((tq, gw), lambda b, g, i: (b * nt + i, COL_AQ // gw + g)),
            pl.BlockSpec((1, 1, LANES, tq), lambda b, g, i: (b, g, 0, i)),
            pl.BlockSpec((t, dh), lambda b, g, i: (b, COL_SK // dh + g)),
            pl.BlockSpec((t, dh), lambda b, g, i: (b, COL_SV // dh + g)),
            pl.BlockSpec((1, NSA_HPG, tk, bias_s.shape[3]), lambda b, g, i: (g, 0, 0, 0)),
            pl.BlockSpec((GATE_COLS, tq), lambda b, g, i: (0, b * nt + i)),
        ],
        out_specs=pl.BlockSpec((1, NSA_HPG, dh, tq), lambda b, g, i: (b, g, 0, i)),
        out_shape=jax.ShapeDtypeStruct((bsz, NSA_HEADS, dh, t), BF16),
        scratch_shapes=[
            pltpu.VMEM((t, dh + LANES), BF16),
            pltpu.VMEM((VROWS, t), BF16),
            pltpu.VMEM((NSA_HPG, dh + LANES, tq), BF16),
            pltpu.VMEM((NSA_HPG, tk, tq), F32),
            pltpu.VMEM((NSA_HPG, tk, tq), BF16),
            pltpu.VMEM((NSA_HPG, 1, tq), F32),
            pltpu.VMEM((NSA_HPG, 1, tq), F32),
            pltpu.VMEM((NSA_HPG, VROWS, tq), F32),
        ],
        compiler_params=_cparams(("parallel", "parallel", "arbitrary")),
        name="slc_attn",
    )(proj, selbias_t, proj, proj, bias_s, gates)


def _win_attn_kernel(q_ref, k_ref, v_ref, bias_ref, gate_ref, oc_ref, os_ref, z_ref, ya_ref, vt, *, n_tiles):
    tq, tk, dh = WIN_TQ, ATT_TK, NSA_HEAD_DIM
    r = tq // tk
    g = pl.program_id(1)
    i = pl.program_id(2)
    heads = range(NSA_HPG)

    @pl.when(i == 0)
    def _():
        _fill_vt(vt, v_ref)

    js = [i * r + (r - 1) - d for d in range(n_tiles)]
    offs = [pl.multiple_of(jnp.maximum(j, 0) * tk, tk) for j in js]
    boffs = [pl.multiple_of(jnp.where(js[d] >= 0, d, n_tiles) * tk, tk) for d in range(n_tiles)]
    k_tiles = [k_ref[pl.ds(off, tk), :] for off in offs]
    v_tiles = [vt[:, pl.ds(off, tk)] for off in offs]
    qs = [_q_scaled(q_ref, hh) for hh in heads]
    ss = [[_dot_nt(k_tiles[d], qs[hh]) for d in range(n_tiles)] for hh in heads]
    ps = []
    for hh in heads:
        s = [ss[hh][d] + bias_ref[0, hh, :, pl.ds(boffs[d], tq)] for d in range(n_tiles)]
        m = functools.reduce(jnp.maximum, [jnp.max(x, axis=0, keepdims=True) for x in s])
        ps.append([jnp.exp2(x - m).astype(BF16) for x in s])
    pvs = [[_dot(v_tiles[d], ps[hh][d]) for d in range(n_tiles)] for hh in heads]
    for hh in heads:
        cols = slice(hh * dh, (hh + 1) * dh)
        o = _out_t(functools.reduce(lambda x, y: x + y, pvs[hh]), _gate_row(gate_ref, g, hh, 2))
        o = o + oc_ref[0, hh].astype(F32) + os_ref[0, hh].astype(F32)
        ya_ref[:, cols] = (o.T * _silu(z_ref[:, cols].astype(F32))).astype(BF16)


def _win_attn(proj, gates, o_cmp, o_slc, bias_w, bsz, t):
    g_, dh, tq, tk = NSA_KV_GROUPS, NSA_HEAD_DIM, WIN_TQ, ATT_TK
    nt = t // tq
    gw = NSA_HPG * dh
    n_tiles = (bias_w.shape[3] - tq) // tk
    return pl.pallas_call(
        functools.partial(_win_attn_kernel, n_tiles=n_tiles),
        grid=(bsz, g_, nt),
        in_specs=[
            pl.BlockSpec((tq, gw), lambda b, g, i: (b * nt + i, COL_AQ // gw + g)),
            pl.BlockSpec((t, dh), lambda b, g, i: (b, COL_WK // dh + g)),
            pl.BlockSpec((t, dh), lambda b, g, i: (b, COL_WV // dh + g)),
            pl.BlockSpec((1, NSA_HPG, tk, bias_w.shape[3]), lambda b, g, i: (g, 0, 0, 0)),
            pl.BlockSpec((GATE_COLS, tq), lambda b, g, i: (0, b * nt + i)),
            pl.BlockSpec((1, NSA_HPG, dh, tq), lambda b, g, i: (b, g, 0, i)),
            pl.BlockSpec((1, NSA_HPG, dh, tq), lambda b, g, i: (b, g, 0, i)),
            pl.BlockSpec((tq, gw), lambda b, g, i: (b * nt + i, COL_AZ // gw + g)),
        ],
        out_specs=pl.BlockSpec((tq, gw), lambda b, g, i: (b * nt + i, g)),
        out_shape=jax.ShapeDtypeStruct((bsz * t, NSA_WIDTH), BF16),
        scratch_shapes=[pltpu.VMEM((VROWS, t), BF16)],
        compiler_params=_cparams(("parallel", "parallel", "arbitrary")),
        name="win_attn",
    )(proj, proj, proj, bias_w, gates, o_cmp, o_slc, proj)


def _outproj_kernel(ym_ref, ya_ref, x_ref, *refs):
    ns = OUTPROJ_W_STREAMS
    w1_refs, w2_refs, (gain_ref, o_ref, rows) = refs[:ns], refs[ns:2 * ns], refs[2 * ns:]
    j = pl.program_id(1)
    nj = rows.shape[0]
    tn = rows.shape[2]
    ts = tn // ns
    for s in range(ns):
        cols = slice(s * ts, (s + 1) * ts)
        rows[j, :, cols] = x_ref[:, cols] + _dot(ym_ref[...], w1_refs[s][...]) + _dot(ya_ref[...], w2_refs[s][...])

    @pl.when(j == nj - 1)
    def _():
        ss = None
        for jj in range(nj):
            y = rows[jj]
            part = jnp.sum(y * y, axis=-1, keepdims=True)
            ss = part if ss is None else ss + part
        inv = lax.rsqrt(ss / (nj * tn) + RMS_EPS)
        for jj in range(nj):
            o_ref[:, jj * tn:(jj + 1) * tn] = rows[jj] * inv * gain_ref[:, jj * tn:(jj + 1) * tn]


def _out_proj(y_m, y_a, x2d, w_out, gain):
    n, d = x2d.shape
    tm, tn, ns = min(OUTPROJ_TM, n), OUTPROJ_TN, OUTPROJ_W_STREAMS
    nj = d // tn
    w_spec = lambda half, s: pl.BlockSpec((MLSTM_WIDTH, tn // ns), lambda i, j: (half, ns * j + s))
    return pl.pallas_call(
        _outproj_kernel,
        grid=(n // tm, nj),
        in_specs=[
            pl.BlockSpec((tm, MLSTM_WIDTH), lambda i, j: (i, 0)),
            pl.BlockSpec((tm, NSA_WIDTH), lambda i, j: (i, 0)),
            pl.BlockSpec((tm, tn), lambda i, j: (i, j)),
            *[w_spec(0, s) for s in range(ns)],
            *[w_spec(1, s) for s in range(ns)],
            pl.BlockSpec((1, d), lambda i, j: (0, 0)),
        ],
        out_specs=pl.BlockSpec((tm, d), lambda i, j: (i, 0)),
        out_shape=jax.ShapeDtypeStruct((n, d), F32),
        scratch_shapes=[pltpu.VMEM((nj, tm, tn), F32)],
        compiler_params=_cparams(("parallel", "arbitrary")),
        name="out_proj",
    )(y_m, y_a, x2d, *([w_out] * (2 * ns)), gain)


def _rel_bucket(dist):
    n = jnp.maximum(dist, 0)
    nf = jnp.maximum(n, REL_MAX_EXACT).astype(jnp.float32)
    large = REL_MAX_EXACT + (jnp.log(nf / REL_MAX_EXACT) / math.log(REL_MAX_DISTANCE / REL_MAX_EXACT)
                             * (REL_BUCKETS - REL_MAX_EXACT)).astype(jnp.int32)
    large = jnp.minimum(large, REL_BUCKETS - 1)
    return jnp.where(n < REL_MAX_EXACT, n, large)


def _toeplitz_vec(by_dist, base, n_pos, n_neg, lo, hi, shift=None):
    w = n_pos + n_neg
    c = np.arange(w)
    dist = np.where(c < n_pos, base + c, base - (w - c))
    ok = (dist >= lo) & (dist < hi)
    vals = by_dist[:, np.clip(dist, 0, by_dist.shape[1] - 1)]
    if shift is not None:
        vals = vals - shift
    return jnp.where(ok[None], vals * LOG2E, NEG_LOGIT)


def _toeplitz_t(w_row, n_keys, n_q, key_step):
    x = jnp.broadcast_to(w_row, (n_keys, w_row.shape[1]))
    return pltpu.roll(x, 0, 1, stride=key_step, stride_axis=0)[:, 0:n_q]


def _bias_tables_kernel(wc_ref, ws_ref, ww_ref, bc_ref, bs_ref, bw_ref):
    nb, t = bc_ref.shape[2], bc_ref.shape[3]
    bc_ref[0, 0] = _toeplitz_t(wc_ref[0], nb, t, CMP_STRIDE)
    bs_ref[0, 0] = _toeplitz_t(ws_ref[0], bs_ref.shape[2], bs_ref.shape[3], 1)
    bw_ref[0, 0] = _toeplitz_t(ww_ref[0], bw_ref.shape[2], bw_ref.shape[3], 1)


def _bias_tables(rel_bias, t):
    tq, tk = ATT_TQ, ATT_TK
    g_, hpg = NSA_KV_GROUPS, NSA_HPG
    rb = rel_bias.astype(F32)
    dmax = REL_MAX_DISTANCE + 2 * max(SLC_TQ, SLC_TK, WIN_TQ, tq, tk)
    by_dist = rb[_rel_bucket(jnp.arange(dmax, dtype=jnp.int32))].T
    far = rb[REL_BUCKETS - 1][:, None]
    big = 1 << 30
    nb = t // CMP_STRIDE
    wc = _toeplitz_vec(by_dist, -(CMP_BLOCK - 1), t, t, 0, big)[:, None, :]
    sq, sk = min(SLC_TQ, t), SLC_TK
    r = sq // sk
    n_s = min(-(-(REL_MAX_DISTANCE + sk - 1) // sk) + r - 1, t // sk)
    wm = (n_s - 1) * sk + sq
    ws = _toeplitz_vec(by_dist, -(r - 1) * sk, wm, sk, 0, big, far)[:, None, :]
    wq = min(WIN_TQ, t)
    rw = wq // tk
    n_w = rw + (WINDOW - 1 + tk - 1) // tk
    wmw = n_w * tk + wq
    ww = _toeplitz_vec(by_dist, -(rw - 1) * tk, wmw, tk, 0, WINDOW)[:, None, :]
    return pl.pallas_call(
        _bias_tables_kernel,
        grid=(NSA_HEADS,),
        in_specs=[
            pl.BlockSpec((1, 1, 2 * t), lambda h: (h, 0, 0)),
            pl.BlockSpec((1, 1, wm + sk), lambda h: (h, 0, 0)),
            pl.BlockSpec((1, 1, wmw + tk), lambda h: (h, 0, 0)),
        ],
        out_specs=[
            pl.BlockSpec((1, 1, nb, t), lambda h: (h // hpg, h % hpg, 0, 0)),
            pl.BlockSpec((1, 1, sk, wm), lambda h: (h // hpg, h % hpg, 0, 0)),
            pl.BlockSpec((1, 1, tk, wmw), lambda h: (h // hpg, h % hpg, 0, 0)),
        ],
        out_shape=[
            jax.ShapeDtypeStruct((g_, hpg, nb, t), F32),
            jax.ShapeDtypeStruct((g_, hpg, sk, wm), F32),
            jax.ShapeDtypeStruct((g_, hpg, tk, wmw), F32),
        ],
        compiler_params=_cparams(("parallel",)),
        name="bias_tables",
    )(wc, ws, ww)


def _cover_t(t):
    nb = t // CMP_STRIDE
    n_cmp = (t - CMP_BLOCK) // CMP_STRIDE + 1
    n_slc = t // SLC_BLOCK
    cs = np.arange(nb) * CMP_STRIDE
    ss = np.arange(n_slc) * SLC_BLOCK
    cover = np.clip(np.minimum(cs[:, None] + CMP_BLOCK, ss[None, :] + SLC_BLOCK)
                    - np.maximum(cs[:, None], ss[None, :]), 0, None) / CMP_BLOCK
    cover[n_cmp:] = 0.0
    return jnp.asarray(cover.T, dtype=BF16)


def kernel(x, norm_gain, w_in, w_conv, b_igate, b_fgate, mlstm_norm_gain, cmp_k_pos, cmp_k_w1, cmp_k_w2,
           cmp_v_pos, cmp_v_w1, cmp_v_w2, rel_bias, w_out, final_norm_gain):
    bsz, t, d = x.shape
    assert d == D_MODEL and t % MLSTM_L == 0 and t % ATT_TQ == 0 and (t // CMP_STRIDE) % LANES == 0
    n = bsz * t
    x2d = x.reshape(n, d)

    w_main, w_gate = _w_prep(w_in.T)
    proj, gates = _in_proj(x2d, norm_gain.reshape(1, d).astype(F32), w_main, w_gate)

    g_rows = gates[:, :2 * MLSTM_HEADS].reshape(bsz, t, 2, MLSTM_HEADS).transpose(0, 3, 2, 1)
    y_m = _mlstm(proj, g_rows, w_conv.astype(F32), b_igate.astype(F32), b_fgate.astype(F32),
                 mlstm_norm_gain.reshape(1, MLSTM_WIDTH).astype(F32), bsz, t)

    dh = NSA_HEAD_DIM
    half = CMP_BLOCK // 2

    def w1cat(w1):
        return jnp.concatenate([w1[:half].reshape(half * dh, dh), w1[half:].reshape(half * dh, dh)], axis=1).astype(BF16)

    k_cmp, v_cmp_t = _compress(
        proj, w1cat(cmp_k_w1), cmp_k_w2.astype(BF16), cmp_k_pos.reshape(2, half * dh).astype(BF16),
        w1cat(cmp_v_w1), cmp_v_w2.astype(BF16), cmp_v_pos.reshape(2, half * dh).astype(BF16), bsz, t)

    bias_c, bias_s, bias_w = _bias_tables(rel_bias, t)
    gates_t = gates.T
    o_cmp, selbias_t = _cmp_attn(proj, gates_t, k_cmp, v_cmp_t, bias_c, _cover_t(t), bsz, t)
    o_slc = _slc_attn(proj, gates_t, selbias_t, bias_s, bsz, t)
    y_a = _win_attn(proj, gates_t, o_cmp, o_slc, bias_w, bsz, t)

    out = _out_proj(y_m, y_a, x2d, w_out.astype(BF16), final_norm_gain.reshape(1, d).astype(F32))
    return out.reshape(bsz, t, d)
```

```python
import functools
import math

import jax
import jax.numpy as jnp
import numpy as np
from jax import lax
from jax.experimental import pallas as pl
from jax.experimental.pallas import tpu as pltpu

F32 = jnp.float32
BF16 = jnp.bfloat16

D_MODEL = 4096
D_MIX = D_MODEL
MLSTM_WIDTH = D_MIX // 2
MLSTM_HEADS = 4
MLSTM_V_DIM = MLSTM_WIDTH // MLSTM_HEADS
MLSTM_QK_DIM = MLSTM_V_DIM // 2
MLSTM_QK_WIDTH = MLSTM_HEADS * MLSTM_QK_DIM
CONV_WIDTH = 4
NSA_WIDTH = D_MIX - MLSTM_WIDTH
NSA_HEAD_DIM = 128
NSA_HEADS = NSA_WIDTH // NSA_HEAD_DIM
NSA_KV_GROUPS = 4
NSA_HPG = NSA_HEADS // NSA_KV_GROUPS
NSA_KV_WIDTH = NSA_KV_GROUPS * NSA_HEAD_DIM
CMP_BLOCK = 32
CMP_STRIDE = 16
SLC_BLOCK = 64
SLC_TOP_N = 16
WINDOW = 512
FORCE_BONUS = 1000.0
REL_BUCKETS = 32
REL_MAX_EXACT = REL_BUCKETS // 2
REL_MAX_DISTANCE = 1024
RMS_EPS = 1e-6
NEG_LOGIT = -1e30

LANES = 128
VMEM_LIMIT_BYTES = 56 * 1024 * 1024

COL_MQ = 0
COL_MK = COL_MQ + MLSTM_QK_WIDTH
COL_MV = COL_MK + MLSTM_QK_WIDTH
COL_MO = COL_MV + MLSTM_WIDTH
COL_MZ = COL_MO + MLSTM_WIDTH
COL_AQ = COL_MZ + MLSTM_WIDTH
COL_CK = COL_AQ + NSA_WIDTH
COL_CV = COL_CK + NSA_KV_WIDTH
COL_SK = COL_CV + NSA_KV_WIDTH
COL_SV = COL_SK + NSA_KV_WIDTH
COL_WK = COL_SV + NSA_KV_WIDTH
COL_WV = COL_WK + NSA_KV_WIDTH
COL_AZ = COL_WV + NSA_KV_WIDTH
MAIN_COLS = COL_AZ + NSA_WIDTH
GATE_COLS = LANES
GATE_I = 0
GATE_F = MLSTM_HEADS
GATE_NSA = 2 * MLSTM_HEADS

INPROJ_TM = 512
INPROJ_TN = 1536
OUTPROJ_TM = 512
OUTPROJ_TN = 1024
MLSTM_L = 256
ATT_TQ = 256
ATT_TK = 256
WPREP_TR = 512
SLC_TQ = 512
SLC_TK = 512
CMP_TQ = 512
WIN_TQ = 256
WIN_GROUPS = 2
MASK_BIG = 2.0 ** 100


def _cparams(sem):
    return pltpu.CompilerParams(dimension_semantics=sem, vmem_limit_bytes=VMEM_LIMIT_BYTES)


def _sigmoid(x):
    return 0.5 * jnp.tanh(0.5 * x) + 0.5


def _silu(x):
    return x * _sigmoid(x)


def _dot(a, b):
    return jnp.dot(a, b, preferred_element_type=F32)


def _dot_nt(a, b):
    return lax.dot_general(a, b, (((1,), (1,)), ((), ())), preferred_element_type=F32)


W_OFF_I = COL_MZ + MLSTM_WIDTH
W_OFF_AQ = W_OFF_I + 2 * MLSTM_HEADS
W_OFF_GATE = W_OFF_AQ + NSA_WIDTH + 6 * NSA_KV_WIDTH
W_OFF_AZ = W_OFF_GATE + 3 * NSA_HEADS


def _wprep_kernel(w_ref, gi_ref, ga_ref, o_ref, g_ref):
    o_ref[...] = w_ref[...].astype(BF16)

    @pl.when(pl.program_id(0) == 0)
    def _():
        pad = jnp.zeros((GATE_COLS - gi_ref.shape[0] - ga_ref.shape[0], g_ref.shape[1]), F32)
        g_ref[...] = jnp.concatenate([gi_ref[...], ga_ref[...], pad], axis=0).astype(BF16)


def _w_prep(w_t):
    rows, d = w_t.shape
    tr = WPREP_TR
    nb_a, nb_b = W_OFF_I // tr, (W_OFF_GATE - W_OFF_AQ) // tr

    def src(m):
        skip = jnp.where(m >= nb_a, W_OFF_AQ - W_OFF_I, 0) + jnp.where(m >= nb_a + nb_b, W_OFF_AZ - W_OFF_GATE, 0)
        return pl.multiple_of(m * tr + skip, 8)

    rows_at = lambda start, size: pl.BlockSpec((pl.Element(size), pl.Element(d)), lambda m: (start, 0))
    return pl.pallas_call(
        _wprep_kernel,
        grid=(MAIN_COLS // tr,),
        in_specs=[
            pl.BlockSpec((pl.Element(tr), pl.Element(d)), lambda m: (src(m), 0)),
            rows_at(W_OFF_I, W_OFF_AQ - W_OFF_I),
            rows_at(W_OFF_GATE, W_OFF_AZ - W_OFF_GATE),
        ],
        out_specs=[pl.BlockSpec((tr, d), lambda m: (m, 0)), pl.BlockSpec((GATE_COLS, d), lambda m: (0, 0))],
        out_shape=[jax.ShapeDtypeStruct((MAIN_COLS, d), BF16), jax.ShapeDtypeStruct((GATE_COLS, d), BF16)],
        compiler_params=_cparams(("arbitrary",)),
        name="w_prep",
    )(w_t, w_t, w_t)


def _inproj_kernel(x_ref, gain_ref, w_ref, wg_ref, o_ref, og_ref, h_ref):
    @pl.when(pl.program_id(1) == 0)
    def _():
        x = x_ref[...]
        ms = jnp.mean(x * x, axis=-1, keepdims=True)
        h = (x * lax.rsqrt(ms + RMS_EPS) * gain_ref[...]).astype(BF16)
        h_ref[...] = h
        og_ref[...] = _dot_nt(h, wg_ref[...])

    o_ref[...] = _dot_nt(h_ref[...], w_ref[...]).astype(BF16)


def _in_proj(x2d, gain, w_main, w_gate):
    n, d = x2d.shape
    tm, tn = min(INPROJ_TM, n), INPROJ_TN
    return pl.pallas_call(
        _inproj_kernel,
        grid=(n // tm, MAIN_COLS // tn),
        in_specs=[
            pl.BlockSpec((tm, d), lambda i, j: (i, 0)),
            pl.BlockSpec((1, d), lambda i, j: (0, 0)),
            pl.BlockSpec((tn, d), lambda i, j: (j, 0)),
            pl.BlockSpec((GATE_COLS, d), lambda i, j: (0, 0)),
        ],
        out_specs=[
            pl.BlockSpec((tm, tn), lambda i, j: (i, j)),
            pl.BlockSpec((tm, GATE_COLS), lambda i, j: (i, 0)),
        ],
        out_shape=[
            jax.ShapeDtypeStruct((n, MAIN_COLS), BF16),
            jax.ShapeDtypeStruct((n, GATE_COLS), F32),
        ],
        scratch_shapes=[pltpu.VMEM((tm, d), BF16)],
        compiler_params=_cparams(("parallel", "arbitrary")),
        name="in_proj",
    )(x2d, gain, w_main, w_gate)


def _mlstm_kernel(bi_ref, bf_ref, q_ref, k_ref, v_ref, o_ref, z_ref, g_ref, wq_ref, wk_ref, ng_ref,
                  y_ref, qext, kext, c_st, n_st, m_st):
    L = MLSTM_L
    HIST = 8
    dqk, dv = MLSTM_QK_DIM, MLSTM_V_DIM

    @pl.when(pl.program_id(1) == 0)
    def _():
        qext[:, 0:HIST, :] = jnp.zeros((MLSTM_HEADS, HIST, dqk), F32)
        kext[:, 0:HIST, :] = jnp.zeros((MLSTM_HEADS, HIST, dqk), F32)
        c_st[...] = jnp.zeros_like(c_st)
        n_st[...] = jnp.zeros_like(n_st)
        m_st[...] = jnp.zeros_like(m_st)

    rr = lax.broadcasted_iota(jnp.int32, (L, L), 0)
    cc = lax.broadcasted_iota(jnp.int32, (L, L), 1)
    upper = (rr <= cc).astype(F32)

    for hd in range(MLSTM_HEADS):
        qcols = slice(hd * dqk, (hd + 1) * dqk)
        vcols = slice(hd * dv, (hd + 1) * dv)
        qext[hd, HIST:HIST + L, :] = q_ref[:, qcols].astype(F32)
        kext[hd, HIST:HIST + L, :] = k_ref[:, qcols].astype(F32)

        def conv_silu(ext, w_ref):
            w = w_ref[:, qcols]
            y = ext[hd, pl.ds(HIST, L), :] * w[CONV_WIDTH - 1:CONV_WIDTH, :]
            for s in range(1, CONV_WIDTH):
                y = y + ext[hd, pl.ds(HIST - s, L), :] * w[CONV_WIDTH - 1 - s:CONV_WIDTH - s, :]
            return _silu(y)

        qc = conv_silu(qext, wq_ref)
        kc = conv_silu(kext, wk_ref) * (dqk ** -0.5)
        qext[hd, 0:HIST, :] = qext[hd, L:L + HIST, :]
        kext[hd, 0:HIST, :] = kext[hd, L:L + HIST, :]

        g = g_ref[0, hd]
        i_row = g[0:1, :] + bi_ref[hd]
        f_row = g[1:2, :] + bf_ref[hd]
        lf_row = jnp.minimum(f_row, 0.0) - jnp.log(1.0 + jnp.exp(-jnp.abs(f_row)))

        bcum_row = jnp.dot(jnp.broadcast_to(lf_row, (8, L)), upper, preferred_element_type=F32,
                           precision=lax.Precision.HIGHEST)[0:1, :]
        bcum_col = jnp.sum(jnp.where(rr == cc, bcum_row, 0.0), axis=1, keepdims=True)
        gsum = bcum_row[:, L - 1:L]
        m_prev = m_st[hd]

        dlog = jnp.where(rr >= cc, bcum_col - bcum_row + i_row, -jnp.inf)
        m_inter = bcum_col + m_prev
        m_t = jnp.maximum(m_inter, jnp.max(dlog, axis=1, keepdims=True))
        dmat = jnp.exp(dlog - m_t)
        inter = jnp.exp(m_inter - m_t)

        qb = qc.astype(BF16)
        kct = kc.T
        vb = v_ref[:, vcols]
        s = _dot(qb, kct.astype(BF16)) * dmat
        c_prev = c_st[hd]
        n_prev = n_st[hd]
        num = _dot(s.astype(BF16), vb) + inter * _dot(qb, c_prev.astype(BF16))
        qn = jnp.sum(s, axis=1, keepdims=True) + inter * jnp.sum(qc * n_prev, axis=1, keepdims=True)
        hh = num / jnp.maximum(jnp.abs(qn), jnp.exp(-m_t))

        wlog = gsum - bcum_row + i_row
        m_next = jnp.maximum(gsum + m_prev, jnp.max(wlog, axis=1, keepdims=True))
        wts = jnp.exp(wlog - m_next)
        keep = jnp.exp(gsum + m_prev - m_next)
        c_st[hd] = keep * c_prev + _dot((kct * wts).astype(BF16), vb)
        n_st[hd] = keep * n_prev + _dot(jnp.broadcast_to(wts, (8, L)).astype(BF16), kc.astype(BF16))[0:1, :]
        m_st[hd] = m_next

        hm = _sigmoid(o_ref[:, vcols].astype(F32)) * hh
        hm = hm * lax.rsqrt(jnp.mean(hm * hm, axis=-1, keepdims=True) + RMS_EPS)
        hm = hm * ng_ref[:, vcols]
        y_ref[:, vcols] = (hm * _silu(z_ref[:, vcols].astype(F32))).astype(BF16)


def _mlstm(proj, g_rows, w_conv, b_igate, b_fgate, norm_gain, bsz, t):
    L = MLSTM_L
    nc = t // L
    nh, dqk, dv = MLSTM_HEADS, MLSTM_QK_DIM, MLSTM_V_DIM
    row = lambda b, c: b * nc + c
    smem = pl.BlockSpec(memory_space=pltpu.SMEM)
    return pl.pallas_call(
        _mlstm_kernel,
        grid=(bsz, nc),
        in_specs=[
            smem, smem,
            pl.BlockSpec((L, MLSTM_QK_WIDTH), lambda b, c: (row(b, c), COL_MQ // MLSTM_QK_WIDTH)),
            pl.BlockSpec((L, MLSTM_QK_WIDTH), lambda b, c: (row(b, c), COL_MK // MLSTM_QK_WIDTH)),
            pl.BlockSpec((L, MLSTM_WIDTH), lambda b, c: (row(b, c), COL_MV // MLSTM_WIDTH)),
            pl.BlockSpec((L, MLSTM_WIDTH), lambda b, c: (row(b, c), COL_MO // MLSTM_WIDTH)),
            pl.BlockSpec((L, MLSTM_WIDTH), lambda b, c: (row(b, c), COL_MZ // MLSTM_WIDTH)),
            pl.BlockSpec((1, nh, 2, L), lambda b, c: (b, 0, 0, c)),
            pl.BlockSpec((CONV_WIDTH, MLSTM_QK_WIDTH), lambda b, c: (0, 0)),
            pl.BlockSpec((CONV_WIDTH, MLSTM_QK_WIDTH), lambda b, c: (0, 1)),
            pl.BlockSpec((1, MLSTM_WIDTH), lambda b, c: (0, 0)),
        ],
        out_specs=pl.BlockSpec((L, MLSTM_WIDTH), lambda b, c: (row(b, c), 0)),
        out_shape=jax.ShapeDtypeStruct((bsz * t, MLSTM_WIDTH), BF16),
        scratch_shapes=[
            pltpu.VMEM((nh, L + 8, dqk), F32),
            pltpu.VMEM((nh, L + 8, dqk), F32),
            pltpu.VMEM((nh, dqk, dv), F32),
            pltpu.VMEM((nh, 1, dqk), F32),
            pltpu.VMEM((nh, 1, 1), F32),
        ],
        compiler_params=_cparams(("parallel", "arbitrary")),
        name="mlstm",
    )(b_igate, b_fgate, proj, proj, proj, proj, proj, g_rows, w_conv, w_conv, norm_gain)


def _compress_kernel(ck_ref, cv_ref, w1k_ref, w2k_ref, pk_ref, w1v_ref, w2v_ref, pv_ref,
                     ok_ref, ov_ref, xf, xcat):
    t = ck_ref.shape[0]
    nb = t // CMP_STRIDE
    dh = NSA_HEAD_DIM

    def one(src_ref, w1_ref, w2_ref, pos_ref):
        xf[...] = src_ref[...].astype(F32)
        for l in range(CMP_STRIDE):
            xcat[:, l * dh:(l + 1) * dh] = xf[pl.ds(l, nb, stride=CMP_STRIDE), :].astype(BF16)
        w1 = w1_ref[...]
        ab = _dot(xcat[...], w1)
        pp = _dot(pos_ref[...], w1)
        pos_term = pp[0:1, 0:dh] + pp[1:2, dh:2 * dh]
        second = pltpu.roll(ab[:, dh:2 * dh], nb - 1, 0)
        hid = _silu(ab[:, 0:dh] + second + pos_term)
        return _dot(hid.astype(BF16), w2_ref[...])

    ok_ref[0, 0] = one(ck_ref, w1k_ref, w2k_ref, pk_ref).astype(BF16)
    ov_ref[0, 0] = one(cv_ref, w1v_ref, w2v_ref, pv_ref).T.astype(BF16)


def _compress(proj, w1k, w2k, pk, w1v, w2v, pv, bsz, t):
    g_, dh = NSA_KV_GROUPS, NSA_HEAD_DIM
    nb = t // CMP_STRIDE
    full = lambda a: pl.BlockSpec(a.shape, lambda b, g: (0,) * a.ndim)
    k_spec = pl.BlockSpec((1, 1, nb, dh), lambda b, g: (b, g, 0, 0))
    vt_spec = pl.BlockSpec((1, 1, dh, nb), lambda b, g: (b, g, 0, 0))
    return pl.pallas_call(
        _compress_kernel,
        grid=(bsz, g_),
        in_specs=[
            pl.BlockSpec((t, dh), lambda b, g: (b, COL_CK // dh + g)),
            pl.BlockSpec((t, dh), lambda b, g: (b, COL_CV // dh + g)),
            full(w1k), full(w2k), full(pk), full(w1v), full(w2v), full(pv),
        ],
        out_specs=[k_spec, vt_spec],
        out_shape=[jax.ShapeDtypeStruct((bsz, g_, nb, dh), BF16), jax.ShapeDtypeStruct((bsz, g_, dh, nb), BF16)],
        scratch_shapes=[pltpu.VMEM((t, dh), F32), pltpu.VMEM((nb, CMP_STRIDE * dh), BF16)],
        compiler_params=_cparams(("parallel", "parallel")),
        name="compress",
    )(proj, proj, w1k, w2k, pk, w1v, w2v, pv)


LOG2E = math.log2(math.e)
VROWS = NSA_HEAD_DIM + 16


def _q_scaled(q_ref, hh):
    dh = NSA_HEAD_DIM
    return (q_ref[:, hh * dh:(hh + 1) * dh].astype(F32) * (dh ** -0.5 * LOG2E)).astype(BF16)


def _q_t(q_ref, hh):
    dh = NSA_HEAD_DIM
    return (q_ref[:, hh * dh:(hh + 1) * dh].astype(F32) * (dh ** -0.5 * LOG2E)).T.astype(BF16)


def _gate_row(gt_ref, g, hh, branch):
    row = GATE_NSA + 3 * (g * NSA_HPG + hh) + branch
    return _sigmoid(gt_ref[pl.ds(row, 1), :])


def _fill_vt(vt, v_ref):
    dh = NSA_HEAD_DIM
    vt[0:dh, :] = v_ref[...].astype(F32).T.astype(BF16)
    vt[dh:VROWS, :] = jnp.ones((VROWS - dh, vt.shape[1]), BF16)


def _out_t(acc, gate_row):
    dh = NSA_HEAD_DIM
    return acc[0:dh, :] * (gate_row / acc[dh:dh + 1, :])


def _cmp_attn_kernel(q_ref, kc_ref, vct_ref, bias_ref, cov_ref, gate_ref, oc_ref, sel_ref, score_ref, *, n_sel):
    tq = q_ref.shape[0]
    dh = NSA_HEAD_DIM
    n_slc = cov_ref.shape[0]
    t0 = pl.program_id(1) * tq
    g = pl.program_id(0)
    kc = kc_ref[0, 0]
    vct = vct_ref[0, 0]
    heads = range(NSA_HPG)
    qk = [_dot_nt(kc, _q_scaled(q_ref, hh)) for hh in heads]
    ps = []
    for hh in heads:
        logit = qk[hh] + bias_ref[0, hh]
        m = jnp.max(logit, axis=0, keepdims=True)
        e = jnp.exp2(logit - m)
        inv = jnp.where(m > 0.5 * NEG_LOGIT, 1.0 / jnp.sum(e, axis=0, keepdims=True), 0.0)
        ps.append(e * inv)
    p_sum = (ps[0] + ps[1]) + (ps[2] + ps[3])
    ots = [_dot(vct, p.astype(BF16)) for p in ps]
    for hh in heads:
        oc_ref[0, hh] = (ots[hh] * _gate_row(gate_ref, g, hh, 0)).astype(BF16)

    cov = cov_ref[...]
    p_hi = p_sum.astype(BF16)
    r1 = p_sum - p_hi.astype(F32)
    p_mid = r1.astype(BF16)
    p_lo = (r1 - p_mid.astype(F32)).astype(BF16)
    st = _dot(cov, p_hi) + _dot(cov, p_mid) + _dot(cov, p_lo)

    jb = lax.broadcasted_iota(jnp.int32, (n_slc, tq), 0)
    cur = (t0 + lax.broadcasted_iota(jnp.int32, (n_slc, tq), 1)) // SLC_BLOCK
    valid = jb <= cur
    forced = (jb == 0) | (jb == cur) | (jb == cur - 1)
    score = jnp.where(valid, st + jnp.where(forced, FORCE_BONUS, 0.0), -1.0)
    sub = 8
    score_ref[...] = score
    groups = [score_ref[r:r + sub, :] for r in range(0, n_slc, sub)]
    ranks = [jnp.zeros((sub, tq), F32) for _ in groups]
    row_id = lax.broadcasted_iota(jnp.int32, (sub, tq), 0)
    for j2 in range(n_slc):
        row = score_ref[j2:j2 + 1, :]
        for gi, sc in enumerate(groups):
            r0 = gi * sub
            if r0 > j2:
                inc = jnp.where(row >= sc, 1.0, 0.0)
            elif r0 + sub - 1 <= j2:
                inc = jnp.where(row > sc, 1.0, 0.0)
            else:
                inc = jnp.where(row_id > j2 - r0, jnp.where(row >= sc, 1.0, 0.0), jnp.where(row > sc, 1.0, 0.0))
            ranks[gi] = ranks[gi] + inc
    for gi, rk in enumerate(ranks):
        score_ref[gi * sub:(gi + 1) * sub, :] = rk
    sel = valid & (score_ref[...] < n_sel)
    sel_ref[0, 0, 0:n_slc, :] = jnp.where(sel, 0.0, -MASK_BIG).astype(BF16)
    if n_slc < LANES:
        sel_ref[0, 0, n_slc:LANES, :] = jnp.zeros((LANES - n_slc, tq), BF16)


def _cmp_attn(proj, gates, k_cmp, v_cmp_t, bias_c, cover_t, bsz, t):
    g_, dh, tq = NSA_KV_GROUPS, NSA_HEAD_DIM, CMP_TQ
    nt = t // tq
    nb = t // CMP_STRIDE
    n_slc = t // SLC_BLOCK
    gw = NSA_HPG * dh
    return pl.pallas_call(
        functools.partial(_cmp_attn_kernel, n_sel=min(SLC_TOP_N, n_slc)),
        grid=(g_, nt, bsz),
        in_specs=[
            pl.BlockSpec((tq, gw), lambda g, i, b: (b * nt + i, COL_AQ // gw + g)),
            pl.BlockSpec((1, 1, nb, dh), lambda g, i, b: (b, g, 0, 0)),
            pl.BlockSpec((1, 1, dh, nb), lambda g, i, b: (b, g, 0, 0)),
            pl.BlockSpec((1, NSA_HPG, nb, tq), lambda g, i, b: (g, 0, 0, i)),
            pl.BlockSpec((n_slc, nb), lambda g, i, b: (0, 0)),
            pl.BlockSpec((GATE_COLS, tq), lambda g, i, b: (0, b * nt + i)),
        ],
        out_specs=[
            pl.BlockSpec((1, NSA_HPG, dh, tq), lambda g, i, b: (b, g, 0, i)),
            pl.BlockSpec((1, 1, LANES, tq), lambda g, i, b: (b, g, 0, i)),
        ],
        out_shape=[
            jax.ShapeDtypeStruct((bsz, NSA_HEADS, dh, t), BF16),
            jax.ShapeDtypeStruct((bsz, g_, LANES, t), BF16),
        ],
        scratch_shapes=[pltpu.VMEM((n_slc, tq), F32)],
        compiler_params=_cparams(("parallel", "parallel", "parallel")),
        name="cmp_attn",
    )(proj, k_cmp, v_cmp_t, bias_c, cover_t, gates)


def _slc_attn_kernel(q_ref, sbt_ref, k_ref, v_ref, bias_ref, gate_ref, os_ref,
                     kaug, vt, qt, s_buf, p_buf, a_buf, m_ref, acc_ref, *, n_near):
    tq, tk, dh = SLC_TQ, SLC_TK, NSA_HEAD_DIM
    r = tq // tk
    t = k_ref.shape[0]
    g = pl.program_id(1)
    i = pl.program_id(2)
    heads = range(NSA_HPG)

    @pl.when(i == 0)
    def _():
        kaug[:, 0:dh] = k_ref[...]
        blk = lax.broadcasted_iota(jnp.int32, (t, LANES), 0) // SLC_BLOCK
        lane = lax.broadcasted_iota(jnp.int32, (t, LANES), 1)
        kaug[:, dh:dh + LANES] = jnp.where(blk == lane, 1.0, 0.0).astype(BF16)
        _fill_vt(vt, v_ref)

    sbt = sbt_ref[0, 0]
    for hh in heads:
        qt[hh, 0:dh, :] = _q_t(q_ref, hh)
        qt[hh, dh:dh + LANES, :] = sbt

    def ktile(j):
        return kaug[pl.ds(pl.multiple_of(j * tk, tk), tk), :]

    def vtile(j):
        return vt[:, pl.ds(pl.multiple_of(j * tk, tk), tk)]

    m_ref[...] = jnp.full(m_ref.shape, NEG_LOGIT, F32)
    acc_ref[...] = jnp.zeros_like(acc_ref)
    p_buf[...] = jnp.zeros_like(p_buf)
    a_buf[...] = jnp.ones_like(a_buf)
    k0 = ktile(0)
    for hh in heads:
        s_buf[hh] = _dot(k0, qt[hh])

    last = (i + 1) * r - 1

    def stage(j, bias_fn):
        v_prev = vtile(jnp.maximum(j - 1, 0))
        k_next = ktile(jnp.minimum(j + 1, last))
        pv = [_dot(v_prev, p_buf[hh]) for hh in heads]
        s_next = [_dot(k_next, qt[hh]) for hh in heads]
        for hh in heads:
            acc_ref[hh] = a_buf[hh] * acc_ref[hh] + pv[hh]
        for hh in heads:
            s = s_buf[hh]
            if bias_fn is not None:
                s = s + bias_fn(hh)
            m_prev = m_ref[hh]
            m_new = jnp.maximum(m_prev, jnp.max(s, axis=0, keepdims=True))
            a_buf[hh] = jnp.exp2(m_prev - m_new)
            p_buf[hh] = jnp.exp2(s - m_new).astype(BF16)
            m_ref[hh] = m_new
        for hh in heads:
            s_buf[hh] = s_next[hh]

    n_far = jnp.maximum(i * r + r - n_near, 0)

    def far_body(j, c):
        stage(j, None)
        return c

    lax.fori_loop(0, n_far, far_body, 0)

    def near_body(j, c):
        off = pl.multiple_of((i * r + (r - 1) - j) * tk, tk)
        stage(j, lambda hh: bias_ref[0, hh, :, pl.ds(off, tq)])
        return c

    lax.fori_loop(n_far, last + 1, near_body, 0)

    v_last = vtile(last)
    pv = [_dot(v_last, p_buf[hh]) for hh in heads]
    for hh in heads:
        acc = a_buf[hh] * acc_ref[hh] + pv[hh]
        os_ref[0, hh] = _out_t(acc, _gate_row(gate_ref, g, hh, 1)).astype(BF16)


def _slc_attn(proj, gates, selbias_t, bias_s, bsz, t):
    g_, dh, tq, tk = NSA_KV_GROUPS, NSA_HEAD_DIM, SLC_TQ, SLC_TK
    nt = t // tq
    gw = NSA_HPG * dh
    n_near = (bias_s.shape[3] - tq) // tk + 1
    return pl.pallas_call(
        functools.partial(_slc_attn_kernel, n_near=n_near),
        grid=(bsz, g_, nt),
        in_specs=[
            pl.BlockSpec((tq, gw), lambda b, g, i: (b * nt + i, COL_AQ // gw + g)),
            pl.BlockSpec((1, 1, LANES, tq), lambda b, g, i: (b, g, 0, i)),
            pl.BlockSpec((t, dh), lambda b, g, i: (b, COL_SK // dh + g)),
            pl.BlockSpec((t, dh), lambda b, g, i: (b, COL_SV // dh + g)),
            pl.BlockSpec((1, NSA_HPG, tk, bias_s.shape[3]), lambda b, g, i: (g, 0, 0, 0)),
            pl.BlockSpec((GATE_COLS, tq), lambda b, g, i: (0, b * nt + i)),
        ],
        out_specs=pl.BlockSpec((1, NSA_HPG, dh, tq), lambda b, g, i: (b, g, 0, i)),
        out_shape=jax.ShapeDtypeStruct((bsz, NSA_HEADS, dh, t), BF16),
        scratch_shapes=[
            pltpu.VMEM((t, dh + LANES), BF16),
            pltpu.VMEM((VROWS, t), BF16),
            pltpu.VMEM((NSA_HPG, dh + LANES, tq), BF16),
            pltpu.VMEM((NSA_HPG, tk, tq), F32),
            pltpu.VMEM((NSA_HPG, tk, tq), BF16),
            pltpu.VMEM((NSA_HPG, 1, tq), F32),
            pltpu.VMEM((NSA_HPG, 1, tq), F32),
            pltpu.VMEM((NSA_HPG, VROWS, tq), F32),
        ],
        compiler_params=_cparams(("parallel", "parallel", "arbitrary")),
        name="slc_attn",
    )(proj, selbias_t, proj, proj, bias_s, gates)


def _win_attn_kernel(q_ref, k_ref, v_ref, bias_ref, gate_ref, oc_ref, os_ref, z_ref, ya_ref, vt, *, n_tiles):
    tq, tk, dh = WIN_TQ, ATT_TK, NSA_HEAD_DIM
    r = tq // tk
    gp = pl.program_id(1)
    i = pl.program_id(2)
    heads = [(gl, hh) for gl in range(WIN_GROUPS) for hh in range(NSA_HPG)]

    @pl.when(i == 0)
    def _():
        for gl in range(WIN_GROUPS):
            vt[gl, 0:dh, :] = v_ref[:, gl * dh:(gl + 1) * dh].astype(F32).T.astype(BF16)
            vt[gl, dh:VROWS, :] = jnp.ones((VROWS - dh, vt.shape[2]), BF16)

    js = [i * r + (r - 1) - d for d in range(n_tiles)]
    offs = [pl.multiple_of(jnp.maximum(j, 0) * tk, tk) for j in js]
    boffs = [pl.multiple_of(jnp.where(js[d] >= 0, d, n_tiles) * tk, tk) for d in range(n_tiles)]
    k_tiles = [[k_ref[pl.ds(off, tk), gl * dh:(gl + 1) * dh] for off in offs] for gl in range(WIN_GROUPS)]
    v_tiles = [[vt[gl, :, pl.ds(off, tk)] for off in offs] for gl in range(WIN_GROUPS)]
    qs = [_q_scaled(q_ref, gl * NSA_HPG + hh) for gl, hh in heads]
    ss = [[_dot_nt(k_tiles[gl][d], qs[n]) for d in range(n_tiles)] for n, (gl, hh) in enumerate(heads)]
    ps = []
    for n, (gl, hh) in enumerate(heads):
        s = [ss[n][d] + bias_ref[gl, hh, :, pl.ds(boffs[d], tq)] for d in range(n_tiles)]
        m = functools.reduce(jnp.maximum, [jnp.max(x, axis=0, keepdims=True) for x in s])
        ps.append([jnp.exp2(x - m).astype(BF16) for x in s])
    pvs = [[_dot(v_tiles[gl][d], ps[n][d]) for d in range(n_tiles)] for n, (gl, hh) in enumerate(heads)]
    for n, (gl, hh) in enumerate(heads):
        cols = slice(n * dh, (n + 1) * dh)
        o = _out_t(functools.reduce(lambda x, y: x + y, pvs[n]), _gate_row(gate_ref, gp * WIN_GROUPS + gl, hh, 2))
        o = o + oc_ref[0, n].astype(F32) + os_ref[0, n].astype(F32)
        ya_ref[:, cols] = (o.T * _silu(z_ref[:, cols].astype(F32))).astype(BF16)


def _win_attn(proj, gates, o_cmp, o_slc, bias_w, bsz, t):
    g_, dh, tq, tk = NSA_KV_GROUPS, NSA_HEAD_DIM, WIN_TQ, ATT_TK
    nt = t // tq
    ng = WIN_GROUPS
    gw = ng * NSA_HPG * dh
    kw = ng * dh
    n_tiles = (bias_w.shape[3] - tq) // tk
    return pl.pallas_call(
        functools.partial(_win_attn_kernel, n_tiles=n_tiles),
        grid=(bsz, g_ // ng, nt),
        in_specs=[
            pl.BlockSpec((tq, gw), lambda b, g, i: (b * nt + i, COL_AQ // gw + g)),
            pl.BlockSpec((t, kw), lambda b, g, i: (b, COL_WK // kw + g)),
            pl.BlockSpec((t, kw), lambda b, g, i: (b, COL_WV // kw + g)),
            pl.BlockSpec((ng, NSA_HPG, tk, bias_w.shape[3]), lambda b, g, i: (g, 0, 0, 0)),
            pl.BlockSpec((GATE_COLS, tq), lambda b, g, i: (0, b * nt + i)),
            pl.BlockSpec((1, ng * NSA_HPG, dh, tq), lambda b, g, i: (b, g, 0, i)),
            pl.BlockSpec((1, ng * NSA_HPG, dh, tq), lambda b, g, i: (b, g, 0, i)),
            pl.BlockSpec((tq, gw), lambda b, g, i: (b * nt + i, COL_AZ // gw + g)),
        ],
        out_specs=pl.BlockSpec((tq, gw), lambda b, g, i: (b * nt + i, g)),
        out_shape=jax.ShapeDtypeStruct((bsz * t, NSA_WIDTH), BF16),
        scratch_shapes=[pltpu.VMEM((ng, VROWS, t), BF16)],
        compiler_params=_cparams(("parallel", "parallel", "arbitrary")),
        name="win_attn",
    )(proj, proj, proj, bias_w, gates, o_cmp, o_slc, proj)


def _outproj_kernel(ym_ref, ya_ref, x_ref, w_ref, gain_ref, o_ref, mix):
    j = pl.program_id(1)
    nj = pl.num_programs(1)
    tn = x_ref.shape[1]

    @pl.when(j == 0)
    def _():
        mix[:, 0:MLSTM_WIDTH] = ym_ref[...]
        mix[:, MLSTM_WIDTH:D_MIX] = ya_ref[...]

    o_ref[:, pl.ds(pl.multiple_of(j * tn, tn), tn)] = x_ref[...] + _dot(mix[...], w_ref[...])

    @pl.when(j == nj - 1)
    def _():
        n_tiles = o_ref.shape[1] // tn
        ss = None
        for jj in range(n_tiles):
            y = o_ref[:, jj * tn:(jj + 1) * tn]
            part = jnp.sum(y * y, axis=-1, keepdims=True)
            ss = part if ss is None else ss + part
        inv = lax.rsqrt(ss / o_ref.shape[1] + RMS_EPS)
        for jj in range(n_tiles):
            cols = slice(jj * tn, (jj + 1) * tn)
            o_ref[:, cols] = o_ref[:, cols] * inv * gain_ref[:, cols]


def _out_proj(y_m, y_a, x2d, w_out, gain):
    n, d = x2d.shape
    tm, tn = min(OUTPROJ_TM, n), OUTPROJ_TN
    nj = d // tn
    return pl.pallas_call(
        _outproj_kernel,
        grid=(n // tm, nj),
        in_specs=[
            pl.BlockSpec((tm, MLSTM_WIDTH), lambda i, j: (i, 0)),
            pl.BlockSpec((tm, NSA_WIDTH), lambda i, j: (i, 0)),
            pl.BlockSpec((tm, tn), lambda i, j: (i, j)),
            pl.BlockSpec((D_MIX, tn), lambda i, j: (0, j)),
            pl.BlockSpec((1, d), lambda i, j: (0, 0)),
        ],
        out_specs=pl.BlockSpec((tm, d), lambda i, j: (i, 0)),
        out_shape=jax.ShapeDtypeStruct((n, d), F32),
        scratch_shapes=[pltpu.VMEM((tm, D_MIX), BF16)],
        compiler_params=_cparams(("parallel", "arbitrary")),
        name="out_proj",
    )(y_m, y_a, x2d, w_out, gain)


def _rel_bucket(dist):
    n = jnp.maximum(dist, 0)
    nf = jnp.maximum(n, REL_MAX_EXACT).astype(jnp.float32)
    large = REL_MAX_EXACT + (jnp.log(nf / REL_MAX_EXACT) / math.log(REL_MAX_DISTANCE / REL_MAX_EXACT)
                             * (REL_BUCKETS - REL_MAX_EXACT)).astype(jnp.int32)
    large = jnp.minimum(large, REL_BUCKETS - 1)
    return jnp.where(n < REL_MAX_EXACT, n, large)


def _toeplitz_vec(by_dist, base, n_pos, n_neg, lo, hi, shift=None):
    w = n_pos + n_neg
    c = np.arange(w)
    dist = np.where(c < n_pos, base + c, base - (w - c))
    ok = (dist >= lo) & (dist < hi)
    vals = by_dist[:, np.clip(dist, 0, by_dist.shape[1] - 1)]
    if shift is not None:
        vals = vals - shift
    return jnp.where(ok[None], vals * LOG2E, NEG_LOGIT)


def _toeplitz_t(w_row, n_keys, n_q, key_step):
    x = jnp.broadcast_to(w_row, (n_keys, w_row.shape[1]))
    return pltpu.roll(x, 0, 1, stride=key_step, stride_axis=0)[:, 0:n_q]


def _bias_tables_kernel(wc_ref, ws_ref, ww_ref, bc_ref, bs_ref, bw_ref):
    nb, t = bc_ref.shape[2], bc_ref.shape[3]
    bc_ref[0, 0] = _toeplitz_t(wc_ref[0], nb, t, CMP_STRIDE)
    bs_ref[0, 0] = _toeplitz_t(ws_ref[0], bs_ref.shape[2], bs_ref.shape[3], 1)
    bw_ref[0, 0] = _toeplitz_t(ww_ref[0], bw_ref.shape[2], bw_ref.shape[3], 1)


def _bias_tables(rel_bias, t):
    tq, tk = ATT_TQ, ATT_TK
    g_, hpg = NSA_KV_GROUPS, NSA_HPG
    rb = rel_bias.astype(F32)
    dmax = REL_MAX_DISTANCE + 2 * max(SLC_TQ, SLC_TK, WIN_TQ, tq, tk)
    onehot = (_rel_bucket(jnp.arange(dmax, dtype=jnp.int32))[None, :] == jnp.arange(REL_BUCKETS, dtype=jnp.int32)[:, None])
    by_dist = jnp.dot(rb.T, onehot.astype(F32), precision=lax.Precision.HIGHEST)
    far = rb[REL_BUCKETS - 1][:, None]
    big = 1 << 30
    nb = t // CMP_STRIDE
    wc = _toeplitz_vec(by_dist, -(CMP_BLOCK - 1), t, t, 0, big)[:, None, :]
    sq, sk = min(SLC_TQ, t), SLC_TK
    r = sq // sk
    n_s = min(-(-(REL_MAX_DISTANCE + sk - 1) // sk) + r - 1, t // sk)
    wm = (n_s - 1) * sk + sq
    ws = _toeplitz_vec(by_dist, -(r - 1) * sk, wm, sk, 0, big, far)[:, None, :]
    wq = min(WIN_TQ, t)
    rw = wq // tk
    n_w = rw + (WINDOW - 1 + tk - 1) // tk
    wmw = n_w * tk + wq
    ww = _toeplitz_vec(by_dist, -(rw - 1) * tk, wmw, tk, 0, WINDOW)[:, None, :]
    return pl.pallas_call(
        _bias_tables_kernel,
        grid=(NSA_HEADS,),
        in_specs=[
            pl.BlockSpec((1, 1, 2 * t), lambda h: (h, 0, 0)),
            pl.BlockSpec((1, 1, wm + sk), lambda h: (h, 0, 0)),
            pl.BlockSpec((1, 1, wmw + tk), lambda h: (h, 0, 0)),
        ],
        out_specs=[
            pl.BlockSpec((1, 1, nb, t), lambda h: (h // hpg, h % hpg, 0, 0)),
            pl.BlockSpec((1, 1, sk, wm), lambda h: (h // hpg, h % hpg, 0, 0)),
            pl.BlockSpec((1, 1, tk, wmw), lambda h: (h // hpg, h % hpg, 0, 0)),
        ],
        out_shape=[
            jax.ShapeDtypeStruct((g_, hpg, nb, t), F32),
            jax.ShapeDtypeStruct((g_, hpg, sk, wm), F32),
            jax.ShapeDtypeStruct((g_, hpg, tk, wmw), F32),
        ],
        compiler_params=_cparams(("parallel",)),
        name="bias_tables",
    )(wc, ws, ww)


def _cover_t(t):
    nb = t // CMP_STRIDE
    n_cmp = (t - CMP_BLOCK) // CMP_STRIDE + 1
    n_slc = t // SLC_BLOCK
    cs = np.arange(nb) * CMP_STRIDE
    ss = np.arange(n_slc) * SLC_BLOCK
    cover = np.clip(np.minimum(cs[:, None] + CMP_BLOCK, ss[None, :] + SLC_BLOCK)
                    - np.maximum(cs[:, None], ss[None, :]), 0, None) / CMP_BLOCK
    cover[n_cmp:] = 0.0
    return jnp.asarray(cover.T, dtype=BF16)


def kernel(x, norm_gain, w_in, w_conv, b_igate, b_fgate, mlstm_norm_gain, cmp_k_pos, cmp_k_w1, cmp_k_w2,
           cmp_v_pos, cmp_v_w1, cmp_v_w2, rel_bias, w_out, final_norm_gain):
    bsz, t, d = x.shape
    assert d == D_MODEL and t % MLSTM_L == 0 and t % ATT_TQ == 0 and (t // CMP_STRIDE) % LANES == 0
    n = bsz * t
    x2d = x.reshape(n, d)

    w_main, w_gate = _w_prep(w_in.T)
    proj, gates = _in_proj(x2d, norm_gain.reshape(1, d).astype(F32), w_main, w_gate)

    g_rows = gates[:, :2 * MLSTM_HEADS].reshape(bsz, t, 2, MLSTM_HEADS).transpose(0, 3, 2, 1)
    y_m = _mlstm(proj, g_rows, w_conv.astype(F32), b_igate.astype(F32), b_fgate.astype(F32),
                 mlstm_norm_gain.reshape(1, MLSTM_WIDTH).astype(F32), bsz, t)

    dh = NSA_HEAD_DIM
    half = CMP_BLOCK // 2

    def w1cat(w1):
        return jnp.concatenate([w1[:half].reshape(half * dh, dh), w1[half:].reshape(half * dh, dh)], axis=1).astype(BF16)

    k_cmp, v_cmp_t = _compress(
        proj, w1cat(cmp_k_w1), cmp_k_w2.astype(BF16), cmp_k_pos.reshape(2, half * dh).astype(BF16),
        w1cat(cmp_v_w1), cmp_v_w2.astype(BF16), cmp_v_pos.reshape(2, half * dh).astype(BF16), bsz, t)

    bias_c, bias_s, bias_w = _bias_tables(rel_bias, t)
    gates_t = gates.T
    o_cmp, selbias_t = _cmp_attn(proj, gates_t, k_cmp, v_cmp_t, bias_c, _cover_t(t), bsz, t)
    o_slc = _slc_attn(proj, gates_t, selbias_t, bias_s, bsz, t)
    y_a = _win_attn(proj, gates_t, o_cmp, o_slc, bias_w, bsz, t)

    out = _out_proj(y_m, y_a, x2d, w_out.astype(BF16), final_norm_gain.reshape(1, d).astype(F32))
    return out.reshape(bsz, t, d)
```

```python
import functools
import math

import jax
import jax.numpy as jnp
import numpy as np
from jax import lax
from jax.experimental import pallas as pl
from jax.experimental.pallas import tpu as pltpu

F32 = jnp.float32
BF16 = jnp.bfloat16

D_MODEL = 4096
D_MIX = D_MODEL
MLSTM_WIDTH = D_MIX // 2
MLSTM_HEADS = 4
MLSTM_V_DIM = MLSTM_WIDTH // MLSTM_HEADS
MLSTM_QK_DIM = MLSTM_V_DIM // 2
MLSTM_QK_WIDTH = MLSTM_HEADS * MLSTM_QK_DIM
CONV_WIDTH = 4
NSA_WIDTH = D_MIX - MLSTM_WIDTH
NSA_HEAD_DIM = 128
NSA_HEADS = NSA_WIDTH // NSA_HEAD_DIM
NSA_KV_GROUPS = 4
NSA_HPG = NSA_HEADS // NSA_KV_GROUPS
NSA_KV_WIDTH = NSA_KV_GROUPS * NSA_HEAD_DIM
CMP_BLOCK = 32
CMP_STRIDE = 16
SLC_BLOCK = 64
SLC_TOP_N = 16
WINDOW = 512
FORCE_BONUS = 1000.0
REL_BUCKETS = 32
REL_MAX_EXACT = REL_BUCKETS // 2
REL_MAX_DISTANCE = 1024
RMS_EPS = 1e-6
NEG_LOGIT = -1e30

LANES = 128
VMEM_LIMIT_BYTES = 56 * 1024 * 1024

COL_MQ = 0
COL_MK = COL_MQ + MLSTM_QK_WIDTH
COL_MV = COL_MK + MLSTM_QK_WIDTH
COL_MO = COL_MV + MLSTM_WIDTH
COL_MZ = COL_MO + MLSTM_WIDTH
COL_AQ = COL_MZ + MLSTM_WIDTH
COL_CK = COL_AQ + NSA_WIDTH
COL_CV = COL_CK + NSA_KV_WIDTH
COL_SK = COL_CV + NSA_KV_WIDTH
COL_SV = COL_SK + NSA_KV_WIDTH
COL_WK = COL_SV + NSA_KV_WIDTH
COL_WV = COL_WK + NSA_KV_WIDTH
COL_AZ = COL_WV + NSA_KV_WIDTH
MAIN_COLS = COL_AZ + NSA_WIDTH
GATE_COLS = LANES
GATE_I = 0
GATE_F = MLSTM_HEADS
GATE_NSA = 2 * MLSTM_HEADS

INPROJ_TM = 512
INPROJ_TN = 1536
OUTPROJ_TM = 512
OUTPROJ_TN = 1024
MLSTM_L = 256
ATT_TQ = 256
ATT_TK = 256
WPREP_TR = 512
SLC_TQ = 512
SLC_TK = 512
CMP_TQ = 512
CMP_GROUPS = 2
WIN_TQ = 256
WIN_GROUPS = 2
MASK_BIG = 2.0 ** 100


def _cparams(sem):
    return pltpu.CompilerParams(dimension_semantics=sem, vmem_limit_bytes=VMEM_LIMIT_BYTES)


def _sigmoid(x):
    return 0.5 * jnp.tanh(0.5 * x) + 0.5


def _silu(x):
    return x * _sigmoid(x)


def _dot(a, b):
    return jnp.dot(a, b, preferred_element_type=F32)


def _dot_nt(a, b):
    return lax.dot_general(a, b, (((1,), (1,)), ((), ())), preferred_element_type=F32)


W_OFF_I = COL_MZ + MLSTM_WIDTH
W_OFF_AQ = W_OFF_I + 2 * MLSTM_HEADS
W_OFF_GATE = W_OFF_AQ + NSA_WIDTH + 6 * NSA_KV_WIDTH
W_OFF_AZ = W_OFF_GATE + 3 * NSA_HEADS


def _wprep_kernel(w_ref, gi_ref, ga_ref, o_ref, g_ref):
    o_ref[...] = w_ref[...].astype(BF16)

    @pl.when(pl.program_id(0) == 0)
    def _():
        pad = jnp.zeros((GATE_COLS - gi_ref.shape[0] - ga_ref.shape[0], g_ref.shape[1]), F32)
        g_ref[...] = jnp.concatenate([gi_ref[...], ga_ref[...], pad], axis=0).astype(BF16)


def _w_prep(w_t):
    rows, d = w_t.shape
    tr = WPREP_TR
    nb_a, nb_b = W_OFF_I // tr, (W_OFF_GATE - W_OFF_AQ) // tr

    def src(m):
        skip = jnp.where(m >= nb_a, W_OFF_AQ - W_OFF_I, 0) + jnp.where(m >= nb_a + nb_b, W_OFF_AZ - W_OFF_GATE, 0)
        return pl.multiple_of(m * tr + skip, 8)

    rows_at = lambda start, size: pl.BlockSpec((pl.Element(size), pl.Element(d)), lambda m: (start, 0))
    return pl.pallas_call(
        _wprep_kernel,
        grid=(MAIN_COLS // tr,),
        in_specs=[
            pl.BlockSpec((pl.Element(tr), pl.Element(d)), lambda m: (src(m), 0)),
            rows_at(W_OFF_I, W_OFF_AQ - W_OFF_I),
            rows_at(W_OFF_GATE, W_OFF_AZ - W_OFF_GATE),
        ],
        out_specs=[pl.BlockSpec((tr, d), lambda m: (m, 0)), pl.BlockSpec((GATE_COLS, d), lambda m: (0, 0))],
        out_shape=[jax.ShapeDtypeStruct((MAIN_COLS, d), BF16), jax.ShapeDtypeStruct((GATE_COLS, d), BF16)],
        compiler_params=_cparams(("arbitrary",)),
        name="w_prep",
    )(w_t, w_t, w_t)


def _inproj_kernel(x_ref, gain_ref, w_ref, wg_ref, o_ref, og_ref, h_ref):
    @pl.when(pl.program_id(1) == 0)
    def _():
        x = x_ref[...]
        ms = jnp.mean(x * x, axis=-1, keepdims=True)
        h = (x * lax.rsqrt(ms + RMS_EPS) * gain_ref[...]).astype(BF16)
        h_ref[...] = h
        og_ref[...] = _dot_nt(h, wg_ref[...])

    o_ref[...] = _dot_nt(h_ref[...], w_ref[...]).astype(BF16)


def _in_proj(x2d, gain, w_main, w_gate):
    n, d = x2d.shape
    tm, tn = min(INPROJ_TM, n), INPROJ_TN
    return pl.pallas_call(
        _inproj_kernel,
        grid=(n // tm, MAIN_COLS // tn),
        in_specs=[
            pl.BlockSpec((tm, d), lambda i, j: (i, 0)),
            pl.BlockSpec((1, d), lambda i, j: (0, 0)),
            pl.BlockSpec((tn, d), lambda i, j: (j, 0)),
            pl.BlockSpec((GATE_COLS, d), lambda i, j: (0, 0)),
        ],
        out_specs=[
            pl.BlockSpec((tm, tn), lambda i, j: (i, j)),
            pl.BlockSpec((tm, GATE_COLS), lambda i, j: (i, 0)),
        ],
        out_shape=[
            jax.ShapeDtypeStruct((n, MAIN_COLS), BF16),
            jax.ShapeDtypeStruct((n, GATE_COLS), F32),
        ],
        scratch_shapes=[pltpu.VMEM((tm, d), BF16)],
        compiler_params=_cparams(("parallel", "arbitrary")),
        name="in_proj",
    )(x2d, gain, w_main, w_gate)


def _mlstm_kernel(bi_ref, bf_ref, q_ref, k_ref, v_ref, o_ref, z_ref, g_ref, wq_ref, wk_ref, ng_ref,
                  y_ref, qext, kext, c_st, n_st, m_st):
    L = MLSTM_L
    HIST = 8
    dqk, dv = MLSTM_QK_DIM, MLSTM_V_DIM

    @pl.when(pl.program_id(1) == 0)
    def _():
        qext[:, 0:HIST, :] = jnp.zeros((MLSTM_HEADS, HIST, dqk), F32)
        kext[:, 0:HIST, :] = jnp.zeros((MLSTM_HEADS, HIST, dqk), F32)
        c_st[...] = jnp.zeros_like(c_st)
        n_st[...] = jnp.zeros_like(n_st)
        m_st[...] = jnp.zeros_like(m_st)

    rr = lax.broadcasted_iota(jnp.int32, (L, L), 0)
    cc = lax.broadcasted_iota(jnp.int32, (L, L), 1)
    upper = (rr <= cc).astype(F32)

    for hd in range(MLSTM_HEADS):
        qcols = slice(hd * dqk, (hd + 1) * dqk)
        vcols = slice(hd * dv, (hd + 1) * dv)
        qext[hd, HIST:HIST + L, :] = q_ref[:, qcols].astype(F32)
        kext[hd, HIST:HIST + L, :] = k_ref[:, qcols].astype(F32)

        def conv_silu(ext, w_ref):
            w = w_ref[:, qcols]
            y = ext[hd, pl.ds(HIST, L), :] * w[CONV_WIDTH - 1:CONV_WIDTH, :]
            for s in range(1, CONV_WIDTH):
                y = y + ext[hd, pl.ds(HIST - s, L), :] * w[CONV_WIDTH - 1 - s:CONV_WIDTH - s, :]
            return _silu(y)

        qc = conv_silu(qext, wq_ref)
        kc = conv_silu(kext, wk_ref) * (dqk ** -0.5)
        qext[hd, 0:HIST, :] = qext[hd, L:L + HIST, :]
        kext[hd, 0:HIST, :] = kext[hd, L:L + HIST, :]

        g = g_ref[0, hd]
        i_row = g[0:1, :] + bi_ref[hd]
        f_row = g[1:2, :] + bf_ref[hd]
        lf_row = jnp.minimum(f_row, 0.0) - jnp.log(1.0 + jnp.exp(-jnp.abs(f_row)))

        bcum_row = jnp.dot(jnp.broadcast_to(lf_row, (8, L)), upper, preferred_element_type=F32,
                           precision=lax.Precision.HIGHEST)[0:1, :]
        bcum_col = jnp.sum(jnp.where(rr == cc, bcum_row, 0.0), axis=1, keepdims=True)
        gsum = bcum_row[:, L - 1:L]
        m_prev = m_st[hd]

        dlog = jnp.where(rr >= cc, bcum_col - bcum_row + i_row, -jnp.inf)
        m_inter = bcum_col + m_prev
        m_t = jnp.maximum(m_inter, jnp.max(dlog, axis=1, keepdims=True))
        dmat = jnp.exp(dlog - m_t)
        inter = jnp.exp(m_inter - m_t)

        qb = qc.astype(BF16)
        kct = kc.T
        vb = v_ref[:, vcols]
        s = _dot(qb, kct.astype(BF16)) * dmat
        c_prev = c_st[hd]
        n_prev = n_st[hd]
        num = _dot(s.astype(BF16), vb) + inter * _dot(qb, c_prev.astype(BF16))
        qn = jnp.sum(s, axis=1, keepdims=True) + inter * jnp.sum(qc * n_prev, axis=1, keepdims=True)
        hh = num / jnp.maximum(jnp.abs(qn), jnp.exp(-m_t))

        wlog = gsum - bcum_row + i_row
        m_next = jnp.maximum(gsum + m_prev, jnp.max(wlog, axis=1, keepdims=True))
        wts = jnp.exp(wlog - m_next)
        keep = jnp.exp(gsum + m_prev - m_next)
        c_st[hd] = keep * c_prev + _dot((kct * wts).astype(BF16), vb)
        n_st[hd] = keep * n_prev + _dot(jnp.broadcast_to(wts, (8, L)).astype(BF16), kc.astype(BF16))[0:1, :]
        m_st[hd] = m_next

        hm = _sigmoid(o_ref[:, vcols].astype(F32)) * hh
        hm = hm * lax.rsqrt(jnp.mean(hm * hm, axis=-1, keepdims=True) + RMS_EPS)
        hm = hm * ng_ref[:, vcols]
        y_ref[:, vcols] = (hm * _silu(z_ref[:, vcols].astype(F32))).astype(BF16)


def _mlstm(proj, g_rows, w_conv, b_igate, b_fgate, norm_gain, bsz, t):
    L = MLSTM_L
    nc = t // L
    nh, dqk, dv = MLSTM_HEADS, MLSTM_QK_DIM, MLSTM_V_DIM
    row = lambda b, c: b * nc + c
    smem = pl.BlockSpec(memory_space=pltpu.SMEM)
    return pl.pallas_call(
        _mlstm_kernel,
        grid=(bsz, nc),
        in_specs=[
            smem, smem,
            pl.BlockSpec((L, MLSTM_QK_WIDTH), lambda b, c: (row(b, c), COL_MQ // MLSTM_QK_WIDTH)),
            pl.BlockSpec((L, MLSTM_QK_WIDTH), lambda b, c: (row(b, c), COL_MK // MLSTM_QK_WIDTH)),
            pl.BlockSpec((L, MLSTM_WIDTH), lambda b, c: (row(b, c), COL_MV // MLSTM_WIDTH)),
            pl.BlockSpec((L, MLSTM_WIDTH), lambda b, c: (row(b, c), COL_MO // MLSTM_WIDTH)),
            pl.BlockSpec((L, MLSTM_WIDTH), lambda b, c: (row(b, c), COL_MZ // MLSTM_WIDTH)),
            pl.BlockSpec((1, nh, 2, L), lambda b, c: (b, 0, 0, c)),
            pl.BlockSpec((CONV_WIDTH, MLSTM_QK_WIDTH), lambda b, c: (0, 0)),
            pl.BlockSpec((CONV_WIDTH, MLSTM_QK_WIDTH), lambda b, c: (0, 1)),
            pl.BlockSpec((1, MLSTM_WIDTH), lambda b, c: (0, 0)),
        ],
        out_specs=pl.BlockSpec((L, MLSTM_WIDTH), lambda b, c: (row(b, c), 0)),
        out_shape=jax.ShapeDtypeStruct((bsz * t, MLSTM_WIDTH), BF16),
        scratch_shapes=[
            pltpu.VMEM((nh, L + 8, dqk), F32),
            pltpu.VMEM((nh, L + 8, dqk), F32),
            pltpu.VMEM((nh, dqk, dv), F32),
            pltpu.VMEM((nh, 1, dqk), F32),
            pltpu.VMEM((nh, 1, 1), F32),
        ],
        compiler_params=_cparams(("parallel", "arbitrary")),
        name="mlstm",
    )(b_igate, b_fgate, proj, proj, proj, proj, proj, g_rows, w_conv, w_conv, norm_gain)


def _compress_kernel(ck_ref, cv_ref, w1k_ref, w2k_ref, pk_ref, w1v_ref, w2v_ref, pv_ref,
                     ok_ref, ov_ref, xf, xcat):
    t = ck_ref.shape[0]
    nb = t // CMP_STRIDE
    dh = NSA_HEAD_DIM

    def one(src_ref, w1_ref, w2_ref, pos_ref):
        xf[...] = src_ref[...].astype(F32)
        for l in range(CMP_STRIDE):
            xcat[:, l * dh:(l + 1) * dh] = xf[pl.ds(l, nb, stride=CMP_STRIDE), :].astype(BF16)
        w1 = w1_ref[...]
        ab = _dot(xcat[...], w1)
        pp = _dot(pos_ref[...], w1)
        pos_term = pp[0:1, 0:dh] + pp[1:2, dh:2 * dh]
        second = pltpu.roll(ab[:, dh:2 * dh], nb - 1, 0)
        hid = _silu(ab[:, 0:dh] + second + pos_term)
        return _dot(hid.astype(BF16), w2_ref[...])

    ok_ref[0, 0] = one(ck_ref, w1k_ref, w2k_ref, pk_ref).astype(BF16)
    ov_ref[0, 0] = one(cv_ref, w1v_ref, w2v_ref, pv_ref).T.astype(BF16)


def _compress(proj, w1k, w2k, pk, w1v, w2v, pv, bsz, t):
    g_, dh = NSA_KV_GROUPS, NSA_HEAD_DIM
    nb = t // CMP_STRIDE
    full = lambda a: pl.BlockSpec(a.shape, lambda b, g: (0,) * a.ndim)
    k_spec = pl.BlockSpec((1, 1, nb, dh), lambda b, g: (b, g, 0, 0))
    vt_spec = pl.BlockSpec((1, 1, dh, nb), lambda b, g: (b, g, 0, 0))
    return pl.pallas_call(
        _compress_kernel,
        grid=(bsz, g_),
        in_specs=[
            pl.BlockSpec((t, dh), lambda b, g: (b, COL_CK // dh + g)),
            pl.BlockSpec((t, dh), lambda b, g: (b, COL_CV // dh + g)),
            full(w1k), full(w2k), full(pk), full(w1v), full(w2v), full(pv),
        ],
        out_specs=[k_spec, vt_spec],
        out_shape=[jax.ShapeDtypeStruct((bsz, g_, nb, dh), BF16), jax.ShapeDtypeStruct((bsz, g_, dh, nb), BF16)],
        scratch_shapes=[pltpu.VMEM((t, dh), F32), pltpu.VMEM((nb, CMP_STRIDE * dh), BF16)],
        compiler_params=_cparams(("parallel", "parallel")),
        name="compress",
    )(proj, proj, w1k, w2k, pk, w1v, w2v, pv)


LOG2E = math.log2(math.e)
VROWS = NSA_HEAD_DIM + 16


def _q_scaled(q_ref, hh):
    dh = NSA_HEAD_DIM
    return (q_ref[:, hh * dh:(hh + 1) * dh].astype(F32) * (dh ** -0.5 * LOG2E)).astype(BF16)


def _q_t(q_ref, hh):
    dh = NSA_HEAD_DIM
    return (q_ref[:, hh * dh:(hh + 1) * dh].astype(F32) * (dh ** -0.5 * LOG2E)).T.astype(BF16)


def _gate_row(gt_ref, g, hh, branch):
    row = GATE_NSA + 3 * (g * NSA_HPG + hh) + branch
    return _sigmoid(gt_ref[pl.ds(row, 1), :])


def _fill_vt(vt, v_ref):
    dh = NSA_HEAD_DIM
    vt[0:dh, :] = v_ref[...].astype(F32).T.astype(BF16)
    vt[dh:VROWS, :] = jnp.ones((VROWS - dh, vt.shape[1]), BF16)


def _out_t(acc, gate_row):
    dh = NSA_HEAD_DIM
    return acc[0:dh, :] * (gate_row / acc[dh:dh + 1, :])


def _cmp_attn_kernel(q_ref, kc_ref, vct_ref, bias_ref, cov_ref, gate_ref, oc_ref, sel_ref, score_ref, *, n_sel):
    tq = q_ref.shape[0]
    dh = NSA_HEAD_DIM
    n_slc = cov_ref.shape[0]
    t0 = pl.program_id(1) * tq
    gp = pl.program_id(0)
    heads = [(gl, hh) for gl in range(CMP_GROUPS) for hh in range(NSA_HPG)]
    qk = [_dot_nt(kc_ref[0, gl], _q_scaled(q_ref, gl * NSA_HPG + hh)) for gl, hh in heads]
    ps = []
    for n, (gl, hh) in enumerate(heads):
        logit = qk[n] + bias_ref[gl, hh]
        m = jnp.max(logit, axis=0, keepdims=True)
        e = jnp.exp2(logit - m)
        inv = jnp.where(m > 0.5 * NEG_LOGIT, 1.0 / jnp.sum(e, axis=0, keepdims=True), 0.0)
        ps.append(e * inv)
    ots = [_dot(vct_ref[0, gl], ps[n].astype(BF16)) for n, (gl, hh) in enumerate(heads)]
    for n, (gl, hh) in enumerate(heads):
        oc_ref[0, n] = (ots[n] * _gate_row(gate_ref, gp * CMP_GROUPS + gl, hh, 0)).astype(BF16)

    cov = cov_ref[...]
    jb = lax.broadcasted_iota(jnp.int32, (n_slc, tq), 0)
    cur = (t0 + lax.broadcasted_iota(jnp.int32, (n_slc, tq), 1)) // SLC_BLOCK
    valid = jb <= cur
    forced = (jb == 0) | (jb == cur) | (jb == cur - 1)
    sub = 8
    row_id = lax.broadcasted_iota(jnp.int32, (sub, tq), 0)
    for gl in range(CMP_GROUPS):
        g4 = ps[gl * NSA_HPG:(gl + 1) * NSA_HPG]
        p_sum = (g4[0] + g4[1]) + (g4[2] + g4[3])
        p_hi = p_sum.astype(BF16)
        r1 = p_sum - p_hi.astype(F32)
        p_mid = r1.astype(BF16)
        p_lo = (r1 - p_mid.astype(F32)).astype(BF16)
        st = _dot(cov, p_hi) + _dot(cov, p_mid) + _dot(cov, p_lo)
        score_ref[gl] = jnp.where(valid, st + jnp.where(forced, FORCE_BONUS, 0.0), -1.0)
        groups = [score_ref[gl, r:r + sub, :] for r in range(0, n_slc, sub)]
        ranks = [jnp.zeros((sub, tq), F32) for _ in groups]
        for j2 in range(n_slc):
            row = score_ref[gl, j2:j2 + 1, :]
            for gi, sc in enumerate(groups):
                r0 = gi * sub
                if r0 > j2:
                    inc = jnp.where(row >= sc, 1.0, 0.0)
                elif r0 + sub - 1 <= j2:
                    inc = jnp.where(row > sc, 1.0, 0.0)
                else:
                    inc = jnp.where(row_id > j2 - r0, jnp.where(row >= sc, 1.0, 0.0), jnp.where(row > sc, 1.0, 0.0))
                ranks[gi] = ranks[gi] + inc
        for gi, rk in enumerate(ranks):
            score_ref[gl, gi * sub:(gi + 1) * sub, :] = rk
        sel = valid & (score_ref[gl] < n_sel)
        sel_ref[0, gl, 0:n_slc, :] = jnp.where(sel, 0.0, -MASK_BIG).astype(BF16)
        if n_slc < LANES:
            sel_ref[0, gl, n_slc:LANES, :] = jnp.zeros((LANES - n_slc, tq), BF16)


def _cmp_attn(proj, gates, k_cmp, v_cmp_t, bias_c, cover_t, bsz, t):
    g_, dh, tq = NSA_KV_GROUPS, NSA_HEAD_DIM, CMP_TQ
    nt = t // tq
    nb = t // CMP_STRIDE
    n_slc = t // SLC_BLOCK
    ng = CMP_GROUPS
    gw = ng * NSA_HPG * dh
    return pl.pallas_call(
        functools.partial(_cmp_attn_kernel, n_sel=min(SLC_TOP_N, n_slc)),
        grid=(g_ // ng, nt, bsz),
        in_specs=[
            pl.BlockSpec((tq, gw), lambda g, i, b: (b * nt + i, COL_AQ // gw + g)),
            pl.BlockSpec((1, ng, nb, dh), lambda g, i, b: (b, g, 0, 0)),
            pl.BlockSpec((1, ng, dh, nb), lambda g, i, b: (b, g, 0, 0)),
            pl.BlockSpec((ng, NSA_HPG, nb, tq), lambda g, i, b: (g, 0, 0, i)),
            pl.BlockSpec((n_slc, nb), lambda g, i, b: (0, 0)),
            pl.BlockSpec((GATE_COLS, tq), lambda g, i, b: (0, b * nt + i)),
        ],
        out_specs=[
            pl.BlockSpec((1, ng * NSA_HPG, dh, tq), lambda g, i, b: (b, g, 0, i)),
            pl.BlockSpec((1, ng, LANES, tq), lambda g, i, b: (b, g, 0, i)),
        ],
        out_shape=[
            jax.ShapeDtypeStruct((bsz, NSA_HEADS, dh, t), BF16),
            jax.ShapeDtypeStruct((bsz, g_, LANES, t), BF16),
        ],
        scratch_shapes=[pltpu.VMEM((ng, n_slc, tq), F32)],
        compiler_params=_cparams(("parallel", "parallel", "parallel")),
        name="cmp_attn",
    )(proj, k_cmp, v_cmp_t, bias_c, cover_t, gates)


def _slc_attn_kernel(q_ref, sbt_ref, k_ref, v_ref, bias_ref, gate_ref, os_ref,
                     kaug, vt, qt, s_buf, p_buf, a_buf, m_ref, acc_ref, *, n_near):
    tq, tk, dh = SLC_TQ, SLC_TK, NSA_HEAD_DIM
    r = tq // tk
    t = k_ref.shape[0]
    g = pl.program_id(1)
    i = pl.program_id(2)
    heads = range(NSA_HPG)

    @pl.when(i == 0)
    def _():
        kaug[:, 0:dh] = k_ref[...]
        blk = lax.broadcasted_iota(jnp.int32, (t, LANES), 0) // SLC_BLOCK
        lane = lax.broadcasted_iota(jnp.int32, (t, LANES), 1)
        kaug[:, dh:dh + LANES] = jnp.where(blk == lane, 1.0, 0.0).astype(BF16)
        _fill_vt(vt, v_ref)

    sbt = sbt_ref[0, 0]
    for hh in heads:
        qt[hh, 0:dh, :] = _q_t(q_ref, hh)
        qt[hh, dh:dh + LANES, :] = sbt

    def ktile(j):
        return kaug[pl.ds(pl.multiple_of(j * tk, tk), tk), :]

    def vtile(j):
        return vt[:, pl.ds(pl.multiple_of(j * tk, tk), tk)]

    m_ref[...] = jnp.full(m_ref.shape, NEG_LOGIT, F32)
    acc_ref[...] = jnp.zeros_like(acc_ref)
    p_buf[...] = jnp.zeros_like(p_buf)
    a_buf[...] = jnp.ones_like(a_buf)
    k0 = ktile(0)
    for hh in heads:
        s_buf[hh] = _dot(k0, qt[hh])

    last = (i + 1) * r - 1

    def stage(j, bias_fn):
        v_prev = vtile(jnp.maximum(j - 1, 0))
        k_next = ktile(jnp.minimum(j + 1, last))
        pv = [_dot(v_prev, p_buf[hh]) for hh in heads]
        s_next = [_dot(k_next, qt[hh]) for hh in heads]
        for hh in heads:
            acc_ref[hh] = a_buf[hh] * acc_ref[hh] + pv[hh]
        for hh in heads:
            s = s_buf[hh]
            if bias_fn is not None:
                s = s + bias_fn(hh)
            m_prev = m_ref[hh]
            m_new = jnp.maximum(m_prev, jnp.max(s, axis=0, keepdims=True))
            a_buf[hh] = jnp.exp2(m_prev - m_new)
            p_buf[hh] = jnp.exp2(s - m_new).astype(BF16)
            m_ref[hh] = m_new
        for hh in heads:
            s_buf[hh] = s_next[hh]

    n_far = jnp.maximum(i * r + r - n_near, 0)

    def far_body(j, c):
        stage(j, None)
        return c

    lax.fori_loop(0, n_far, far_body, 0)

    def near_body(j, c):
        off = pl.multiple_of((i * r + (r - 1) - j) * tk, tk)
        stage(j, lambda hh: bias_ref[0, hh, :, pl.ds(off, tq)])
        return c

    lax.fori_loop(n_far, last + 1, near_body, 0)

    v_last = vtile(last)
    pv = [_dot(v_last, p_buf[hh]) for hh in heads]
    for hh in heads:
        acc = a_buf[hh] * acc_ref[hh] + pv[hh]
        os_ref[0, hh] = _out_t(acc, _gate_row(gate_ref, g, hh, 1)).astype(BF16)


def _slc_attn(proj, gates, selbias_t, bias_s, bsz, t):
    g_, dh, tq, tk = NSA_KV_GROUPS, NSA_HEAD_DIM, SLC_TQ, SLC_TK
    nt = t // tq
    gw = NSA_HPG * dh
    n_near = (bias_s.shape[3] - tq) // tk + 1
    return pl.pallas_call(
        functools.partial(_slc_attn_kernel, n_near=n_near),
        grid=(bsz, g_, nt),
        in_specs=[
            pl.BlockSpec((tq, gw), lambda b, g, i: (b * nt + i, COL_AQ // gw + g)),
            pl.BlockSpec((1, 1, LANES, tq), lambda b, g, i: (b, g, 0, i)),
            pl.BlockSpec((t, dh), lambda b, g, i: (b, COL_SK // dh + g)),
            pl.BlockSpec((t, dh), lambda b, g, i: (b, COL_SV // dh + g)),
            pl.BlockSpec((1, NSA_HPG, tk, bias_s.shape[3]), lambda b, g, i: (g, 0, 0, 0)),
            pl.BlockSpec((GATE_COLS, tq), lambda b, g, i: (0, b * nt + i)),
        ],
        out_specs=pl.BlockSpec((1, NSA_HPG, dh, tq), lambda b, g, i: (b, g, 0, i)),
        out_shape=jax.ShapeDtypeStruct((bsz, NSA_HEADS, dh, t), BF16),
        scratch_shapes=[
            pltpu.VMEM((t, dh + LANES), BF16),
            pltpu.VMEM((VROWS, t), BF16),
            pltpu.VMEM((NSA_HPG, dh + LANES, tq), BF16),
            pltpu.VMEM((NSA_HPG, tk, tq), F32),
            pltpu.VMEM((NSA_HPG, tk, tq), BF16),
            pltpu.VMEM((NSA_HPG, 1, tq), F32),
            pltpu.VMEM((NSA_HPG, 1, tq), F32),
            pltpu.VMEM((NSA_HPG, VROWS, tq), F32),
        ],
        compiler_params=_cparams(("parallel", "parallel", "arbitrary")),
        name="slc_attn",
    )(proj, selbias_t, proj, proj, bias_s, gates)


def _win_attn_kernel(q_ref, k_ref, v_ref, bias_ref, gate_ref, oc_ref, os_ref, z_ref, ya_ref, vt, *, n_tiles):
    tq, tk, dh = WIN_TQ, ATT_TK, NSA_HEAD_DIM
    r = tq // tk
    gp = pl.program_id(1)
    i = pl.program_id(2)
    heads = [(gl, hh) for gl in range(WIN_GROUPS) for hh in range(NSA_HPG)]

    @pl.when(i == 0)
    def _():
        for gl in range(WIN_GROUPS):
            vt[gl, 0:dh, :] = v_ref[:, gl * dh:(gl + 1) * dh].astype(F32).T.astype(BF16)
            vt[gl, dh:VROWS, :] = jnp.ones((VROWS - dh, vt.shape[2]), BF16)

    js = [i * r + (r - 1) - d for d in range(n_tiles)]
    offs = [pl.multiple_of(jnp.maximum(j, 0) * tk, tk) for j in js]
    boffs = [pl.multiple_of(jnp.where(js[d] >= 0, d, n_tiles) * tk, tk) for d in range(n_tiles)]
    k_tiles = [[k_ref[pl.ds(off, tk), gl * dh:(gl + 1) * dh] for off in offs] for gl in range(WIN_GROUPS)]
    v_tiles = [[vt[gl, :, pl.ds(off, tk)] for off in offs] for gl in range(WIN_GROUPS)]
    qs = [_q_scaled(q_ref, gl * NSA_HPG + hh) for gl, hh in heads]
    ss = [[_dot_nt(k_tiles[gl][d], qs[n]) for d in range(n_tiles)] for n, (gl, hh) in enumerate(heads)]
    ps = []
    for n, (gl, hh) in enumerate(heads):
        s = [ss[n][d] + bias_ref[gl, hh, :, pl.ds(boffs[d], tq)] for d in range(n_tiles)]
        m = functools.reduce(jnp.maximum, [jnp.max(x, axis=0, keepdims=True) for x in s])
        ps.append([jnp.exp2(x - m).astype(BF16) for x in s])
    pvs = [[_dot(v_tiles[gl][d], ps[n][d]) for d in range(n_tiles)] for n, (gl, hh) in enumerate(heads)]
    for n, (gl, hh) in enumerate(heads):
        cols = slice(n * dh, (n + 1) * dh)
        o = _out_t(functools.reduce(lambda x, y: x + y, pvs[n]), _gate_row(gate_ref, gp * WIN_GROUPS + gl, hh, 2))
        o = o + oc_ref[0, n].astype(F32) + os_ref[0, n].astype(F32)
        ya_ref[:, cols] = (o.T * _silu(z_ref[:, cols].astype(F32))).astype(BF16)


def _win_attn(proj, gates, o_cmp, o_slc, bias_w, bsz, t):
    g_, dh, tq, tk = NSA_KV_GROUPS, NSA_HEAD_DIM, WIN_TQ, ATT_TK
    nt = t // tq
    ng = WIN_GROUPS
    gw = ng * NSA_HPG * dh
    kw = ng * dh
    n_tiles = (bias_w.shape[3] - tq) // tk
    return pl.pallas_call(
        functools.partial(_win_attn_kernel, n_tiles=n_tiles),
        grid=(bsz, g_ // ng, nt),
        in_specs=[
            pl.BlockSpec((tq, gw), lambda b, g, i: (b * nt + i, COL_AQ // gw + g)),
            pl.BlockSpec((t, kw), lambda b, g, i: (b, COL_WK // kw + g)),
            pl.BlockSpec((t, kw), lambda b, g, i: (b, COL_WV // kw + g)),
            pl.BlockSpec((ng, NSA_HPG, tk, bias_w.shape[3]), lambda b, g, i: (g, 0, 0, 0)),
            pl.BlockSpec((GATE_COLS, tq), lambda b, g, i: (0, b * nt + i)),
            pl.BlockSpec((1, ng * NSA_HPG, dh, tq), lambda b, g, i: (b, g, 0, i)),
            pl.BlockSpec((1, ng * NSA_HPG, dh, tq), lambda b, g, i: (b, g, 0, i)),
            pl.BlockSpec((tq, gw), lambda b, g, i: (b * nt + i, COL_AZ // gw + g)),
        ],
        out_specs=pl.BlockSpec((tq, gw), lambda b, g, i: (b * nt + i, g)),
        out_shape=jax.ShapeDtypeStruct((bsz * t, NSA_WIDTH), BF16),
        scratch_shapes=[pltpu.VMEM((ng, VROWS, t), BF16)],
        compiler_params=_cparams(("parallel", "parallel", "arbitrary")),
        name="win_attn",
    )(proj, proj, proj, bias_w, gates, o_cmp, o_slc, proj)


def _outproj_kernel(ym_ref, ya_ref, x_ref, w_ref, gain_ref, o_ref, mix):
    j = pl.program_id(1)
    nj = pl.num_programs(1)
    tn = x_ref.shape[1]

    @pl.when(j == 0)
    def _():
        mix[:, 0:MLSTM_WIDTH] = ym_ref[...]
        mix[:, MLSTM_WIDTH:D_MIX] = ya_ref[...]

    o_ref[:, pl.ds(pl.multiple_of(j * tn, tn), tn)] = x_ref[...] + _dot(mix[...], w_ref[...])

    @pl.when(j == nj - 1)
    def _():
        n_tiles = o_ref.shape[1] // tn
        ss = None
        for jj in range(n_tiles):
            y = o_ref[:, jj * tn:(jj + 1) * tn]
            part = jnp.sum(y * y, axis=-1, keepdims=True)
            ss = part if ss is None else ss + part
        inv = lax.rsqrt(ss / o_ref.shape[1] + RMS_EPS)
        for jj in range(n_tiles):
            cols = slice(jj * tn, (jj + 1) * tn)
            o_ref[:, cols] = o_ref[:, cols] * inv * gain_ref[:, cols]


def _out_proj(y_m, y_a, x2d, w_out, gain):
    n, d = x2d.shape
    tm, tn = min(OUTPROJ_TM, n), OUTPROJ_TN
    nj = d // tn
    return pl.pallas_call(
        _outproj_kernel,
        grid=(n // tm, nj),
        in_specs=[
            pl.BlockSpec((tm, MLSTM_WIDTH), lambda i, j: (i, 0)),
            pl.BlockSpec((tm, NSA_WIDTH), lambda i, j: (i, 0)),
            pl.BlockSpec((tm, tn), lambda i, j: (i, j)),
            pl.BlockSpec((D_MIX, tn), lambda i, j: (0, j)),
            pl.BlockSpec((1, d), lambda i, j: (0, 0)),
        ],
        out_specs=pl.BlockSpec((tm, d), lambda i, j: (i, 0)),
        out_shape=jax.ShapeDtypeStruct((n, d), F32),
        scratch_shapes=[pltpu.VMEM((tm, D_MIX), BF16)],
        compiler_params=_cparams(("parallel", "arbitrary")),
        name="out_proj",
    )(y_m, y_a, x2d, w_out, gain)


def _rel_bucket(dist):
    n = jnp.maximum(dist, 0)
    nf = jnp.maximum(n, REL_MAX_EXACT).astype(jnp.float32)
    large = REL_MAX_EXACT + (jnp.log(nf / REL_MAX_EXACT) / math.log(REL_MAX_DISTANCE / REL_MAX_EXACT)
                             * (REL_BUCKETS - REL_MAX_EXACT)).astype(jnp.int32)
    large = jnp.minimum(large, REL_BUCKETS - 1)
    return jnp.where(n < REL_MAX_EXACT, n, large)


def _toeplitz_vec(by_dist, base, n_pos, n_neg, lo, hi, shift=None):
    w = n_pos + n_neg
    c = np.arange(w)
    dist = np.where(c < n_pos, base + c, base - (w - c))
    ok = (dist >= lo) & (dist < hi)
    dmax = by_dist.shape[1]

    def run(start, length):
        left, right = max(0, -start), max(0, start + length - dmax)
        ext = jnp.pad(by_dist, ((0, 0), (left, right)), mode="edge")
        return lax.slice_in_dim(ext, start + left, start + left + length, axis=1)

    vals = jnp.concatenate([run(base, n_pos), run(base - n_neg, n_neg)], axis=1)
    if shift is not None:
        vals = vals - shift
    return jnp.where(ok[None], vals * LOG2E, NEG_LOGIT)


def _toeplitz_t(w_row, n_keys, n_q, key_step):
    x = jnp.broadcast_to(w_row, (n_keys, w_row.shape[1]))
    return pltpu.roll(x, 0, 1, stride=key_step, stride_axis=0)[:, 0:n_q]


def _bias_tables_kernel(wc_ref, ws_ref, ww_ref, bc_ref, bs_ref, bw_ref):
    nb, t = bc_ref.shape[2], bc_ref.shape[3]
    bc_ref[0, 0] = _toeplitz_t(wc_ref[0], nb, t, CMP_STRIDE)
    bs_ref[0, 0] = _toeplitz_t(ws_ref[0], bs_ref.shape[2], bs_ref.shape[3], 1)
    bw_ref[0, 0] = _toeplitz_t(ww_ref[0], bw_ref.shape[2], bw_ref.shape[3], 1)


def _bias_tables(rel_bias, t):
    tq, tk = ATT_TQ, ATT_TK
    g_, hpg = NSA_KV_GROUPS, NSA_HPG
    rb = rel_bias.astype(F32)
    dmax = REL_MAX_DISTANCE + 2 * max(SLC_TQ, SLC_TK, WIN_TQ, tq, tk)
    onehot = (_rel_bucket(jnp.arange(dmax, dtype=jnp.int32))[None, :] == jnp.arange(REL_BUCKETS, dtype=jnp.int32)[:, None])
    by_dist = jnp.dot(rb.T, onehot.astype(F32), precision=lax.Precision.HIGHEST)
    far = rb[REL_BUCKETS - 1][:, None]
    big = 1 << 30
    nb = t // CMP_STRIDE
    wc = _toeplitz_vec(by_dist, -(CMP_BLOCK - 1), t, t, 0, big)[:, None, :]
    sq, sk = min(SLC_TQ, t), SLC_TK
    r = sq // sk
    n_s = min(-(-(REL_MAX_DISTANCE + sk - 1) // sk) + r - 1, t // sk)
    wm = (n_s - 1) * sk + sq
    ws = _toeplitz_vec(by_dist, -(r - 1) * sk, wm, sk, 0, big, far)[:, None, :]
    wq = min(WIN_TQ, t)
    rw = wq // tk
    n_w = rw + (WINDOW - 1 + tk - 1) // tk
    wmw = n_w * tk + wq
    ww = _toeplitz_vec(by_dist, -(rw - 1) * tk, wmw, tk, 0, WINDOW)[:, None, :]
    return pl.pallas_call(
        _bias_tables_kernel,
        grid=(NSA_HEADS,),
        in_specs=[
            pl.BlockSpec((1, 1, 2 * t), lambda h: (h, 0, 0)),
            pl.BlockSpec((1, 1, wm + sk), lambda h: (h, 0, 0)),
            pl.BlockSpec((1, 1, wmw + tk), lambda h: (h, 0, 0)),
        ],
        out_specs=[
            pl.BlockSpec((1, 1, nb, t), lambda h: (h // hpg, h % hpg, 0, 0)),
            pl.BlockSpec((1, 1, sk, wm), lambda h: (h // hpg, h % hpg, 0, 0)),
            pl.BlockSpec((1, 1, tk, wmw), lambda h: (h // hpg, h % hpg, 0, 0)),
        ],
        out_shape=[
            jax.ShapeDtypeStruct((g_, hpg, nb, t), F32),
            jax.ShapeDtypeStruct((g_, hpg, sk, wm), F32),
            jax.ShapeDtypeStruct((g_, hpg, tk, wmw), F32),
        ],
        compiler_params=_cparams(("parallel",)),
        name="bias_tables",
    )(wc, ws, ww)


def _cover_t(t):
    nb = t // CMP_STRIDE
    n_cmp = (t - CMP_BLOCK) // CMP_STRIDE + 1
    n_slc = t // SLC_BLOCK
    cs = np.arange(nb) * CMP_STRIDE
    ss = np.arange(n_slc) * SLC_BLOCK
    cover = np.clip(np.minimum(cs[:, None] + CMP_BLOCK, ss[None, :] + SLC_BLOCK)
                    - np.maximum(cs[:, None], ss[None, :]), 0, None) / CMP_BLOCK
    cover[n_cmp:] = 0.0
    return jnp.asarray(cover.T, dtype=BF16)


def kernel(x, norm_gain, w_in, w_conv, b_igate, b_fgate, mlstm_norm_gain, cmp_k_pos, cmp_k_w1, cmp_k_w2,
           cmp_v_pos, cmp_v_w1, cmp_v_w2, rel_bias, w_out, final_norm_gain):
    bsz, t, d = x.shape
    assert d == D_MODEL and t % MLSTM_L == 0 and t % ATT_TQ == 0 and (t // CMP_STRIDE) % LANES == 0
    n = bsz * t
    x2d = x.reshape(n, d)

    w_main, w_gate = _w_prep(w_in.T)
    proj, gates = _in_proj(x2d, norm_gain.reshape(1, d).astype(F32), w_main, w_gate)

    g_rows = gates[:, :2 * MLSTM_HEADS].reshape(bsz, t, 2, MLSTM_HEADS).transpose(0, 3, 2, 1)
    y_m = _mlstm(proj, g_rows, w_conv.astype(F32), b_igate.astype(F32), b_fgate.astype(F32),
                 mlstm_norm_gain.reshape(1, MLSTM_WIDTH).astype(F32), bsz, t)

    dh = NSA_HEAD_DIM
    half = CMP_BLOCK // 2

    def w1cat(w1):
        return jnp.concatenate([w1[:half].reshape(half * dh, dh), w1[half:].reshape(half * dh, dh)], axis=1).astype(BF16)

    k_cmp, v_cmp_t = _compress(
        proj, w1cat(cmp_k_w1), cmp_k_w2.astype(BF16), cmp_k_pos.reshape(2, half * dh).astype(BF16),
        w1cat(cmp_v_w1), cmp_v_w2.astype(BF16), cmp_v_pos.reshape(2, half * dh).astype(BF16), bsz, t)

    bias_c, bias_s, bias_w = _bias_tables(rel_bias, t)
    gates_t = gates.T
    o_cmp, selbias_t = _cmp_attn(proj, gates_t, k_cmp, v_cmp_t, bias_c, _cover_t(t), bsz, t)
    o_slc = _slc_attn(proj, gates_t, selbias_t, bias_s, bsz, t)
    y_a = _win_attn(proj, gates_t, o_cmp, o_slc, bias_w, bsz, t)

    out = _out_proj(y_m, y_a, x2d, w_out.astype(BF16), final_norm_gain.reshape(1, d).astype(F32))
    return out.reshape(bsz, t, d)
```

```python
import functools
import math

import jax
import jax.numpy as jnp
import numpy as np
from jax import lax
from jax.experimental import pallas as pl
from jax.experimental.pallas import tpu as pltpu

F32 = jnp.float32
BF16 = jnp.bfloat16

D_MODEL = 4096
D_MIX = D_MODEL
MLSTM_WIDTH = D_MIX // 2
MLSTM_HEADS = 4
MLSTM_V_DIM = MLSTM_WIDTH // MLSTM_HEADS
MLSTM_QK_DIM = MLSTM_V_DIM // 2
MLSTM_QK_WIDTH = MLSTM_HEADS * MLSTM_QK_DIM
CONV_WIDTH = 4
NSA_WIDTH = D_MIX - MLSTM_WIDTH
NSA_HEAD_DIM = 128
NSA_HEADS = NSA_WIDTH // NSA_HEAD_DIM
NSA_KV_GROUPS = 4
NSA_HPG = NSA_HEADS // NSA_KV_GROUPS
NSA_KV_WIDTH = NSA_KV_GROUPS * NSA_HEAD_DIM
CMP_BLOCK = 32
CMP_STRIDE = 16
SLC_BLOCK = 64
SLC_TOP_N = 16
WINDOW = 512
FORCE_BONUS = 1000.0
REL_BUCKETS = 32
REL_MAX_EXACT = REL_BUCKETS // 2
REL_MAX_DISTANCE = 1024
RMS_EPS = 1e-6
NEG_LOGIT = -1e30

LANES = 128
VMEM_LIMIT_BYTES = 56 * 1024 * 1024

COL_MQ = 0
COL_MK = COL_MQ + MLSTM_QK_WIDTH
COL_MV = COL_MK + MLSTM_QK_WIDTH
COL_MO = COL_MV + MLSTM_WIDTH
COL_MZ = COL_MO + MLSTM_WIDTH
COL_AQ = COL_MZ + MLSTM_WIDTH
COL_CK = COL_AQ + NSA_WIDTH
COL_CV = COL_CK + NSA_KV_WIDTH
COL_SK = COL_CV + NSA_KV_WIDTH
COL_SV = COL_SK + NSA_KV_WIDTH
COL_WK = COL_SV + NSA_KV_WIDTH
COL_WV = COL_WK + NSA_KV_WIDTH
COL_AZ = COL_WV + NSA_KV_WIDTH
MAIN_COLS = COL_AZ + NSA_WIDTH
GATE_COLS = LANES
GATE_I = 0
GATE_F = MLSTM_HEADS
GATE_NSA = 2 * MLSTM_HEADS

INPROJ_TM = 512
INPROJ_TN = 1536
OUTPROJ_TM = 512
OUTPROJ_TN = 1024
MLSTM_L = 256
ATT_TQ = 256
ATT_TK = 256
WPREP_TR = 512
SLC_TQ = 512
SLC_TK = 512
CMP_TQ = 512
CMP_GROUPS = 2
WIN_TQ = 256
WIN_GROUPS = 2
MASK_BIG = 2.0 ** 100


def _cparams(sem):
    return pltpu.CompilerParams(dimension_semantics=sem, vmem_limit_bytes=VMEM_LIMIT_BYTES)


def _sigmoid(x):
    return 0.5 * jnp.tanh(0.5 * x) + 0.5


def _silu(x):
    return x * _sigmoid(x)


def _dot(a, b):
    return jnp.dot(a, b, preferred_element_type=F32)


def _dot_nt(a, b):
    return lax.dot_general(a, b, (((1,), (1,)), ((), ())), preferred_element_type=F32)


W_OFF_I = COL_MZ + MLSTM_WIDTH
W_OFF_AQ = W_OFF_I + 2 * MLSTM_HEADS
W_OFF_GATE = W_OFF_AQ + NSA_WIDTH + 6 * NSA_KV_WIDTH
W_OFF_AZ = W_OFF_GATE + 3 * NSA_HEADS


def _wprep_kernel(w_ref, gi_ref, ga_ref, o_ref, g_ref):
    o_ref[...] = w_ref[...].astype(BF16)

    @pl.when(pl.program_id(0) == 0)
    def _():
        pad = jnp.zeros((GATE_COLS - gi_ref.shape[0] - ga_ref.shape[0], g_ref.shape[1]), F32)
        g_ref[...] = jnp.concatenate([gi_ref[...], ga_ref[...], pad], axis=0).astype(BF16)


def _w_prep(w_t):
    rows, d = w_t.shape
    tr = WPREP_TR
    nb_a, nb_b = W_OFF_I // tr, (W_OFF_GATE - W_OFF_AQ) // tr

    def src(m):
        skip = jnp.where(m >= nb_a, W_OFF_AQ - W_OFF_I, 0) + jnp.where(m >= nb_a + nb_b, W_OFF_AZ - W_OFF_GATE, 0)
        return pl.multiple_of(m * tr + skip, 8)

    rows_at = lambda start, size: pl.BlockSpec((pl.Element(size), pl.Element(d)), lambda m: (start, 0))
    return pl.pallas_call(
        _wprep_kernel,
        grid=(MAIN_COLS // tr,),
        in_specs=[
            pl.BlockSpec((pl.Element(tr), pl.Element(d)), lambda m: (src(m), 0)),
            rows_at(W_OFF_I, W_OFF_AQ - W_OFF_I),
            rows_at(W_OFF_GATE, W_OFF_AZ - W_OFF_GATE),
        ],
        out_specs=[pl.BlockSpec((tr, d), lambda m: (m, 0)), pl.BlockSpec((GATE_COLS, d), lambda m: (0, 0))],
        out_shape=[jax.ShapeDtypeStruct((MAIN_COLS, d), BF16), jax.ShapeDtypeStruct((GATE_COLS, d), BF16)],
        compiler_params=_cparams(("arbitrary",)),
        name="w_prep",
    )(w_t, w_t, w_t)


def _inproj_kernel(x_ref, gain_ref, w_hbm, wg_ref, o_hbm, og_ref, h_ref, wbuf, obuf, w_sem, o_sem):
    i = pl.program_id(0)
    ni = pl.num_programs(0)
    tm, tn = obuf.shape[1], obuf.shape[2]
    nj = w_hbm.shape[0] // tn

    def w_copy(j, slot):
        return pltpu.make_async_copy(w_hbm.at[pl.ds(pl.multiple_of(j * tn, tn), tn), :], wbuf.at[slot], w_sem.at[slot])

    def o_copy(j, slot):
        dst = o_hbm.at[pl.ds(pl.multiple_of(i * tm, tm), tm), pl.ds(pl.multiple_of(j * tn, tn), tn)]
        return pltpu.make_async_copy(obuf.at[slot], dst, o_sem.at[slot])

    @pl.when(i == 0)
    def _():
        w_copy(0, 0).start()

    x = x_ref[...]
    ms = jnp.mean(x * x, axis=-1, keepdims=True)
    h = (x * lax.rsqrt(ms + RMS_EPS) * gain_ref[...]).astype(BF16)
    h_ref[...] = h
    og_ref[...] = _dot_nt(h, wg_ref[...])

    def body(j, carry):
        slot = j % 2
        nxt = (j + 1) % nj
        w_copy(j, slot).wait()

        @pl.when((j + 1 < nj) | (i + 1 < ni))
        def _():
            w_copy(nxt, 1 - slot).start()

        @pl.when((j >= 2) | (i > 0))
        def _():
            o_copy(j, slot).wait()

        obuf[slot] = _dot_nt(h_ref[...], wbuf[slot]).astype(BF16)
        o_copy(j, slot).start()
        return carry

    lax.fori_loop(0, nj, body, 0)

    @pl.when(i == ni - 1)
    def _():
        o_copy(nj - 2, (nj - 2) % 2).wait()
        o_copy(nj - 1, (nj - 1) % 2).wait()


def _in_proj(x2d, gain, w_main, w_gate):
    n, d = x2d.shape
    tm, tn = min(INPROJ_TM, n), INPROJ_TN
    assert (MAIN_COLS // tn) % 2 == 0
    return pl.pallas_call(
        _inproj_kernel,
        grid=(n // tm,),
        in_specs=[
            pl.BlockSpec((tm, d), lambda i: (i, 0)),
            pl.BlockSpec((1, d), lambda i: (0, 0)),
            pl.BlockSpec(memory_space=pl.ANY),
            pl.BlockSpec((GATE_COLS, d), lambda i: (0, 0)),
        ],
        out_specs=[
            pl.BlockSpec(memory_space=pl.ANY),
            pl.BlockSpec((tm, GATE_COLS), lambda i: (i, 0)),
        ],
        out_shape=[
            jax.ShapeDtypeStruct((n, MAIN_COLS), BF16),
            jax.ShapeDtypeStruct((n, GATE_COLS), F32),
        ],
        scratch_shapes=[
            pltpu.VMEM((tm, d), BF16),
            pltpu.VMEM((2, tn, d), BF16),
            pltpu.VMEM((2, tm, tn), BF16),
            pltpu.SemaphoreType.DMA((2,)),
            pltpu.SemaphoreType.DMA((2,)),
        ],
        compiler_params=_cparams(("arbitrary",)),
        name="in_proj",
    )(x2d, gain, w_main, w_gate)


def _mlstm_kernel(bi_ref, bf_ref, q_ref, k_ref, v_ref, o_ref, z_ref, g_ref, wq_ref, wk_ref, ng_ref,
                  y_ref, qext, kext, c_st, n_st, m_st):
    L = MLSTM_L
    HIST = 8
    dqk, dv = MLSTM_QK_DIM, MLSTM_V_DIM

    @pl.when(pl.program_id(1) == 0)
    def _():
        qext[:, 0:HIST, :] = jnp.zeros((MLSTM_HEADS, HIST, dqk), F32)
        kext[:, 0:HIST, :] = jnp.zeros((MLSTM_HEADS, HIST, dqk), F32)
        c_st[...] = jnp.zeros_like(c_st)
        n_st[...] = jnp.zeros_like(n_st)
        m_st[...] = jnp.zeros_like(m_st)

    rr = lax.broadcasted_iota(jnp.int32, (L, L), 0)
    cc = lax.broadcasted_iota(jnp.int32, (L, L), 1)
    upper = (rr <= cc).astype(F32)

    for hd in range(MLSTM_HEADS):
        qcols = slice(hd * dqk, (hd + 1) * dqk)
        vcols = slice(hd * dv, (hd + 1) * dv)
        qext[hd, HIST:HIST + L, :] = q_ref[:, qcols].astype(F32)
        kext[hd, HIST:HIST + L, :] = k_ref[:, qcols].astype(F32)

        def conv_silu(ext, w_ref):
            w = w_ref[:, qcols]
            y = ext[hd, pl.ds(HIST, L), :] * w[CONV_WIDTH - 1:CONV_WIDTH, :]
            for s in range(1, CONV_WIDTH):
                y = y + ext[hd, pl.ds(HIST - s, L), :] * w[CONV_WIDTH - 1 - s:CONV_WIDTH - s, :]
            return _silu(y)

        qc = conv_silu(qext, wq_ref)
        kc = conv_silu(kext, wk_ref) * (dqk ** -0.5)
        qext[hd, 0:HIST, :] = qext[hd, L:L + HIST, :]
        kext[hd, 0:HIST, :] = kext[hd, L:L + HIST, :]

        g = g_ref[0, hd]
        i_row = g[0:1, :] + bi_ref[hd]
        f_row = g[1:2, :] + bf_ref[hd]
        lf_row = jnp.minimum(f_row, 0.0) - jnp.log(1.0 + jnp.exp(-jnp.abs(f_row)))

        bcum_row = jnp.dot(jnp.broadcast_to(lf_row, (8, L)), upper, preferred_element_type=F32,
                           precision=lax.Precision.HIGHEST)[0:1, :]
        bcum_col = jnp.sum(jnp.where(rr == cc, bcum_row, 0.0), axis=1, keepdims=True)
        gsum = bcum_row[:, L - 1:L]
        m_prev = m_st[hd]

        dlog = jnp.where(rr >= cc, bcum_col - bcum_row + i_row, -jnp.inf)
        m_inter = bcum_col + m_prev
        m_t = jnp.maximum(m_inter, jnp.max(dlog, axis=1, keepdims=True))
        dmat = jnp.exp(dlog - m_t)
        inter = jnp.exp(m_inter - m_t)

        qb = qc.astype(BF16)
        kct = kc.T
        vb = v_ref[:, vcols]
        s = _dot(qb, kct.astype(BF16)) * dmat
        c_prev = c_st[hd]
        n_prev = n_st[hd]
        num = _dot(s.astype(BF16), vb) + inter * _dot(qb, c_prev.astype(BF16))
        qn = jnp.sum(s, axis=1, keepdims=True) + inter * jnp.sum(qc * n_prev, axis=1, keepdims=True)
        hh = num / jnp.maximum(jnp.abs(qn), jnp.exp(-m_t))

        wlog = gsum - bcum_row + i_row
        m_next = jnp.maximum(gsum + m_prev, jnp.max(wlog, axis=1, keepdims=True))
        wts = jnp.exp(wlog - m_next)
        keep = jnp.exp(gsum + m_prev - m_next)
        c_st[hd] = keep * c_prev + _dot((kct * wts).astype(BF16), vb)
        n_st[hd] = keep * n_prev + _dot(jnp.broadcast_to(wts, (8, L)).astype(BF16), kc.astype(BF16))[0:1, :]
        m_st[hd] = m_next

        hm = _sigmoid(o_ref[:, vcols].astype(F32)) * hh
        hm = hm * lax.rsqrt(jnp.mean(hm * hm, axis=-1, keepdims=True) + RMS_EPS)
        hm = hm * ng_ref[:, vcols]
        y_ref[:, vcols] = (hm * _silu(z_ref[:, vcols].astype(F32))).astype(BF16)


def _mlstm(proj, g_rows, w_conv, b_igate, b_fgate, norm_gain, bsz, t):
    L = MLSTM_L
    nc = t // L
    nh, dqk, dv = MLSTM_HEADS, MLSTM_QK_DIM, MLSTM_V_DIM
    row = lambda b, c: b * nc + c
    smem = pl.BlockSpec(memory_space=pltpu.SMEM)
    return pl.pallas_call(
        _mlstm_kernel,
        grid=(bsz, nc),
        in_specs=[
            smem, smem,
            pl.BlockSpec((L, MLSTM_QK_WIDTH), lambda b, c: (row(b, c), COL_MQ // MLSTM_QK_WIDTH)),
            pl.BlockSpec((L, MLSTM_QK_WIDTH), lambda b, c: (row(b, c), COL_MK // MLSTM_QK_WIDTH)),
            pl.BlockSpec((L, MLSTM_WIDTH), lambda b, c: (row(b, c), COL_MV // MLSTM_WIDTH)),
            pl.BlockSpec((L, MLSTM_WIDTH), lambda b, c: (row(b, c), COL_MO // MLSTM_WIDTH)),
            pl.BlockSpec((L, MLSTM_WIDTH), lambda b, c: (row(b, c), COL_MZ // MLSTM_WIDTH)),
            pl.BlockSpec((1, nh, 2, L), lambda b, c: (b, 0, 0, c)),
            pl.BlockSpec((CONV_WIDTH, MLSTM_QK_WIDTH), lambda b, c: (0, 0)),
            pl.BlockSpec((CONV_WIDTH, MLSTM_QK_WIDTH), lambda b, c: (0, 1)),
            pl.BlockSpec((1, MLSTM_WIDTH), lambda b, c: (0, 0)),
        ],
        out_specs=pl.BlockSpec((L, MLSTM_WIDTH), lambda b, c: (row(b, c), 0)),
        out_shape=jax.ShapeDtypeStruct((bsz * t, MLSTM_WIDTH), BF16),
        scratch_shapes=[
            pltpu.VMEM((nh, L + 8, dqk), F32),
            pltpu.VMEM((nh, L + 8, dqk), F32),
            pltpu.VMEM((nh, dqk, dv), F32),
            pltpu.VMEM((nh, 1, dqk), F32),
            pltpu.VMEM((nh, 1, 1), F32),
        ],
        compiler_params=_cparams(("parallel", "arbitrary")),
        name="mlstm",
    )(b_igate, b_fgate, proj, proj, proj, proj, proj, g_rows, w_conv, w_conv, norm_gain)


def _compress_kernel(ck_ref, cv_ref, w1k_ref, w2k_ref, pk_ref, w1v_ref, w2v_ref, pv_ref,
                     ok_ref, ov_ref, xf, xcat):
    t = ck_ref.shape[0]
    nb = t // CMP_STRIDE
    dh = NSA_HEAD_DIM

    def one(src_ref, w1_ref, w2_ref, pos_ref):
        xf[...] = src_ref[...].astype(F32)
        for l in range(CMP_STRIDE):
            xcat[:, l * dh:(l + 1) * dh] = xf[pl.ds(l, nb, stride=CMP_STRIDE), :].astype(BF16)
        w1 = w1_ref[...]
        ab = _dot(xcat[...], w1)
        pp = _dot(pos_ref[...], w1)
        pos_term = pp[0:1, 0:dh] + pp[1:2, dh:2 * dh]
        second = pltpu.roll(ab[:, dh:2 * dh], nb - 1, 0)
        hid = _silu(ab[:, 0:dh] + second + pos_term)
        return _dot(hid.astype(BF16), w2_ref[...])

    ok_ref[0, 0] = one(ck_ref, w1k_ref, w2k_ref, pk_ref).astype(BF16)
    ov_ref[0, 0] = one(cv_ref, w1v_ref, w2v_ref, pv_ref).T.astype(BF16)


def _compress(proj, w1k, w2k, pk, w1v, w2v, pv, bsz, t):
    g_, dh = NSA_KV_GROUPS, NSA_HEAD_DIM
    nb = t // CMP_STRIDE
    full = lambda a: pl.BlockSpec(a.shape, lambda b, g: (0,) * a.ndim)
    k_spec = pl.BlockSpec((1, 1, nb, dh), lambda b, g: (b, g, 0, 0))
    vt_spec = pl.BlockSpec((1, 1, dh, nb), lambda b, g: (b, g, 0, 0))
    return pl.pallas_call(
        _compress_kernel,
        grid=(bsz, g_),
        in_specs=[
            pl.BlockSpec((t, dh), lambda b, g: (b, COL_CK // dh + g)),
            pl.BlockSpec((t, dh), lambda b, g: (b, COL_CV // dh + g)),
            full(w1k), full(w2k), full(pk), full(w1v), full(w2v), full(pv),
        ],
        out_specs=[k_spec, vt_spec],
        out_shape=[jax.ShapeDtypeStruct((bsz, g_, nb, dh), BF16), jax.ShapeDtypeStruct((bsz, g_, dh, nb), BF16)],
        scratch_shapes=[pltpu.VMEM((t, dh), F32), pltpu.VMEM((nb, CMP_STRIDE * dh), BF16)],
        compiler_params=_cparams(("parallel", "parallel")),
        name="compress",
    )(proj, proj, w1k, w2k, pk, w1v, w2v, pv)


LOG2E = math.log2(math.e)
VROWS = NSA_HEAD_DIM + 16


def _q_scaled(q_ref, hh):
    dh = NSA_HEAD_DIM
    return (q_ref[:, hh * dh:(hh + 1) * dh].astype(F32) * (dh ** -0.5 * LOG2E)).astype(BF16)


def _q_t(q_ref, hh):
    dh = NSA_HEAD_DIM
    return (q_ref[:, hh * dh:(hh + 1) * dh].astype(F32) * (dh ** -0.5 * LOG2E)).T.astype(BF16)


def _gate_row(gt_ref, g, hh, branch):
    row = GATE_NSA + 3 * (g * NSA_HPG + hh) + branch
    return _sigmoid(gt_ref[pl.ds(row, 1), :])


def _fill_vt(vt, v_ref):
    dh = NSA_HEAD_DIM
    vt[0:dh, :] = v_ref[...].astype(F32).T.astype(BF16)
    vt[dh:VROWS, :] = jnp.ones((VROWS - dh, vt.shape[1]), BF16)


def _out_t(acc, gate_row):
    dh = NSA_HEAD_DIM
    return acc[0:dh, :] * (gate_row / acc[dh:dh + 1, :])


def _cmp_attn_kernel(q_ref, kc_ref, vct_ref, bias_ref, cov_ref, gate_ref, oc_ref, sel_ref, score_ref, *, n_sel):
    tq = q_ref.shape[0]
    dh = NSA_HEAD_DIM
    n_slc = cov_ref.shape[0]
    t0 = pl.program_id(1) * tq
    gp = pl.program_id(0)
    heads = [(gl, hh) for gl in range(CMP_GROUPS) for hh in range(NSA_HPG)]
    qk = [_dot_nt(kc_ref[0, gl], _q_scaled(q_ref, gl * NSA_HPG + hh)) for gl, hh in heads]
    ps = []
    for n, (gl, hh) in enumerate(heads):
        logit = qk[n] + bias_ref[gl, hh]
        m = jnp.max(logit, axis=0, keepdims=True)
        e = jnp.exp2(logit - m)
        inv = jnp.where(m > 0.5 * NEG_LOGIT, 1.0 / jnp.sum(e, axis=0, keepdims=True), 0.0)
        ps.append(e * inv)
    ots = [_dot(vct_ref[0, gl], ps[n].astype(BF16)) for n, (gl, hh) in enumerate(heads)]
    for n, (gl, hh) in enumerate(heads):
        oc_ref[0, n] = (ots[n] * _gate_row(gate_ref, gp * CMP_GROUPS + gl, hh, 0)).astype(BF16)

    cov = cov_ref[...]
    jb = lax.broadcasted_iota(jnp.int32, (n_slc, tq), 0)
    cur = (t0 + lax.broadcasted_iota(jnp.int32, (n_slc, tq), 1)) // SLC_BLOCK
    valid = jb <= cur
    forced = (jb == 0) | (jb == cur) | (jb == cur - 1)
    sub = 8
    row_id = lax.broadcasted_iota(jnp.int32, (sub, tq), 0)
    for gl in range(CMP_GROUPS):
        g4 = ps[gl * NSA_HPG:(gl + 1) * NSA_HPG]
        p_sum = (g4[0] + g4[1]) + (g4[2] + g4[3])
        p_hi = p_sum.astype(BF16)
        r1 = p_sum - p_hi.astype(F32)
        p_mid = r1.astype(BF16)
        p_lo = (r1 - p_mid.astype(F32)).astype(BF16)
        st = _dot(cov, p_hi) + _dot(cov, p_mid) + _dot(cov, p_lo)
        score_ref[gl] = jnp.where(valid, st + jnp.where(forced, FORCE_BONUS, 0.0), -1.0)
        groups = [score_ref[gl, r:r + sub, :] for r in range(0, n_slc, sub)]
        ranks = [jnp.zeros((sub, tq), F32) for _ in groups]
        for j2 in range(n_slc):
            row = score_ref[gl, j2:j2 + 1, :]
            for gi, sc in enumerate(groups):
                r0 = gi * sub
                if r0 > j2:
                    inc = jnp.where(row >= sc, 1.0, 0.0)
                elif r0 + sub - 1 <= j2:
                    inc = jnp.where(row > sc, 1.0, 0.0)
                else:
                    inc = jnp.where(row_id > j2 - r0, jnp.where(row >= sc, 1.0, 0.0), jnp.where(row > sc, 1.0, 0.0))
                ranks[gi] = ranks[gi] + inc
        for gi, rk in enumerate(ranks):
            score_ref[gl, gi * sub:(gi + 1) * sub, :] = rk
        sel = valid & (score_ref[gl] < n_sel)
        sel_ref[0, gl, 0:n_slc, :] = jnp.where(sel, 0.0, -MASK_BIG).astype(BF16)
        if n_slc < LANES:
            sel_ref[0, gl, n_slc:LANES, :] = jnp.zeros((LANES - n_slc, tq), BF16)


def _cmp_attn(proj, gates, k_cmp, v_cmp_t, bias_c, cover_t, bsz, t):
    g_, dh, tq = NSA_KV_GROUPS, NSA_HEAD_DIM, CMP_TQ
    nt = t // tq
    nb = t // CMP_STRIDE
    n_slc = t // SLC_BLOCK
    ng = CMP_GROUPS
    gw = ng * NSA_HPG * dh
    return pl.pallas_call(
        functools.partial(_cmp_attn_kernel, n_sel=min(SLC_TOP_N, n_slc)),
        grid=(g_ // ng, nt, bsz),
        in_specs=[
            pl.BlockSpec((tq, gw), lambda g, i, b: (b * nt + i, COL_AQ // gw + g)),
            pl.BlockSpec((1, ng, nb, dh), lambda g, i, b: (b, g, 0, 0)),
            pl.BlockSpec((1, ng, dh, nb), lambda g, i, b: (b, g, 0, 0)),
            pl.BlockSpec((ng, NSA_HPG, nb, tq), lambda g, i, b: (g, 0, 0, i)),
            pl.BlockSpec((n_slc, nb), lambda g, i, b: (0, 0)),
            pl.BlockSpec((GATE_COLS, tq), lambda g, i, b: (0, b * nt + i)),
        ],
        out_specs=[
            pl.BlockSpec((1, ng * NSA_HPG, dh, tq), lambda g, i, b: (b, g, 0, i)),
            pl.BlockSpec((1, ng, LANES, tq), lambda g, i, b: (b, g, 0, i)),
        ],
        out_shape=[
            jax.ShapeDtypeStruct((bsz, NSA_HEADS, dh, t), BF16),
            jax.ShapeDtypeStruct((bsz, g_, LANES, t), BF16),
        ],
        scratch_shapes=[pltpu.VMEM((ng, n_slc, tq), F32)],
        compiler_params=_cparams(("parallel", "parallel", "parallel")),
        name="cmp_attn",
    )(proj, k_cmp, v_cmp_t, bias_c, cover_t, gates)


def _slc_attn_kernel(q_ref, sbt_ref, k_ref, v_ref, bias_ref, gate_ref, os_ref,
                     kaug, vt, qt, s_buf, p_buf, a_buf, m_ref, acc_ref, *, n_near):
    tq, tk, dh = SLC_TQ, SLC_TK, NSA_HEAD_DIM
    r = tq // tk
    t = k_ref.shape[0]
    g = pl.program_id(1)
    i = pl.program_id(2)
    heads = range(NSA_HPG)

    @pl.when(i == 0)
    def _():
        kaug[:, 0:dh] = k_ref[...]
        blk = lax.broadcasted_iota(jnp.int32, (t, LANES), 0) // SLC_BLOCK
        lane = lax.broadcasted_iota(jnp.int32, (t, LANES), 1)
        kaug[:, dh:dh + LANES] = jnp.where(blk == lane, 1.0, 0.0).astype(BF16)
        _fill_vt(vt, v_ref)

    sbt = sbt_ref[0, 0]
    for hh in heads:
        qt[hh, 0:dh, :] = _q_t(q_ref, hh)
        qt[hh, dh:dh + LANES, :] = sbt

    def ktile(j):
        return kaug[pl.ds(pl.multiple_of(j * tk, tk), tk), :]

    def vtile(j):
        return vt[:, pl.ds(pl.multiple_of(j * tk, tk), tk)]

    m_ref[...] = jnp.full(m_ref.shape, NEG_LOGIT, F32)
    acc_ref[...] = jnp.zeros_like(acc_ref)
    p_buf[...] = jnp.zeros_like(p_buf)
    a_buf[...] = jnp.ones_like(a_buf)
    k0 = ktile(0)
    for hh in heads:
        s_buf[hh] = _dot(k0, qt[hh])

    last = (i + 1) * r - 1

    def stage(j, bias_fn):
        v_prev = vtile(jnp.maximum(j - 1, 0))
        k_next = ktile(jnp.minimum(j + 1, last))
        pv = [_dot(v_prev, p_buf[hh]) for hh in heads]
        s_next = [_dot(k_next, qt[hh]) for hh in heads]
        for hh in heads:
            acc_ref[hh] = a_buf[hh] * acc_ref[hh] + pv[hh]
        for hh in heads:
            s = s_buf[hh]
            if bias_fn is not None:
                s = s + bias_fn(hh)
            m_prev = m_ref[hh]
            m_new = jnp.maximum(m_prev, jnp.max(s, axis=0, keepdims=True))
            a_buf[hh] = jnp.exp2(m_prev - m_new)
            p_buf[hh] = jnp.exp2(s - m_new).astype(BF16)
            m_ref[hh] = m_new
        for hh in heads:
            s_buf[hh] = s_next[hh]

    n_far = jnp.maximum(i * r + r - n_near, 0)

    def far_body(j, c):
        stage(j, None)
        return c

    lax.fori_loop(0, n_far, far_body, 0)

    def near_body(j, c):
        off = pl.multiple_of((i * r + (r - 1) - j) * tk, tk)
        stage(j, lambda hh: bias_ref[0, hh, :, pl.ds(off, tq)])
        return c

    lax.fori_loop(n_far, last + 1, near_body, 0)

    v_last = vtile(last)
    pv = [_dot(v_last, p_buf[hh]) for hh in heads]
    for hh in heads:
        acc = a_buf[hh] * acc_ref[hh] + pv[hh]
        os_ref[0, hh] = _out_t(acc, _gate_row(gate_ref, g, hh, 1)).astype(BF16)


def _slc_attn(proj, gates, selbias_t, bias_s, bsz, t):
    g_, dh, tq, tk = NSA_KV_GROUPS, NSA_HEAD_DIM, SLC_TQ, SLC_TK
    nt = t // tq
    gw = NSA_HPG * dh
    n_near = (bias_s.shape[3] - tq) // tk + 1
    return pl.pallas_call(
        functools.partial(_slc_attn_kernel, n_near=n_near),
        grid=(bsz, g_, nt),
        in_specs=[
            pl.BlockSpec((tq, gw), lambda b, g, i: (b * nt + i, COL_AQ // gw + g)),
            pl.BlockSpec((1, 1, LANES, tq), lambda b, g, i: (b, g, 0, i)),
            pl.BlockSpec((t, dh), lambda b, g, i: (b, COL_SK // dh + g)),
            pl.BlockSpec((t, dh), lambda b, g, i: (b, COL_SV // dh + g)),
            pl.BlockSpec((1, NSA_HPG, tk, bias_s.shape[3]), lambda b, g, i: (g, 0, 0, 0)),
            pl.BlockSpec((GATE_COLS, tq), lambda b, g, i: (0, b * nt + i)),
        ],
        out_specs=pl.BlockSpec((1, NSA_HPG, dh, tq), lambda b, g, i: (b, g, 0, i)),
        out_shape=jax.ShapeDtypeStruct((bsz, NSA_HEADS, dh, t), BF16),
        scratch_shapes=[
            pltpu.VMEM((t, dh + LANES), BF16),
            pltpu.VMEM((VROWS, t), BF16),
            pltpu.VMEM((NSA_HPG, dh + LANES, tq), BF16),
            pltpu.VMEM((NSA_HPG, tk, tq), F32),
            pltpu.VMEM((NSA_HPG, tk, tq), BF16),
            pltpu.VMEM((NSA_HPG, 1, tq), F32),
            pltpu.VMEM((NSA_HPG, 1, tq), F32),
            pltpu.VMEM((NSA_HPG, VROWS, tq), F32),
        ],
        compiler_params=_cparams(("parallel", "parallel", "arbitrary")),
        name="slc_attn",
    )(proj, selbias_t, proj, proj, bias_s, gates)


def _win_attn_kernel(q_ref, k_ref, v_ref, bias_ref, gate_ref, oc_ref, os_ref, z_ref, ya_ref, vt, *, n_tiles):
    tq, tk, dh = WIN_TQ, ATT_TK, NSA_HEAD_DIM
    r = tq // tk
    gp = pl.program_id(1)
    i = pl.program_id(2)
    heads = [(gl, hh) for gl in range(WIN_GROUPS) for hh in range(NSA_HPG)]

    @pl.when(i == 0)
    def _():
        for gl in range(WIN_GROUPS):
            vt[gl, 0:dh, :] = v_ref[:, gl * dh:(gl + 1) * dh].astype(F32).T.astype(BF16)
            vt[gl, dh:VROWS, :] = jnp.ones((VROWS - dh, vt.shape[2]), BF16)

    js = [i * r + (r - 1) - d for d in range(n_tiles)]
    offs = [pl.multiple_of(jnp.maximum(j, 0) * tk, tk) for j in js]
    boffs = [pl.multiple_of(jnp.where(js[d] >= 0, d, n_tiles) * tk, tk) for d in range(n_tiles)]
    k_tiles = [[k_ref[pl.ds(off, tk), gl * dh:(gl + 1) * dh] for off in offs] for gl in range(WIN_GROUPS)]
    v_tiles = [[vt[gl, :, pl.ds(off, tk)] for off in offs] for gl in range(WIN_GROUPS)]
    qs = [_q_scaled(q_ref, gl * NSA_HPG + hh) for gl, hh in heads]
    ss = [[_dot_nt(k_tiles[gl][d], qs[n]) for d in range(n_tiles)] for n, (gl, hh) in enumerate(heads)]
    ps = []
    for n, (gl, hh) in enumerate(heads):
        s = [ss[n][d] + bias_ref[gl, hh, :, pl.ds(boffs[d], tq)] for d in range(n_tiles)]
        m = functools.reduce(jnp.maximum, [jnp.max(x, axis=0, keepdims=True) for x in s])
        ps.append([jnp.exp2(x - m).astype(BF16) for x in s])
    pvs = [[_dot(v_tiles[gl][d], ps[n][d]) for d in range(n_tiles)] for n, (gl, hh) in enumerate(heads)]
    for n, (gl, hh) in enumerate(heads):
        cols = slice(n * dh, (n + 1) * dh)
        o = _out_t(functools.reduce(lambda x, y: x + y, pvs[n]), _gate_row(gate_ref, gp * WIN_GROUPS + gl, hh, 2))
        o = o + oc_ref[0, n].astype(F32) + os_ref[0, n].astype(F32)
        ya_ref[:, cols] = (o.T * _silu(z_ref[:, cols].astype(F32))).astype(BF16)


def _win_attn(proj, gates, o_cmp, o_slc, bias_w, bsz, t):
    g_, dh, tq, tk = NSA_KV_GROUPS, NSA_HEAD_DIM, WIN_TQ, ATT_TK
    nt = t // tq
    ng = WIN_GROUPS
    gw = ng * NSA_HPG * dh
    kw = ng * dh
    n_tiles = (bias_w.shape[3] - tq) // tk
    return pl.pallas_call(
        functools.partial(_win_attn_kernel, n_tiles=n_tiles),
        grid=(bsz, g_ // ng, nt),
        in_specs=[
            pl.BlockSpec((tq, gw), lambda b, g, i: (b * nt + i, COL_AQ // gw + g)),
            pl.BlockSpec((t, kw), lambda b, g, i: (b, COL_WK // kw + g)),
            pl.BlockSpec((t, kw), lambda b, g, i: (b, COL_WV // kw + g)),
            pl.BlockSpec((ng, NSA_HPG, tk, bias_w.shape[3]), lambda b, g, i: (g, 0, 0, 0)),
            pl.BlockSpec((GATE_COLS, tq), lambda b, g, i: (0, b * nt + i)),
            pl.BlockSpec((1, ng * NSA_HPG, dh, tq), lambda b, g, i: (b, g, 0, i)),
            pl.BlockSpec((1, ng * NSA_HPG, dh, tq), lambda b, g, i: (b, g, 0, i)),
            pl.BlockSpec((tq, gw), lambda b, g, i: (b * nt + i, COL_AZ // gw + g)),
        ],
        out_specs=pl.BlockSpec((tq, gw), lambda b, g, i: (b * nt + i, g)),
        out_shape=jax.ShapeDtypeStruct((bsz * t, NSA_WIDTH), BF16),
        scratch_shapes=[pltpu.VMEM((ng, VROWS, t), BF16)],
        compiler_params=_cparams(("parallel", "parallel", "arbitrary")),
        name="win_attn",
    )(proj, proj, proj, bias_w, gates, o_cmp, o_slc, proj)


def _outproj_kernel(ym_ref, ya_ref, x_ref, w_ref, gain_ref, o_ref, mix):
    j = pl.program_id(1)
    nj = pl.num_programs(1)
    tn = x_ref.shape[1]

    @pl.when(j == 0)
    def _():
        mix[:, 0:MLSTM_WIDTH] = ym_ref[...]
        mix[:, MLSTM_WIDTH:D_MIX] = ya_ref[...]

    o_ref[:, pl.ds(pl.multiple_of(j * tn, tn), tn)] = x_ref[...] + _dot(mix[...], w_ref[...])

    @pl.when(j == nj - 1)
    def _():
        n_tiles = o_ref.shape[1] // tn
        ss = None
        for jj in range(n_tiles):
            y = o_ref[:, jj * tn:(jj + 1) * tn]
            part = jnp.sum(y * y, axis=-1, keepdims=True)
            ss = part if ss is None else ss + part
        inv = lax.rsqrt(ss / o_ref.shape[1] + RMS_EPS)
        for jj in range(n_tiles):
            cols = slice(jj * tn, (jj + 1) * tn)
            o_ref[:, cols] = o_ref[:, cols] * inv * gain_ref[:, cols]


def _out_proj(y_m, y_a, x2d, w_out, gain):
    n, d = x2d.shape
    tm, tn = min(OUTPROJ_TM, n), OUTPROJ_TN
    nj = d // tn
    return pl.pallas_call(
        _outproj_kernel,
        grid=(n // tm, nj),
        in_specs=[
            pl.BlockSpec((tm, MLSTM_WIDTH), lambda i, j: (i, 0)),
            pl.BlockSpec((tm, NSA_WIDTH), lambda i, j: (i, 0)),
            pl.BlockSpec((tm, tn), lambda i, j: (i, j)),
            pl.BlockSpec((D_MIX, tn), lambda i, j: (0, j)),
            pl.BlockSpec((1, d), lambda i, j: (0, 0)),
        ],
        out_specs=pl.BlockSpec((tm, d), lambda i, j: (i, 0)),
        out_shape=jax.ShapeDtypeStruct((n, d), F32),
        scratch_shapes=[pltpu.VMEM((tm, D_MIX), BF16)],
        compiler_params=_cparams(("parallel", "arbitrary")),
        name="out_proj",
    )(y_m, y_a, x2d, w_out, gain)


def _rel_bucket(dist):
    n = jnp.maximum(dist, 0)
    nf = jnp.maximum(n, REL_MAX_EXACT).astype(jnp.float32)
    large = REL_MAX_EXACT + (jnp.log(nf / REL_MAX_EXACT) / math.log(REL_MAX_DISTANCE / REL_MAX_EXACT)
                             * (REL_BUCKETS - REL_MAX_EXACT)).astype(jnp.int32)
    large = jnp.minimum(large, REL_BUCKETS - 1)
    return jnp.where(n < REL_MAX_EXACT, n, large)


def _toeplitz_vec(by_dist, base, n_pos, n_neg, lo, hi, shift=None):
    w = n_pos + n_neg
    c = np.arange(w)
    dist = np.where(c < n_pos, base + c, base - (w - c))
    ok = (dist >= lo) & (dist < hi)
    dmax = by_dist.shape[1]

    def run(start, length):
        left, right = max(0, -start), max(0, start + length - dmax)
        ext = jnp.pad(by_dist, ((0, 0), (left, right)), mode="edge")
        return lax.slice_in_dim(ext, start + left, start + left + length, axis=1)

    vals = jnp.concatenate([run(base, n_pos), run(base - n_neg, n_neg)], axis=1)
    if shift is not None:
        vals = vals - shift
    return jnp.where(ok[None], vals * LOG2E, NEG_LOGIT)


def _toeplitz_t(w_row, n_keys, n_q, key_step):
    x = jnp.broadcast_to(w_row, (n_keys, w_row.shape[1]))
    return pltpu.roll(x, 0, 1, stride=key_step, stride_axis=0)[:, 0:n_q]


def _bias_tables_kernel(wc_ref, ws_ref, ww_ref, bc_ref, bs_ref, bw_ref):
    nb, t = bc_ref.shape[2], bc_ref.shape[3]
    bc_ref[0, 0] = _toeplitz_t(wc_ref[0], nb, t, CMP_STRIDE)
    bs_ref[0, 0] = _toeplitz_t(ws_ref[0], bs_ref.shape[2], bs_ref.shape[3], 1)
    bw_ref[0, 0] = _toeplitz_t(ww_ref[0], bw_ref.shape[2], bw_ref.shape[3], 1)


def _bias_tables(rel_bias, t):
    tq, tk = ATT_TQ, ATT_TK
    g_, hpg = NSA_KV_GROUPS, NSA_HPG
    rb = rel_bias.astype(F32)
    dmax = REL_MAX_DISTANCE + 2 * max(SLC_TQ, SLC_TK, WIN_TQ, tq, tk)
    onehot = (_rel_bucket(jnp.arange(dmax, dtype=jnp.int32))[None, :] == jnp.arange(REL_BUCKETS, dtype=jnp.int32)[:, None])
    by_dist = jnp.dot(rb.T, onehot.astype(F32), precision=lax.Precision.HIGHEST)
    far = rb[REL_BUCKETS - 1][:, None]
    big = 1 << 30
    nb = t // CMP_STRIDE
    wc = _toeplitz_vec(by_dist, -(CMP_BLOCK - 1), t, t, 0, big)[:, None, :]
    sq, sk = min(SLC_TQ, t), SLC_TK
    r = sq // sk
    n_s = min(-(-(REL_MAX_DISTANCE + sk - 1) // sk) + r - 1, t // sk)
    wm = (n_s - 1) * sk + sq
    ws = _toeplitz_vec(by_dist, -(r - 1) * sk, wm, sk, 0, big, far)[:, None, :]
    wq = min(WIN_TQ, t)
    rw = wq // tk
    n_w = rw + (WINDOW - 1 + tk - 1) // tk
    wmw = n_w * tk + wq
    ww = _toeplitz_vec(by_dist, -(rw - 1) * tk, wmw, tk, 0, WINDOW)[:, None, :]
    return pl.pallas_call(
        _bias_tables_kernel,
        grid=(NSA_HEADS,),
        in_specs=[
            pl.BlockSpec((1, 1, 2 * t), lambda h: (h, 0, 0)),
            pl.BlockSpec((1, 1, wm + sk), lambda h: (h, 0, 0)),
            pl.BlockSpec((1, 1, wmw + tk), lambda h: (h, 0, 0)),
        ],
        out_specs=[
            pl.BlockSpec((1, 1, nb, t), lambda h: (h // hpg, h % hpg, 0, 0)),
            pl.BlockSpec((1, 1, sk, wm), lambda h: (h // hpg, h % hpg, 0, 0)),
            pl.BlockSpec((1, 1, tk, wmw), lambda h: (h // hpg, h % hpg, 0, 0)),
        ],
        out_shape=[
            jax.ShapeDtypeStruct((g_, hpg, nb, t), F32),
            jax.ShapeDtypeStruct((g_, hpg, sk, wm), F32),
            jax.ShapeDtypeStruct((g_, hpg, tk, wmw), F32),
        ],
        compiler_params=_cparams(("parallel",)),
        name="bias_tables",
    )(wc, ws, ww)


def _cover_t(t):
    nb = t // CMP_STRIDE
    n_cmp = (t - CMP_BLOCK) // CMP_STRIDE + 1
    n_slc = t // SLC_BLOCK
    cs = np.arange(nb) * CMP_STRIDE
    ss = np.arange(n_slc) * SLC_BLOCK
    cover = np.clip(np.minimum(cs[:, None] + CMP_BLOCK, ss[None, :] + SLC_BLOCK)
                    - np.maximum(cs[:, None], ss[None, :]), 0, None) / CMP_BLOCK
    cover[n_cmp:] = 0.0
    return jnp.asarray(cover.T, dtype=BF16)


def kernel(x, norm_gain, w_in, w_conv, b_igate, b_fgate, mlstm_norm_gain, cmp_k_pos, cmp_k_w1, cmp_k_w2,
           cmp_v_pos, cmp_v_w1, cmp_v_w2, rel_bias, w_out, final_norm_gain):
    bsz, t, d = x.shape
    assert d == D_MODEL and t % MLSTM_L == 0 and t % ATT_TQ == 0 and (t // CMP_STRIDE) % LANES == 0
    n = bsz * t
    x2d = x.reshape(n, d)

    w_main, w_gate = _w_prep(w_in.T)
    proj, gates = _in_proj(x2d, norm_gain.reshape(1, d).astype(F32), w_main, w_gate)

    g_rows = gates[:, :2 * MLSTM_HEADS].reshape(bsz, t, 2, MLSTM_HEADS).transpose(0, 3, 2, 1)
    y_m = _mlstm(proj, g_rows, w_conv.astype(F32), b_igate.astype(F32), b_fgate.astype(F32),
                 mlstm_norm_gain.reshape(1, MLSTM_WIDTH).astype(F32), bsz, t)

    dh = NSA_HEAD_DIM
    half = CMP_BLOCK // 2

    def w1cat(w1):
        return jnp.concatenate([w1[:half].reshape(half * dh, dh), w1[half:].reshape(half * dh, dh)], axis=1).astype(BF16)

    k_cmp, v_cmp_t = _compress(
        proj, w1cat(cmp_k_w1), cmp_k_w2.astype(BF16), cmp_k_pos.reshape(2, half * dh).astype(BF16),
        w1cat(cmp_v_w1), cmp_v_w2.astype(BF16), cmp_v_pos.reshape(2, half * dh).astype(BF16), bsz, t)

    bias_c, bias_s, bias_w = _bias_tables(rel_bias, t)
    gates_t = gates.T
    o_cmp, selbias_t = _cmp_attn(proj, gates_t, k_cmp, v_cmp_t, bias_c, _cover_t(t), bsz, t)
    o_slc = _slc_attn(proj, gates_t, selbias_t, bias_s, bsz, t)
    y_a = _win_attn(proj, gates_t, o_cmp, o_slc, bias_w, bsz, t)

    out = _out_proj(y_m, y_a, x2d, w_out.astype(BF16), final_norm_gain.reshape(1, d).astype(F32))
    return out.reshape(bsz, t, d)
```

```python
import functools
import math

import jax
import jax.numpy as jnp
import numpy as np
from jax import lax
from jax.experimental import pallas as pl
from jax.experimental.pallas import tpu as pltpu

F32 = jnp.float32
BF16 = jnp.bfloat16

D_MODEL = 4096
D_MIX = D_MODEL
MLSTM_WIDTH = D_MIX // 2
MLSTM_HEADS = 4
MLSTM_V_DIM = MLSTM_WIDTH // MLSTM_HEADS
MLSTM_QK_DIM = MLSTM_V_DIM // 2
MLSTM_QK_WIDTH = MLSTM_HEADS * MLSTM_QK_DIM
CONV_WIDTH = 4
NSA_WIDTH = D_MIX - MLSTM_WIDTH
NSA_HEAD_DIM = 128
NSA_HEADS = NSA_WIDTH // NSA_HEAD_DIM
NSA_KV_GROUPS = 4
NSA_HPG = NSA_HEADS // NSA_KV_GROUPS
NSA_KV_WIDTH = NSA_KV_GROUPS * NSA_HEAD_DIM
CMP_BLOCK = 32
CMP_STRIDE = 16
SLC_BLOCK = 64
SLC_TOP_N = 16
WINDOW = 512
FORCE_BONUS = 1000.0
REL_BUCKETS = 32
REL_MAX_EXACT = REL_BUCKETS // 2
REL_MAX_DISTANCE = 1024
RMS_EPS = 1e-6
NEG_LOGIT = -1e30

LANES = 128
VMEM_LIMIT_BYTES = 56 * 1024 * 1024

COL_MQ = 0
COL_MK = COL_MQ + MLSTM_QK_WIDTH
COL_MV = COL_MK + MLSTM_QK_WIDTH
COL_MO = COL_MV + MLSTM_WIDTH
COL_MZ = COL_MO + MLSTM_WIDTH
COL_AQ = COL_MZ + MLSTM_WIDTH
COL_CK = COL_AQ + NSA_WIDTH
COL_CV = COL_CK + NSA_KV_WIDTH
COL_SK = COL_CV + NSA_KV_WIDTH
COL_SV = COL_SK + NSA_KV_WIDTH
COL_WK = COL_SV + NSA_KV_WIDTH
COL_WV = COL_WK + NSA_KV_WIDTH
COL_AZ = COL_WV + NSA_KV_WIDTH
MAIN_COLS = COL_AZ + NSA_WIDTH
GATE_COLS = LANES
GATE_I = 0
GATE_F = MLSTM_HEADS
GATE_NSA = 2 * MLSTM_HEADS

INPROJ_TM = 512
INPROJ_TN = 1536
OUTPROJ_TM = 512
OUTPROJ_TN = 1024
MLSTM_L = 256
ATT_TQ = 256
ATT_TK = 256
WPREP_TR = 512
SLC_TQ = 512
SLC_TK = 512
CMP_TQ = 512
CMP_GROUPS = 2
WIN_TQ = 256
WIN_GROUPS = 2
MASK_BIG = 2.0 ** 100


def _cparams(sem):
    return pltpu.CompilerParams(dimension_semantics=sem, vmem_limit_bytes=VMEM_LIMIT_BYTES)


def _sigmoid(x):
    return 0.5 * jnp.tanh(0.5 * x) + 0.5


def _silu(x):
    return x * _sigmoid(x)


def _dot(a, b):
    return jnp.dot(a, b, preferred_element_type=F32)


def _dot_nt(a, b):
    return lax.dot_general(a, b, (((1,), (1,)), ((), ())), preferred_element_type=F32)


W_OFF_I = COL_MZ + MLSTM_WIDTH
W_OFF_AQ = W_OFF_I + 2 * MLSTM_HEADS
W_OFF_GATE = W_OFF_AQ + NSA_WIDTH + 6 * NSA_KV_WIDTH
W_OFF_AZ = W_OFF_GATE + 3 * NSA_HEADS


def _wprep_kernel(w_ref, gi_ref, ga_ref, o_ref, g_ref):
    o_ref[...] = w_ref[...].astype(BF16)

    @pl.when(pl.program_id(0) == 0)
    def _():
        pad = jnp.zeros((GATE_COLS - gi_ref.shape[0] - ga_ref.shape[0], g_ref.shape[1]), F32)
        g_ref[...] = jnp.concatenate([gi_ref[...], ga_ref[...], pad], axis=0).astype(BF16)


def _w_prep(w_t):
    rows, d = w_t.shape
    tr = WPREP_TR
    nb_a, nb_b = W_OFF_I // tr, (W_OFF_GATE - W_OFF_AQ) // tr

    def src(m):
        skip = jnp.where(m >= nb_a, W_OFF_AQ - W_OFF_I, 0) + jnp.where(m >= nb_a + nb_b, W_OFF_AZ - W_OFF_GATE, 0)
        return pl.multiple_of(m * tr + skip, 8)

    rows_at = lambda start, size: pl.BlockSpec((pl.Element(size), pl.Element(d)), lambda m: (start, 0))
    return pl.pallas_call(
        _wprep_kernel,
        grid=(MAIN_COLS // tr,),
        in_specs=[
            pl.BlockSpec((pl.Element(tr), pl.Element(d)), lambda m: (src(m), 0)),
            rows_at(W_OFF_I, W_OFF_AQ - W_OFF_I),
            rows_at(W_OFF_GATE, W_OFF_AZ - W_OFF_GATE),
        ],
        out_specs=[pl.BlockSpec((tr, d), lambda m: (m, 0)), pl.BlockSpec((GATE_COLS, d), lambda m: (0, 0))],
        out_shape=[jax.ShapeDtypeStruct((MAIN_COLS, d), BF16), jax.ShapeDtypeStruct((GATE_COLS, d), BF16)],
        compiler_params=_cparams(("arbitrary",)),
        name="w_prep",
    )(w_t, w_t, w_t)


def _inproj_kernel(x_ref, gain_ref, w_hbm, wg_ref, o_hbm, og_ref, h_ref, wbuf, obuf, w_sem, o_sem):
    i = pl.program_id(0)
    ni = pl.num_programs(0)
    tm, tn = obuf.shape[1], obuf.shape[2]
    nj = w_hbm.shape[0] // tn

    def w_copy(j, slot):
        return pltpu.make_async_copy(w_hbm.at[pl.ds(pl.multiple_of(j * tn, tn), tn), :], wbuf.at[slot], w_sem.at[slot])

    def o_copy(j, slot):
        dst = o_hbm.at[pl.ds(pl.multiple_of(i * tm, tm), tm), pl.ds(pl.multiple_of(j * tn, tn), tn)]
        return pltpu.make_async_copy(obuf.at[slot], dst, o_sem.at[slot])

    @pl.when(i == 0)
    def _():
        w_copy(0, 0).start()

    x = x_ref[...]
    ms = jnp.mean(x * x, axis=-1, keepdims=True)
    h = (x * lax.rsqrt(ms + RMS_EPS) * gain_ref[...]).astype(BF16)
    h_ref[...] = h
    og_ref[...] = _dot_nt(h, wg_ref[...])

    def body(j, carry):
        slot = j % 2
        nxt = (j + 1) % nj
        w_copy(j, slot).wait()

        @pl.when((j + 1 < nj) | (i + 1 < ni))
        def _():
            w_copy(nxt, 1 - slot).start()

        @pl.when((j >= 2) | (i > 0))
        def _():
            o_copy(j, slot).wait()

        obuf[slot] = _dot_nt(h_ref[...], wbuf[slot]).astype(BF16)
        o_copy(j, slot).start()
        return carry

    lax.fori_loop(0, nj, body, 0)

    @pl.when(i == ni - 1)
    def _():
        o_copy(nj - 2, (nj - 2) % 2).wait()
        o_copy(nj - 1, (nj - 1) % 2).wait()


def _in_proj(x2d, gain, w_main, w_gate):
    n, d = x2d.shape
    tm, tn = min(INPROJ_TM, n), INPROJ_TN
    assert (MAIN_COLS // tn) % 2 == 0
    return pl.pallas_call(
        _inproj_kernel,
        grid=(n // tm,),
        in_specs=[
            pl.BlockSpec((tm, d), lambda i: (i, 0)),
            pl.BlockSpec((1, d), lambda i: (0, 0)),
            pl.BlockSpec(memory_space=pl.ANY),
            pl.BlockSpec((GATE_COLS, d), lambda i: (0, 0)),
        ],
        out_specs=[
            pl.BlockSpec(memory_space=pl.ANY),
            pl.BlockSpec((tm, GATE_COLS), lambda i: (i, 0)),
        ],
        out_shape=[
            jax.ShapeDtypeStruct((n, MAIN_COLS), BF16),
            jax.ShapeDtypeStruct((n, GATE_COLS), F32),
        ],
        scratch_shapes=[
            pltpu.VMEM((tm, d), BF16),
            pltpu.VMEM((2, tn, d), BF16),
            pltpu.VMEM((2, tm, tn), BF16),
            pltpu.SemaphoreType.DMA((2,)),
            pltpu.SemaphoreType.DMA((2,)),
        ],
        compiler_params=_cparams(("arbitrary",)),
        name="in_proj",
    )(x2d, gain, w_main, w_gate)


def _mlstm_kernel(bi_ref, bf_ref, q_ref, k_ref, v_ref, o_ref, z_ref, g_ref, wq_ref, wk_ref, ng_ref,
                  y_ref, qext, kext, c_st, n_st, m_st):
    L = MLSTM_L
    HIST = 8
    dqk, dv = MLSTM_QK_DIM, MLSTM_V_DIM

    @pl.when(pl.program_id(1) == 0)
    def _():
        qext[:, 0:HIST, :] = jnp.zeros((MLSTM_HEADS, HIST, dqk), F32)
        kext[:, 0:HIST, :] = jnp.zeros((MLSTM_HEADS, HIST, dqk), F32)
        c_st[...] = jnp.zeros_like(c_st)
        n_st[...] = jnp.zeros_like(n_st)
        m_st[...] = jnp.zeros_like(m_st)

    rr = lax.broadcasted_iota(jnp.int32, (L, L), 0)
    cc = lax.broadcasted_iota(jnp.int32, (L, L), 1)
    upper = (rr <= cc).astype(F32)

    for hd in range(MLSTM_HEADS):
        qcols = slice(hd * dqk, (hd + 1) * dqk)
        vcols = slice(hd * dv, (hd + 1) * dv)
        qext[hd, HIST:HIST + L, :] = q_ref[:, qcols].astype(F32)
        kext[hd, HIST:HIST + L, :] = k_ref[:, qcols].astype(F32)

        def conv_silu(ext, w_ref):
            w = w_ref[:, qcols]
            y = ext[hd, pl.ds(HIST, L), :] * w[CONV_WIDTH - 1:CONV_WIDTH, :]
            for s in range(1, CONV_WIDTH):
                y = y + ext[hd, pl.ds(HIST - s, L), :] * w[CONV_WIDTH - 1 - s:CONV_WIDTH - s, :]
            return _silu(y)

        qc = conv_silu(qext, wq_ref)
        kc = conv_silu(kext, wk_ref) * (dqk ** -0.5)
        qext[hd, 0:HIST, :] = qext[hd, L:L + HIST, :]
        kext[hd, 0:HIST, :] = kext[hd, L:L + HIST, :]

        g = g_ref[0, hd]
        i_row = g[0:1, :] + bi_ref[hd]
        f_row = g[1:2, :] + bf_ref[hd]
        lf_row = jnp.minimum(f_row, 0.0) - jnp.log(1.0 + jnp.exp(-jnp.abs(f_row)))

        bcum_row = jnp.dot(jnp.broadcast_to(lf_row, (8, L)), upper, preferred_element_type=F32,
                           precision=lax.Precision.HIGHEST)[0:1, :]
        bcum_col = jnp.sum(jnp.where(rr == cc, bcum_row, 0.0), axis=1, keepdims=True)
        gsum = bcum_row[:, L - 1:L]
        m_prev = m_st[hd]

        dlog = jnp.where(rr >= cc, bcum_col - bcum_row + i_row, -jnp.inf)
        m_inter = bcum_col + m_prev
        m_t = jnp.maximum(m_inter, jnp.max(dlog, axis=1, keepdims=True))
        dmat = jnp.exp(dlog - m_t)
        inter = jnp.exp(m_inter - m_t)

        qb = qc.astype(BF16)
        kct = kc.T
        vb = v_ref[:, vcols]
        s = _dot(qb, kct.astype(BF16)) * dmat
        c_prev = c_st[hd]
        n_prev = n_st[hd]
        num = _dot(s.astype(BF16), vb) + inter * _dot(qb, c_prev.astype(BF16))
        qn = jnp.sum(s, axis=1, keepdims=True) + inter * jnp.sum(qc * n_prev, axis=1, keepdims=True)
        hh = num / jnp.maximum(jnp.abs(qn), jnp.exp(-m_t))

        wlog = gsum - bcum_row + i_row
        m_next = jnp.maximum(gsum + m_prev, jnp.max(wlog, axis=1, keepdims=True))
        wts = jnp.exp(wlog - m_next)
        keep = jnp.exp(gsum + m_prev - m_next)
        c_st[hd] = keep * c_prev + _dot((kct * wts).astype(BF16), vb)
        n_st[hd] = keep * n_prev + _dot(jnp.broadcast_to(wts, (8, L)).astype(BF16), kc.astype(BF16))[0:1, :]
        m_st[hd] = m_next

        hm = _sigmoid(o_ref[:, vcols].astype(F32)) * hh
        hm = hm * lax.rsqrt(jnp.mean(hm * hm, axis=-1, keepdims=True) + RMS_EPS)
        hm = hm * ng_ref[:, vcols]
        y_ref[:, vcols] = (hm * _silu(z_ref[:, vcols].astype(F32))).astype(BF16)


def _mlstm(proj, g_rows, w_conv, b_igate, b_fgate, norm_gain, bsz, t):
    L = MLSTM_L
    nc = t // L
    nh, dqk, dv = MLSTM_HEADS, MLSTM_QK_DIM, MLSTM_V_DIM
    row = lambda b, c: b * nc + c
    smem = pl.BlockSpec(memory_space=pltpu.SMEM)
    return pl.pallas_call(
        _mlstm_kernel,
        grid=(bsz, nc),
        in_specs=[
            smem, smem,
            pl.BlockSpec((L, MLSTM_QK_WIDTH), lambda b, c: (row(b, c), COL_MQ // MLSTM_QK_WIDTH)),
            pl.BlockSpec((L, MLSTM_QK_WIDTH), lambda b, c: (row(b, c), COL_MK // MLSTM_QK_WIDTH)),
            pl.BlockSpec((L, MLSTM_WIDTH), lambda b, c: (row(b, c), COL_MV // MLSTM_WIDTH)),
            pl.BlockSpec((L, MLSTM_WIDTH), lambda b, c: (row(b, c), COL_MO // MLSTM_WIDTH)),
            pl.BlockSpec((L, MLSTM_WIDTH), lambda b, c: (row(b, c), COL_MZ // MLSTM_WIDTH)),
            pl.BlockSpec((1, nh, 2, L), lambda b, c: (b, 0, 0, c)),
            pl.BlockSpec((CONV_WIDTH, MLSTM_QK_WIDTH), lambda b, c: (0, 0)),
            pl.BlockSpec((CONV_WIDTH, MLSTM_QK_WIDTH), lambda b, c: (0, 1)),
            pl.BlockSpec((1, MLSTM_WIDTH), lambda b, c: (0, 0)),
        ],
        out_specs=pl.BlockSpec((L, MLSTM_WIDTH), lambda b, c: (row(b, c), 0)),
        out_shape=jax.ShapeDtypeStruct((bsz * t, MLSTM_WIDTH), BF16),
        scratch_shapes=[
            pltpu.VMEM((nh, L + 8, dqk), F32),
            pltpu.VMEM((nh, L + 8, dqk), F32),
            pltpu.VMEM((nh, dqk, dv), F32),
            pltpu.VMEM((nh, 1, dqk), F32),
            pltpu.VMEM((nh, 1, 1), F32),
        ],
        compiler_params=_cparams(("parallel", "arbitrary")),
        name="mlstm",
    )(b_igate, b_fgate, proj, proj, proj, proj, proj, g_rows, w_conv, w_conv, norm_gain)


def _compress_kernel(ck_ref, cv_ref, w1k_ref, w2k_ref, pk_ref, w1v_ref, w2v_ref, pv_ref,
                     ok_ref, ov_ref, xf, xcat):
    t = ck_ref.shape[0]
    nb = t // CMP_STRIDE
    dh = NSA_HEAD_DIM

    def one(src_ref, w1_ref, w2_ref, pos_ref):
        xf[...] = src_ref[...].astype(F32)
        for l in range(CMP_STRIDE):
            xcat[:, l * dh:(l + 1) * dh] = xf[pl.ds(l, nb, stride=CMP_STRIDE), :].astype(BF16)
        w1 = w1_ref[...]
        ab = _dot(xcat[...], w1)
        pp = _dot(pos_ref[...], w1)
        pos_term = pp[0:1, 0:dh] + pp[1:2, dh:2 * dh]
        second = pltpu.roll(ab[:, dh:2 * dh], nb - 1, 0)
        hid = _silu(ab[:, 0:dh] + second + pos_term)
        return _dot(hid.astype(BF16), w2_ref[...])

    ok_ref[0, 0] = one(ck_ref, w1k_ref, w2k_ref, pk_ref).astype(BF16)
    ov_ref[0, 0] = one(cv_ref, w1v_ref, w2v_ref, pv_ref).T.astype(BF16)


def _compress(proj, w1k, w2k, pk, w1v, w2v, pv, bsz, t):
    g_, dh = NSA_KV_GROUPS, NSA_HEAD_DIM
    nb = t // CMP_STRIDE
    full = lambda a: pl.BlockSpec(a.shape, lambda b, g: (0,) * a.ndim)
    k_spec = pl.BlockSpec((1, 1, nb, dh), lambda b, g: (b, g, 0, 0))
    vt_spec = pl.BlockSpec((1, 1, dh, nb), lambda b, g: (b, g, 0, 0))
    return pl.pallas_call(
        _compress_kernel,
        grid=(bsz, g_),
        in_specs=[
            pl.BlockSpec((t, dh), lambda b, g: (b, COL_CK // dh + g)),
            pl.BlockSpec((t, dh), lambda b, g: (b, COL_CV // dh + g)),
            full(w1k), full(w2k), full(pk), full(w1v), full(w2v), full(pv),
        ],
        out_specs=[k_spec, vt_spec],
        out_shape=[jax.ShapeDtypeStruct((bsz, g_, nb, dh), BF16), jax.ShapeDtypeStruct((bsz, g_, dh, nb), BF16)],
        scratch_shapes=[pltpu.VMEM((t, dh), F32), pltpu.VMEM((nb, CMP_STRIDE * dh), BF16)],
        compiler_params=_cparams(("parallel", "parallel")),
        name="compress",
    )(proj, proj, w1k, w2k, pk, w1v, w2v, pv)


LOG2E = math.log2(math.e)
VROWS = NSA_HEAD_DIM + 16


def _q_scaled(q_ref, hh):
    dh = NSA_HEAD_DIM
    return (q_ref[:, hh * dh:(hh + 1) * dh].astype(F32) * (dh ** -0.5 * LOG2E)).astype(BF16)


def _q_t(q_ref, hh):
    dh = NSA_HEAD_DIM
    return (q_ref[:, hh * dh:(hh + 1) * dh].astype(F32) * (dh ** -0.5 * LOG2E)).T.astype(BF16)


def _gate_row(gt_ref, g, hh, branch):
    row = GATE_NSA + 3 * (g * NSA_HPG + hh) + branch
    return _sigmoid(gt_ref[pl.ds(row, 1), :])


def _fill_vt(vt, v_ref):
    dh = NSA_HEAD_DIM
    vt[0:dh, :] = v_ref[...].astype(F32).T.astype(BF16)
    vt[dh:VROWS, :] = jnp.ones((VROWS - dh, vt.shape[1]), BF16)


def _out_t(acc, gate_row):
    dh = NSA_HEAD_DIM
    return acc[0:dh, :] * (gate_row / acc[dh:dh + 1, :])


def _cmp_attn_kernel(q_ref, kc_ref, vct_ref, bias_ref, cov_ref, gate_ref, oc_ref, sel_ref, score_ref, *, n_sel):
    tq = q_ref.shape[0]
    dh = NSA_HEAD_DIM
    n_slc = cov_ref.shape[0]
    t0 = pl.program_id(1) * tq
    gp = pl.program_id(0)
    heads = [(gl, hh) for gl in range(CMP_GROUPS) for hh in range(NSA_HPG)]
    qk = [_dot_nt(kc_ref[0, gl], _q_scaled(q_ref, gl * NSA_HPG + hh)) for gl, hh in heads]
    ps = []
    for n, (gl, hh) in enumerate(heads):
        logit = qk[n] + bias_ref[gl, hh]
        m = jnp.max(logit, axis=0, keepdims=True)
        e = jnp.exp2(logit - m)
        inv = jnp.where(m > 0.5 * NEG_LOGIT, 1.0 / jnp.sum(e, axis=0, keepdims=True), 0.0)
        ps.append(e * inv)
    ots = [_dot(vct_ref[0, gl], ps[n].astype(BF16)) for n, (gl, hh) in enumerate(heads)]
    for n, (gl, hh) in enumerate(heads):
        oc_ref[0, n] = (ots[n] * _gate_row(gate_ref, gp * CMP_GROUPS + gl, hh, 0)).astype(BF16)

    cov = cov_ref[...]
    jb = lax.broadcasted_iota(jnp.int32, (n_slc, tq), 0)
    cur = (t0 + lax.broadcasted_iota(jnp.int32, (n_slc, tq), 1)) // SLC_BLOCK
    valid = jb <= cur
    forced = (jb == 0) | (jb == cur) | (jb == cur - 1)
    sub = 8
    row_id = lax.broadcasted_iota(jnp.int32, (sub, tq), 0)
    for gl in range(CMP_GROUPS):
        g4 = ps[gl * NSA_HPG:(gl + 1) * NSA_HPG]
        p_sum = (g4[0] + g4[1]) + (g4[2] + g4[3])
        p_hi = p_sum.astype(BF16)
        r1 = p_sum - p_hi.astype(F32)
        p_mid = r1.astype(BF16)
        p_lo = (r1 - p_mid.astype(F32)).astype(BF16)
        st = _dot(cov, p_hi) + _dot(cov, p_mid) + _dot(cov, p_lo)
        score_ref[gl] = jnp.where(valid, st + jnp.where(forced, FORCE_BONUS, 0.0), -1.0)
        groups = [score_ref[gl, r:r + sub, :] for r in range(0, n_slc, sub)]
        ranks = [jnp.zeros((sub, tq), F32) for _ in groups]
        for j2 in range(n_slc):
            row = score_ref[gl, j2:j2 + 1, :]
            for gi, sc in enumerate(groups):
                r0 = gi * sub
                if r0 > j2:
                    inc = jnp.where(row >= sc, 1.0, 0.0)
                elif r0 + sub - 1 <= j2:
                    inc = jnp.where(row > sc, 1.0, 0.0)
                else:
                    inc = jnp.where(row_id > j2 - r0, jnp.where(row >= sc, 1.0, 0.0), jnp.where(row > sc, 1.0, 0.0))
                ranks[gi] = ranks[gi] + inc
        for gi, rk in enumerate(ranks):
            score_ref[gl, gi * sub:(gi + 1) * sub, :] = rk
        sel = valid & (score_ref[gl] < n_sel)
        sel_ref[0, gl, 0:n_slc, :] = jnp.where(sel, 0.0, -MASK_BIG).astype(BF16)
        if n_slc < LANES:
            sel_ref[0, gl, n_slc:LANES, :] = jnp.zeros((LANES - n_slc, tq), BF16)


def _cmp_attn(proj, gates, k_cmp, v_cmp_t, bias_c, cover_t, bsz, t):
    g_, dh, tq = NSA_KV_GROUPS, NSA_HEAD_DIM, CMP_TQ
    nt = t // tq
    nb = t // CMP_STRIDE
    n_slc = t // SLC_BLOCK
    ng = CMP_GROUPS
    gw = ng * NSA_HPG * dh
    return pl.pallas_call(
        functools.partial(_cmp_attn_kernel, n_sel=min(SLC_TOP_N, n_slc)),
        grid=(g_ // ng, nt, bsz),
        in_specs=[
            pl.BlockSpec((tq, gw), lambda g, i, b: (b * nt + i, COL_AQ // gw + g)),
            pl.BlockSpec((1, ng, nb, dh), lambda g, i, b: (b, g, 0, 0)),
            pl.BlockSpec((1, ng, dh, nb), lambda g, i, b: (b, g, 0, 0)),
            pl.BlockSpec((ng, NSA_HPG, nb, tq), lambda g, i, b: (g, 0, 0, i)),
            pl.BlockSpec((n_slc, nb), lambda g, i, b: (0, 0)),
            pl.BlockSpec((GATE_COLS, tq), lambda g, i, b: (0, b * nt + i)),
        ],
        out_specs=[
            pl.BlockSpec((1, ng * NSA_HPG, dh, tq), lambda g, i, b: (b, g, 0, i)),
            pl.BlockSpec((1, ng, LANES, tq), lambda g, i, b: (b, g, 0, i)),
        ],
        out_shape=[
            jax.ShapeDtypeStruct((bsz, NSA_HEADS, dh, t), BF16),
            jax.ShapeDtypeStruct((bsz, g_, LANES, t), BF16),
        ],
        scratch_shapes=[pltpu.VMEM((ng, n_slc, tq), F32)],
        compiler_params=_cparams(("parallel", "parallel", "parallel")),
        name="cmp_attn",
    )(proj, k_cmp, v_cmp_t, bias_c, cover_t, gates)


def _slc_attn_kernel(q_ref, sbt_ref, k_ref, v_ref, bias_ref, gate_ref, os_ref,
                     kaug, vt, qt, s_buf, p_buf, a_buf, m_ref, acc_ref, *, n_near):
    tq, tk, dh = SLC_TQ, SLC_TK, NSA_HEAD_DIM
    r = tq // tk
    t = k_ref.shape[0]
    g = pl.program_id(1)
    i = pl.program_id(2)
    heads = range(NSA_HPG)

    @pl.when(i == 0)
    def _():
        kaug[:, 0:dh] = k_ref[...]
        blk = lax.broadcasted_iota(jnp.int32, (t, LANES), 0) // SLC_BLOCK
        lane = lax.broadcasted_iota(jnp.int32, (t, LANES), 1)
        kaug[:, dh:dh + LANES] = jnp.where(blk == lane, 1.0, 0.0).astype(BF16)
        _fill_vt(vt, v_ref)

    sbt = sbt_ref[0, 0]
    for hh in heads:
        qt[hh, 0:dh, :] = _q_t(q_ref, hh)
        qt[hh, dh:dh + LANES, :] = sbt

    def ktile(j):
        return kaug[pl.ds(pl.multiple_of(j * tk, tk), tk), :]

    def vtile(j):
        return vt[:, pl.ds(pl.multiple_of(j * tk, tk), tk)]

    m_ref[...] = jnp.full(m_ref.shape, NEG_LOGIT, F32)
    acc_ref[...] = jnp.zeros_like(acc_ref)
    p_buf[...] = jnp.zeros_like(p_buf)
    a_buf[...] = jnp.ones_like(a_buf)
    k0 = ktile(0)
    for hh in heads:
        s_buf[hh] = _dot(k0, qt[hh])

    last = (i + 1) * r - 1

    def stage(j, bias_fn):
        v_prev = vtile(jnp.maximum(j - 1, 0))
        k_next = ktile(jnp.minimum(j + 1, last))
        pv = [_dot(v_prev, p_buf[hh]) for hh in heads]
        s_next = [_dot(k_next, qt[hh]) for hh in heads]
        for hh in heads:
            acc_ref[hh] = a_buf[hh] * acc_ref[hh] + pv[hh]
        for hh in heads:
            s = s_buf[hh]
            if bias_fn is not None:
                s = s + bias_fn(hh)
            m_prev = m_ref[hh]
            m_new = jnp.maximum(m_prev, jnp.max(s, axis=0, keepdims=True))
            a_buf[hh] = jnp.exp2(m_prev - m_new)
            p_buf[hh] = jnp.exp2(s - m_new).astype(BF16)
            m_ref[hh] = m_new
        for hh in heads:
            s_buf[hh] = s_next[hh]

    n_far = jnp.maximum(i * r + r - n_near, 0)

    def far_body(j, c):
        stage(j, None)
        return c

    lax.fori_loop(0, n_far, far_body, 0)

    def near_body(j, c):
        off = pl.multiple_of((i * r + (r - 1) - j) * tk, tk)
        stage(j, lambda hh: bias_ref[0, hh, :, pl.ds(off, tq)])
        return c

    lax.fori_loop(n_far, last + 1, near_body, 0)

    v_last = vtile(last)
    pv = [_dot(v_last, p_buf[hh]) for hh in heads]
    for hh in heads:
        acc = a_buf[hh] * acc_ref[hh] + pv[hh]
        os_ref[0, hh] = _out_t(acc, _gate_row(gate_ref, g, hh, 1)).astype(BF16)


def _slc_attn(proj, gates, selbias_t, bias_s, bsz, t):
    g_, dh, tq, tk = NSA_KV_GROUPS, NSA_HEAD_DIM, SLC_TQ, SLC_TK
    nt = t // tq
    gw = NSA_HPG * dh
    n_near = (bias_s.shape[3] - tq) // tk + 1
    return pl.pallas_call(
        functools.partial(_slc_attn_kernel, n_near=n_near),
        grid=(bsz, g_, nt),
        in_specs=[
            pl.BlockSpec((tq, gw), lambda b, g, i: (b * nt + i, COL_AQ // gw + g)),
            pl.BlockSpec((1, 1, LANES, tq), lambda b, g, i: (b, g, 0, i)),
            pl.BlockSpec((t, dh), lambda b, g, i: (b, COL_SK // dh + g)),
            pl.BlockSpec((t, dh), lambda b, g, i: (b, COL_SV // dh + g)),
            pl.BlockSpec((1, NSA_HPG, tk, bias_s.shape[3]), lambda b, g, i: (g, 0, 0, 0)),
            pl.BlockSpec((GATE_COLS, tq), lambda b, g, i: (0, b * nt + i)),
        ],
        out_specs=pl.BlockSpec((1, NSA_HPG, dh, tq), lambda b, g, i: (b, g, 0, i)),
        out_shape=jax.ShapeDtypeStruct((bsz, NSA_HEADS, dh, t), BF16),
        scratch_shapes=[
            pltpu.VMEM((t, dh + LANES), BF16),
            pltpu.VMEM((VROWS, t), BF16),
            pltpu.VMEM((NSA_HPG, dh + LANES, tq), BF16),
            pltpu.VMEM((NSA_HPG, tk, tq), F32),
            pltpu.VMEM((NSA_HPG, tk, tq), BF16),
            pltpu.VMEM((NSA_HPG, 1, tq), F32),
            pltpu.VMEM((NSA_HPG, 1, tq), F32),
            pltpu.VMEM((NSA_HPG, VROWS, tq), F32),
        ],
        compiler_params=_cparams(("parallel", "parallel", "arbitrary")),
        name="slc_attn",
    )(proj, selbias_t, proj, proj, bias_s, gates)


def _win_attn_kernel(q_ref, k_ref, v_ref, bias_ref, gate_ref, oc_ref, os_ref, z_ref, ya_ref, vt, *, n_tiles):
    tq, tk, dh = WIN_TQ, ATT_TK, NSA_HEAD_DIM
    r = tq // tk
    gp = pl.program_id(1)
    i = pl.program_id(2)
    heads = [(gl, hh) for gl in range(WIN_GROUPS) for hh in range(NSA_HPG)]

    @pl.when(i == 0)
    def _():
        for gl in range(WIN_GROUPS):
            vt[gl, 0:dh, :] = v_ref[:, gl * dh:(gl + 1) * dh].astype(F32).T.astype(BF16)
            vt[gl, dh:VROWS, :] = jnp.ones((VROWS - dh, vt.shape[2]), BF16)

    js = [i * r + (r - 1) - d for d in range(n_tiles)]
    offs = [pl.multiple_of(jnp.maximum(j, 0) * tk, tk) for j in js]
    boffs = [pl.multiple_of(jnp.where(js[d] >= 0, d, n_tiles) * tk, tk) for d in range(n_tiles)]
    k_tiles = [[k_ref[pl.ds(off, tk), gl * dh:(gl + 1) * dh] for off in offs] for gl in range(WIN_GROUPS)]
    v_tiles = [[vt[gl, :, pl.ds(off, tk)] for off in offs] for gl in range(WIN_GROUPS)]
    qs = [_q_scaled(q_ref, gl * NSA_HPG + hh) for gl, hh in heads]
    ss = [[_dot_nt(k_tiles[gl][d], qs[n]) for d in range(n_tiles)] for n, (gl, hh) in enumerate(heads)]
    ps = []
    for n, (gl, hh) in enumerate(heads):
        s = [ss[n][d] + bias_ref[gl, hh, :, pl.ds(boffs[d], tq)] for d in range(n_tiles)]
        m = functools.reduce(jnp.maximum, [jnp.max(x, axis=0, keepdims=True) for x in s])
        ps.append([jnp.exp2(x - m).astype(BF16) for x in s])
    pvs = [[_dot(v_tiles[gl][d], ps[n][d]) for d in range(n_tiles)] for n, (gl, hh) in enumerate(heads)]
    for n, (gl, hh) in enumerate(heads):
        cols = slice(n * dh, (n + 1) * dh)
        o = _out_t(functools.reduce(lambda x, y: x + y, pvs[n]), _gate_row(gate_ref, gp * WIN_GROUPS + gl, hh, 2))
        o = o + oc_ref[0, n].astype(F32) + os_ref[0, n].astype(F32)
        ya_ref[:, cols] = (o.T * _silu(z_ref[:, cols].astype(F32))).astype(BF16)


def _win_attn(proj, gates, o_cmp, o_slc, bias_w, bsz, t):
    g_, dh, tq, tk = NSA_KV_GROUPS, NSA_HEAD_DIM, WIN_TQ, ATT_TK
    nt = t // tq
    ng = WIN_GROUPS
    gw = ng * NSA_HPG * dh
    kw = ng * dh
    n_tiles = (bias_w.shape[3] - tq) // tk
    return pl.pallas_call(
        functools.partial(_win_attn_kernel, n_tiles=n_tiles),
        grid=(bsz, g_ // ng, nt),
        in_specs=[
            pl.BlockSpec((tq, gw), lambda b, g, i: (b * nt + i, COL_AQ // gw + g)),
            pl.BlockSpec((t, kw), lambda b, g, i: (b, COL_WK // kw + g)),
            pl.BlockSpec((t, kw), lambda b, g, i: (b, COL_WV // kw + g)),
            pl.BlockSpec((ng, NSA_HPG, tk, bias_w.shape[3]), lambda b, g, i: (g, 0, 0, 0)),
            pl.BlockSpec((GATE_COLS, tq), lambda b, g, i: (0, b * nt + i)),
            pl.BlockSpec((1, ng * NSA_HPG, dh, tq), lambda b, g, i: (b, g, 0, i)),
            pl.BlockSpec((1, ng * NSA_HPG, dh, tq), lambda b, g, i: (b, g, 0, i)),
            pl.BlockSpec((tq, gw), lambda b, g, i: (b * nt + i, COL_AZ // gw + g)),
        ],
        out_specs=pl.BlockSpec((tq, gw), lambda b, g, i: (b * nt + i, g)),
        out_shape=jax.ShapeDtypeStruct((bsz * t, NSA_WIDTH), BF16),
        scratch_shapes=[pltpu.VMEM((ng, VROWS, t), BF16)],
        compiler_params=_cparams(("parallel", "parallel", "arbitrary")),
        name="win_attn",
    )(proj, proj, proj, bias_w, gates, o_cmp, o_slc, proj)


def _outproj_kernel(ym_ref, ya_ref, x_hbm, w_hbm, gain_ref, o_ref, mix, wbuf, xbuf, w_sem, x_sem):
    i = pl.program_id(0)
    ni = pl.num_programs(0)
    tm, tn = xbuf.shape[1], xbuf.shape[2]
    nj = o_ref.shape[1] // tn

    def w_copy(j, slot):
        return pltpu.make_async_copy(w_hbm.at[:, pl.ds(pl.multiple_of(j * tn, tn), tn)], wbuf.at[slot], w_sem.at[slot])

    def x_copy(ii, j, slot):
        src = x_hbm.at[pl.ds(pl.multiple_of(ii * tm, tm), tm), pl.ds(pl.multiple_of(j * tn, tn), tn)]
        return pltpu.make_async_copy(src, xbuf.at[slot], x_sem.at[slot])

    @pl.when(i == 0)
    def _():
        w_copy(0, 0).start()
        x_copy(0, 0, 0).start()

    mix[:, 0:MLSTM_WIDTH] = ym_ref[...]
    mix[:, MLSTM_WIDTH:D_MIX] = ya_ref[...]

    def body(j, carry):
        slot = j % 2
        w_copy(j, slot).wait()
        x_copy(i, j, slot).wait()
        wrap = j + 1 == nj

        @pl.when((j + 1 < nj) | (i + 1 < ni))
        def _():
            w_copy((j + 1) % nj, 1 - slot).start()
            x_copy(i + jnp.where(wrap, 1, 0), (j + 1) % nj, 1 - slot).start()

        o_ref[:, pl.ds(pl.multiple_of(j * tn, tn), tn)] = xbuf[slot] + _dot(mix[...], wbuf[slot])
        return carry

    lax.fori_loop(0, nj, body, 0)

    ss = None
    for jj in range(nj):
        y = o_ref[:, jj * tn:(jj + 1) * tn]
        part = jnp.sum(y * y, axis=-1, keepdims=True)
        ss = part if ss is None else ss + part
    inv = lax.rsqrt(ss / o_ref.shape[1] + RMS_EPS)
    for jj in range(nj):
        cols = slice(jj * tn, (jj + 1) * tn)
        o_ref[:, cols] = o_ref[:, cols] * inv * gain_ref[:, cols]


def _out_proj(y_m, y_a, x2d, w_out, gain):
    n, d = x2d.shape
    tm, tn = min(OUTPROJ_TM, n), OUTPROJ_TN
    assert (d // tn) % 2 == 0
    any_space = pl.BlockSpec(memory_space=pl.ANY)
    return pl.pallas_call(
        _outproj_kernel,
        grid=(n // tm,),
        in_specs=[
            pl.BlockSpec((tm, MLSTM_WIDTH), lambda i: (i, 0)),
            pl.BlockSpec((tm, NSA_WIDTH), lambda i: (i, 0)),
            any_space,
            any_space,
            pl.BlockSpec((1, d), lambda i: (0, 0)),
        ],
        out_specs=pl.BlockSpec((tm, d), lambda i: (i, 0)),
        out_shape=jax.ShapeDtypeStruct((n, d), F32),
        scratch_shapes=[
            pltpu.VMEM((tm, D_MIX), BF16),
            pltpu.VMEM((2, D_MIX, tn), BF16),
            pltpu.VMEM((2, tm, tn), F32),
            pltpu.SemaphoreType.DMA((2,)),
            pltpu.SemaphoreType.DMA((2,)),
        ],
        compiler_params=_cparams(("arbitrary",)),
        name="out_proj",
    )(y_m, y_a, x2d, w_out, gain)


def _rel_bucket(dist):
    n = jnp.maximum(dist, 0)
    nf = jnp.maximum(n, REL_MAX_EXACT).astype(jnp.float32)
    large = REL_MAX_EXACT + (jnp.log(nf / REL_MAX_EXACT) / math.log(REL_MAX_DISTANCE / REL_MAX_EXACT)
                             * (REL_BUCKETS - REL_MAX_EXACT)).astype(jnp.int32)
    large = jnp.minimum(large, REL_BUCKETS - 1)
    return jnp.where(n < REL_MAX_EXACT, n, large)


def _toeplitz_vec(by_dist, base, n_pos, n_neg, lo, hi, shift=None):
    w = n_pos + n_neg
    c = np.arange(w)
    dist = np.where(c < n_pos, base + c, base - (w - c))
    ok = (dist >= lo) & (dist < hi)
    dmax = by_dist.shape[1]

    def run(start, length):
        left, right = max(0, -start), max(0, start + length - dmax)
        ext = jnp.pad(by_dist, ((0, 0), (left, right)), mode="edge")
        return lax.slice_in_dim(ext, start + left, start + left + length, axis=1)

    vals = jnp.concatenate([run(base, n_pos), run(base - n_neg, n_neg)], axis=1)
    if shift is not None:
        vals = vals - shift
    return jnp.where(ok[None], vals * LOG2E, NEG_LOGIT)


def _toeplitz_t(w_row, n_keys, n_q, key_step):
    x = jnp.broadcast_to(w_row, (n_keys, w_row.shape[1]))
    return pltpu.roll(x, 0, 1, stride=key_step, stride_axis=0)[:, 0:n_q]


def _bias_tables_kernel(wc_ref, ws_ref, ww_ref, bc_ref, bs_ref, bw_ref):
    nb, t = bc_ref.shape[2], bc_ref.shape[3]
    bc_ref[0, 0] = _toeplitz_t(wc_ref[0], nb, t, CMP_STRIDE)
    bs_ref[0, 0] = _toeplitz_t(ws_ref[0], bs_ref.shape[2], bs_ref.shape[3], 1)
    bw_ref[0, 0] = _toeplitz_t(ww_ref[0], bw_ref.shape[2], bw_ref.shape[3], 1)


def _bias_tables(rel_bias, t):
    tq, tk = ATT_TQ, ATT_TK
    g_, hpg = NSA_KV_GROUPS, NSA_HPG
    rb = rel_bias.astype(F32)
    dmax = REL_MAX_DISTANCE + 2 * max(SLC_TQ, SLC_TK, WIN_TQ, tq, tk)
    onehot = (_rel_bucket(jnp.arange(dmax, dtype=jnp.int32))[None, :] == jnp.arange(REL_BUCKETS, dtype=jnp.int32)[:, None])
    by_dist = jnp.dot(rb.T, onehot.astype(F32), precision=lax.Precision.HIGHEST)
    far = rb[REL_BUCKETS - 1][:, None]
    big = 1 << 30
    nb = t // CMP_STRIDE
    wc = _toeplitz_vec(by_dist, -(CMP_BLOCK - 1), t, t, 0, big)[:, None, :]
    sq, sk = min(SLC_TQ, t), SLC_TK
    r = sq // sk
    n_s = min(-(-(REL_MAX_DISTANCE + sk - 1) // sk) + r - 1, t // sk)
    wm = (n_s - 1) * sk + sq
    ws = _toeplitz_vec(by_dist, -(r - 1) * sk, wm, sk, 0, big, far)[:, None, :]
    wq = min(WIN_TQ, t)
    rw = wq // tk
    n_w = rw + (WINDOW - 1 + tk - 1) // tk
    wmw = n_w * tk + wq
    ww = _toeplitz_vec(by_dist, -(rw - 1) * tk, wmw, tk, 0, WINDOW)[:, None, :]
    return pl.pallas_call(
        _bias_tables_kernel,
        grid=(NSA_HEADS,),
        in_specs=[
            pl.BlockSpec((1, 1, 2 * t), lambda h: (h, 0, 0)),
            pl.BlockSpec((1, 1, wm + sk), lambda h: (h, 0, 0)),
            pl.BlockSpec((1, 1, wmw + tk), lambda h: (h, 0, 0)),
        ],
        out_specs=[
            pl.BlockSpec((1, 1, nb, t), lambda h: (h // hpg, h % hpg, 0, 0)),
            pl.BlockSpec((1, 1, sk, wm), lambda h: (h // hpg, h % hpg, 0, 0)),
            pl.BlockSpec((1, 1, tk, wmw), lambda h: (h // hpg, h % hpg, 0, 0)),
        ],
        out_shape=[
            jax.ShapeDtypeStruct((g_, hpg, nb, t), F32),
            jax.ShapeDtypeStruct((g_, hpg, sk, wm), F32),
            jax.ShapeDtypeStruct((g_, hpg, tk, wmw), F32),
        ],
        compiler_params=_cparams(("parallel",)),
        name="bias_tables",
    )(wc, ws, ww)


def _cover_t(t):
    nb = t // CMP_STRIDE
    n_cmp = (t - CMP_BLOCK) // CMP_STRIDE + 1
    n_slc = t // SLC_BLOCK
    cs = np.arange(nb) * CMP_STRIDE
    ss = np.arange(n_slc) * SLC_BLOCK
    cover = np.clip(np.minimum(cs[:, None] + CMP_BLOCK, ss[None, :] + SLC_BLOCK)
                    - np.maximum(cs[:, None], ss[None, :]), 0, None) / CMP_BLOCK
    cover[n_cmp:] = 0.0
    return jnp.asarray(cover.T, dtype=BF16)


def kernel(x, norm_gain, w_in, w_conv, b_igate, b_fgate, mlstm_norm_gain, cmp_k_pos, cmp_k_w1, cmp_k_w2,
           cmp_v_pos, cmp_v_w1, cmp_v_w2, rel_bias, w_out, final_norm_gain):
    bsz, t, d = x.shape
    assert d == D_MODEL and t % MLSTM_L == 0 and t % ATT_TQ == 0 and (t // CMP_STRIDE) % LANES == 0
    n = bsz * t
    x2d = x.reshape(n, d)

    w_main, w_gate = _w_prep(w_in.T)
    proj, gates = _in_proj(x2d, norm_gain.reshape(1, d).astype(F32), w_main, w_gate)

    g_rows = gates[:, :2 * MLSTM_HEADS].reshape(bsz, t, 2, MLSTM_HEADS).transpose(0, 3, 2, 1)
    y_m = _mlstm(proj, g_rows, w_conv.astype(F32), b_igate.astype(F32), b_fgate.astype(F32),
                 mlstm_norm_gain.reshape(1, MLSTM_WIDTH).astype(F32), bsz, t)

    dh = NSA_HEAD_DIM
    half = CMP_BLOCK // 2

    def w1cat(w1):
        return jnp.concatenate([w1[:half].reshape(half * dh, dh), w1[half:].reshape(half * dh, dh)], axis=1).astype(BF16)

    k_cmp, v_cmp_t = _compress(
        proj, w1cat(cmp_k_w1), cmp_k_w2.astype(BF16), cmp_k_pos.reshape(2, half * dh).astype(BF16),
        w1cat(cmp_v_w1), cmp_v_w2.astype(BF16), cmp_v_pos.reshape(2, half * dh).astype(BF16), bsz, t)

    bias_c, bias_s, bias_w = _bias_tables(rel_bias, t)
    gates_t = gates.T
    o_cmp, selbias_t = _cmp_attn(proj, gates_t, k_cmp, v_cmp_t, bias_c, _cover_t(t), bsz, t)
    o_slc = _slc_attn(proj, gates_t, selbias_t, bias_s, bsz, t)
    y_a = _win_attn(proj, gates_t, o_cmp, o_slc, bias_w, bsz, t)

    out = _out_proj(y_m, y_a, x2d, w_out.astype(BF16), final_norm_gain.reshape(1, d).astype(F32))
    return out.reshape(bsz, t, d)
```

```python
import functools
import math

import jax
import jax.numpy as jnp
import numpy as np
from jax import lax
from jax.experimental import pallas as pl
from jax.experimental.pallas import tpu as pltpu

F32 = jnp.float32
BF16 = jnp.bfloat16

D_MODEL = 4096
D_MIX = D_MODEL
MLSTM_WIDTH = D_MIX // 2
MLSTM_HEADS = 4
MLSTM_V_DIM = MLSTM_WIDTH // MLSTM_HEADS
MLSTM_QK_DIM = MLSTM_V_DIM // 2
MLSTM_QK_WIDTH = MLSTM_HEADS * MLSTM_QK_DIM
CONV_WIDTH = 4
NSA_WIDTH = D_MIX - MLSTM_WIDTH
NSA_HEAD_DIM = 128
NSA_HEADS = NSA_WIDTH // NSA_HEAD_DIM
NSA_KV_GROUPS = 4
NSA_HPG = NSA_HEADS // NSA_KV_GROUPS
NSA_KV_WIDTH = NSA_KV_GROUPS * NSA_HEAD_DIM
CMP_BLOCK = 32
CMP_STRIDE = 16
SLC_BLOCK = 64
SLC_TOP_N = 16
WINDOW = 512
FORCE_BONUS = 1000.0
REL_BUCKETS = 32
REL_MAX_EXACT = REL_BUCKETS // 2
REL_MAX_DISTANCE = 1024
RMS_EPS = 1e-6
NEG_LOGIT = -1e30

LANES = 128
VMEM_LIMIT_BYTES = 56 * 1024 * 1024

COL_MQ = 0
COL_MK = COL_MQ + MLSTM_QK_WIDTH
COL_MV = COL_MK + MLSTM_QK_WIDTH
COL_MO = COL_MV + MLSTM_WIDTH
COL_MZ = COL_MO + MLSTM_WIDTH
COL_AQ = COL_MZ + MLSTM_WIDTH
COL_CK = COL_AQ + NSA_WIDTH
COL_CV = COL_CK + NSA_KV_WIDTH
COL_SK = COL_CV + NSA_KV_WIDTH
COL_SV = COL_SK + NSA_KV_WIDTH
COL_WK = COL_SV + NSA_KV_WIDTH
COL_WV = COL_WK + NSA_KV_WIDTH
COL_AZ = COL_WV + NSA_KV_WIDTH
MAIN_COLS = COL_AZ + NSA_WIDTH
GATE_COLS = LANES
GATE_I = 0
GATE_F = MLSTM_HEADS
GATE_NSA = 2 * MLSTM_HEADS

INPROJ_TM = 512
INPROJ_TN = 1536
OUTPROJ_TM = 512
OUTPROJ_TN = 1024
MLSTM_L = 256
ATT_TQ = 256
ATT_TK = 256
WPREP_TR = 512
SLC_TQ = 512
SLC_TK = 512
CMP_TQ = 512
CMP_GROUPS = 2
WIN_TQ = 256
WIN_GROUPS = 2
MASK_BIG = 2.0 ** 100


def _cparams(sem):
    return pltpu.CompilerParams(dimension_semantics=sem, vmem_limit_bytes=VMEM_LIMIT_BYTES)


def _sigmoid(x):
    return 0.5 * jnp.tanh(0.5 * x) + 0.5


def _silu(x):
    return x * _sigmoid(x)


def _dot(a, b):
    return jnp.dot(a, b, preferred_element_type=F32)


def _dot_nt(a, b):
    return lax.dot_general(a, b, (((1,), (1,)), ((), ())), preferred_element_type=F32)


W_OFF_I = COL_MZ + MLSTM_WIDTH
W_OFF_AQ = W_OFF_I + 2 * MLSTM_HEADS
W_OFF_GATE = W_OFF_AQ + NSA_WIDTH + 6 * NSA_KV_WIDTH
W_OFF_AZ = W_OFF_GATE + 3 * NSA_HEADS


def _wprep_kernel(w_ref, gi_ref, ga_ref, o_ref, g_ref):
    o_ref[...] = w_ref[...].astype(BF16)

    @pl.when(pl.program_id(0) == 0)
    def _():
        pad = jnp.zeros((GATE_COLS - gi_ref.shape[0] - ga_ref.shape[0], g_ref.shape[1]), F32)
        g_ref[...] = jnp.concatenate([gi_ref[...], ga_ref[...], pad], axis=0).astype(BF16)


def _w_prep(w_t):
    rows, d = w_t.shape
    tr = WPREP_TR
    nb_a, nb_b = W_OFF_I // tr, (W_OFF_GATE - W_OFF_AQ) // tr

    def src(m):
        skip = jnp.where(m >= nb_a, W_OFF_AQ - W_OFF_I, 0) + jnp.where(m >= nb_a + nb_b, W_OFF_AZ - W_OFF_GATE, 0)
        return pl.multiple_of(m * tr + skip, 8)

    rows_at = lambda start, size: pl.BlockSpec((pl.Element(size), pl.Element(d)), lambda m: (start, 0))
    return pl.pallas_call(
        _wprep_kernel,
        grid=(MAIN_COLS // tr,),
        in_specs=[
            pl.BlockSpec((pl.Element(tr), pl.Element(d)), lambda m: (src(m), 0)),
            rows_at(W_OFF_I, W_OFF_AQ - W_OFF_I),
            rows_at(W_OFF_GATE, W_OFF_AZ - W_OFF_GATE),
        ],
        out_specs=[pl.BlockSpec((tr, d), lambda m: (m, 0)), pl.BlockSpec((GATE_COLS, d), lambda m: (0, 0))],
        out_shape=[jax.ShapeDtypeStruct((MAIN_COLS, d), BF16), jax.ShapeDtypeStruct((GATE_COLS, d), BF16)],
        compiler_params=_cparams(("arbitrary",)),
        name="w_prep",
    )(w_t, w_t, w_t)


def _inproj_kernel(x_ref, gain_ref, w_hbm, wg_ref, o_hbm, og_ref, h_ref, wbuf, obuf, w_sem, o_sem):
    i = pl.program_id(0)
    ni = pl.num_programs(0)
    tm, tn = obuf.shape[1], obuf.shape[2]
    nj = w_hbm.shape[0] // tn

    def w_copy(j, slot):
        return pltpu.make_async_copy(w_hbm.at[pl.ds(pl.multiple_of(j * tn, tn), tn), :], wbuf.at[slot], w_sem.at[slot])

    def o_copy(j, slot):
        dst = o_hbm.at[pl.ds(pl.multiple_of(i * tm, tm), tm), pl.ds(pl.multiple_of(j * tn, tn), tn)]
        return pltpu.make_async_copy(obuf.at[slot], dst, o_sem.at[slot])

    @pl.when(i == 0)
    def _():
        w_copy(0, 0).start()

    x = x_ref[...]
    ms = jnp.mean(x * x, axis=-1, keepdims=True)
    h = (x * lax.rsqrt(ms + RMS_EPS) * gain_ref[...]).astype(BF16)
    h_ref[...] = h
    og_ref[...] = _dot_nt(h, wg_ref[...])

    def body(j, carry):
        slot = j % 2
        nxt = (j + 1) % nj
        w_copy(j, slot).wait()

        @pl.when((j + 1 < nj) | (i + 1 < ni))
        def _():
            w_copy(nxt, 1 - slot).start()

        @pl.when((j >= 2) | (i > 0))
        def _():
            o_copy(j, slot).wait()

        obuf[slot] = _dot_nt(h_ref[...], wbuf[slot]).astype(BF16)
        o_copy(j, slot).start()
        return carry

    lax.fori_loop(0, nj, body, 0)

    @pl.when(i == ni - 1)
    def _():
        o_copy(nj - 2, (nj - 2) % 2).wait()
        o_copy(nj - 1, (nj - 1) % 2).wait()


def _in_proj(x2d, gain, w_main, w_gate):
    n, d = x2d.shape
    tm, tn = min(INPROJ_TM, n), INPROJ_TN
    assert (MAIN_COLS // tn) % 2 == 0
    return pl.pallas_call(
        _inproj_kernel,
        grid=(n // tm,),
        in_specs=[
            pl.BlockSpec((tm, d), lambda i: (i, 0)),
            pl.BlockSpec((1, d), lambda i: (0, 0)),
            pl.BlockSpec(memory_space=pl.ANY),
            pl.BlockSpec((GATE_COLS, d), lambda i: (0, 0)),
        ],
        out_specs=[
            pl.BlockSpec(memory_space=pl.ANY),
            pl.BlockSpec((tm, GATE_COLS), lambda i: (i, 0)),
        ],
        out_shape=[
            jax.ShapeDtypeStruct((n, MAIN_COLS), BF16),
            jax.ShapeDtypeStruct((n, GATE_COLS), F32),
        ],
        scratch_shapes=[
            pltpu.VMEM((tm, d), BF16),
            pltpu.VMEM((2, tn, d), BF16),
            pltpu.VMEM((2, tm, tn), BF16),
            pltpu.SemaphoreType.DMA((2,)),
            pltpu.SemaphoreType.DMA((2,)),
        ],
        compiler_params=_cparams(("arbitrary",)),
        name="in_proj",
    )(x2d, gain, w_main, w_gate)


def _mlstm_kernel(bi_ref, bf_ref, q_ref, k_ref, v_ref, o_ref, z_ref, g_ref, wq_ref, wk_ref, ng_ref,
                  y_ref, qext, kext, c_st, n_st, m_st):
    L = MLSTM_L
    HIST = 8
    dqk, dv = MLSTM_QK_DIM, MLSTM_V_DIM

    @pl.when(pl.program_id(1) == 0)
    def _():
        qext[:, 0:HIST, :] = jnp.zeros((MLSTM_HEADS, HIST, dqk), F32)
        kext[:, 0:HIST, :] = jnp.zeros((MLSTM_HEADS, HIST, dqk), F32)
        c_st[...] = jnp.zeros_like(c_st)
        n_st[...] = jnp.zeros_like(n_st)
        m_st[...] = jnp.zeros_like(m_st)

    rr = lax.broadcasted_iota(jnp.int32, (L, L), 0)
    cc = lax.broadcasted_iota(jnp.int32, (L, L), 1)
    upper = (rr <= cc).astype(F32)

    for hd in range(MLSTM_HEADS):
        qcols = slice(hd * dqk, (hd + 1) * dqk)
        vcols = slice(hd * dv, (hd + 1) * dv)
        qext[hd, HIST:HIST + L, :] = q_ref[:, qcols].astype(F32)
        kext[hd, HIST:HIST + L, :] = k_ref[:, qcols].astype(F32)

        def conv_silu(ext, w_ref):
            w = w_ref[:, qcols]
            y = ext[hd, pl.ds(HIST, L), :] * w[CONV_WIDTH - 1:CONV_WIDTH, :]
            for s in range(1, CONV_WIDTH):
                y = y + ext[hd, pl.ds(HIST - s, L), :] * w[CONV_WIDTH - 1 - s:CONV_WIDTH - s, :]
            return _silu(y)

        qc = conv_silu(qext, wq_ref)
        kc = conv_silu(kext, wk_ref) * (dqk ** -0.5)
        qext[hd, 0:HIST, :] = qext[hd, L:L + HIST, :]
        kext[hd, 0:HIST, :] = kext[hd, L:L + HIST, :]

        g = g_ref[0, hd]
        i_row = g[0:1, :] + bi_ref[hd]
        f_row = g[1:2, :] + bf_ref[hd]
        lf_row = jnp.minimum(f_row, 0.0) - jnp.log(1.0 + jnp.exp(-jnp.abs(f_row)))

        bcum_row = jnp.dot(jnp.broadcast_to(lf_row, (8, L)), upper, preferred_element_type=F32,
                           precision=lax.Precision.HIGHEST)[0:1, :]
        bcum_col = jnp.sum(jnp.where(rr == cc, bcum_row, 0.0), axis=1, keepdims=True)
        gsum = bcum_row[:, L - 1:L]
        m_prev = m_st[hd]

        dlog = jnp.where(rr >= cc, bcum_col - bcum_row + i_row, -jnp.inf)
        m_inter = bcum_col + m_prev
        m_t = jnp.maximum(m_inter, jnp.max(dlog, axis=1, keepdims=True))
        dmat = jnp.exp(dlog - m_t)
        inter = jnp.exp(m_inter - m_t)

        qb = qc.astype(BF16)
        kct = kc.T
        vb = v_ref[:, vcols]
        s = _dot(qb, kct.astype(BF16)) * dmat
        c_prev = c_st[hd]
        n_prev = n_st[hd]
        num = _dot(s.astype(BF16), vb) + inter * _dot(qb, c_prev.astype(BF16))
        qn = jnp.sum(s, axis=1, keepdims=True) + inter * jnp.sum(qc * n_prev, axis=1, keepdims=True)
        hh = num / jnp.maximum(jnp.abs(qn), jnp.exp(-m_t))

        wlog = gsum - bcum_row + i_row
        m_next = jnp.maximum(gsum + m_prev, jnp.max(wlog, axis=1, keepdims=True))
        wts = jnp.exp(wlog - m_next)
        keep = jnp.exp(gsum + m_prev - m_next)
        c_st[hd] = keep * c_prev + _dot((kct * wts).astype(BF16), vb)
        n_st[hd] = keep * n_prev + _dot(jnp.broadcast_to(wts, (8, L)).astype(BF16), kc.astype(BF16))[0:1, :]
        m_st[hd] = m_next

        hm = _sigmoid(o_ref[:, vcols].astype(F32)) * hh
        hm = hm * lax.rsqrt(jnp.mean(hm * hm, axis=-1, keepdims=True) + RMS_EPS)
        hm = hm * ng_ref[:, vcols]
        y_ref[:, vcols] = (hm * _silu(z_ref[:, vcols].astype(F32))).astype(BF16)


def _mlstm(proj, g_rows, w_conv, b_igate, b_fgate, norm_gain, bsz, t):
    L = MLSTM_L
    nc = t // L
    nh, dqk, dv = MLSTM_HEADS, MLSTM_QK_DIM, MLSTM_V_DIM
    row = lambda b, c: b * nc + c
    smem = pl.BlockSpec(memory_space=pltpu.SMEM)
    return pl.pallas_call(
        _mlstm_kernel,
        grid=(bsz, nc),
        in_specs=[
            smem, smem,
            pl.BlockSpec((L, MLSTM_QK_WIDTH), lambda b, c: (row(b, c), COL_MQ // MLSTM_QK_WIDTH)),
            pl.BlockSpec((L, MLSTM_QK_WIDTH), lambda b, c: (row(b, c), COL_MK // MLSTM_QK_WIDTH)),
            pl.BlockSpec((L, MLSTM_WIDTH), lambda b, c: (row(b, c), COL_MV // MLSTM_WIDTH)),
            pl.BlockSpec((L, MLSTM_WIDTH), lambda b, c: (row(b, c), COL_MO // MLSTM_WIDTH)),
            pl.BlockSpec((L, MLSTM_WIDTH), lambda b, c: (row(b, c), COL_MZ // MLSTM_WIDTH)),
            pl.BlockSpec((1, nh, 2, L), lambda b, c: (b, 0, 0, c)),
            pl.BlockSpec((CONV_WIDTH, MLSTM_QK_WIDTH), lambda b, c: (0, 0)),
            pl.BlockSpec((CONV_WIDTH, MLSTM_QK_WIDTH), lambda b, c: (0, 1)),
            pl.BlockSpec((1, MLSTM_WIDTH), lambda b, c: (0, 0)),
        ],
        out_specs=pl.BlockSpec((L, MLSTM_WIDTH), lambda b, c: (row(b, c), 0)),
        out_shape=jax.ShapeDtypeStruct((bsz * t, MLSTM_WIDTH), BF16),
        scratch_shapes=[
            pltpu.VMEM((nh, L + 8, dqk), F32),
            pltpu.VMEM((nh, L + 8, dqk), F32),
            pltpu.VMEM((nh, dqk, dv), F32),
            pltpu.VMEM((nh, 1, dqk), F32),
            pltpu.VMEM((nh, 1, 1), F32),
        ],
        compiler_params=_cparams(("parallel", "arbitrary")),
        name="mlstm",
    )(b_igate, b_fgate, proj, proj, proj, proj, proj, g_rows, w_conv, w_conv, norm_gain)


def _compress_kernel(ck_ref, cv_ref, w1k_ref, w2k_ref, pk_ref, w1v_ref, w2v_ref, pv_ref,
                     ok_ref, ov_ref, xf, xcat):
    t = ck_ref.shape[0]
    nb = t // CMP_STRIDE
    dh = NSA_HEAD_DIM

    def one(src_ref, w1_ref, w2_ref, pos_ref):
        xf[...] = src_ref[...].astype(F32)
        for l in range(CMP_STRIDE):
            xcat[:, l * dh:(l + 1) * dh] = xf[pl.ds(l, nb, stride=CMP_STRIDE), :].astype(BF16)
        w1 = w1_ref[...]
        ab = _dot(xcat[...], w1)
        pp = _dot(pos_ref[...], w1)
        pos_term = pp[0:1, 0:dh] + pp[1:2, dh:2 * dh]
        second = pltpu.roll(ab[:, dh:2 * dh], nb - 1, 0)
        hid = _silu(ab[:, 0:dh] + second + pos_term)
        return _dot(hid.astype(BF16), w2_ref[...])

    ok_ref[0, 0] = one(ck_ref, w1k_ref, w2k_ref, pk_ref).astype(BF16)
    ov_ref[0, 0] = one(cv_ref, w1v_ref, w2v_ref, pv_ref).T.astype(BF16)


def _compress(proj, w1k, w2k, pk, w1v, w2v, pv, bsz, t):
    g_, dh = NSA_KV_GROUPS, NSA_HEAD_DIM
    nb = t // CMP_STRIDE
    full = lambda a: pl.BlockSpec(a.shape, lambda b, g: (0,) * a.ndim)
    k_spec = pl.BlockSpec((1, 1, nb, dh), lambda b, g: (b, g, 0, 0))
    vt_spec = pl.BlockSpec((1, 1, dh, nb), lambda b, g: (b, g, 0, 0))
    return pl.pallas_call(
        _compress_kernel,
        grid=(bsz, g_),
        in_specs=[
            pl.BlockSpec((t, dh), lambda b, g: (b, COL_CK // dh + g)),
            pl.BlockSpec((t, dh), lambda b, g: (b, COL_CV // dh + g)),
            full(w1k), full(w2k), full(pk), full(w1v), full(w2v), full(pv),
        ],
        out_specs=[k_spec, vt_spec],
        out_shape=[jax.ShapeDtypeStruct((bsz, g_, nb, dh), BF16), jax.ShapeDtypeStruct((bsz, g_, dh, nb), BF16)],
        scratch_shapes=[pltpu.VMEM((t, dh), F32), pltpu.VMEM((nb, CMP_STRIDE * dh), BF16)],
        compiler_params=_cparams(("parallel", "parallel")),
        name="compress",
    )(proj, proj, w1k, w2k, pk, w1v, w2v, pv)


LOG2E = math.log2(math.e)
VROWS = NSA_HEAD_DIM + 16


def _q_scaled(q_ref, hh):
    dh = NSA_HEAD_DIM
    return (q_ref[:, hh * dh:(hh + 1) * dh].astype(F32) * (dh ** -0.5 * LOG2E)).astype(BF16)


def _q_t(q_ref, hh):
    dh = NSA_HEAD_DIM
    return (q_ref[:, hh * dh:(hh + 1) * dh].astype(F32) * (dh ** -0.5 * LOG2E)).T.astype(BF16)


def _gate_row(gt_ref, g, hh, branch):
    row = GATE_NSA + 3 * (g * NSA_HPG + hh) + branch
    return _sigmoid(gt_ref[pl.ds(row, 1), :])


def _fill_vt(vt, v_ref):
    dh = NSA_HEAD_DIM
    vt[0:dh, :] = v_ref[...].astype(F32).T.astype(BF16)
    vt[dh:VROWS, :] = jnp.ones((VROWS - dh, vt.shape[1]), BF16)


def _out_t(acc, gate_row):
    dh = NSA_HEAD_DIM
    return acc[0:dh, :] * (gate_row / acc[dh:dh + 1, :])


def _cmp_attn_kernel(q_ref, kc_ref, vct_ref, bias_ref, cov_ref, gate_ref, oc_ref, sel_ref, score_ref, *, n_sel):
    tq = q_ref.shape[0]
    dh = NSA_HEAD_DIM
    n_slc = cov_ref.shape[0]
    t0 = pl.program_id(1) * tq
    gp = pl.program_id(0)
    heads = [(gl, hh) for gl in range(CMP_GROUPS) for hh in range(NSA_HPG)]
    qk = [_dot_nt(kc_ref[0, gl], _q_scaled(q_ref, gl * NSA_HPG + hh)) for gl, hh in heads]
    ps = []
    for n, (gl, hh) in enumerate(heads):
        logit = qk[n] + bias_ref[gl, hh]
        m = jnp.max(logit, axis=0, keepdims=True)
        e = jnp.exp2(logit - m)
        inv = jnp.where(m > 0.5 * NEG_LOGIT, 1.0 / jnp.sum(e, axis=0, keepdims=True), 0.0)
        ps.append(e * inv)
    ots = [_dot(vct_ref[0, gl], ps[n].astype(BF16)) for n, (gl, hh) in enumerate(heads)]
    for n, (gl, hh) in enumerate(heads):
        oc_ref[0, n] = (ots[n] * _gate_row(gate_ref, gp * CMP_GROUPS + gl, hh, 0)).astype(BF16)

    cov = cov_ref[...]
    jb = lax.broadcasted_iota(jnp.int32, (n_slc, tq), 0)
    cur = (t0 + lax.broadcasted_iota(jnp.int32, (n_slc, tq), 1)) // SLC_BLOCK
    valid = jb <= cur
    forced = (jb == 0) | (jb == cur) | (jb == cur - 1)
    sub = 8
    row_id = lax.broadcasted_iota(jnp.int32, (sub, tq), 0)
    for gl in range(CMP_GROUPS):
        g4 = ps[gl * NSA_HPG:(gl + 1) * NSA_HPG]
        p_sum = (g4[0] + g4[1]) + (g4[2] + g4[3])
        p_hi = p_sum.astype(BF16)
        r1 = p_sum - p_hi.astype(F32)
        p_mid = r1.astype(BF16)
        p_lo = (r1 - p_mid.astype(F32)).astype(BF16)
        st = _dot(cov, p_hi) + _dot(cov, p_mid) + _dot(cov, p_lo)
        score_ref[gl] = jnp.where(valid, st + jnp.where(forced, FORCE_BONUS, 0.0), -1.0)
        groups = [score_ref[gl, r:r + sub, :] for r in range(0, n_slc, sub)]
        ranks = [jnp.zeros((sub, tq), F32) for _ in groups]
        for j2 in range(n_slc):
            row = score_ref[gl, j2:j2 + 1, :]
            for gi, sc in enumerate(groups):
                r0 = gi * sub
                if r0 > j2:
                    inc = jnp.where(row >= sc, 1.0, 0.0)
                elif r0 + sub - 1 <= j2:
                    inc = jnp.where(row > sc, 1.0, 0.0)
                else:
                    inc = jnp.where(row_id > j2 - r0, jnp.where(row >= sc, 1.0, 0.0), jnp.where(row > sc, 1.0, 0.0))
                ranks[gi] = ranks[gi] + inc
        for gi, rk in enumerate(ranks):
            score_ref[gl, gi * sub:(gi + 1) * sub, :] = rk
        sel = valid & (score_ref[gl] < n_sel)
        sel_ref[0, gl, 0:n_slc, :] = jnp.where(sel, 0.0, -MASK_BIG).astype(BF16)
        if n_slc < LANES:
            sel_ref[0, gl, n_slc:LANES, :] = jnp.zeros((LANES - n_slc, tq), BF16)


def _cmp_attn(proj, gates, k_cmp, v_cmp_t, bias_c, cover_t, bsz, t):
    g_, dh, tq = NSA_KV_GROUPS, NSA_HEAD_DIM, CMP_TQ
    nt = t // tq
    nb = t // CMP_STRIDE
    n_slc = t // SLC_BLOCK
    ng = CMP_GROUPS
    gw = ng * NSA_HPG * dh
    return pl.pallas_call(
        functools.partial(_cmp_attn_kernel, n_sel=min(SLC_TOP_N, n_slc)),
        grid=(g_ // ng, nt, bsz),
        in_specs=[
            pl.BlockSpec((tq, gw), lambda g, i, b: (b * nt + i, COL_AQ // gw + g)),
            pl.BlockSpec((1, ng, nb, dh), lambda g, i, b: (b, g, 0, 0)),
            pl.BlockSpec((1, ng, dh, nb), lambda g, i, b: (b, g, 0, 0)),
            pl.BlockSpec((ng, NSA_HPG, nb, tq), lambda g, i, b: (g, 0, 0, i)),
            pl.BlockSpec((n_slc, nb), lambda g, i, b: (0, 0)),
            pl.BlockSpec((GATE_COLS, tq), lambda g, i, b: (0, b * nt + i)),
        ],
        out_specs=[
            pl.BlockSpec((1, ng * NSA_HPG, dh, tq), lambda g, i, b: (b, g, 0, i)),
            pl.BlockSpec((1, ng, LANES, tq), lambda g, i, b: (b, g, 0, i)),
        ],
        out_shape=[
            jax.ShapeDtypeStruct((bsz, NSA_HEADS, dh, t), BF16),
            jax.ShapeDtypeStruct((bsz, g_, LANES, t), BF16),
        ],
        scratch_shapes=[pltpu.VMEM((ng, n_slc, tq), F32)],
        compiler_params=_cparams(("parallel", "parallel", "parallel")),
        name="cmp_attn",
    )(proj, k_cmp, v_cmp_t, bias_c, cover_t, gates)


def _slc_attn_kernel(q_ref, sbt_ref, k_ref, v_ref, bias_ref, gate_ref, os_ref,
                     kaug, vt, qt, s_buf, p_buf, a_buf, m_ref, acc_ref, *, n_near):
    tq, tk, dh = SLC_TQ, SLC_TK, NSA_HEAD_DIM
    r = tq // tk
    t = k_ref.shape[0]
    g = pl.program_id(1)
    i = pl.program_id(2)
    heads = range(NSA_HPG)

    @pl.when(i == 0)
    def _():
        kaug[:, 0:dh] = k_ref[...]
        blk = lax.broadcasted_iota(jnp.int32, (t, LANES), 0) // SLC_BLOCK
        lane = lax.broadcasted_iota(jnp.int32, (t, LANES), 1)
        kaug[:, dh:dh + LANES] = jnp.where(blk == lane, 1.0, 0.0).astype(BF16)
        _fill_vt(vt, v_ref)

    sbt = sbt_ref[0, 0]
    for hh in heads:
        qt[hh, 0:dh, :] = _q_t(q_ref, hh)
        qt[hh, dh:dh + LANES, :] = sbt

    def ktile(j):
        return kaug[pl.ds(pl.multiple_of(j * tk, tk), tk), :]

    def vtile(j):
        return vt[:, pl.ds(pl.multiple_of(j * tk, tk), tk)]

    m_ref[...] = jnp.full(m_ref.shape, NEG_LOGIT, F32)
    acc_ref[...] = jnp.zeros_like(acc_ref)
    k0 = ktile(0)
    for hh in heads:
        s_buf[hh] = _dot(k0, qt[hh])

    last = (i + 1) * r - 1

    def stage(j, bias_fn, with_pv=True, with_qk=True):
        if with_pv:
            v_prev = vtile(j - 1)
            pv = [_dot(v_prev, p_buf[hh]) for hh in heads]
        if with_qk:
            k_next = ktile(j + 1)
            s_next = [_dot(k_next, qt[hh]) for hh in heads]
        if with_pv:
            for hh in heads:
                acc_ref[hh] = a_buf[hh] * acc_ref[hh] + pv[hh]
        for hh in heads:
            s = s_buf[hh]
            if bias_fn is not None:
                s = s + bias_fn(hh)
            m_prev = m_ref[hh]
            m_new = jnp.maximum(m_prev, jnp.max(s, axis=0, keepdims=True))
            a_buf[hh] = jnp.exp2(m_prev - m_new)
            p_buf[hh] = jnp.exp2(s - m_new).astype(BF16)
            m_ref[hh] = m_new
        if with_qk:
            for hh in heads:
                s_buf[hh] = s_next[hh]

    n_far = jnp.maximum(i * r + r - n_near, 0)

    def near_bias(j):
        off = pl.multiple_of((i * r + (r - 1) - j) * tk, tk)
        return lambda hh: bias_ref[0, hh, :, pl.ds(off, tq)]

    @pl.when(n_far > 0)
    def _():
        stage(0, None, with_pv=False)

    @pl.when((n_far == 0) & (last > 0))
    def _():
        stage(0, near_bias(0), with_pv=False)

    @pl.when(last == 0)
    def _():
        stage(0, near_bias(0), with_pv=False, with_qk=False)

    def far_body(j, c):
        stage(j, None)
        return c

    lax.fori_loop(1, n_far, far_body, 0)

    def near_body(j, c):
        stage(j, near_bias(j))
        return c

    lax.fori_loop(jnp.maximum(n_far, 1), last, near_body, 0)

    @pl.when(last > 0)
    def _():
        stage(last, near_bias(last), with_qk=False)

    v_last = vtile(last)
    pv = [_dot(v_last, p_buf[hh]) for hh in heads]
    for hh in heads:
        acc = a_buf[hh] * acc_ref[hh] + pv[hh]
        os_ref[0, hh] = _out_t(acc, _gate_row(gate_ref, g, hh, 1)).astype(BF16)


def _slc_attn(proj, gates, selbias_t, bias_s, bsz, t):
    g_, dh, tq, tk = NSA_KV_GROUPS, NSA_HEAD_DIM, SLC_TQ, SLC_TK
    nt = t // tq
    gw = NSA_HPG * dh
    n_near = (bias_s.shape[3] - tq) // tk + 1
    return pl.pallas_call(
        functools.partial(_slc_attn_kernel, n_near=n_near),
        grid=(bsz, g_, nt),
        in_specs=[
            pl.BlockSpec((tq, gw), lambda b, g, i: (b * nt + i, COL_AQ // gw + g)),
            pl.BlockSpec((1, 1, LANES, tq), lambda b, g, i: (b, g, 0, i)),
            pl.BlockSpec((t, dh), lambda b, g, i: (b, COL_SK // dh + g)),
            pl.BlockSpec((t, dh), lambda b, g, i: (b, COL_SV // dh + g)),
            pl.BlockSpec((1, NSA_HPG, tk, bias_s.shape[3]), lambda b, g, i: (g, 0, 0, 0)),
            pl.BlockSpec((GATE_COLS, tq), lambda b, g, i: (0, b * nt + i)),
        ],
        out_specs=pl.BlockSpec((1, NSA_HPG, dh, tq), lambda b, g, i: (b, g, 0, i)),
        out_shape=jax.ShapeDtypeStruct((bsz, NSA_HEADS, dh, t), BF16),
        scratch_shapes=[
            pltpu.VMEM((t, dh + LANES), BF16),
            pltpu.VMEM((VROWS, t), BF16),
            pltpu.VMEM((NSA_HPG, dh + LANES, tq), BF16),
            pltpu.VMEM((NSA_HPG, tk, tq), F32),
            pltpu.VMEM((NSA_HPG, tk, tq), BF16),
            pltpu.VMEM((NSA_HPG, 1, tq), F32),
            pltpu.VMEM((NSA_HPG, 1, tq), F32),
            pltpu.VMEM((NSA_HPG, VROWS, tq), F32),
        ],
        compiler_params=_cparams(("parallel", "parallel", "arbitrary")),
        name="slc_attn",
    )(proj, selbias_t, proj, proj, bias_s, gates)


def _win_attn_kernel(q_ref, k_ref, v_ref, bias_ref, gate_ref, oc_ref, os_ref, z_ref, ya_ref, vt, *, n_tiles):
    tq, tk, dh = WIN_TQ, ATT_TK, NSA_HEAD_DIM
    r = tq // tk
    gp = pl.program_id(1)
    i = pl.program_id(2)
    heads = [(gl, hh) for gl in range(WIN_GROUPS) for hh in range(NSA_HPG)]

    @pl.when(i == 0)
    def _():
        for gl in range(WIN_GROUPS):
            vt[gl, 0:dh, :] = v_ref[:, gl * dh:(gl + 1) * dh].astype(F32).T.astype(BF16)
            vt[gl, dh:VROWS, :] = jnp.ones((VROWS - dh, vt.shape[2]), BF16)

    js = [i * r + (r - 1) - d for d in range(n_tiles)]
    offs = [pl.multiple_of(jnp.maximum(j, 0) * tk, tk) for j in js]
    boffs = [pl.multiple_of(jnp.where(js[d] >= 0, d, n_tiles) * tk, tk) for d in range(n_tiles)]
    k_tiles = [[k_ref[pl.ds(off, tk), gl * dh:(gl + 1) * dh] for off in offs] for gl in range(WIN_GROUPS)]
    v_tiles = [[vt[gl, :, pl.ds(off, tk)] for off in offs] for gl in range(WIN_GROUPS)]
    qs = [_q_scaled(q_ref, gl * NSA_HPG + hh) for gl, hh in heads]
    ss = [[_dot_nt(k_tiles[gl][d], qs[n]) for d in range(n_tiles)] for n, (gl, hh) in enumerate(heads)]
    ps = []
    for n, (gl, hh) in enumerate(heads):
        s = [ss[n][d] + bias_ref[gl, hh, :, pl.ds(boffs[d], tq)] for d in range(n_tiles)]
        m = functools.reduce(jnp.maximum, [jnp.max(x, axis=0, keepdims=True) for x in s])
        ps.append([jnp.exp2(x - m).astype(BF16) for x in s])
    pvs = [[_dot(v_tiles[gl][d], ps[n][d]) for d in range(n_tiles)] for n, (gl, hh) in enumerate(heads)]
    for n, (gl, hh) in enumerate(heads):
        cols = slice(n * dh, (n + 1) * dh)
        o = _out_t(functools.reduce(lambda x, y: x + y, pvs[n]), _gate_row(gate_ref, gp * WIN_GROUPS + gl, hh, 2))
        o = o + oc_ref[0, n].astype(F32) + os_ref[0, n].astype(F32)
        ya_ref[:, cols] = (o.T * _silu(z_ref[:, cols].astype(F32))).astype(BF16)


def _win_attn(proj, gates, o_cmp, o_slc, bias_w, bsz, t):
    g_, dh, tq, tk = NSA_KV_GROUPS, NSA_HEAD_DIM, WIN_TQ, ATT_TK
    nt = t // tq
    ng = WIN_GROUPS
    gw = ng * NSA_HPG * dh
    kw = ng * dh
    n_tiles = (bias_w.shape[3] - tq) // tk
    return pl.pallas_call(
        functools.partial(_win_attn_kernel, n_tiles=n_tiles),
        grid=(bsz, g_ // ng, nt),
        in_specs=[
            pl.BlockSpec((tq, gw), lambda b, g, i: (b * nt + i, COL_AQ // gw + g)),
            pl.BlockSpec((t, kw), lambda b, g, i: (b, COL_WK // kw + g)),
            pl.BlockSpec((t, kw), lambda b, g, i: (b, COL_WV // kw + g)),
            pl.BlockSpec((ng, NSA_HPG, tk, bias_w.shape[3]), lambda b, g, i: (g, 0, 0, 0)),
            pl.BlockSpec((GATE_COLS, tq), lambda b, g, i: (0, b * nt + i)),
            pl.BlockSpec((1, ng * NSA_HPG, dh, tq), lambda b, g, i: (b, g, 0, i)),
            pl.BlockSpec((1, ng * NSA_HPG, dh, tq), lambda b, g, i: (b, g, 0, i)),
            pl.BlockSpec((tq, gw), lambda b, g, i: (b * nt + i, COL_AZ // gw + g)),
        ],
        out_specs=pl.BlockSpec((tq, gw), lambda b, g, i: (b * nt + i, g)),
        out_shape=jax.ShapeDtypeStruct((bsz * t, NSA_WIDTH), BF16),
        scratch_shapes=[pltpu.VMEM((ng, VROWS, t), BF16)],
        compiler_params=_cparams(("parallel", "parallel", "arbitrary")),
        name="win_attn",
    )(proj, proj, proj, bias_w, gates, o_cmp, o_slc, proj)


def _outproj_kernel(ym_ref, ya_ref, x_ref, w_ref, gain_ref, o_ref, mix):
    j = pl.program_id(1)
    nj = pl.num_programs(1)
    tn = x_ref.shape[1]

    @pl.when(j == 0)
    def _():
        mix[:, 0:MLSTM_WIDTH] = ym_ref[...]
        mix[:, MLSTM_WIDTH:D_MIX] = ya_ref[...]

    o_ref[:, pl.ds(pl.multiple_of(j * tn, tn), tn)] = x_ref[...] + _dot(mix[...], w_ref[...])

    @pl.when(j == nj - 1)
    def _():
        n_tiles = o_ref.shape[1] // tn
        ss = None
        for jj in range(n_tiles):
            y = o_ref[:, jj * tn:(jj + 1) * tn]
            part = jnp.sum(y * y, axis=-1, keepdims=True)
            ss = part if ss is None else ss + part
        inv = lax.rsqrt(ss / o_ref.shape[1] + RMS_EPS)
        for jj in range(n_tiles):
            cols = slice(jj * tn, (jj + 1) * tn)
            o_ref[:, cols] = o_ref[:, cols] * inv * gain_ref[:, cols]


def _out_proj(y_m, y_a, x2d, w_out, gain):
    n, d = x2d.shape
    tm, tn = min(OUTPROJ_TM, n), OUTPROJ_TN
    nj = d // tn
    return pl.pallas_call(
        _outproj_kernel,
        grid=(n // tm, nj),
        in_specs=[
            pl.BlockSpec((tm, MLSTM_WIDTH), lambda i, j: (i, 0)),
            pl.BlockSpec((tm, NSA_WIDTH), lambda i, j: (i, 0)),
            pl.BlockSpec((tm, tn), lambda i, j: (i, j)),
            pl.BlockSpec((D_MIX, tn), lambda i, j: (0, j)),
            pl.BlockSpec((1, d), lambda i, j: (0, 0)),
        ],
        out_specs=pl.BlockSpec((tm, d), lambda i, j: (i, 0)),
        out_shape=jax.ShapeDtypeStruct((n, d), F32),
        scratch_shapes=[pltpu.VMEM((tm, D_MIX), BF16)],
        compiler_params=_cparams(("parallel", "arbitrary")),
        name="out_proj",
    )(y_m, y_a, x2d, w_out, gain)


def _rel_bucket(dist):
    n = jnp.maximum(dist, 0)
    nf = jnp.maximum(n, REL_MAX_EXACT).astype(jnp.float32)
    large = REL_MAX_EXACT + (jnp.log(nf / REL_MAX_EXACT) / math.log(REL_MAX_DISTANCE / REL_MAX_EXACT)
                             * (REL_BUCKETS - REL_MAX_EXACT)).astype(jnp.int32)
    large = jnp.minimum(large, REL_BUCKETS - 1)
    return jnp.where(n < REL_MAX_EXACT, n, large)


def _toeplitz_vec(by_dist, base, n_pos, n_neg, lo, hi, shift=None):
    w = n_pos + n_neg
    c = np.arange(w)
    dist = np.where(c < n_pos, base + c, base - (w - c))
    ok = (dist >= lo) & (dist < hi)
    dmax = by_dist.shape[1]

    def run(start, length):
        left, right = max(0, -start), max(0, start + length - dmax)
        ext = jnp.pad(by_dist, ((0, 0), (left, right)), mode="edge")
        return lax.slice_in_dim(ext, start + left, start + left + length, axis=1)

    vals = jnp.concatenate([run(base, n_pos), run(base - n_neg, n_neg)], axis=1)
    if shift is not None:
        vals = vals - shift
    return jnp.where(ok[None], vals * LOG2E, NEG_LOGIT)


def _toeplitz_t(w_row, n_keys, n_q, key_step):
    x = jnp.broadcast_to(w_row, (n_keys, w_row.shape[1]))
    return pltpu.roll(x, 0, 1, stride=key_step, stride_axis=0)[:, 0:n_q]


def _bias_tables_kernel(wc_ref, ws_ref, ww_ref, bc_ref, bs_ref, bw_ref):
    nb, t = bc_ref.shape[2], bc_ref.shape[3]
    bc_ref[0, 0] = _toeplitz_t(wc_ref[0], nb, t, CMP_STRIDE)
    bs_ref[0, 0] = _toeplitz_t(ws_ref[0], bs_ref.shape[2], bs_ref.shape[3], 1)
    bw_ref[0, 0] = _toeplitz_t(ww_ref[0], bw_ref.shape[2], bw_ref.shape[3], 1)


def _bias_tables(rel_bias, t):
    tq, tk = ATT_TQ, ATT_TK
    g_, hpg = NSA_KV_GROUPS, NSA_HPG
    rb = rel_bias.astype(F32)
    dmax = REL_MAX_DISTANCE + 2 * max(SLC_TQ, SLC_TK, WIN_TQ, tq, tk)
    onehot = (_rel_bucket(jnp.arange(dmax, dtype=jnp.int32))[None, :] == jnp.arange(REL_BUCKETS, dtype=jnp.int32)[:, None])
    by_dist = jnp.dot(rb.T, onehot.astype(F32), precision=lax.Precision.HIGHEST)
    far = rb[REL_BUCKETS - 1][:, None]
    big = 1 << 30
    nb = t // CMP_STRIDE
    wc = _toeplitz_vec(by_dist, -(CMP_BLOCK - 1), t, t, 0, big)[:, None, :]
    sq, sk = min(SLC_TQ, t), SLC_TK
    r = sq // sk
    n_s = min(-(-(REL_MAX_DISTANCE + sk - 1) // sk) + r - 1, t // sk)
    wm = (n_s - 1) * sk + sq
    ws = _toeplitz_vec(by_dist, -(r - 1) * sk, wm, sk, 0, big, far)[:, None, :]
    wq = min(WIN_TQ, t)
    rw = wq // tk
    n_w = rw + (WINDOW - 1 + tk - 1) // tk
    wmw = n_w * tk + wq
    ww = _toeplitz_vec(by_dist, -(rw - 1) * tk, wmw, tk, 0, WINDOW)[:, None, :]
    return pl.pallas_call(
        _bias_tables_kernel,
        grid=(NSA_HEADS,),
        in_specs=[
            pl.BlockSpec((1, 1, 2 * t), lambda h: (h, 0, 0)),
            pl.BlockSpec((1, 1, wm + sk), lambda h: (h, 0, 0)),
            pl.BlockSpec((1, 1, wmw + tk), lambda h: (h, 0, 0)),
        ],
        out_specs=[
            pl.BlockSpec((1, 1, nb, t), lambda h: (h // hpg, h % hpg, 0, 0)),
            pl.BlockSpec((1, 1, sk, wm), lambda h: (h // hpg, h % hpg, 0, 0)),
            pl.BlockSpec((1, 1, tk, wmw), lambda h: (h // hpg, h % hpg, 0, 0)),
        ],
        out_shape=[
            jax.ShapeDtypeStruct((g_, hpg, nb, t), F32),
            jax.ShapeDtypeStruct((g_, hpg, sk, wm), F32),
            jax.ShapeDtypeStruct((g_, hpg, tk, wmw), F32),
        ],
        compiler_params=_cparams(("parallel",)),
        name="bias_tables",
    )(wc, ws, ww)


def _cover_t(t):
    nb = t // CMP_STRIDE
    n_cmp = (t - CMP_BLOCK) // CMP_STRIDE + 1
    n_slc = t // SLC_BLOCK
    cs = np.arange(nb) * CMP_STRIDE
    ss = np.arange(n_slc) * SLC_BLOCK
    cover = np.clip(np.minimum(cs[:, None] + CMP_BLOCK, ss[None, :] + SLC_BLOCK)
                    - np.maximum(cs[:, None], ss[None, :]), 0, None) / CMP_BLOCK
    cover[n_cmp:] = 0.0
    return jnp.asarray(cover.T, dtype=BF16)


def kernel(x, norm_gain, w_in, w_conv, b_igate, b_fgate, mlstm_norm_gain, cmp_k_pos, cmp_k_w1, cmp_k_w2,
           cmp_v_pos, cmp_v_w1, cmp_v_w2, rel_bias, w_out, final_norm_gain):
    bsz, t, d = x.shape
    assert d == D_MODEL and t % MLSTM_L == 0 and t % ATT_TQ == 0 and (t // CMP_STRIDE) % LANES == 0
    n = bsz * t
    x2d = x.reshape(n, d)

    w_main, w_gate = _w_prep(w_in.T)
    proj, gates = _in_proj(x2d, norm_gain.reshape(1, d).astype(F32), w_main, w_gate)

    g_rows = gates[:, :2 * MLSTM_HEADS].reshape(bsz, t, 2, MLSTM_HEADS).transpose(0, 3, 2, 1)
    y_m = _mlstm(proj, g_rows, w_conv.astype(F32), b_igate.astype(F32), b_fgate.astype(F32),
                 mlstm_norm_gain.reshape(1, MLSTM_WIDTH).astype(F32), bsz, t)

    dh = NSA_HEAD_DIM
    half = CMP_BLOCK // 2

    def w1cat(w1):
        return jnp.concatenate([w1[:half].reshape(half * dh, dh), w1[half:].reshape(half * dh, dh)], axis=1).astype(BF16)

    k_cmp, v_cmp_t = _compress(
        proj, w1cat(cmp_k_w1), cmp_k_w2.astype(BF16), cmp_k_pos.reshape(2, half * dh).astype(BF16),
        w1cat(cmp_v_w1), cmp_v_w2.astype(BF16), cmp_v_pos.reshape(2, half * dh).astype(BF16), bsz, t)

    bias_c, bias_s, bias_w = _bias_tables(rel_bias, t)
    gates_t = gates.T
    o_cmp, selbias_t = _cmp_attn(proj, gates_t, k_cmp, v_cmp_t, bias_c, _cover_t(t), bsz, t)
    o_slc = _slc_attn(proj, gates_t, selbias_t, bias_s, bsz, t)
    y_a = _win_attn(proj, gates_t, o_cmp, o_slc, bias_w, bsz, t)

    out = _out_proj(y_m, y_a, x2d, w_out.astype(BF16), final_norm_gain.reshape(1, d).astype(F32))
    return out.reshape(bsz, t, d)
```

```python
import functools
import math

import jax
import jax.numpy as jnp
import numpy as np
from jax import lax
from jax.experimental import pallas as pl
from jax.experimental.pallas import tpu as pltpu

F32 = jnp.float32
BF16 = jnp.bfloat16

D_MODEL = 4096
D_MIX = D_MODEL
MLSTM_WIDTH = D_MIX // 2
MLSTM_HEADS = 4
MLSTM_V_DIM = MLSTM_WIDTH // MLSTM_HEADS
MLSTM_QK_DIM = MLSTM_V_DIM // 2
MLSTM_QK_WIDTH = MLSTM_HEADS * MLSTM_QK_DIM
CONV_WIDTH = 4
NSA_WIDTH = D_MIX - MLSTM_WIDTH
NSA_HEAD_DIM = 128
NSA_HEADS = NSA_WIDTH // NSA_HEAD_DIM
NSA_KV_GROUPS = 4
NSA_HPG = NSA_HEADS // NSA_KV_GROUPS
NSA_KV_WIDTH = NSA_KV_GROUPS * NSA_HEAD_DIM
CMP_BLOCK = 32
CMP_STRIDE = 16
SLC_BLOCK = 64
SLC_TOP_N = 16
WINDOW = 512
FORCE_BONUS = 1000.0
REL_BUCKETS = 32
REL_MAX_EXACT = REL_BUCKETS // 2
REL_MAX_DISTANCE = 1024
RMS_EPS = 1e-6
NEG_LOGIT = -1e30

LANES = 128
VMEM_LIMIT_BYTES = 56 * 1024 * 1024

COL_MQ = 0
COL_MK = COL_MQ + MLSTM_QK_WIDTH
COL_MV = COL_MK + MLSTM_QK_WIDTH
COL_MO = COL_MV + MLSTM_WIDTH
COL_MZ = COL_MO + MLSTM_WIDTH
COL_AQ = COL_MZ + MLSTM_WIDTH
COL_CK = COL_AQ + NSA_WIDTH
COL_CV = COL_CK + NSA_KV_WIDTH
COL_SK = COL_CV + NSA_KV_WIDTH
COL_SV = COL_SK + NSA_KV_WIDTH
COL_WK = COL_SV + NSA_KV_WIDTH
COL_WV = COL_WK + NSA_KV_WIDTH
COL_AZ = COL_WV + NSA_KV_WIDTH
MAIN_COLS = COL_AZ + NSA_WIDTH
GATE_COLS = LANES
GATE_I = 0
GATE_F = MLSTM_HEADS
GATE_NSA = 2 * MLSTM_HEADS

INPROJ_TM = 512
INPROJ_TN = 1536
OUTPROJ_TM = 512
OUTPROJ_TN = 1024
MLSTM_L = 256
ATT_TQ = 256
ATT_TK = 256
WPREP_TR = 512
SLC_TQ = 512
SLC_TK = 512
CMP_TQ = 512
CMP_GROUPS = 2
WIN_TQ = 256
WIN_GROUPS = 2
MASK_BIG = 2.0 ** 100


def _cparams(sem):
    return pltpu.CompilerParams(dimension_semantics=sem, vmem_limit_bytes=VMEM_LIMIT_BYTES)


def _sigmoid(x):
    return 0.5 * jnp.tanh(0.5 * x) + 0.5


def _silu(x):
    return x * _sigmoid(x)


def _dot(a, b):
    return jnp.dot(a, b, preferred_element_type=F32)


def _dot_nt(a, b):
    return lax.dot_general(a, b, (((1,), (1,)), ((), ())), preferred_element_type=F32)


W_OFF_I = COL_MZ + MLSTM_WIDTH
W_OFF_AQ = W_OFF_I + 2 * MLSTM_HEADS
W_OFF_GATE = W_OFF_AQ + NSA_WIDTH + 6 * NSA_KV_WIDTH
W_OFF_AZ = W_OFF_GATE + 3 * NSA_HEADS


def _wprep_kernel(w_ref, gi_ref, ga_ref, o_ref, g_ref):
    o_ref[...] = w_ref[...].astype(BF16)

    @pl.when(pl.program_id(0) == 0)
    def _():
        pad = jnp.zeros((GATE_COLS - gi_ref.shape[0] - ga_ref.shape[0], g_ref.shape[1]), F32)
        g_ref[...] = jnp.concatenate([gi_ref[...], ga_ref[...], pad], axis=0).astype(BF16)


def _w_prep(w_t):
    rows, d = w_t.shape
    tr = WPREP_TR
    nb_a, nb_b = W_OFF_I // tr, (W_OFF_GATE - W_OFF_AQ) // tr

    def src(m):
        skip = jnp.where(m >= nb_a, W_OFF_AQ - W_OFF_I, 0) + jnp.where(m >= nb_a + nb_b, W_OFF_AZ - W_OFF_GATE, 0)
        return pl.multiple_of(m * tr + skip, 8)

    rows_at = lambda start, size: pl.BlockSpec((pl.Element(size), pl.Element(d)), lambda m: (start, 0))
    return pl.pallas_call(
        _wprep_kernel,
        grid=(MAIN_COLS // tr,),
        in_specs=[
            pl.BlockSpec((pl.Element(tr), pl.Element(d)), lambda m: (src(m), 0)),
            rows_at(W_OFF_I, W_OFF_AQ - W_OFF_I),
            rows_at(W_OFF_GATE, W_OFF_AZ - W_OFF_GATE),
        ],
        out_specs=[pl.BlockSpec((tr, d), lambda m: (m, 0)), pl.BlockSpec((GATE_COLS, d), lambda m: (0, 0))],
        out_shape=[jax.ShapeDtypeStruct((MAIN_COLS, d), BF16), jax.ShapeDtypeStruct((GATE_COLS, d), BF16)],
        compiler_params=_cparams(("arbitrary",)),
        name="w_prep",
    )(w_t, w_t, w_t)


def _inproj_kernel(x_ref, gain_ref, w_hbm, wg_ref, o_hbm, og_ref, h_ref, wbuf, obuf, w_sem, o_sem):
    i = pl.program_id(0)
    ni = pl.num_programs(0)
    tm, tn = obuf.shape[1], obuf.shape[2]
    nj = w_hbm.shape[0] // tn

    def w_copy(j, slot):
        return pltpu.make_async_copy(w_hbm.at[pl.ds(pl.multiple_of(j * tn, tn), tn), :], wbuf.at[slot], w_sem.at[slot])

    def o_copy(j, slot):
        dst = o_hbm.at[pl.ds(pl.multiple_of(i * tm, tm), tm), pl.ds(pl.multiple_of(j * tn, tn), tn)]
        return pltpu.make_async_copy(obuf.at[slot], dst, o_sem.at[slot])

    @pl.when(i == 0)
    def _():
        w_copy(0, 0).start()

    x = x_ref[...]
    ms = jnp.mean(x * x, axis=-1, keepdims=True)
    h = (x * lax.rsqrt(ms + RMS_EPS) * gain_ref[...]).astype(BF16)
    h_ref[...] = h
    og_ref[...] = _dot_nt(h, wg_ref[...])

    def body(j, carry):
        slot = j % 2
        nxt = (j + 1) % nj
        w_copy(j, slot).wait()

        @pl.when((j + 1 < nj) | (i + 1 < ni))
        def _():
            w_copy(nxt, 1 - slot).start()

        @pl.when((j >= 2) | (i > 0))
        def _():
            o_copy(j, slot).wait()

        obuf[slot] = _dot_nt(h_ref[...], wbuf[slot]).astype(BF16)
        o_copy(j, slot).start()
        return carry

    lax.fori_loop(0, nj, body, 0)

    @pl.when(i == ni - 1)
    def _():
        o_copy(nj - 2, (nj - 2) % 2).wait()
        o_copy(nj - 1, (nj - 1) % 2).wait()


def _in_proj(x2d, gain, w_main, w_gate):
    n, d = x2d.shape
    tm, tn = min(INPROJ_TM, n), INPROJ_TN
    assert (MAIN_COLS // tn) % 2 == 0
    return pl.pallas_call(
        _inproj_kernel,
        grid=(n // tm,),
        in_specs=[
            pl.BlockSpec((tm, d), lambda i: (i, 0)),
            pl.BlockSpec((1, d), lambda i: (0, 0)),
            pl.BlockSpec(memory_space=pl.ANY),
            pl.BlockSpec((GATE_COLS, d), lambda i: (0, 0)),
        ],
        out_specs=[
            pl.BlockSpec(memory_space=pl.ANY),
            pl.BlockSpec((tm, GATE_COLS), lambda i: (i, 0)),
        ],
        out_shape=[
            jax.ShapeDtypeStruct((n, MAIN_COLS), BF16),
            jax.ShapeDtypeStruct((n, GATE_COLS), F32),
        ],
        scratch_shapes=[
            pltpu.VMEM((tm, d), BF16),
            pltpu.VMEM((2, tn, d), BF16),
            pltpu.VMEM((2, tm, tn), BF16),
            pltpu.SemaphoreType.DMA((2,)),
            pltpu.SemaphoreType.DMA((2,)),
        ],
        compiler_params=_cparams(("arbitrary",)),
        name="in_proj",
    )(x2d, gain, w_main, w_gate)


def _mlstm_kernel(bi_ref, bf_ref, q_ref, k_ref, v_ref, o_ref, z_ref, g_ref, wq_ref, wk_ref, ng_ref,
                  y_ref, qext, kext, c_st, n_st, m_st):
    L = MLSTM_L
    HIST = 8
    dqk, dv = MLSTM_QK_DIM, MLSTM_V_DIM

    @pl.when(pl.program_id(1) == 0)
    def _():
        qext[:, 0:HIST, :] = jnp.zeros((MLSTM_HEADS, HIST, dqk), F32)
        kext[:, 0:HIST, :] = jnp.zeros((MLSTM_HEADS, HIST, dqk), F32)
        c_st[...] = jnp.zeros_like(c_st)
        n_st[...] = jnp.zeros_like(n_st)
        m_st[...] = jnp.zeros_like(m_st)

    rr = lax.broadcasted_iota(jnp.int32, (L, L), 0)
    cc = lax.broadcasted_iota(jnp.int32, (L, L), 1)
    upper = (rr <= cc).astype(F32)

    for hd in range(MLSTM_HEADS):
        qcols = slice(hd * dqk, (hd + 1) * dqk)
        vcols = slice(hd * dv, (hd + 1) * dv)
        qext[hd, HIST:HIST + L, :] = q_ref[:, qcols].astype(F32)
        kext[hd, HIST:HIST + L, :] = k_ref[:, qcols].astype(F32)

        def conv_silu(ext, w_ref):
            w = w_ref[:, qcols]
            y = ext[hd, pl.ds(HIST, L), :] * w[CONV_WIDTH - 1:CONV_WIDTH, :]
            for s in range(1, CONV_WIDTH):
                y = y + ext[hd, pl.ds(HIST - s, L), :] * w[CONV_WIDTH - 1 - s:CONV_WIDTH - s, :]
            return _silu(y)

        qc = conv_silu(qext, wq_ref)
        kc = conv_silu(kext, wk_ref) * (dqk ** -0.5)
        qext[hd, 0:HIST, :] = qext[hd, L:L + HIST, :]
        kext[hd, 0:HIST, :] = kext[hd, L:L + HIST, :]

        g = g_ref[0, hd]
        i_row = g[0:1, :] + bi_ref[hd]
        f_row = g[1:2, :] + bf_ref[hd]
        lf_row = jnp.minimum(f_row, 0.0) - jnp.log(1.0 + jnp.exp(-jnp.abs(f_row)))

        bcum_row = jnp.dot(jnp.broadcast_to(lf_row, (8, L)), upper, preferred_element_type=F32,
                           precision=lax.Precision.HIGHEST)[0:1, :]
        bcum_col = jnp.sum(jnp.where(rr == cc, bcum_row, 0.0), axis=1, keepdims=True)
        gsum = bcum_row[:, L - 1:L]
        m_prev = m_st[hd]

        dlog = jnp.where(rr >= cc, bcum_col - bcum_row + i_row, -jnp.inf)
        m_inter = bcum_col + m_prev
        m_t = jnp.maximum(m_inter, jnp.max(dlog, axis=1, keepdims=True))
        dmat = jnp.exp(dlog - m_t)
        inter = jnp.exp(m_inter - m_t)

        qb = qc.astype(BF16)
        kct = kc.T
        vb = v_ref[:, vcols]
        s = _dot(qb, kct.astype(BF16)) * dmat
        c_prev = c_st[hd]
        n_prev = n_st[hd]
        num = _dot(s.astype(BF16), vb) + inter * _dot(qb, c_prev.astype(BF16))
        qn = jnp.sum(s, axis=1, keepdims=True) + inter * jnp.sum(qc * n_prev, axis=1, keepdims=True)
        hh = num / jnp.maximum(jnp.abs(qn), jnp.exp(-m_t))

        wlog = gsum - bcum_row + i_row
        m_next = jnp.maximum(gsum + m_prev, jnp.max(wlog, axis=1, keepdims=True))
        wts = jnp.exp(wlog - m_next)
        keep = jnp.exp(gsum + m_prev - m_next)
        c_st[hd] = keep * c_prev + _dot((kct * wts).astype(BF16), vb)
        n_st[hd] = keep * n_prev + _dot(jnp.broadcast_to(wts, (8, L)).astype(BF16), kc.astype(BF16))[0:1, :]
        m_st[hd] = m_next

        hm = _sigmoid(o_ref[:, vcols].astype(F32)) * hh
        hm = hm * lax.rsqrt(jnp.mean(hm * hm, axis=-1, keepdims=True) + RMS_EPS)
        hm = hm * ng_ref[:, vcols]
        y_ref[:, vcols] = (hm * _silu(z_ref[:, vcols].astype(F32))).astype(BF16)


def _mlstm(proj, g_rows, w_conv, b_igate, b_fgate, norm_gain, bsz, t):
    L = MLSTM_L
    nc = t // L
    nh, dqk, dv = MLSTM_HEADS, MLSTM_QK_DIM, MLSTM_V_DIM
    row = lambda b, c: b * nc + c
    smem = pl.BlockSpec(memory_space=pltpu.SMEM)
    return pl.pallas_call(
        _mlstm_kernel,
        grid=(bsz, nc),
        in_specs=[
            smem, smem,
            pl.BlockSpec((L, MLSTM_QK_WIDTH), lambda b, c: (row(b, c), COL_MQ // MLSTM_QK_WIDTH)),
            pl.BlockSpec((L, MLSTM_QK_WIDTH), lambda b, c: (row(b, c), COL_MK // MLSTM_QK_WIDTH)),
            pl.BlockSpec((L, MLSTM_WIDTH), lambda b, c: (row(b, c), COL_MV // MLSTM_WIDTH)),
            pl.BlockSpec((L, MLSTM_WIDTH), lambda b, c: (row(b, c), COL_MO // MLSTM_WIDTH)),
            pl.BlockSpec((L, MLSTM_WIDTH), lambda b, c: (row(b, c), COL_MZ // MLSTM_WIDTH)),
            pl.BlockSpec((1, nh, 2, L), lambda b, c: (b, 0, 0, c)),
            pl.BlockSpec((CONV_WIDTH, MLSTM_QK_WIDTH), lambda b, c: (0, 0)),
            pl.BlockSpec((CONV_WIDTH, MLSTM_QK_WIDTH), lambda b, c: (0, 1)),
            pl.BlockSpec((1, MLSTM_WIDTH), lambda b, c: (0, 0)),
        ],
        out_specs=pl.BlockSpec((L, MLSTM_WIDTH), lambda b, c: (row(b, c), 0)),
        out_shape=jax.ShapeDtypeStruct((bsz * t, MLSTM_WIDTH), BF16),
        scratch_shapes=[
            pltpu.VMEM((nh, L + 8, dqk), F32),
            pltpu.VMEM((nh, L + 8, dqk), F32),
            pltpu.VMEM((nh, dqk, dv), F32),
            pltpu.VMEM((nh, 1, dqk), F32),
            pltpu.VMEM((nh, 1, 1), F32),
        ],
        compiler_params=_cparams(("parallel", "arbitrary")),
        name="mlstm",
    )(b_igate, b_fgate, proj, proj, proj, proj, proj, g_rows, w_conv, w_conv, norm_gain)


def _compress_kernel(ck_ref, cv_ref, w1k_ref, w2k_ref, pk_ref, w1v_ref, w2v_ref, pv_ref,
                     ok_ref, ov_ref, xf, xcat):
    t = ck_ref.shape[0]
    nb = t // CMP_STRIDE
    dh = NSA_HEAD_DIM

    def one(src_ref, w1_ref, w2_ref, pos_ref):
        xf[...] = src_ref[...].astype(F32)
        for l in range(CMP_STRIDE):
            xcat[:, l * dh:(l + 1) * dh] = xf[pl.ds(l, nb, stride=CMP_STRIDE), :].astype(BF16)
        w1 = w1_ref[...]
        ab = _dot(xcat[...], w1)
        pp = _dot(pos_ref[...], w1)
        pos_term = pp[0:1, 0:dh] + pp[1:2, dh:2 * dh]
        second = pltpu.roll(ab[:, dh:2 * dh], nb - 1, 0)
        hid = _silu(ab[:, 0:dh] + second + pos_term)
        return _dot(hid.astype(BF16), w2_ref[...])

    ok_ref[0, 0] = one(ck_ref, w1k_ref, w2k_ref, pk_ref).astype(BF16)
    ov_ref[0, 0] = one(cv_ref, w1v_ref, w2v_ref, pv_ref).T.astype(BF16)


def _compress(proj, w1k, w2k, pk, w1v, w2v, pv, bsz, t):
    g_, dh = NSA_KV_GROUPS, NSA_HEAD_DIM
    nb = t // CMP_STRIDE
    full = lambda a: pl.BlockSpec(a.shape, lambda b, g: (0,) * a.ndim)
    k_spec = pl.BlockSpec((1, 1, nb, dh), lambda b, g: (b, g, 0, 0))
    vt_spec = pl.BlockSpec((1, 1, dh, nb), lambda b, g: (b, g, 0, 0))
    return pl.pallas_call(
        _compress_kernel,
        grid=(bsz, g_),
        in_specs=[
            pl.BlockSpec((t, dh), lambda b, g: (b, COL_CK // dh + g)),
            pl.BlockSpec((t, dh), lambda b, g: (b, COL_CV // dh + g)),
            full(w1k), full(w2k), full(pk), full(w1v), full(w2v), full(pv),
        ],
        out_specs=[k_spec, vt_spec],
        out_shape=[jax.ShapeDtypeStruct((bsz, g_, nb, dh), BF16), jax.ShapeDtypeStruct((bsz, g_, dh, nb), BF16)],
        scratch_shapes=[pltpu.VMEM((t, dh), F32), pltpu.VMEM((nb, CMP_STRIDE * dh), BF16)],
        compiler_params=_cparams(("parallel", "parallel")),
        name="compress",
    )(proj, proj, w1k, w2k, pk, w1v, w2v, pv)


LOG2E = math.log2(math.e)
VROWS = NSA_HEAD_DIM + 16


def _q_scaled(q_ref, hh):
    dh = NSA_HEAD_DIM
    return (q_ref[:, hh * dh:(hh + 1) * dh].astype(F32) * (dh ** -0.5 * LOG2E)).astype(BF16)


def _q_t(q_ref, hh):
    dh = NSA_HEAD_DIM
    return (q_ref[:, hh * dh:(hh + 1) * dh].astype(F32) * (dh ** -0.5 * LOG2E)).T.astype(BF16)


def _gate_row(gt_ref, g, hh, branch):
    row = GATE_NSA + 3 * (g * NSA_HPG + hh) + branch
    return _sigmoid(gt_ref[pl.ds(row, 1), :])


def _fill_vt(vt, v_ref):
    dh = NSA_HEAD_DIM
    vt[0:dh, :] = v_ref[...].astype(F32).T.astype(BF16)
    vt[dh:VROWS, :] = jnp.ones((VROWS - dh, vt.shape[1]), BF16)


def _out_t(acc, gate_row):
    dh = NSA_HEAD_DIM
    return acc[0:dh, :] * (gate_row / acc[dh:dh + 1, :])


def _cmp_attn_kernel(q_ref, kc_ref, vct_ref, bias_ref, cov_ref, gate_ref, oc_ref, sel_ref, score_ref, *, n_sel):
    tq = q_ref.shape[0]
    dh = NSA_HEAD_DIM
    n_slc = cov_ref.shape[0]
    t0 = pl.program_id(1) * tq
    gp = pl.program_id(0)
    heads = [(gl, hh) for gl in range(CMP_GROUPS) for hh in range(NSA_HPG)]
    qk = [_dot_nt(kc_ref[0, gl], _q_scaled(q_ref, gl * NSA_HPG + hh)) for gl, hh in heads]
    ps = []
    for n, (gl, hh) in enumerate(heads):
        logit = qk[n] + bias_ref[gl, hh]
        m = jnp.max(logit, axis=0, keepdims=True)
        e = jnp.exp2(logit - m)
        inv = jnp.where(m > 0.5 * NEG_LOGIT, 1.0 / jnp.sum(e, axis=0, keepdims=True), 0.0)
        ps.append(e * inv)
    ots = [_dot(vct_ref[0, gl], ps[n].astype(BF16)) for n, (gl, hh) in enumerate(heads)]
    for n, (gl, hh) in enumerate(heads):
        oc_ref[0, n] = (ots[n] * _gate_row(gate_ref, gp * CMP_GROUPS + gl, hh, 0)).astype(BF16)

    cov = cov_ref[...]
    jb = lax.broadcasted_iota(jnp.int32, (n_slc, tq), 0)
    cur = (t0 + lax.broadcasted_iota(jnp.int32, (n_slc, tq), 1)) // SLC_BLOCK
    valid = jb <= cur
    forced = (jb == 0) | (jb == cur) | (jb == cur - 1)
    sub = 8
    row_id = lax.broadcasted_iota(jnp.int32, (sub, tq), 0)
    for gl in range(CMP_GROUPS):
        g4 = ps[gl * NSA_HPG:(gl + 1) * NSA_HPG]
        p_sum = (g4[0] + g4[1]) + (g4[2] + g4[3])
        p_hi = p_sum.astype(BF16)
        r1 = p_sum - p_hi.astype(F32)
        p_mid = r1.astype(BF16)
        p_lo = (r1 - p_mid.astype(F32)).astype(BF16)
        st = _dot(cov, p_hi) + _dot(cov, p_mid) + _dot(cov, p_lo)
        score_ref[gl] = jnp.where(valid, st + jnp.where(forced, FORCE_BONUS, 0.0), -1.0)
        groups = [score_ref[gl, r:r + sub, :] for r in range(0, n_slc, sub)]
        ranks = [jnp.zeros((sub, tq), F32) for _ in groups]
        for j2 in range(n_slc):
            row = score_ref[gl, j2:j2 + 1, :]
            for gi, sc in enumerate(groups):
                r0 = gi * sub
                if r0 > j2:
                    inc = jnp.where(row >= sc, 1.0, 0.0)
                elif r0 + sub - 1 <= j2:
                    inc = jnp.where(row > sc, 1.0, 0.0)
                else:
                    inc = jnp.where(row_id > j2 - r0, jnp.where(row >= sc, 1.0, 0.0), jnp.where(row > sc, 1.0, 0.0))
                ranks[gi] = ranks[gi] + inc
        for gi, rk in enumerate(ranks):
            score_ref[gl, gi * sub:(gi + 1) * sub, :] = rk
        sel = valid & (score_ref[gl] < n_sel)
        sel_ref[0, gl, 0:n_slc, :] = jnp.where(sel, 0.0, -MASK_BIG).astype(BF16)
        if n_slc < LANES:
            sel_ref[0, gl, n_slc:LANES, :] = jnp.zeros((LANES - n_slc, tq), BF16)


def _cmp_attn(proj, gates, k_cmp, v_cmp_t, bias_c, cover_t, bsz, t):
    g_, dh, tq = NSA_KV_GROUPS, NSA_HEAD_DIM, CMP_TQ
    nt = t // tq
    nb = t // CMP_STRIDE
    n_slc = t // SLC_BLOCK
    ng = CMP_GROUPS
    gw = ng * NSA_HPG * dh
    return pl.pallas_call(
        functools.partial(_cmp_attn_kernel, n_sel=min(SLC_TOP_N, n_slc)),
        grid=(g_ // ng, nt, bsz),
        in_specs=[
            pl.BlockSpec((tq, gw), lambda g, i, b: (b * nt + i, COL_AQ // gw + g)),
            pl.BlockSpec((1, ng, nb, dh), lambda g, i, b: (b, g, 0, 0)),
            pl.BlockSpec((1, ng, dh, nb), lambda g, i, b: (b, g, 0, 0)),
            pl.BlockSpec((ng, NSA_HPG, nb, tq), lambda g, i, b: (g, 0, 0, i)),
            pl.BlockSpec((n_slc, nb), lambda g, i, b: (0, 0)),
            pl.BlockSpec((GATE_COLS, tq), lambda g, i, b: (0, b * nt + i)),
        ],
        out_specs=[
            pl.BlockSpec((1, ng * NSA_HPG, dh, tq), lambda g, i, b: (b, g, 0, i)),
            pl.BlockSpec((1, ng, LANES, tq), lambda g, i, b: (b, g, 0, i)),
        ],
        out_shape=[
            jax.ShapeDtypeStruct((bsz, NSA_HEADS, dh, t), BF16),
            jax.ShapeDtypeStruct((bsz, g_, LANES, t), BF16),
        ],
        scratch_shapes=[pltpu.VMEM((ng, n_slc, tq), F32)],
        compiler_params=_cparams(("parallel", "parallel", "parallel")),
        name="cmp_attn",
    )(proj, k_cmp, v_cmp_t, bias_c, cover_t, gates)


def _slc_attn_kernel(q_ref, sbt_ref, k_ref, v_ref, bias_ref, gate_ref, os_ref,
                     kaug, vt, qt, s_buf, p_buf, a_buf, m_ref, acc_ref, *, n_near):
    tq, tk, dh = SLC_TQ, SLC_TK, NSA_HEAD_DIM
    r = tq // tk
    t = k_ref.shape[0]
    g = pl.program_id(1)
    i = pl.program_id(2)
    heads = range(NSA_HPG)

    @pl.when(i == 0)
    def _():
        kaug[:, 0:dh] = k_ref[...]
        blk = lax.broadcasted_iota(jnp.int32, (t, LANES), 0) // SLC_BLOCK
        lane = lax.broadcasted_iota(jnp.int32, (t, LANES), 1)
        kaug[:, dh:dh + LANES] = jnp.where(blk == lane, 1.0, 0.0).astype(BF16)
        _fill_vt(vt, v_ref)

    sbt = sbt_ref[0, 0]
    for hh in heads:
        qt[hh, 0:dh, :] = _q_t(q_ref, hh)
        qt[hh, dh:dh + LANES, :] = sbt

    def ktile(j):
        return kaug[pl.ds(pl.multiple_of(j * tk, tk), tk), :]

    def vtile(j):
        return vt[:, pl.ds(pl.multiple_of(j * tk, tk), tk)]

    m_ref[...] = jnp.full(m_ref.shape, NEG_LOGIT, F32)
    acc_ref[...] = jnp.zeros_like(acc_ref)
    last = (i + 1) * r - 1

    def stage(j, bias_fn, with_pv=True, with_qk=True):
        if with_pv:
            v_prev = vtile(j - 1)
            pv = [_dot(v_prev, p_buf[hh]) for hh in heads]
        else:
            k_cur = ktile(j)
            s_cur = [_dot(k_cur, qt[hh]) for hh in heads]
        if with_qk:
            k_next = ktile(j + 1)
            s_next = [_dot(k_next, qt[hh]) for hh in heads]
        else:
            v_cur = vtile(j)
        if with_pv:
            for hh in heads:
                acc_ref[hh] = a_buf[hh] * acc_ref[hh] + pv[hh]
        for hh in heads:
            s = s_buf[hh] if with_pv else s_cur[hh]
            if bias_fn is not None:
                s = s + bias_fn(hh)
            m_prev = m_ref[hh]
            m_new = jnp.maximum(m_prev, jnp.max(s, axis=0, keepdims=True))
            alpha = jnp.exp2(m_prev - m_new)
            p = jnp.exp2(s - m_new).astype(BF16)
            if with_qk:
                a_buf[hh] = alpha
                p_buf[hh] = p
                m_ref[hh] = m_new
            else:
                acc = alpha * acc_ref[hh] + _dot(v_cur, p)
                os_ref[0, hh] = _out_t(acc, _gate_row(gate_ref, g, hh, 1)).astype(BF16)
        if with_qk:
            for hh in heads:
                s_buf[hh] = s_next[hh]

    n_far = jnp.maximum(i * r + r - n_near, 0)

    def near_bias(j):
        off = pl.multiple_of((i * r + (r - 1) - j) * tk, tk)
        return lambda hh: bias_ref[0, hh, :, pl.ds(off, tq)]

    @pl.when(n_far > 0)
    def _():
        stage(0, None, with_pv=False)

    @pl.when((n_far == 0) & (last > 0))
    def _():
        stage(0, near_bias(0), with_pv=False)

    @pl.when(last == 0)
    def _():
        stage(0, near_bias(0), with_pv=False, with_qk=False)

    def far_body(j, c):
        stage(j, None)
        return c

    lax.fori_loop(1, n_far, far_body, 0)

    def near_body(j, c):
        stage(j, near_bias(j))
        return c

    lax.fori_loop(jnp.maximum(n_far, 1), last, near_body, 0)

    @pl.when(last > 0)
    def _():
        stage(last, near_bias(last), with_qk=False)


def _slc_attn(proj, gates, selbias_t, bias_s, bsz, t):
    g_, dh, tq, tk = NSA_KV_GROUPS, NSA_HEAD_DIM, SLC_TQ, SLC_TK
    nt = t // tq
    gw = NSA_HPG * dh
    n_near = (bias_s.shape[3] - tq) // tk + 1
    return pl.pallas_call(
        functools.partial(_slc_attn_kernel, n_near=n_near),
        grid=(bsz, g_, nt),
        in_specs=[
            pl.BlockSpec((tq, gw), lambda b, g, i: (b * nt + i, COL_AQ // gw + g)),
            pl.BlockSpec((1, 1, LANES, tq), lambda b, g, i: (b, g, 0, i)),
            pl.BlockSpec((t, dh), lambda b, g, i: (b, COL_SK // dh + g)),
            pl.BlockSpec((t, dh), lambda b, g, i: (b, COL_SV // dh + g)),
            pl.BlockSpec((1, NSA_HPG, tk, bias_s.shape[3]), lambda b, g, i: (g, 0, 0, 0)),
            pl.BlockSpec((GATE_COLS, tq), lambda b, g, i: (0, b * nt + i)),
        ],
        out_specs=pl.BlockSpec((1, NSA_HPG, dh, tq), lambda b, g, i: (b, g, 0, i)),
        out_shape=jax.ShapeDtypeStruct((bsz, NSA_HEADS, dh, t), BF16),
        scratch_shapes=[
            pltpu.VMEM((t, dh + LANES), BF16),
            pltpu.VMEM((VROWS, t), BF16),
            pltpu.VMEM((NSA_HPG, dh + LANES, tq), BF16),
            pltpu.VMEM((NSA_HPG, tk, tq), F32),
            pltpu.VMEM((NSA_HPG, tk, tq), BF16),
            pltpu.VMEM((NSA_HPG, 1, tq), F32),
            pltpu.VMEM((NSA_HPG, 1, tq), F32),
            pltpu.VMEM((NSA_HPG, VROWS, tq), F32),
        ],
        compiler_params=_cparams(("parallel", "parallel", "arbitrary")),
        name="slc_attn",
    )(proj, selbias_t, proj, proj, bias_s, gates)


def _win_attn_kernel(q_ref, k_ref, v_ref, bias_ref, gate_ref, oc_ref, os_ref, z_ref, ya_ref, vt, *, n_tiles):
    tq, tk, dh = WIN_TQ, ATT_TK, NSA_HEAD_DIM
    r = tq // tk
    gp = pl.program_id(1)
    i = pl.program_id(2)
    heads = [(gl, hh) for gl in range(WIN_GROUPS) for hh in range(NSA_HPG)]

    @pl.when(i == 0)
    def _():
        for gl in range(WIN_GROUPS):
            vt[gl, 0:dh, :] = v_ref[:, gl * dh:(gl + 1) * dh].astype(F32).T.astype(BF16)
            vt[gl, dh:VROWS, :] = jnp.ones((VROWS - dh, vt.shape[2]), BF16)

    js = [i * r + (r - 1) - d for d in range(n_tiles)]
    offs = [pl.multiple_of(jnp.maximum(j, 0) * tk, tk) for j in js]
    boffs = [pl.multiple_of(jnp.where(js[d] >= 0, d, n_tiles) * tk, tk) for d in range(n_tiles)]
    k_tiles = [[k_ref[pl.ds(off, tk), gl * dh:(gl + 1) * dh] for off in offs] for gl in range(WIN_GROUPS)]
    v_tiles = [[vt[gl, :, pl.ds(off, tk)] for off in offs] for gl in range(WIN_GROUPS)]
    qs = [_q_scaled(q_ref, gl * NSA_HPG + hh) for gl, hh in heads]
    ss = [[_dot_nt(k_tiles[gl][d], qs[n]) for d in range(n_tiles)] for n, (gl, hh) in enumerate(heads)]
    ps = []
    for n, (gl, hh) in enumerate(heads):
        s = [ss[n][d] + bias_ref[gl, hh, :, pl.ds(boffs[d], tq)] for d in range(n_tiles)]
        m = functools.reduce(jnp.maximum, [jnp.max(x, axis=0, keepdims=True) for x in s])
        ps.append([jnp.exp2(x - m).astype(BF16) for x in s])
    pvs = [[_dot(v_tiles[gl][d], ps[n][d]) for d in range(n_tiles)] for n, (gl, hh) in enumerate(heads)]
    for n, (gl, hh) in enumerate(heads):
        cols = slice(n * dh, (n + 1) * dh)
        o = _out_t(functools.reduce(lambda x, y: x + y, pvs[n]), _gate_row(gate_ref, gp * WIN_GROUPS + gl, hh, 2))
        o = o + oc_ref[0, n].astype(F32) + os_ref[0, n].astype(F32)
        ya_ref[:, cols] = (o.T * _silu(z_ref[:, cols].astype(F32))).astype(BF16)


def _win_attn(proj, gates, o_cmp, o_slc, bias_w, bsz, t):
    g_, dh, tq, tk = NSA_KV_GROUPS, NSA_HEAD_DIM, WIN_TQ, ATT_TK
    nt = t // tq
    ng = WIN_GROUPS
    gw = ng * NSA_HPG * dh
    kw = ng * dh
    n_tiles = (bias_w.shape[3] - tq) // tk
    return pl.pallas_call(
        functools.partial(_win_attn_kernel, n_tiles=n_tiles),
        grid=(bsz, g_ // ng, nt),
        in_specs=[
            pl.BlockSpec((tq, gw), lambda b, g, i: (b * nt + i, COL_AQ // gw + g)),
            pl.BlockSpec((t, kw), lambda b, g, i: (b, COL_WK // kw + g)),
            pl.BlockSpec((t, kw), lambda b, g, i: (b, COL_WV // kw + g)),
            pl.BlockSpec((ng, NSA_HPG, tk, bias_w.shape[3]), lambda b, g, i: (g, 0, 0, 0)),
            pl.BlockSpec((GATE_COLS, tq), lambda b, g, i: (0, b * nt + i)),
            pl.BlockSpec((1, ng * NSA_HPG, dh, tq), lambda b, g, i: (b, g, 0, i)),
            pl.BlockSpec((1, ng * NSA_HPG, dh, tq), lambda b, g, i: (b, g, 0, i)),
            pl.BlockSpec((tq, gw), lambda b, g, i: (b * nt + i, COL_AZ // gw + g)),
        ],
        out_specs=pl.BlockSpec((tq, gw), lambda b, g, i: (b * nt + i, g)),
        out_shape=jax.ShapeDtypeStruct((bsz * t, NSA_WIDTH), BF16),
        scratch_shapes=[pltpu.VMEM((ng, VROWS, t), BF16)],
        compiler_params=_cparams(("parallel", "parallel", "arbitrary")),
        name="win_attn",
    )(proj, proj, proj, bias_w, gates, o_cmp, o_slc, proj)


def _outproj_kernel(ym_ref, ya_ref, x_ref, w_ref, gain_ref, o_ref, mix):
    j = pl.program_id(1)
    nj = pl.num_programs(1)
    tn = x_ref.shape[1]

    @pl.when(j == 0)
    def _():
        mix[:, 0:MLSTM_WIDTH] = ym_ref[...]
        mix[:, MLSTM_WIDTH:D_MIX] = ya_ref[...]

    o_ref[:, pl.ds(pl.multiple_of(j * tn, tn), tn)] = x_ref[...] + _dot(mix[...], w_ref[...])

    @pl.when(j == nj - 1)
    def _():
        n_tiles = o_ref.shape[1] // tn
        ss = None
        for jj in range(n_tiles):
            y = o_ref[:, jj * tn:(jj + 1) * tn]
            part = jnp.sum(y * y, axis=-1, keepdims=True)
            ss = part if ss is None else ss + part
        inv = lax.rsqrt(ss / o_ref.shape[1] + RMS_EPS)
        for jj in range(n_tiles):
            cols = slice(jj * tn, (jj + 1) * tn)
            o_ref[:, cols] = o_ref[:, cols] * inv * gain_ref[:, cols]


def _out_proj(y_m, y_a, x2d, w_out, gain):
    n, d = x2d.shape
    tm, tn = min(OUTPROJ_TM, n), OUTPROJ_TN
    nj = d // tn
    return pl.pallas_call(
        _outproj_kernel,
        grid=(n // tm, nj),
        in_specs=[
            pl.BlockSpec((tm, MLSTM_WIDTH), lambda i, j: (i, 0)),
            pl.BlockSpec((tm, NSA_WIDTH), lambda i, j: (i, 0)),
            pl.BlockSpec((tm, tn), lambda i, j: (i, j)),
            pl.BlockSpec((D_MIX, tn), lambda i, j: (0, j)),
            pl.BlockSpec((1, d), lambda i, j: (0, 0)),
        ],
        out_specs=pl.BlockSpec((tm, d), lambda i, j: (i, 0)),
        out_shape=jax.ShapeDtypeStruct((n, d), F32),
        scratch_shapes=[pltpu.VMEM((tm, D_MIX), BF16)],
        compiler_params=_cparams(("parallel", "arbitrary")),
        name="out_proj",
    )(y_m, y_a, x2d, w_out, gain)


def _rel_bucket(dist):
    n = jnp.maximum(dist, 0)
    nf = jnp.maximum(n, REL_MAX_EXACT).astype(jnp.float32)
    large = REL_MAX_EXACT + (jnp.log(nf / REL_MAX_EXACT) / math.log(REL_MAX_DISTANCE / REL_MAX_EXACT)
                             * (REL_BUCKETS - REL_MAX_EXACT)).astype(jnp.int32)
    large = jnp.minimum(large, REL_BUCKETS - 1)
    return jnp.where(n < REL_MAX_EXACT, n, large)


def _toeplitz_vec(by_dist, base, n_pos, n_neg, lo, hi, shift=None):
    w = n_pos + n_neg
    c = np.arange(w)
    dist = np.where(c < n_pos, base + c, base - (w - c))
    ok = (dist >= lo) & (dist < hi)
    dmax = by_dist.shape[1]

    def run(start, length):
        left, right = max(0, -start), max(0, start + length - dmax)
        ext = jnp.pad(by_dist, ((0, 0), (left, right)), mode="edge")
        return lax.slice_in_dim(ext, start + left, start + left + length, axis=1)

    vals = jnp.concatenate([run(base, n_pos), run(base - n_neg, n_neg)], axis=1)
    if shift is not None:
        vals = vals - shift
    return jnp.where(ok[None], vals * LOG2E, NEG_LOGIT)


def _toeplitz_t(w_row, n_keys, n_q, key_step):
    x = jnp.broadcast_to(w_row, (n_keys, w_row.shape[1]))
    return pltpu.roll(x, 0, 1, stride=key_step, stride_axis=0)[:, 0:n_q]


def _bias_tables_kernel(wc_ref, ws_ref, ww_ref, bc_ref, bs_ref, bw_ref):
    nb, t = bc_ref.shape[2], bc_ref.shape[3]
    bc_ref[0, 0] = _toeplitz_t(wc_ref[0], nb, t, CMP_STRIDE)
    bs_ref[0, 0] = _toeplitz_t(ws_ref[0], bs_ref.shape[2], bs_ref.shape[3], 1)
    bw_ref[0, 0] = _toeplitz_t(ww_ref[0], bw_ref.shape[2], bw_ref.shape[3], 1)


def _bias_tables(rel_bias, t):
    tq, tk = ATT_TQ, ATT_TK
    g_, hpg = NSA_KV_GROUPS, NSA_HPG
    rb = rel_bias.astype(F32)
    dmax = REL_MAX_DISTANCE + 2 * max(SLC_TQ, SLC_TK, WIN_TQ, tq, tk)
    onehot = (_rel_bucket(jnp.arange(dmax, dtype=jnp.int32))[None, :] == jnp.arange(REL_BUCKETS, dtype=jnp.int32)[:, None])
    by_dist = jnp.dot(rb.T, onehot.astype(F32), precision=lax.Precision.HIGHEST)
    far = rb[REL_BUCKETS - 1][:, None]
    big = 1 << 30
    nb = t // CMP_STRIDE
    wc = _toeplitz_vec(by_dist, -(CMP_BLOCK - 1), t, t, 0, big)[:, None, :]
    sq, sk = min(SLC_TQ, t), SLC_TK
    r = sq // sk
    n_s = min(-(-(REL_MAX_DISTANCE + sk - 1) // sk) + r - 1, t // sk)
    wm = (n_s - 1) * sk + sq
    ws = _toeplitz_vec(by_dist, -(r - 1) * sk, wm, sk, 0, big, far)[:, None, :]
    wq = min(WIN_TQ, t)
    rw = wq // tk
    n_w = rw + (WINDOW - 1 + tk - 1) // tk
    wmw = n_w * tk + wq
    ww = _toeplitz_vec(by_dist, -(rw - 1) * tk, wmw, tk, 0, WINDOW)[:, None, :]
    return pl.pallas_call(
        _bias_tables_kernel,
        grid=(NSA_HEADS,),
        in_specs=[
            pl.BlockSpec((1, 1, 2 * t), lambda h: (h, 0, 0)),
            pl.BlockSpec((1, 1, wm + sk), lambda h: (h, 0, 0)),
            pl.BlockSpec((1, 1, wmw + tk), lambda h: (h, 0, 0)),
        ],
        out_specs=[
            pl.BlockSpec((1, 1, nb, t), lambda h: (h // hpg, h % hpg, 0, 0)),
            pl.BlockSpec((1, 1, sk, wm), lambda h: (h // hpg, h % hpg, 0, 0)),
            pl.BlockSpec((1, 1, tk, wmw), lambda h: (h // hpg, h % hpg, 0, 0)),
        ],
        out_shape=[
            jax.ShapeDtypeStruct((g_, hpg, nb, t), F32),
            jax.ShapeDtypeStruct((g_, hpg, sk, wm), F32),
            jax.ShapeDtypeStruct((g_, hpg, tk, wmw), F32),
        ],
        compiler_params=_cparams(("parallel",)),
        name="bias_tables",
    )(wc, ws, ww)


def _cover_t(t):
    nb = t // CMP_STRIDE
    n_cmp = (t - CMP_BLOCK) // CMP_STRIDE + 1
    n_slc = t // SLC_BLOCK
    cs = np.arange(nb) * CMP_STRIDE
    ss = np.arange(n_slc) * SLC_BLOCK
    cover = np.clip(np.minimum(cs[:, None] + CMP_BLOCK, ss[None, :] + SLC_BLOCK)
                    - np.maximum(cs[:, None], ss[None, :]), 0, None) / CMP_BLOCK
    cover[n_cmp:] = 0.0
    return jnp.asarray(cover.T, dtype=BF16)


def kernel(x, norm_gain, w_in, w_conv, b_igate, b_fgate, mlstm_norm_gain, cmp_k_pos, cmp_k_w1, cmp_k_w2,
           cmp_v_pos, cmp_v_w1, cmp_v_w2, rel_bias, w_out, final_norm_gain):
    bsz, t, d = x.shape
    assert d == D_MODEL and t % MLSTM_L == 0 and t % ATT_TQ == 0 and (t // CMP_STRIDE) % LANES == 0
    n = bsz * t
    x2d = x.reshape(n, d)

    w_main, w_gate = _w_prep(w_in.T)
    proj, gates = _in_proj(x2d, norm_gain.reshape(1, d).astype(F32), w_main, w_gate)

    g_rows = gates[:, :2 * MLSTM_HEADS].reshape(bsz, t, 2, MLSTM_HEADS).transpose(0, 3, 2, 1)
    y_m = _mlstm(proj, g_rows, w_conv.astype(F32), b_igate.astype(F32), b_fgate.astype(F32),
                 mlstm_norm_gain.reshape(1, MLSTM_WIDTH).astype(F32), bsz, t)

    dh = NSA_HEAD_DIM
    half = CMP_BLOCK // 2

    def w1cat(w1):
        return jnp.concatenate([w1[:half].reshape(half * dh, dh), w1[half:].reshape(half * dh, dh)], axis=1).astype(BF16)

    k_cmp, v_cmp_t = _compress(
        proj, w1cat(cmp_k_w1), cmp_k_w2.astype(BF16), cmp_k_pos.reshape(2, half * dh).astype(BF16),
        w1cat(cmp_v_w1), cmp_v_w2.astype(BF16), cmp_v_pos.reshape(2, half * dh).astype(BF16), bsz, t)

    bias_c, bias_s, bias_w = _bias_tables(rel_bias, t)
    gates_t = gates.T
    o_cmp, selbias_t = _cmp_attn(proj, gates_t, k_cmp, v_cmp_t, bias_c, _cover_t(t), bsz, t)
    o_slc = _slc_attn(proj, gates_t, selbias_t, bias_s, bsz, t)
    y_a = _win_attn(proj, gates_t, o_cmp, o_slc, bias_w, bsz, t)

    out = _out_proj(y_m, y_a, x2d, w_out.astype(BF16), final_norm_gain.reshape(1, d).astype(F32))
    return out.reshape(bsz, t, d)
```

```python
import functools
import math

import jax
import jax.numpy as jnp
import numpy as np
from jax import lax
from jax.experimental import pallas as pl
from jax.experimental.pallas import tpu as pltpu

F32 = jnp.float32
BF16 = jnp.bfloat16

D_MODEL = 4096
D_MIX = D_MODEL
MLSTM_WIDTH = D_MIX // 2
MLSTM_HEADS = 4
MLSTM_V_DIM = MLSTM_WIDTH // MLSTM_HEADS
MLSTM_QK_DIM = MLSTM_V_DIM // 2
MLSTM_QK_WIDTH = MLSTM_HEADS * MLSTM_QK_DIM
CONV_WIDTH = 4
NSA_WIDTH = D_MIX - MLSTM_WIDTH
NSA_HEAD_DIM = 128
NSA_HEADS = NSA_WIDTH // NSA_HEAD_DIM
NSA_KV_GROUPS = 4
NSA_HPG = NSA_HEADS // NSA_KV_GROUPS
NSA_KV_WIDTH = NSA_KV_GROUPS * NSA_HEAD_DIM
CMP_BLOCK = 32
CMP_STRIDE = 16
SLC_BLOCK = 64
SLC_TOP_N = 16
WINDOW = 512
FORCE_BONUS = 1000.0
REL_BUCKETS = 32
REL_MAX_EXACT = REL_BUCKETS // 2
REL_MAX_DISTANCE = 1024
RMS_EPS = 1e-6
NEG_LOGIT = -1e30

LANES = 128
VMEM_LIMIT_BYTES = 56 * 1024 * 1024

COL_MQ = 0
COL_MK = COL_MQ + MLSTM_QK_WIDTH
COL_MV = COL_MK + MLSTM_QK_WIDTH
COL_MO = COL_MV + MLSTM_WIDTH
COL_MZ = COL_MO + MLSTM_WIDTH
COL_AQ = COL_MZ + MLSTM_WIDTH
COL_CK = COL_AQ + NSA_WIDTH
COL_CV = COL_CK + NSA_KV_WIDTH
COL_SK = COL_CV + NSA_KV_WIDTH
COL_SV = COL_SK + NSA_KV_WIDTH
COL_WK = COL_SV + NSA_KV_WIDTH
COL_WV = COL_WK + NSA_KV_WIDTH
COL_AZ = COL_WV + NSA_KV_WIDTH
MAIN_COLS = COL_AZ + NSA_WIDTH
GATE_COLS = LANES
GATE_NSA = 2 * MLSTM_HEADS

INPROJ_TM = 512
INPROJ_TN = 1536
OUTPROJ_TM = 512
OUTPROJ_TN = 1024
MLSTM_L = 256
ATT_TQ = 256
ATT_TK = 256
WPREP_TR = 1024
SLC_TQ = 512
SLC_TK = 512
CMP_TQ = 512
CMP_GROUPS = 2
WIN_TQ = 256
WIN_GROUPS = 2
MASK_BIG = 2.0 ** 100


def _cparams(sem):
    return pltpu.CompilerParams(dimension_semantics=sem, vmem_limit_bytes=VMEM_LIMIT_BYTES)


def _sigmoid(x):
    return 0.5 * jnp.tanh(0.5 * x) + 0.5


def _silu(x):
    return x * _sigmoid(x)


def _dot(a, b):
    return jnp.dot(a, b, preferred_element_type=F32)


def _dot_nt(a, b):
    return lax.dot_general(a, b, (((1,), (1,)), ((), ())), preferred_element_type=F32)


W_OFF_I = COL_MZ + MLSTM_WIDTH
W_OFF_AQ = W_OFF_I + 2 * MLSTM_HEADS
W_OFF_GATE = W_OFF_AQ + NSA_WIDTH + 6 * NSA_KV_WIDTH
W_OFF_AZ = W_OFF_GATE + 3 * NSA_HEADS


def _wprep_kernel(w_ref, gi_ref, ga_ref, o_ref, g_ref):
    o_ref[...] = w_ref[...].astype(BF16)

    @pl.when(pl.program_id(0) == 0)
    def _():
        pad = jnp.zeros((GATE_COLS - gi_ref.shape[0] - ga_ref.shape[0], g_ref.shape[1]), F32)
        g_ref[...] = jnp.concatenate([gi_ref[...], ga_ref[...], pad], axis=0).astype(BF16)


def _w_prep(w_t):
    rows, d = w_t.shape
    tr = WPREP_TR
    nb_a, nb_b = W_OFF_I // tr, (W_OFF_GATE - W_OFF_AQ) // tr

    def src(m):
        skip = jnp.where(m >= nb_a, W_OFF_AQ - W_OFF_I, 0) + jnp.where(m >= nb_a + nb_b, W_OFF_AZ - W_OFF_GATE, 0)
        return pl.multiple_of(m * tr + skip, 8)

    rows_at = lambda start, size: pl.BlockSpec((pl.Element(size), pl.Element(d)), lambda m: (start, 0))
    return pl.pallas_call(
        _wprep_kernel,
        grid=(MAIN_COLS // tr,),
        in_specs=[
            pl.BlockSpec((pl.Element(tr), pl.Element(d)), lambda m: (src(m), 0)),
            rows_at(W_OFF_I, W_OFF_AQ - W_OFF_I),
            rows_at(W_OFF_GATE, W_OFF_AZ - W_OFF_GATE),
        ],
        out_specs=[pl.BlockSpec((tr, d), lambda m: (m, 0)), pl.BlockSpec((GATE_COLS, d), lambda m: (0, 0))],
        out_shape=[jax.ShapeDtypeStruct((MAIN_COLS, d), BF16), jax.ShapeDtypeStruct((GATE_COLS, d), BF16)],
        compiler_params=_cparams(("arbitrary",)),
        name="w_prep",
    )(w_t, w_t, w_t)


def _inproj_kernel(x_ref, gain_ref, w_hbm, wg_ref, o_hbm, og_ref, h_ref, wbuf, obuf, w_sem, o_sem):
    i = pl.program_id(0)
    ni = pl.num_programs(0)
    tm, tn = obuf.shape[1], obuf.shape[2]
    nj = w_hbm.shape[0] // tn

    def w_copy(j, slot):
        return pltpu.make_async_copy(w_hbm.at[pl.ds(pl.multiple_of(j * tn, tn), tn), :], wbuf.at[slot], w_sem.at[slot])

    def o_copy(j, slot):
        dst = o_hbm.at[pl.ds(pl.multiple_of(i * tm, tm), tm), pl.ds(pl.multiple_of(j * tn, tn), tn)]
        return pltpu.make_async_copy(obuf.at[slot], dst, o_sem.at[slot])

    @pl.when(i == 0)
    def _():
        w_copy(0, 0).start()

    x = x_ref[...]
    ms = jnp.mean(x * x, axis=-1, keepdims=True)
    h = (x * lax.rsqrt(ms + RMS_EPS) * gain_ref[...]).astype(BF16)
    h_ref[...] = h
    og_ref[...] = _dot_nt(h, wg_ref[...])

    def body(j, carry):
        slot = j % 2
        nxt = (j + 1) % nj
        w_copy(j, slot).wait()

        @pl.when((j + 1 < nj) | (i + 1 < ni))
        def _():
            w_copy(nxt, 1 - slot).start()

        @pl.when((j >= 2) | (i > 0))
        def _():
            o_copy(j, slot).wait()

        obuf[slot] = _dot_nt(h_ref[...], wbuf[slot]).astype(BF16)
        o_copy(j, slot).start()
        return carry

    lax.fori_loop(0, nj, body, 0)

    @pl.when(i == ni - 1)
    def _():
        o_copy(nj - 2, (nj - 2) % 2).wait()
        o_copy(nj - 1, (nj - 1) % 2).wait()


def _in_proj(x2d, gain, w_main, w_gate):
    n, d = x2d.shape
    tm, tn = min(INPROJ_TM, n), INPROJ_TN
    assert (MAIN_COLS // tn) % 2 == 0
    return pl.pallas_call(
        _inproj_kernel,
        grid=(n // tm,),
        in_specs=[
            pl.BlockSpec((tm, d), lambda i: (i, 0)),
            pl.BlockSpec((1, d), lambda i: (0, 0)),
            pl.BlockSpec(memory_space=pl.ANY),
            pl.BlockSpec((GATE_COLS, d), lambda i: (0, 0)),
        ],
        out_specs=[
            pl.BlockSpec(memory_space=pl.ANY),
            pl.BlockSpec((tm, GATE_COLS), lambda i: (i, 0)),
        ],
        out_shape=[
            jax.ShapeDtypeStruct((n, MAIN_COLS), BF16),
            jax.ShapeDtypeStruct((n, GATE_COLS), F32),
        ],
        scratch_shapes=[
            pltpu.VMEM((tm, d), BF16),
            pltpu.VMEM((2, tn, d), BF16),
            pltpu.VMEM((2, tm, tn), BF16),
            pltpu.SemaphoreType.DMA((2,)),
            pltpu.SemaphoreType.DMA((2,)),
        ],
        compiler_params=_cparams(("arbitrary",)),
        name="in_proj",
    )(x2d, gain, w_main, w_gate)


def _mlstm_kernel(bi_ref, bf_ref, q_ref, k_ref, v_ref, o_ref, z_ref, g_ref, wq_ref, wk_ref, ng_ref,
                  y_ref, qext, kext, c_st, n_st, m_st):
    L = MLSTM_L
    HIST = 8
    dqk, dv = MLSTM_QK_DIM, MLSTM_V_DIM

    @pl.when(pl.program_id(1) == 0)
    def _():
        qext[:, 0:HIST, :] = jnp.zeros((MLSTM_HEADS, HIST, dqk), F32)
        kext[:, 0:HIST, :] = jnp.zeros((MLSTM_HEADS, HIST, dqk), F32)
        c_st[...] = jnp.zeros_like(c_st)
        n_st[...] = jnp.zeros_like(n_st)
        m_st[...] = jnp.zeros_like(m_st)

    rr = lax.broadcasted_iota(jnp.int32, (L, L), 0)
    cc = lax.broadcasted_iota(jnp.int32, (L, L), 1)
    upper = (rr <= cc).astype(F32)

    for hd in range(MLSTM_HEADS):
        qcols = slice(hd * dqk, (hd + 1) * dqk)
        vcols = slice(hd * dv, (hd + 1) * dv)
        qext[hd, HIST:HIST + L, :] = q_ref[:, qcols].astype(F32)
        kext[hd, HIST:HIST + L, :] = k_ref[:, qcols].astype(F32)

        def conv_silu(ext, w_ref):
            w = w_ref[:, qcols]
            y = ext[hd, pl.ds(HIST, L), :] * w[CONV_WIDTH - 1:CONV_WIDTH, :]
            for s in range(1, CONV_WIDTH):
                y = y + ext[hd, pl.ds(HIST - s, L), :] * w[CONV_WIDTH - 1 - s:CONV_WIDTH - s, :]
            return _silu(y)

        qc = conv_silu(qext, wq_ref)
        kc = conv_silu(kext, wk_ref) * (dqk ** -0.5)
        qext[hd, 0:HIST, :] = qext[hd, L:L + HIST, :]
        kext[hd, 0:HIST, :] = kext[hd, L:L + HIST, :]

        g = g_ref[0, hd]
        i_row = g[0:1, :] + bi_ref[hd]
        f_row = g[1:2, :] + bf_ref[hd]
        lf_row = jnp.minimum(f_row, 0.0) - jnp.log(1.0 + jnp.exp(-jnp.abs(f_row)))

        bcum_row = jnp.dot(jnp.broadcast_to(lf_row, (8, L)), upper, preferred_element_type=F32,
                           precision=lax.Precision.HIGHEST)[0:1, :]
        bcum_col = jnp.sum(jnp.where(rr == cc, bcum_row, 0.0), axis=1, keepdims=True)
        gsum = bcum_row[:, L - 1:L]
        m_prev = m_st[hd]

        dlog = jnp.where(rr >= cc, bcum_col - bcum_row + i_row, -jnp.inf)
        m_inter = bcum_col + m_prev
        m_t = jnp.maximum(m_inter, jnp.max(dlog, axis=1, keepdims=True))
        dmat = jnp.exp(dlog - m_t)
        inter = jnp.exp(m_inter - m_t)

        qb = qc.astype(BF16)
        kct = kc.T
        vb = v_ref[:, vcols]
        s = _dot(qb, kct.astype(BF16)) * dmat
        c_prev = c_st[hd]
        n_prev = n_st[hd]
        num = _dot(s.astype(BF16), vb) + inter * _dot(qb, c_prev.astype(BF16))
        qn = jnp.sum(s, axis=1, keepdims=True) + inter * jnp.sum(qc * n_prev, axis=1, keepdims=True)
        hh = num / jnp.maximum(jnp.abs(qn), jnp.exp(-m_t))

        wlog = gsum - bcum_row + i_row
        m_next = jnp.maximum(gsum + m_prev, jnp.max(wlog, axis=1, keepdims=True))
        wts = jnp.exp(wlog - m_next)
        keep = jnp.exp(gsum + m_prev - m_next)
        c_st[hd] = keep * c_prev + _dot((kct * wts).astype(BF16), vb)
        n_st[hd] = keep * n_prev + _dot(jnp.broadcast_to(wts, (8, L)).astype(BF16), kc.astype(BF16))[0:1, :]
        m_st[hd] = m_next

        hm = _sigmoid(o_ref[:, vcols].astype(F32)) * hh
        hm = hm * lax.rsqrt(jnp.mean(hm * hm, axis=-1, keepdims=True) + RMS_EPS)
        hm = hm * ng_ref[:, vcols]
        y_ref[:, vcols] = (hm * _silu(z_ref[:, vcols].astype(F32))).astype(BF16)


def _mlstm(proj, g_rows, w_conv, b_igate, b_fgate, norm_gain, bsz, t):
    L = MLSTM_L
    nc = t // L
    nh, dqk, dv = MLSTM_HEADS, MLSTM_QK_DIM, MLSTM_V_DIM
    row = lambda b, c: b * nc + c
    smem = pl.BlockSpec(memory_space=pltpu.SMEM)
    return pl.pallas_call(
        _mlstm_kernel,
        grid=(bsz, nc),
        in_specs=[
            smem, smem,
            pl.BlockSpec((L, MLSTM_QK_WIDTH), lambda b, c: (row(b, c), COL_MQ // MLSTM_QK_WIDTH)),
            pl.BlockSpec((L, MLSTM_QK_WIDTH), lambda b, c: (row(b, c), COL_MK // MLSTM_QK_WIDTH)),
            pl.BlockSpec((L, MLSTM_WIDTH), lambda b, c: (row(b, c), COL_MV // MLSTM_WIDTH)),
            pl.BlockSpec((L, MLSTM_WIDTH), lambda b, c: (row(b, c), COL_MO // MLSTM_WIDTH)),
            pl.BlockSpec((L, MLSTM_WIDTH), lambda b, c: (row(b, c), COL_MZ // MLSTM_WIDTH)),
            pl.BlockSpec((1, nh, 2, L), lambda b, c: (b, 0, 0, c)),
            pl.BlockSpec((CONV_WIDTH, MLSTM_QK_WIDTH), lambda b, c: (0, 0)),
            pl.BlockSpec((CONV_WIDTH, MLSTM_QK_WIDTH), lambda b, c: (0, 1)),
            pl.BlockSpec((1, MLSTM_WIDTH), lambda b, c: (0, 0)),
        ],
        out_specs=pl.BlockSpec((L, MLSTM_WIDTH), lambda b, c: (row(b, c), 0)),
        out_shape=jax.ShapeDtypeStruct((bsz * t, MLSTM_WIDTH), BF16),
        scratch_shapes=[
            pltpu.VMEM((nh, L + 8, dqk), F32),
            pltpu.VMEM((nh, L + 8, dqk), F32),
            pltpu.VMEM((nh, dqk, dv), F32),
            pltpu.VMEM((nh, 1, dqk), F32),
            pltpu.VMEM((nh, 1, 1), F32),
        ],
        compiler_params=_cparams(("parallel", "arbitrary")),
        name="mlstm",
    )(b_igate, b_fgate, proj, proj, proj, proj, proj, g_rows, w_conv, w_conv, norm_gain)


def _compress_kernel(ck_ref, cv_ref, w1k_ref, w2k_ref, pk_ref, w1v_ref, w2v_ref, pv_ref,
                     ok_ref, ov_ref, xf, xcat):
    t = ck_ref.shape[0]
    nb = t // CMP_STRIDE
    dh = NSA_HEAD_DIM

    def one(src_ref, w1_ref, w2_ref, pos_ref):
        xf[...] = src_ref[...].astype(F32)
        for l in range(CMP_STRIDE):
            xcat[:, l * dh:(l + 1) * dh] = xf[pl.ds(l, nb, stride=CMP_STRIDE), :].astype(BF16)
        w1 = w1_ref[...]
        ab = _dot(xcat[...], w1)
        pp = _dot(pos_ref[...], w1)
        pos_term = pp[0:1, 0:dh] + pp[1:2, dh:2 * dh]
        second = pltpu.roll(ab[:, dh:2 * dh], nb - 1, 0)
        hid = _silu(ab[:, 0:dh] + second + pos_term)
        return _dot(hid.astype(BF16), w2_ref[...])

    ok_ref[0, 0] = one(ck_ref, w1k_ref, w2k_ref, pk_ref).astype(BF16)
    ov_ref[0, 0] = one(cv_ref, w1v_ref, w2v_ref, pv_ref).T.astype(BF16)


def _compress(proj, w1k, w2k, pk, w1v, w2v, pv, bsz, t):
    g_, dh = NSA_KV_GROUPS, NSA_HEAD_DIM
    nb = t // CMP_STRIDE
    full = lambda a: pl.BlockSpec(a.shape, lambda b, g: (0,) * a.ndim)
    k_spec = pl.BlockSpec((1, 1, nb, dh), lambda b, g: (b, g, 0, 0))
    vt_spec = pl.BlockSpec((1, 1, dh, nb), lambda b, g: (b, g, 0, 0))
    return pl.pallas_call(
        _compress_kernel,
        grid=(bsz, g_),
        in_specs=[
            pl.BlockSpec((t, dh), lambda b, g: (b, COL_CK // dh + g)),
            pl.BlockSpec((t, dh), lambda b, g: (b, COL_CV // dh + g)),
            full(w1k), full(w2k), full(pk), full(w1v), full(w2v), full(pv),
        ],
        out_specs=[k_spec, vt_spec],
        out_shape=[jax.ShapeDtypeStruct((bsz, g_, nb, dh), BF16), jax.ShapeDtypeStruct((bsz, g_, dh, nb), BF16)],
        scratch_shapes=[pltpu.VMEM((t, dh), F32), pltpu.VMEM((nb, CMP_STRIDE * dh), BF16)],
        compiler_params=_cparams(("parallel", "parallel")),
        name="compress",
    )(proj, proj, w1k, w2k, pk, w1v, w2v, pv)


LOG2E = math.log2(math.e)
VROWS = NSA_HEAD_DIM + 16


def _q_scaled(q_ref, hh):
    dh = NSA_HEAD_DIM
    return (q_ref[:, hh * dh:(hh + 1) * dh].astype(F32) * (dh ** -0.5 * LOG2E)).astype(BF16)


def _q_t(q_ref, hh):
    dh = NSA_HEAD_DIM
    return (q_ref[:, hh * dh:(hh + 1) * dh].astype(F32) * (dh ** -0.5 * LOG2E)).T.astype(BF16)


def _gate_row(gt_ref, g, hh, branch):
    row = GATE_NSA + 3 * (g * NSA_HPG + hh) + branch
    return _sigmoid(gt_ref[pl.ds(row, 1), :])


def _fill_vt(vt, v_ref):
    dh = NSA_HEAD_DIM
    vt[0:dh, :] = v_ref[...].astype(F32).T.astype(BF16)
    vt[dh:VROWS, :] = jnp.ones((VROWS - dh, vt.shape[1]), BF16)


def _out_t(acc, gate_row):
    dh = NSA_HEAD_DIM
    return acc[0:dh, :] * (gate_row / acc[dh:dh + 1, :])


def _cmp_attn_kernel(q_ref, kc_ref, vct_ref, bias_ref, cov_ref, gate_ref, oc_ref, sel_ref, score_ref, *, n_sel):
    tq = q_ref.shape[0]
    dh = NSA_HEAD_DIM
    n_slc = cov_ref.shape[0]
    t0 = pl.program_id(1) * tq
    gp = pl.program_id(0)
    heads = [(gl, hh) for gl in range(CMP_GROUPS) for hh in range(NSA_HPG)]
    qk = [_dot_nt(kc_ref[0, gl], _q_scaled(q_ref, gl * NSA_HPG + hh)) for gl, hh in heads]
    ps = []
    for n, (gl, hh) in enumerate(heads):
        logit = qk[n] + bias_ref[gl, hh]
        m = jnp.max(logit, axis=0, keepdims=True)
        e = jnp.exp2(logit - m)
        inv = jnp.where(m > 0.5 * NEG_LOGIT, 1.0 / jnp.sum(e, axis=0, keepdims=True), 0.0)
        ps.append(e * inv)
    ots = [_dot(vct_ref[0, gl], ps[n].astype(BF16)) for n, (gl, hh) in enumerate(heads)]
    for n, (gl, hh) in enumerate(heads):
        oc_ref[0, n] = (ots[n] * _gate_row(gate_ref, gp * CMP_GROUPS + gl, hh, 0)).astype(BF16)

    cov = cov_ref[...]
    jb = lax.broadcasted_iota(jnp.int32, (n_slc, tq), 0)
    cur = (t0 + lax.broadcasted_iota(jnp.int32, (n_slc, tq), 1)) // SLC_BLOCK
    valid = jb <= cur
    forced = (jb == 0) | (jb == cur) | (jb == cur - 1)
    sub = 8
    row_id = lax.broadcasted_iota(jnp.int32, (sub, tq), 0)
    for gl in range(CMP_GROUPS):
        g4 = ps[gl * NSA_HPG:(gl + 1) * NSA_HPG]
        p_sum = (g4[0] + g4[1]) + (g4[2] + g4[3])
        p_hi = p_sum.astype(BF16)
        r1 = p_sum - p_hi.astype(F32)
        p_mid = r1.astype(BF16)
        p_lo = (r1 - p_mid.astype(F32)).astype(BF16)
        st = _dot(cov, p_hi) + _dot(cov, p_mid) + _dot(cov, p_lo)
        score_ref[gl] = jnp.where(valid, st + jnp.where(forced, FORCE_BONUS, 0.0), -1.0)
        groups = [score_ref[gl, r:r + sub, :] for r in range(0, n_slc, sub)]
        ranks = [jnp.zeros((sub, tq), F32) for _ in groups]
        for j2 in range(n_slc):
            row = score_ref[gl, j2:j2 + 1, :]
            for gi, sc in enumerate(groups):
                r0 = gi * sub
                if r0 > j2:
                    inc = jnp.where(row >= sc, 1.0, 0.0)
                elif r0 + sub - 1 <= j2:
                    inc = jnp.where(row > sc, 1.0, 0.0)
                else:
                    inc = jnp.where(row_id > j2 - r0, jnp.where(row >= sc, 1.0, 0.0), jnp.where(row > sc, 1.0, 0.0))
                ranks[gi] = ranks[gi] + inc
        for gi, rk in enumerate(ranks):
            score_ref[gl, gi * sub:(gi + 1) * sub, :] = rk
        sel = valid & (score_ref[gl] < n_sel)
        sel_ref[0, gl, 0:n_slc, :] = jnp.where(sel, 0.0, -MASK_BIG).astype(BF16)
        if n_slc < LANES:
            sel_ref[0, gl, n_slc:LANES, :] = jnp.zeros((LANES - n_slc, tq), BF16)


def _cmp_attn(proj, gates, k_cmp, v_cmp_t, bias_c, cover_t, bsz, t):
    g_, dh, tq = NSA_KV_GROUPS, NSA_HEAD_DIM, CMP_TQ
    nt = t // tq
    nb = t // CMP_STRIDE
    n_slc = t // SLC_BLOCK
    ng = CMP_GROUPS
    gw = ng * NSA_HPG * dh
    return pl.pallas_call(
        functools.partial(_cmp_attn_kernel, n_sel=min(SLC_TOP_N, n_slc)),
        grid=(g_ // ng, nt, bsz),
        in_specs=[
            pl.BlockSpec((tq, gw), lambda g, i, b: (b * nt + i, COL_AQ // gw + g)),
            pl.BlockSpec((1, ng, nb, dh), lambda g, i, b: (b, g, 0, 0)),
            pl.BlockSpec((1, ng, dh, nb), lambda g, i, b: (b, g, 0, 0)),
            pl.BlockSpec((ng, NSA_HPG, nb, tq), lambda g, i, b: (g, 0, 0, i)),
            pl.BlockSpec((n_slc, nb), lambda g, i, b: (0, 0)),
            pl.BlockSpec((GATE_COLS, tq), lambda g, i, b: (0, b * nt + i)),
        ],
        out_specs=[
            pl.BlockSpec((1, ng * NSA_HPG, dh, tq), lambda g, i, b: (b, g, 0, i)),
            pl.BlockSpec((1, ng, LANES, tq), lambda g, i, b: (b, g, 0, i)),
        ],
        out_shape=[
            jax.ShapeDtypeStruct((bsz, NSA_HEADS, dh, t), BF16),
            jax.ShapeDtypeStruct((bsz, g_, LANES, t), BF16),
        ],
        scratch_shapes=[pltpu.VMEM((ng, n_slc, tq), F32)],
        compiler_params=_cparams(("parallel", "parallel", "parallel")),
        name="cmp_attn",
    )(proj, k_cmp, v_cmp_t, bias_c, cover_t, gates)


def _slc_attn_kernel(q_ref, sbt_ref, k_ref, v_ref, bias_ref, gate_ref, os_ref,
                     kaug, vt, qt, s_buf, p_buf, a_buf, m_ref, acc_ref, *, n_near):
    tq, tk, dh = SLC_TQ, SLC_TK, NSA_HEAD_DIM
    r = tq // tk
    t = k_ref.shape[0]
    g = pl.program_id(1)
    i = pl.program_id(2)
    heads = range(NSA_HPG)

    @pl.when(i == 0)
    def _():
        kaug[:, 0:dh] = k_ref[...]
        blk = lax.broadcasted_iota(jnp.int32, (t, LANES), 0) // SLC_BLOCK
        lane = lax.broadcasted_iota(jnp.int32, (t, LANES), 1)
        kaug[:, dh:dh + LANES] = jnp.where(blk == lane, 1.0, 0.0).astype(BF16)
        _fill_vt(vt, v_ref)

    sbt = sbt_ref[0, 0]
    for hh in heads:
        qt[hh, 0:dh, :] = _q_t(q_ref, hh)
        qt[hh, dh:dh + LANES, :] = sbt

    def ktile(j):
        return kaug[pl.ds(pl.multiple_of(j * tk, tk), tk), :]

    def vtile(j):
        return vt[:, pl.ds(pl.multiple_of(j * tk, tk), tk)]

    m_ref[...] = jnp.full(m_ref.shape, NEG_LOGIT, F32)
    acc_ref[...] = jnp.zeros_like(acc_ref)
    k0 = ktile(0)
    for hh in heads:
        s_buf[hh] = _dot(k0, qt[hh])

    last = (i + 1) * r - 1

    def stage(j, bias_fn, with_pv=True, with_qk=True):
        if with_pv:
            v_prev = vtile(j - 1)
            pv = [_dot(v_prev, p_buf[hh]) for hh in heads]
        if with_qk:
            k_next = ktile(j + 1)
            s_next = [_dot(k_next, qt[hh]) for hh in heads]
        if with_pv:
            for hh in heads:
                acc_ref[hh] = a_buf[hh] * acc_ref[hh] + pv[hh]
        for hh in heads:
            s = s_buf[hh]
            if bias_fn is not None:
                s = s + bias_fn(hh)
            m_prev = m_ref[hh]
            m_new = jnp.maximum(m_prev, jnp.max(s, axis=0, keepdims=True))
            a_buf[hh] = jnp.exp2(m_prev - m_new)
            p_buf[hh] = jnp.exp2(s - m_new).astype(BF16)
            m_ref[hh] = m_new
        if with_qk:
            for hh in heads:
                s_buf[hh] = s_next[hh]

    n_far = jnp.maximum(i * r + r - n_near, 0)

    def near_bias(j):
        off = pl.multiple_of((i * r + (r - 1) - j) * tk, tk)
        return lambda hh: bias_ref[0, hh, :, pl.ds(off, tq)]

    @pl.when(n_far > 0)
    def _():
        stage(0, None, with_pv=False)

    @pl.when((n_far == 0) & (last > 0))
    def _():
        stage(0, near_bias(0), with_pv=False)

    @pl.when(last == 0)
    def _():
        stage(0, near_bias(0), with_pv=False, with_qk=False)

    def far_body(j, c):
        stage(j, None)
        return c

    lax.fori_loop(1, n_far, far_body, 0)

    def near_body(j, c):
        stage(j, near_bias(j))
        return c

    lax.fori_loop(jnp.maximum(n_far, 1), last, near_body, 0)

    @pl.when(last > 0)
    def _():
        stage(last, near_bias(last), with_qk=False)

    v_last = vtile(last)
    pv = [_dot(v_last, p_buf[hh]) for hh in heads]
    for hh in heads:
        acc = a_buf[hh] * acc_ref[hh] + pv[hh]
        os_ref[0, hh] = _out_t(acc, _gate_row(gate_ref, g, hh, 1)).astype(BF16)


def _slc_attn(proj, gates, selbias_t, bias_s, bsz, t):
    g_, dh, tq, tk = NSA_KV_GROUPS, NSA_HEAD_DIM, SLC_TQ, SLC_TK
    nt = t // tq
    gw = NSA_HPG * dh
    n_near = (bias_s.shape[3] - tq) // tk + 1
    return pl.pallas_call(
        functools.partial(_slc_attn_kernel, n_near=n_near),
        grid=(bsz, g_, nt),
        in_specs=[
            pl.BlockSpec((tq, gw), lambda b, g, i: (b * nt + i, COL_AQ // gw + g)),
            pl.BlockSpec((1, 1, LANES, tq), lambda b, g, i: (b, g, 0, i)),
            pl.BlockSpec((t, dh), lambda b, g, i: (b, COL_SK // dh + g)),
            pl.BlockSpec((t, dh), lambda b, g, i: (b, COL_SV // dh + g)),
            pl.BlockSpec((1, NSA_HPG, tk, bias_s.shape[3]), lambda b, g, i: (g, 0, 0, 0)),
            pl.BlockSpec((GATE_COLS, tq), lambda b, g, i: (0, b * nt + i)),
        ],
        out_specs=pl.BlockSpec((1, NSA_HPG, dh, tq), lambda b, g, i: (b, g, 0, i)),
        out_shape=jax.ShapeDtypeStruct((bsz, NSA_HEADS, dh, t), BF16),
        scratch_shapes=[
            pltpu.VMEM((t, dh + LANES), BF16),
            pltpu.VMEM((VROWS, t), BF16),
            pltpu.VMEM((NSA_HPG, dh + LANES, tq), BF16),
            pltpu.VMEM((NSA_HPG, tk, tq), F32),
            pltpu.VMEM((NSA_HPG, tk, tq), BF16),
            pltpu.VMEM((NSA_HPG, 1, tq), F32),
            pltpu.VMEM((NSA_HPG, 1, tq), F32),
            pltpu.VMEM((NSA_HPG, VROWS, tq), F32),
        ],
        compiler_params=_cparams(("parallel", "parallel", "arbitrary")),
        name="slc_attn",
    )(proj, selbias_t, proj, proj, bias_s, gates)


def _win_attn_kernel(q_ref, k_ref, v_ref, bias_ref, gate_ref, oc_ref, os_ref, z_ref, ya_ref, vt, *, n_tiles):
    tq, tk, dh = WIN_TQ, ATT_TK, NSA_HEAD_DIM
    r = tq // tk
    gp = pl.program_id(1)
    i = pl.program_id(2)
    heads = [(gl, hh) for gl in range(WIN_GROUPS) for hh in range(NSA_HPG)]

    @pl.when(i == 0)
    def _():
        for gl in range(WIN_GROUPS):
            vt[gl, 0:dh, :] = v_ref[:, gl * dh:(gl + 1) * dh].astype(F32).T.astype(BF16)
            vt[gl, dh:VROWS, :] = jnp.ones((VROWS - dh, vt.shape[2]), BF16)

    js = [i * r + (r - 1) - d for d in range(n_tiles)]
    offs = [pl.multiple_of(jnp.maximum(j, 0) * tk, tk) for j in js]
    boffs = [pl.multiple_of(jnp.where(js[d] >= 0, d, n_tiles) * tk, tk) for d in range(n_tiles)]
    k_tiles = [[k_ref[pl.ds(off, tk), gl * dh:(gl + 1) * dh] for off in offs] for gl in range(WIN_GROUPS)]
    v_tiles = [[vt[gl, :, pl.ds(off, tk)] for off in offs] for gl in range(WIN_GROUPS)]
    qs = [_q_scaled(q_ref, gl * NSA_HPG + hh) for gl, hh in heads]
    ss = [[_dot_nt(k_tiles[gl][d], qs[n]) for d in range(n_tiles)] for n, (gl, hh) in enumerate(heads)]
    ps = []
    for n, (gl, hh) in enumerate(heads):
        s = [ss[n][d] + bias_ref[gl, hh, :, pl.ds(boffs[d], tq)] for d in range(n_tiles)]
        m = functools.reduce(jnp.maximum, [jnp.max(x, axis=0, keepdims=True) for x in s])
        ps.append([jnp.exp2(x - m).astype(BF16) for x in s])
    pvs = [[_dot(v_tiles[gl][d], ps[n][d]) for d in range(n_tiles)] for n, (gl, hh) in enumerate(heads)]
    for n, (gl, hh) in enumerate(heads):
        cols = slice(n * dh, (n + 1) * dh)
        o = _out_t(functools.reduce(lambda x, y: x + y, pvs[n]), _gate_row(gate_ref, gp * WIN_GROUPS + gl, hh, 2))
        o = o + oc_ref[0, n].astype(F32) + os_ref[0, n].astype(F32)
        ya_ref[:, cols] = (o.T * _silu(z_ref[:, cols].astype(F32))).astype(BF16)


def _win_attn(proj, gates, o_cmp, o_slc, bias_w, bsz, t):
    g_, dh, tq, tk = NSA_KV_GROUPS, NSA_HEAD_DIM, WIN_TQ, ATT_TK
    nt = t // tq
    ng = WIN_GROUPS
    gw = ng * NSA_HPG * dh
    kw = ng * dh
    n_tiles = (bias_w.shape[3] - tq) // tk
    return pl.pallas_call(
        functools.partial(_win_attn_kernel, n_tiles=n_tiles),
        grid=(bsz, g_ // ng, nt),
        in_specs=[
            pl.BlockSpec((tq, gw), lambda b, g, i: (b * nt + i, COL_AQ // gw + g)),
            pl.BlockSpec((t, kw), lambda b, g, i: (b, COL_WK // kw + g)),
            pl.BlockSpec((t, kw), lambda b, g, i: (b, COL_WV // kw + g)),
            pl.BlockSpec((ng, NSA_HPG, tk, bias_w.shape[3]), lambda b, g, i: (g, 0, 0, 0)),
            pl.BlockSpec((GATE_COLS, tq), lambda b, g, i: (0, b * nt + i)),
            pl.BlockSpec((1, ng * NSA_HPG, dh, tq), lambda b, g, i: (b, g, 0, i)),
            pl.BlockSpec((1, ng * NSA_HPG, dh, tq), lambda b, g, i: (b, g, 0, i)),
            pl.BlockSpec((tq, gw), lambda b, g, i: (b * nt + i, COL_AZ // gw + g)),
        ],
        out_specs=pl.BlockSpec((tq, gw), lambda b, g, i: (b * nt + i, g)),
        out_shape=jax.ShapeDtypeStruct((bsz * t, NSA_WIDTH), BF16),
        scratch_shapes=[pltpu.VMEM((ng, VROWS, t), BF16)],
        compiler_params=_cparams(("parallel", "parallel", "arbitrary")),
        name="win_attn",
    )(proj, proj, proj, bias_w, gates, o_cmp, o_slc, proj)


def _outproj_kernel(ym_ref, ya_ref, x_ref, w_ref, gain_ref, o_ref, mix):
    j = pl.program_id(1)
    nj = pl.num_programs(1)
    tn = x_ref.shape[1]

    @pl.when(j == 0)
    def _():
        mix[:, 0:MLSTM_WIDTH] = ym_ref[...]
        mix[:, MLSTM_WIDTH:D_MIX] = ya_ref[...]

    o_ref[:, pl.ds(pl.multiple_of(j * tn, tn), tn)] = x_ref[...] + _dot(mix[...], w_ref[...])

    @pl.when(j == nj - 1)
    def _():
        n_tiles = o_ref.shape[1] // tn
        ss = None
        for jj in range(n_tiles):
            y = o_ref[:, jj * tn:(jj + 1) * tn]
            part = jnp.sum(y * y, axis=-1, keepdims=True)
            ss = part if ss is None else ss + part
        inv = lax.rsqrt(ss / o_ref.shape[1] + RMS_EPS)
        for jj in range(n_tiles):
            cols = slice(jj * tn, (jj + 1) * tn)
            o_ref[:, cols] = o_ref[:, cols] * inv * gain_ref[:, cols]


def _out_proj(y_m, y_a, x2d, w_out, gain):
    n, d = x2d.shape
    tm, tn = min(OUTPROJ_TM, n), OUTPROJ_TN
    nj = d // tn
    return pl.pallas_call(
        _outproj_kernel,
        grid=(n // tm, nj),
        in_specs=[
            pl.BlockSpec((tm, MLSTM_WIDTH), lambda i, j: (i, 0)),
            pl.BlockSpec((tm, NSA_WIDTH), lambda i, j: (i, 0)),
            pl.BlockSpec((tm, tn), lambda i, j: (i, j)),
            pl.BlockSpec((D_MIX, tn), lambda i, j: (0, j)),
            pl.BlockSpec((1, d), lambda i, j: (0, 0)),
        ],
        out_specs=pl.BlockSpec((tm, d), lambda i, j: (i, 0)),
        out_shape=jax.ShapeDtypeStruct((n, d), F32),
        scratch_shapes=[pltpu.VMEM((tm, D_MIX), BF16)],
        compiler_params=_cparams(("parallel", "arbitrary")),
        name="out_proj",
    )(y_m, y_a, x2d, w_out, gain)


def _rel_bucket(dist):
    n = jnp.maximum(dist, 0)
    nf = jnp.maximum(n, REL_MAX_EXACT).astype(jnp.float32)
    large = REL_MAX_EXACT + (jnp.log(nf / REL_MAX_EXACT) / math.log(REL_MAX_DISTANCE / REL_MAX_EXACT)
                             * (REL_BUCKETS - REL_MAX_EXACT)).astype(jnp.int32)
    large = jnp.minimum(large, REL_BUCKETS - 1)
    return jnp.where(n < REL_MAX_EXACT, n, large)


def _toeplitz_vec(by_dist, base, n_pos, n_neg, lo, hi, shift=None):
    w = n_pos + n_neg
    c = np.arange(w)
    dist = np.where(c < n_pos, base + c, base - (w - c))
    ok = (dist >= lo) & (dist < hi)
    dmax = by_dist.shape[1]

    def run(start, length):
        left, right = max(0, -start), max(0, start + length - dmax)
        ext = jnp.pad(by_dist, ((0, 0), (left, right)), mode="edge")
        return lax.slice_in_dim(ext, start + left, start + left + length, axis=1)

    vals = jnp.concatenate([run(base, n_pos), run(base - n_neg, n_neg)], axis=1)
    if shift is not None:
        vals = vals - shift
    return jnp.where(ok[None], vals * LOG2E, NEG_LOGIT)


def _toeplitz_t(w_row, n_keys, n_q, key_step):
    x = jnp.broadcast_to(w_row, (n_keys, w_row.shape[1]))
    return pltpu.roll(x, 0, 1, stride=key_step, stride_axis=0)[:, 0:n_q]


def _bias_tables_kernel(wc_ref, ws_ref, ww_ref, bc_ref, bs_ref, bw_ref):
    nb, t = bc_ref.shape[2], bc_ref.shape[3]
    bc_ref[0, 0] = _toeplitz_t(wc_ref[0], nb, t, CMP_STRIDE)
    bs_ref[0, 0] = _toeplitz_t(ws_ref[0], bs_ref.shape[2], bs_ref.shape[3], 1)
    bw_ref[0, 0] = _toeplitz_t(ww_ref[0], bw_ref.shape[2], bw_ref.shape[3], 1)


def _bias_tables(rel_bias, t):
    tq, tk = ATT_TQ, ATT_TK
    g_, hpg = NSA_KV_GROUPS, NSA_HPG
    rb = rel_bias.astype(F32)
    dmax = REL_MAX_DISTANCE + 2 * max(SLC_TQ, SLC_TK, WIN_TQ, tq, tk)
    onehot = (_rel_bucket(jnp.arange(dmax, dtype=jnp.int32))[None, :] == jnp.arange(REL_BUCKETS, dtype=jnp.int32)[:, None])
    by_dist = jnp.dot(rb.T, onehot.astype(F32), precision=lax.Precision.HIGHEST)
    far = rb[REL_BUCKETS - 1][:, None]
    big = 1 << 30
    nb = t // CMP_STRIDE
    wc = _toeplitz_vec(by_dist, -(CMP_BLOCK - 1), t, t, 0, big)[:, None, :]
    sq, sk = min(SLC_TQ, t), SLC_TK
    r = sq // sk
    n_s = min(-(-(REL_MAX_DISTANCE + sk - 1) // sk) + r - 1, t // sk)
    wm = (n_s - 1) * sk + sq
    ws = _toeplitz_vec(by_dist, -(r - 1) * sk, wm, sk, 0, big, far)[:, None, :]
    wq = min(WIN_TQ, t)
    rw = wq // tk
    n_w = rw + (WINDOW - 1 + tk - 1) // tk
    wmw = n_w * tk + wq
    ww = _toeplitz_vec(by_dist, -(rw - 1) * tk, wmw, tk, 0, WINDOW)[:, None, :]
    return pl.pallas_call(
        _bias_tables_kernel,
        grid=(NSA_HEADS,),
        in_specs=[
            pl.BlockSpec((1, 1, 2 * t), lambda h: (h, 0, 0)),
            pl.BlockSpec((1, 1, wm + sk), lambda h: (h, 0, 0)),
            pl.BlockSpec((1, 1, wmw + tk), lambda h: (h, 0, 0)),
        ],
        out_specs=[
            pl.BlockSpec((1, 1, nb, t), lambda h: (h // hpg, h % hpg, 0, 0)),
            pl.BlockSpec((1, 1, sk, wm), lambda h: (h // hpg, h % hpg, 0, 0)),
            pl.BlockSpec((1, 1, tk, wmw), lambda h: (h // hpg, h % hpg, 0, 0)),
        ],
        out_shape=[
            jax.ShapeDtypeStruct((g_, hpg, nb, t), F32),
            jax.ShapeDtypeStruct((g_, hpg, sk, wm), F32),
            jax.ShapeDtypeStruct((g_, hpg, tk, wmw), F32),
        ],
        compiler_params=_cparams(("parallel",)),
        name="bias_tables",
    )(wc, ws, ww)


def _cover_t(t):
    nb = t // CMP_STRIDE
    n_cmp = (t - CMP_BLOCK) // CMP_STRIDE + 1
    n_slc = t // SLC_BLOCK
    cs = np.arange(nb) * CMP_STRIDE
    ss = np.arange(n_slc) * SLC_BLOCK
    cover = np.clip(np.minimum(cs[:, None] + CMP_BLOCK, ss[None, :] + SLC_BLOCK)
                    - np.maximum(cs[:, None], ss[None, :]), 0, None) / CMP_BLOCK
    cover[n_cmp:] = 0.0
    return jnp.asarray(cover.T, dtype=BF16)


def kernel(x, norm_gain, w_in, w_conv, b_igate, b_fgate, mlstm_norm_gain, cmp_k_pos, cmp_k_w1, cmp_k_w2,
           cmp_v_pos, cmp_v_w1, cmp_v_w2, rel_bias, w_out, final_norm_gain):
    bsz, t, d = x.shape
    assert d == D_MODEL and t % MLSTM_L == 0 and t % ATT_TQ == 0 and (t // CMP_STRIDE) % LANES == 0
    n = bsz * t
    x2d = x.reshape(n, d)

    w_main, w_gate = _w_prep(w_in.T)
    proj, gates = _in_proj(x2d, norm_gain.reshape(1, d).astype(F32), w_main, w_gate)

    g_rows = gates[:, :2 * MLSTM_HEADS].reshape(bsz, t, 2, MLSTM_HEADS).transpose(0, 3, 2, 1)
    y_m = _mlstm(proj, g_rows, w_conv.astype(F32), b_igate.astype(F32), b_fgate.astype(F32),
                 mlstm_norm_gain.reshape(1, MLSTM_WIDTH).astype(F32), bsz, t)

    dh = NSA_HEAD_DIM
    half = CMP_BLOCK // 2

    def w1cat(w1):
        return jnp.concatenate([w1[:half].reshape(half * dh, dh), w1[half:].reshape(half * dh, dh)], axis=1).astype(BF16)

    k_cmp, v_cmp_t = _compress(
        proj, w1cat(cmp_k_w1), cmp_k_w2.astype(BF16), cmp_k_pos.reshape(2, half * dh).astype(BF16),
        w1cat(cmp_v_w1), cmp_v_w2.astype(BF16), cmp_v_pos.reshape(2, half * dh).astype(BF16), bsz, t)

    bias_c, bias_s, bias_w = _bias_tables(rel_bias, t)
    gates_t = gates.T
    o_cmp, selbias_t = _cmp_attn(proj, gates_t, k_cmp, v_cmp_t, bias_c, _cover_t(t), bsz, t)
    o_slc = _slc_attn(proj, gates_t, selbias_t, bias_s, bsz, t)
    y_a = _win_attn(proj, gates_t, o_cmp, o_slc, bias_w, bsz, t)

    out = _out_proj(y_m, y_a, x2d, w_out.astype(BF16), final_norm_gain.reshape(1, d).astype(F32))
    return out.reshape(bsz, t, d)
```

```python
import functools
import math

import jax
import jax.numpy as jnp
import numpy as np
from jax import lax
from jax.experimental import pallas as pl
from jax.experimental.pallas import tpu as pltpu

F32 = jnp.float32
BF16 = jnp.bfloat16

D_MODEL = 4096
D_MIX = D_MODEL
MLSTM_WIDTH = D_MIX // 2
MLSTM_HEADS = 4
MLSTM_V_DIM = MLSTM_WIDTH // MLSTM_HEADS
MLSTM_QK_DIM = MLSTM_V_DIM // 2
MLSTM_QK_WIDTH = MLSTM_HEADS * MLSTM_QK_DIM
CONV_WIDTH = 4
NSA_WIDTH = D_MIX - MLSTM_WIDTH
NSA_HEAD_DIM = 128
NSA_HEADS = NSA_WIDTH // NSA_HEAD_DIM
NSA_KV_GROUPS = 4
NSA_HPG = NSA_HEADS // NSA_KV_GROUPS
NSA_KV_WIDTH = NSA_KV_GROUPS * NSA_HEAD_DIM
CMP_BLOCK = 32
CMP_STRIDE = 16
SLC_BLOCK = 64
SLC_TOP_N = 16
WINDOW = 512
FORCE_BONUS = 1000.0
REL_BUCKETS = 32
REL_MAX_EXACT = REL_BUCKETS // 2
REL_MAX_DISTANCE = 1024
RMS_EPS = 1e-6
NEG_LOGIT = -1e30

LANES = 128
VMEM_LIMIT_BYTES = 56 * 1024 * 1024

COL_MQ = 0
COL_MK = COL_MQ + MLSTM_QK_WIDTH
COL_MV = COL_MK + MLSTM_QK_WIDTH
COL_MO = COL_MV + MLSTM_WIDTH
COL_MZ = COL_MO + MLSTM_WIDTH
COL_AQ = COL_MZ + MLSTM_WIDTH
COL_CK = COL_AQ + NSA_WIDTH
COL_CV = COL_CK + NSA_KV_WIDTH
COL_SK = COL_CV + NSA_KV_WIDTH
COL_SV = COL_SK + NSA_KV_WIDTH
COL_WK = COL_SV + NSA_KV_WIDTH
COL_WV = COL_WK + NSA_KV_WIDTH
COL_AZ = COL_WV + NSA_KV_WIDTH
MAIN_COLS = COL_AZ + NSA_WIDTH
GATE_COLS = LANES
GATE_I = 0
GATE_F = MLSTM_HEADS
GATE_NSA = 2 * MLSTM_HEADS

INPROJ_TM = 512
INPROJ_TN = 1536
OUTPROJ_TM = 512
OUTPROJ_TN = 1024
OUTPROJ_W_BUFFERS = 3
MLSTM_L = 256
ATT_TQ = 256
ATT_TK = 256
WPREP_TR = 512
SLC_TQ = 512
SLC_TK = 512
CMP_TQ = 512
CMP_GROUPS = 2
WIN_TQ = 256
WIN_GROUPS = 2
MASK_BIG = 2.0 ** 100


def _cparams(sem):
    return pltpu.CompilerParams(dimension_semantics=sem, vmem_limit_bytes=VMEM_LIMIT_BYTES)


def _sigmoid(x):
    return 0.5 * jnp.tanh(0.5 * x) + 0.5


def _silu(x):
    return x * _sigmoid(x)


def _dot(a, b):
    return jnp.dot(a, b, preferred_element_type=F32)


def _dot_nt(a, b):
    return lax.dot_general(a, b, (((1,), (1,)), ((), ())), preferred_element_type=F32)


W_OFF_I = COL_MZ + MLSTM_WIDTH
W_OFF_AQ = W_OFF_I + 2 * MLSTM_HEADS
W_OFF_GATE = W_OFF_AQ + NSA_WIDTH + 6 * NSA_KV_WIDTH
W_OFF_AZ = W_OFF_GATE + 3 * NSA_HEADS


def _wprep_kernel(w_ref, gi_ref, ga_ref, o_ref, g_ref):
    o_ref[...] = w_ref[...].astype(BF16)

    @pl.when(pl.program_id(0) == 0)
    def _():
        pad = jnp.zeros((GATE_COLS - gi_ref.shape[0] - ga_ref.shape[0], g_ref.shape[1]), F32)
        g_ref[...] = jnp.concatenate([gi_ref[...], ga_ref[...], pad], axis=0).astype(BF16)


def _w_prep(w_t):
    rows, d = w_t.shape
    tr = WPREP_TR
    nb_a, nb_b = W_OFF_I // tr, (W_OFF_GATE - W_OFF_AQ) // tr

    def src(m):
        skip = jnp.where(m >= nb_a, W_OFF_AQ - W_OFF_I, 0) + jnp.where(m >= nb_a + nb_b, W_OFF_AZ - W_OFF_GATE, 0)
        return pl.multiple_of(m * tr + skip, 8)

    rows_at = lambda start, size: pl.BlockSpec((pl.Element(size), pl.Element(d)), lambda m: (start, 0))
    return pl.pallas_call(
        _wprep_kernel,
        grid=(MAIN_COLS // tr,),
        in_specs=[
            pl.BlockSpec((pl.Element(tr), pl.Element(d)), lambda m: (src(m), 0)),
            rows_at(W_OFF_I, W_OFF_AQ - W_OFF_I),
            rows_at(W_OFF_GATE, W_OFF_AZ - W_OFF_GATE),
        ],
        out_specs=[pl.BlockSpec((tr, d), lambda m: (m, 0)), pl.BlockSpec((GATE_COLS, d), lambda m: (0, 0))],
        out_shape=[jax.ShapeDtypeStruct((MAIN_COLS, d), BF16), jax.ShapeDtypeStruct((GATE_COLS, d), BF16)],
        compiler_params=_cparams(("arbitrary",)),
        name="w_prep",
    )(w_t, w_t, w_t)


def _inproj_kernel(x_ref, gain_ref, w_hbm, wg_ref, o_hbm, og_ref, h_ref, wbuf, obuf, w_sem, o_sem):
    i = pl.program_id(0)
    ni = pl.num_programs(0)
    tm, tn = obuf.shape[1], obuf.shape[2]
    nj = w_hbm.shape[0] // tn

    def w_copy(j, slot):
        return pltpu.make_async_copy(w_hbm.at[pl.ds(pl.multiple_of(j * tn, tn), tn), :], wbuf.at[slot], w_sem.at[slot])

    def o_copy(j, slot):
        dst = o_hbm.at[pl.ds(pl.multiple_of(i * tm, tm), tm), pl.ds(pl.multiple_of(j * tn, tn), tn)]
        return pltpu.make_async_copy(obuf.at[slot], dst, o_sem.at[slot])

    @pl.when(i == 0)
    def _():
        w_copy(0, 0).start()

    x = x_ref[...]
    ms = jnp.mean(x * x, axis=-1, keepdims=True)
    h = (x * lax.rsqrt(ms + RMS_EPS) * gain_ref[...]).astype(BF16)
    h_ref[...] = h
    og_ref[...] = _dot_nt(h, wg_ref[...])

    def body(j, carry):
        slot = j % 2
        nxt = (j + 1) % nj
        w_copy(j, slot).wait()

        @pl.when((j + 1 < nj) | (i + 1 < ni))
        def _():
            w_copy(nxt, 1 - slot).start()

        @pl.when((j >= 2) | (i > 0))
        def _():
            o_copy(j, slot).wait()

        obuf[slot] = _dot_nt(h_ref[...], wbuf[slot]).astype(BF16)
        o_copy(j, slot).start()
        return carry

    lax.fori_loop(0, nj, body, 0)

    @pl.when(i == ni - 1)
    def _():
        o_copy(nj - 2, (nj - 2) % 2).wait()
        o_copy(nj - 1, (nj - 1) % 2).wait()


def _in_proj(x2d, gain, w_main, w_gate):
    n, d = x2d.shape
    tm, tn = min(INPROJ_TM, n), INPROJ_TN
    assert (MAIN_COLS // tn) % 2 == 0
    return pl.pallas_call(
        _inproj_kernel,
        grid=(n // tm,),
        in_specs=[
            pl.BlockSpec((tm, d), lambda i: (i, 0)),
            pl.BlockSpec((1, d), lambda i: (0, 0)),
            pl.BlockSpec(memory_space=pl.ANY),
            pl.BlockSpec((GATE_COLS, d), lambda i: (0, 0)),
        ],
        out_specs=[
            pl.BlockSpec(memory_space=pl.ANY),
            pl.BlockSpec((tm, GATE_COLS), lambda i: (i, 0)),
        ],
        out_shape=[
            jax.ShapeDtypeStruct((n, MAIN_COLS), BF16),
            jax.ShapeDtypeStruct((n, GATE_COLS), F32),
        ],
        scratch_shapes=[
            pltpu.VMEM((tm, d), BF16),
            pltpu.VMEM((2, tn, d), BF16),
            pltpu.VMEM((2, tm, tn), BF16),
            pltpu.SemaphoreType.DMA((2,)),
            pltpu.SemaphoreType.DMA((2,)),
        ],
        compiler_params=_cparams(("arbitrary",)),
        name="in_proj",
    )(x2d, gain, w_main, w_gate)


def _mlstm_kernel(bi_ref, bf_ref, q_ref, k_ref, v_ref, o_ref, z_ref, g_ref, wq_ref, wk_ref, ng_ref,
                  y_ref, qext, kext, c_st, n_st, m_st):
    L = MLSTM_L
    HIST = 8
    dqk, dv = MLSTM_QK_DIM, MLSTM_V_DIM

    @pl.when(pl.program_id(1) == 0)
    def _():
        qext[:, 0:HIST, :] = jnp.zeros((MLSTM_HEADS, HIST, dqk), F32)
        kext[:, 0:HIST, :] = jnp.zeros((MLSTM_HEADS, HIST, dqk), F32)
        c_st[...] = jnp.zeros_like(c_st)
        n_st[...] = jnp.zeros_like(n_st)
        m_st[...] = jnp.zeros_like(m_st)

    rr = lax.broadcasted_iota(jnp.int32, (L, L), 0)
    cc = lax.broadcasted_iota(jnp.int32, (L, L), 1)
    upper = (rr <= cc).astype(F32)

    for hd in range(MLSTM_HEADS):
        qcols = slice(hd * dqk, (hd + 1) * dqk)
        vcols = slice(hd * dv, (hd + 1) * dv)
        qext[hd, HIST:HIST + L, :] = q_ref[:, qcols].astype(F32)
        kext[hd, HIST:HIST + L, :] = k_ref[:, qcols].astype(F32)

        def conv_silu(ext, w_ref):
            w = w_ref[:, qcols]
            y = ext[hd, pl.ds(HIST, L), :] * w[CONV_WIDTH - 1:CONV_WIDTH, :]
            for s in range(1, CONV_WIDTH):
                y = y + ext[hd, pl.ds(HIST - s, L), :] * w[CONV_WIDTH - 1 - s:CONV_WIDTH - s, :]
            return _silu(y)

        qc = conv_silu(qext, wq_ref)
        kc = conv_silu(kext, wk_ref) * (dqk ** -0.5)
        qext[hd, 0:HIST, :] = qext[hd, L:L + HIST, :]
        kext[hd, 0:HIST, :] = kext[hd, L:L + HIST, :]

        g = g_ref[0, hd]
        i_row = g[0:1, :] + bi_ref[hd]
        f_row = g[1:2, :] + bf_ref[hd]
        lf_row = jnp.minimum(f_row, 0.0) - jnp.log(1.0 + jnp.exp(-jnp.abs(f_row)))

        bcum_row = jnp.dot(jnp.broadcast_to(lf_row, (8, L)), upper, preferred_element_type=F32,
                           precision=lax.Precision.HIGHEST)[0:1, :]
        bcum_col = jnp.sum(jnp.where(rr == cc, bcum_row, 0.0), axis=1, keepdims=True)
        gsum = bcum_row[:, L - 1:L]
        m_prev = m_st[hd]

        dlog = jnp.where(rr >= cc, bcum_col - bcum_row + i_row, -jnp.inf)
        m_inter = bcum_col + m_prev
        m_t = jnp.maximum(m_inter, jnp.max(dlog, axis=1, keepdims=True))
        dmat = jnp.exp(dlog - m_t)
        inter = jnp.exp(m_inter - m_t)

        qb = qc.astype(BF16)
        kct = kc.T
        vb = v_ref[:, vcols]
        s = _dot(qb, kct.astype(BF16)) * dmat
        c_prev = c_st[hd]
        n_prev = n_st[hd]
        num = _dot(s.astype(BF16), vb) + inter * _dot(qb, c_prev.astype(BF16))
        qn = jnp.sum(s, axis=1, keepdims=True) + inter * jnp.sum(qc * n_prev, axis=1, keepdims=True)
        hh = num / jnp.maximum(jnp.abs(qn), jnp.exp(-m_t))

        wlog = gsum - bcum_row + i_row
        m_next = jnp.maximum(gsum + m_prev, jnp.max(wlog, axis=1, keepdims=True))
        wts = jnp.exp(wlog - m_next)
        keep = jnp.exp(gsum + m_prev - m_next)
        c_st[hd] = keep * c_prev + _dot((kct * wts).astype(BF16), vb)
        n_st[hd] = keep * n_prev + _dot(jnp.broadcast_to(wts, (8, L)).astype(BF16), kc.astype(BF16))[0:1, :]
        m_st[hd] = m_next

        hm = _sigmoid(o_ref[:, vcols].astype(F32)) * hh
        hm = hm * lax.rsqrt(jnp.mean(hm * hm, axis=-1, keepdims=True) + RMS_EPS)
        hm = hm * ng_ref[:, vcols]
        y_ref[:, vcols] = (hm * _silu(z_ref[:, vcols].astype(F32))).astype(BF16)


def _mlstm(proj, g_rows, w_conv, b_igate, b_fgate, norm_gain, bsz, t):
    L = MLSTM_L
    nc = t // L
    nh, dqk, dv = MLSTM_HEADS, MLSTM_QK_DIM, MLSTM_V_DIM
    row = lambda b, c: b * nc + c
    smem = pl.BlockSpec(memory_space=pltpu.SMEM)
    return pl.pallas_call(
        _mlstm_kernel,
        grid=(bsz, nc),
        in_specs=[
            smem, smem,
            pl.BlockSpec((L, MLSTM_QK_WIDTH), lambda b, c: (row(b, c), COL_MQ // MLSTM_QK_WIDTH)),
            pl.BlockSpec((L, MLSTM_QK_WIDTH), lambda b, c: (row(b, c), COL_MK // MLSTM_QK_WIDTH)),
            pl.BlockSpec((L, MLSTM_WIDTH), lambda b, c: (row(b, c), COL_MV // MLSTM_WIDTH)),
            pl.BlockSpec((L, MLSTM_WIDTH), lambda b, c: (row(b, c), COL_MO // MLSTM_WIDTH)),
            pl.BlockSpec((L, MLSTM_WIDTH), lambda b, c: (row(b, c), COL_MZ // MLSTM_WIDTH)),
            pl.BlockSpec((1, nh, 2, L), lambda b, c: (b, 0, 0, c)),
            pl.BlockSpec((CONV_WIDTH, MLSTM_QK_WIDTH), lambda b, c: (0, 0)),
            pl.BlockSpec((CONV_WIDTH, MLSTM_QK_WIDTH), lambda b, c: (0, 1)),
            pl.BlockSpec((1, MLSTM_WIDTH), lambda b, c: (0, 0)),
        ],
        out_specs=pl.BlockSpec((L, MLSTM_WIDTH), lambda b, c: (row(b, c), 0)),
        out_shape=jax.ShapeDtypeStruct((bsz * t, MLSTM_WIDTH), BF16),
        scratch_shapes=[
            pltpu.VMEM((nh, L + 8, dqk), F32),
            pltpu.VMEM((nh, L + 8, dqk), F32),
            pltpu.VMEM((nh, dqk, dv), F32),
            pltpu.VMEM((nh, 1, dqk), F32),
            pltpu.VMEM((nh, 1, 1), F32),
        ],
        compiler_params=_cparams(("parallel", "arbitrary")),
        name="mlstm",
    )(b_igate, b_fgate, proj, proj, proj, proj, proj, g_rows, w_conv, w_conv, norm_gain)


def _compress_kernel(ck_ref, cv_ref, w1k_ref, w2k_ref, pk_ref, w1v_ref, w2v_ref, pv_ref,
                     ok_ref, ov_ref, xf, xcat):
    t = ck_ref.shape[0]
    nb = t // CMP_STRIDE
    dh = NSA_HEAD_DIM

    def one(src_ref, w1_ref, w2_ref, pos_ref):
        xf[...] = src_ref[...].astype(F32)
        for l in range(CMP_STRIDE):
            xcat[:, l * dh:(l + 1) * dh] = xf[pl.ds(l, nb, stride=CMP_STRIDE), :].astype(BF16)
        w1 = w1_ref[...]
        ab = _dot(xcat[...], w1)
        pp = _dot(pos_ref[...], w1)
        pos_term = pp[0:1, 0:dh] + pp[1:2, dh:2 * dh]
        second = pltpu.roll(ab[:, dh:2 * dh], nb - 1, 0)
        hid = _silu(ab[:, 0:dh] + second + pos_term)
        return _dot(hid.astype(BF16), w2_ref[...])

    ok_ref[0, 0] = one(ck_ref, w1k_ref, w2k_ref, pk_ref).astype(BF16)
    ov_ref[0, 0] = one(cv_ref, w1v_ref, w2v_ref, pv_ref).T.astype(BF16)


def _compress(proj, w1k, w2k, pk, w1v, w2v, pv, bsz, t):
    g_, dh = NSA_KV_GROUPS, NSA_HEAD_DIM
    nb = t // CMP_STRIDE
    full = lambda a: pl.BlockSpec(a.shape, lambda b, g: (0,) * a.ndim)
    k_spec = pl.BlockSpec((1, 1, nb, dh), lambda b, g: (b, g, 0, 0))
    vt_spec = pl.BlockSpec((1, 1, dh, nb), lambda b, g: (b, g, 0, 0))
    return pl.pallas_call(
        _compress_kernel,
        grid=(bsz, g_),
        in_specs=[
            pl.BlockSpec((t, dh), lambda b, g: (b, COL_CK // dh + g)),
            pl.BlockSpec((t, dh), lambda b, g: (b, COL_CV // dh + g)),
            full(w1k), full(w2k), full(pk), full(w1v), full(w2v), full(pv),
        ],
        out_specs=[k_spec, vt_spec],
        out_shape=[jax.ShapeDtypeStruct((bsz, g_, nb, dh), BF16), jax.ShapeDtypeStruct((bsz, g_, dh, nb), BF16)],
        scratch_shapes=[pltpu.VMEM((t, dh), F32), pltpu.VMEM((nb, CMP_STRIDE * dh), BF16)],
        compiler_params=_cparams(("parallel", "parallel")),
        name="compress",
    )(proj, proj, w1k, w2k, pk, w1v, w2v, pv)


LOG2E = math.log2(math.e)
VROWS = NSA_HEAD_DIM + 16


def _q_scaled(q_ref, hh):
    dh = NSA_HEAD_DIM
    return (q_ref[:, hh * dh:(hh + 1) * dh].astype(F32) * (dh ** -0.5 * LOG2E)).astype(BF16)


def _q_t(q_ref, hh):
    dh = NSA_HEAD_DIM
    return (q_ref[:, hh * dh:(hh + 1) * dh].astype(F32) * (dh ** -0.5 * LOG2E)).T.astype(BF16)


def _gate_row(gt_ref, g, hh, branch):
    row = GATE_NSA + 3 * (g * NSA_HPG + hh) + branch
    return _sigmoid(gt_ref[pl.ds(row, 1), :])


def _fill_vt(vt, v_ref):
    dh = NSA_HEAD_DIM
    vt[0:dh, :] = v_ref[...].astype(F32).T.astype(BF16)
    vt[dh:VROWS, :] = jnp.ones((VROWS - dh, vt.shape[1]), BF16)


def _out_t(acc, gate_row):
    dh = NSA_HEAD_DIM
    return acc[0:dh, :] * (gate_row / acc[dh:dh + 1, :])


def _cmp_attn_kernel(q_ref, kc_ref, vct_ref, bias_ref, cov_ref, gate_ref, oc_ref, sel_ref, score_ref, *, n_sel):
    tq = q_ref.shape[0]
    dh = NSA_HEAD_DIM
    n_slc = cov_ref.shape[0]
    t0 = pl.program_id(1) * tq
    gp = pl.program_id(0)
    heads = [(gl, hh) for gl in range(CMP_GROUPS) for hh in range(NSA_HPG)]
    qk = [_dot_nt(kc_ref[0, gl], _q_scaled(q_ref, gl * NSA_HPG + hh)) for gl, hh in heads]
    ps = []
    for n, (gl, hh) in enumerate(heads):
        logit = qk[n] + bias_ref[gl, hh]
        m = jnp.max(logit, axis=0, keepdims=True)
        e = jnp.exp2(logit - m)
        inv = jnp.where(m > 0.5 * NEG_LOGIT, 1.0 / jnp.sum(e, axis=0, keepdims=True), 0.0)
        ps.append(e * inv)
    ots = [_dot(vct_ref[0, gl], ps[n].astype(BF16)) for n, (gl, hh) in enumerate(heads)]
    for n, (gl, hh) in enumerate(heads):
        oc_ref[0, n] = (ots[n] * _gate_row(gate_ref, gp * CMP_GROUPS + gl, hh, 0)).astype(BF16)

    cov = cov_ref[...]
    jb = lax.broadcasted_iota(jnp.int32, (n_slc, tq), 0)
    cur = (t0 + lax.broadcasted_iota(jnp.int32, (n_slc, tq), 1)) // SLC_BLOCK
    valid = jb <= cur
    forced = (jb == 0) | (jb == cur) | (jb == cur - 1)
    sub = 8
    row_id = lax.broadcasted_iota(jnp.int32, (sub, tq), 0)
    for gl in range(CMP_GROUPS):
        g4 = ps[gl * NSA_HPG:(gl + 1) * NSA_HPG]
        p_sum = (g4[0] + g4[1]) + (g4[2] + g4[3])
        p_hi = p_sum.astype(BF16)
        r1 = p_sum - p_hi.astype(F32)
        p_mid = r1.astype(BF16)
        p_lo = (r1 - p_mid.astype(F32)).astype(BF16)
        st = _dot(cov, p_hi) + _dot(cov, p_mid) + _dot(cov, p_lo)
        score_ref[gl] = jnp.where(valid, st + jnp.where(forced, FORCE_BONUS, 0.0), -1.0)
        groups = [score_ref[gl, r:r + sub, :] for r in range(0, n_slc, sub)]
        ranks = [jnp.zeros((sub, tq), F32) for _ in groups]
        for j2 in range(n_slc):
            row = score_ref[gl, j2:j2 + 1, :]
            for gi, sc in enumerate(groups):
                r0 = gi * sub
                if r0 > j2:
                    inc = jnp.where(row >= sc, 1.0, 0.0)
                elif r0 + sub - 1 <= j2:
                    inc = jnp.where(row > sc, 1.0, 0.0)
                else:
                    inc = jnp.where(row_id > j2 - r0, jnp.where(row >= sc, 1.0, 0.0), jnp.where(row > sc, 1.0, 0.0))
                ranks[gi] = ranks[gi] + inc
        for gi, rk in enumerate(ranks):
            score_ref[gl, gi * sub:(gi + 1) * sub, :] = rk
        sel = valid & (score_ref[gl] < n_sel)
        sel_ref[0, gl, 0:n_slc, :] = jnp.where(sel, 0.0, -MASK_BIG).astype(BF16)
        if n_slc < LANES:
            sel_ref[0, gl, n_slc:LANES, :] = jnp.zeros((LANES - n_slc, tq), BF16)


def _cmp_attn(proj, gates, k_cmp, v_cmp_t, bias_c, cover_t, bsz, t):
    g_, dh, tq = NSA_KV_GROUPS, NSA_HEAD_DIM, CMP_TQ
    nt = t // tq
    nb = t // CMP_STRIDE
    n_slc = t // SLC_BLOCK
    ng = CMP_GROUPS
    gw = ng * NSA_HPG * dh
    return pl.pallas_call(
        functools.partial(_cmp_attn_kernel, n_sel=min(SLC_TOP_N, n_slc)),
        grid=(g_ // ng, nt, bsz),
        in_specs=[
            pl.BlockSpec((tq, gw), lambda g, i, b: (b * nt + i, COL_AQ // gw + g)),
            pl.BlockSpec((1, ng, nb, dh), lambda g, i, b: (b, g, 0, 0)),
            pl.BlockSpec((1, ng, dh, nb), lambda g, i, b: (b, g, 0, 0)),
            pl.BlockSpec((ng, NSA_HPG, nb, tq), lambda g, i, b: (g, 0, 0, i)),
            pl.BlockSpec((n_slc, nb), lambda g, i, b: (0, 0)),
            pl.BlockSpec((GATE_COLS, tq), lambda g, i, b: (0, b * nt + i)),
        ],
        out_specs=[
            pl.BlockSpec((1, ng * NSA_HPG, dh, tq), lambda g, i, b: (b, g, 0, i)),
            pl.BlockSpec((1, ng, LANES, tq), lambda g, i, b: (b, g, 0, i)),
        ],
        out_shape=[
            jax.ShapeDtypeStruct((bsz, NSA_HEADS, dh, t), BF16),
            jax.ShapeDtypeStruct((bsz, g_, LANES, t), BF16),
        ],
        scratch_shapes=[pltpu.VMEM((ng, n_slc, tq), F32)],
        compiler_params=_cparams(("parallel", "parallel", "parallel")),
        name="cmp_attn",
    )(proj, k_cmp, v_cmp_t, bias_c, cover_t, gates)


def _slc_attn_kernel(q_ref, sbt_ref, k_ref, v_ref, bias_ref, gate_ref, os_ref,
                     kaug, vt, qt, s_buf, p_buf, a_buf, m_ref, acc_ref, *, n_near):
    tq, tk, dh = SLC_TQ, SLC_TK, NSA_HEAD_DIM
    r = tq // tk
    t = k_ref.shape[0]
    g = pl.program_id(1)
    i = pl.program_id(2)
    heads = range(NSA_HPG)

    @pl.when(i == 0)
    def _():
        kaug[:, 0:dh] = k_ref[...]
        blk = lax.broadcasted_iota(jnp.int32, (t, LANES), 0) // SLC_BLOCK
        lane = lax.broadcasted_iota(jnp.int32, (t, LANES), 1)
        kaug[:, dh:dh + LANES] = jnp.where(blk == lane, 1.0, 0.0).astype(BF16)
        _fill_vt(vt, v_ref)

    sbt = sbt_ref[0, 0]
    for hh in heads:
        qt[hh, 0:dh, :] = _q_t(q_ref, hh)
        qt[hh, dh:dh + LANES, :] = sbt

    def ktile(j):
        return kaug[pl.ds(pl.multiple_of(j * tk, tk), tk), :]

    def vtile(j):
        return vt[:, pl.ds(pl.multiple_of(j * tk, tk), tk)]

    m_ref[...] = jnp.full(m_ref.shape, NEG_LOGIT, F32)
    acc_ref[...] = jnp.zeros_like(acc_ref)
    k0 = ktile(0)
    for hh in heads:
        s_buf[hh] = _dot(k0, qt[hh])

    last = (i + 1) * r - 1

    def stage(j, bias_fn, with_pv=True, with_qk=True):
        if with_pv:
            v_prev = vtile(j - 1)
            pv = [_dot(v_prev, p_buf[hh]) for hh in heads]
        if with_qk:
            k_next = ktile(j + 1)
            s_next = [_dot(k_next, qt[hh]) for hh in heads]
        if with_pv:
            for hh in heads:
                acc_ref[hh] = a_buf[hh] * acc_ref[hh] + pv[hh]
        for hh in heads:
            s = s_buf[hh]
            if bias_fn is not None:
                s = s + bias_fn(hh)
            m_prev = m_ref[hh]
            m_new = jnp.maximum(m_prev, jnp.max(s, axis=0, keepdims=True))
            a_buf[hh] = jnp.exp2(m_prev - m_new)
            p_buf[hh] = jnp.exp2(s - m_new).astype(BF16)
            m_ref[hh] = m_new
        if with_qk:
            for hh in heads:
                s_buf[hh] = s_next[hh]

    n_far = jnp.maximum(i * r + r - n_near, 0)

    def near_bias(j):
        off = pl.multiple_of((i * r + (r - 1) - j) * tk, tk)
        return lambda hh: bias_ref[0, hh, :, pl.ds(off, tq)]

    @pl.when(n_far > 0)
    def _():
        stage(0, None, with_pv=False)

    @pl.when((n_far == 0) & (last > 0))
    def _():
        stage(0, near_bias(0), with_pv=False)

    @pl.when(last == 0)
    def _():
        stage(0, near_bias(0), with_pv=False, with_qk=False)

    def far_body(j, c):
        stage(j, None)
        return c

    lax.fori_loop(1, n_far, far_body, 0)

    def near_body(j, c):
        stage(j, near_bias(j))
        return c

    lax.fori_loop(jnp.maximum(n_far, 1), last, near_body, 0)

    @pl.when(last > 0)
    def _():
        stage(last, near_bias(last), with_qk=False)

    v_last = vtile(last)
    pv = [_dot(v_last, p_buf[hh]) for hh in heads]
    for hh in heads:
        acc = a_buf[hh] * acc_ref[hh] + pv[hh]
        os_ref[0, hh] = _out_t(acc, _gate_row(gate_ref, g, hh, 1)).astype(BF16)


def _slc_attn(proj, gates, selbias_t, bias_s, bsz, t):
    g_, dh, tq, tk = NSA_KV_GROUPS, NSA_HEAD_DIM, SLC_TQ, SLC_TK
    nt = t // tq
    gw = NSA_HPG * dh
    n_near = (bias_s.shape[3] - tq) // tk + 1
    return pl.pallas_call(
        functools.partial(_slc_attn_kernel, n_near=n_near),
        grid=(bsz, g_, nt),
        in_specs=[
            pl.BlockSpec((tq, gw), lambda b, g, i: (b * nt + i, COL_AQ // gw + g)),
            pl.BlockSpec((1, 1, LANES, tq), lambda b, g, i: (b, g, 0, i)),
            pl.BlockSpec((t, dh), lambda b, g, i: (b, COL_SK // dh + g)),
            pl.BlockSpec((t, dh), lambda b, g, i: (b, COL_SV // dh + g)),
            pl.BlockSpec((1, NSA_HPG, tk, bias_s.shape[3]), lambda b, g, i: (g, 0, 0, 0)),
            pl.BlockSpec((GATE_COLS, tq), lambda b, g, i: (0, b * nt + i)),
        ],
        out_specs=pl.BlockSpec((1, NSA_HPG, dh, tq), lambda b, g, i: (b, g, 0, i)),
        out_shape=jax.ShapeDtypeStruct((bsz, NSA_HEADS, dh, t), BF16),
        scratch_shapes=[
            pltpu.VMEM((t, dh + LANES), BF16),
            pltpu.VMEM((VROWS, t), BF16),
            pltpu.VMEM((NSA_HPG, dh + LANES, tq), BF16),
            pltpu.VMEM((NSA_HPG, tk, tq), F32),
            pltpu.VMEM((NSA_HPG, tk, tq), BF16),
            pltpu.VMEM((NSA_HPG, 1, tq), F32),
            pltpu.VMEM((NSA_HPG, 1, tq), F32),
            pltpu.VMEM((NSA_HPG, VROWS, tq), F32),
        ],
        compiler_params=_cparams(("parallel", "parallel", "arbitrary")),
        name="slc_attn",
    )(proj, selbias_t, proj, proj, bias_s, gates)


def _win_attn_kernel(q_ref, k_ref, v_ref, bias_ref, gate_ref, oc_ref, os_ref, z_ref, ya_ref, vt, *, n_tiles):
    tq, tk, dh = WIN_TQ, ATT_TK, NSA_HEAD_DIM
    r = tq // tk
    gp = pl.program_id(1)
    i = pl.program_id(2)
    heads = [(gl, hh) for gl in range(WIN_GROUPS) for hh in range(NSA_HPG)]

    @pl.when(i == 0)
    def _():
        for gl in range(WIN_GROUPS):
            vt[gl, 0:dh, :] = v_ref[:, gl * dh:(gl + 1) * dh].astype(F32).T.astype(BF16)
            vt[gl, dh:VROWS, :] = jnp.ones((VROWS - dh, vt.shape[2]), BF16)

    js = [i * r + (r - 1) - d for d in range(n_tiles)]
    offs = [pl.multiple_of(jnp.maximum(j, 0) * tk, tk) for j in js]
    boffs = [pl.multiple_of(jnp.where(js[d] >= 0, d, n_tiles) * tk, tk) for d in range(n_tiles)]
    k_tiles = [[k_ref[pl.ds(off, tk), gl * dh:(gl + 1) * dh] for off in offs] for gl in range(WIN_GROUPS)]
    v_tiles = [[vt[gl, :, pl.ds(off, tk)] for off in offs] for gl in range(WIN_GROUPS)]
    qs = [_q_scaled(q_ref, gl * NSA_HPG + hh) for gl, hh in heads]
    ss = [[_dot_nt(k_tiles[gl][d], qs[n]) for d in range(n_tiles)] for n, (gl, hh) in enumerate(heads)]
    ps = []
    for n, (gl, hh) in enumerate(heads):
        s = [ss[n][d] + bias_ref[gl, hh, :, pl.ds(boffs[d], tq)] for d in range(n_tiles)]
        m = functools.reduce(jnp.maximum, [jnp.max(x, axis=0, keepdims=True) for x in s])
        ps.append([jnp.exp2(x - m).astype(BF16) for x in s])
    pvs = [[_dot(v_tiles[gl][d], ps[n][d]) for d in range(n_tiles)] for n, (gl, hh) in enumerate(heads)]
    for n, (gl, hh) in enumerate(heads):
        cols = slice(n * dh, (n + 1) * dh)
        o = _out_t(functools.reduce(lambda x, y: x + y, pvs[n]), _gate_row(gate_ref, gp * WIN_GROUPS + gl, hh, 2))
        o = o + oc_ref[0, n].astype(F32) + os_ref[0, n].astype(F32)
        ya_ref[:, cols] = (o.T * _silu(z_ref[:, cols].astype(F32))).astype(BF16)


def _win_attn(proj, gates, o_cmp, o_slc, bias_w, bsz, t):
    g_, dh, tq, tk = NSA_KV_GROUPS, NSA_HEAD_DIM, WIN_TQ, ATT_TK
    nt = t // tq
    ng = WIN_GROUPS
    gw = ng * NSA_HPG * dh
    kw = ng * dh
    n_tiles = (bias_w.shape[3] - tq) // tk
    return pl.pallas_call(
        functools.partial(_win_attn_kernel, n_tiles=n_tiles),
        grid=(bsz, g_ // ng, nt),
        in_specs=[
            pl.BlockSpec((tq, gw), lambda b, g, i: (b * nt + i, COL_AQ // gw + g)),
            pl.BlockSpec((t, kw), lambda b, g, i: (b, COL_WK // kw + g)),
            pl.BlockSpec((t, kw), lambda b, g, i: (b, COL_WV // kw + g)),
            pl.BlockSpec((ng, NSA_HPG, tk, bias_w.shape[3]), lambda b, g, i: (g, 0, 0, 0)),
            pl.BlockSpec((GATE_COLS, tq), lambda b, g, i: (0, b * nt + i)),
            pl.BlockSpec((1, ng * NSA_HPG, dh, tq), lambda b, g, i: (b, g, 0, i)),
            pl.BlockSpec((1, ng * NSA_HPG, dh, tq), lambda b, g, i: (b, g, 0, i)),
            pl.BlockSpec((tq, gw), lambda b, g, i: (b * nt + i, COL_AZ // gw + g)),
        ],
        out_specs=pl.BlockSpec((tq, gw), lambda b, g, i: (b * nt + i, g)),
        out_shape=jax.ShapeDtypeStruct((bsz * t, NSA_WIDTH), BF16),
        scratch_shapes=[pltpu.VMEM((ng, VROWS, t), BF16)],
        compiler_params=_cparams(("parallel", "parallel", "arbitrary")),
        name="win_attn",
    )(proj, proj, proj, bias_w, gates, o_cmp, o_slc, proj)


def _outproj_step(idx, ym_ref, ya_ref, x_ref, w1_ref, w2_ref, gain_ref, o_ref):
    j = idx[1]
    tn = x_ref.shape[1]
    n_tiles = o_ref.shape[1] // tn
    o_ref[:, pl.ds(pl.multiple_of(j * tn, tn), tn)] = (
        x_ref[...] + _dot(ym_ref[...], w1_ref[...]) + _dot(ya_ref[...], w2_ref[...]))

    @pl.when(j == n_tiles - 1)
    def _():
        ss = None
        for jj in range(n_tiles):
            y = o_ref[:, jj * tn:(jj + 1) * tn]
            part = jnp.sum(y * y, axis=-1, keepdims=True)
            ss = part if ss is None else ss + part
        inv = lax.rsqrt(ss / o_ref.shape[1] + RMS_EPS)
        for jj in range(n_tiles):
            cols = slice(jj * tn, (jj + 1) * tn)
            o_ref[:, cols] = o_ref[:, cols] * inv * gain_ref[:, cols]


def _out_proj(y_m, y_a, x2d, w_out, gain):
    n, d = x2d.shape
    tm, tn = min(OUTPROJ_TM, n), OUTPROJ_TN
    nj = d // tn
    deep = pl.Buffered(OUTPROJ_W_BUFFERS)
    pipeline = pltpu.emit_pipeline(
        _outproj_step,
        grid=(n // tm, nj),
        in_specs=[
            pl.BlockSpec((tm, MLSTM_WIDTH), lambda i, j: (i, 0)),
            pl.BlockSpec((tm, NSA_WIDTH), lambda i, j: (i, 0)),
            pl.BlockSpec((tm, tn), lambda i, j: (i, j)),
            pl.BlockSpec((MLSTM_WIDTH, tn), lambda i, j: (0, j), pipeline_mode=deep),
            pl.BlockSpec((NSA_WIDTH, tn), lambda i, j: (MLSTM_WIDTH // NSA_WIDTH, j), pipeline_mode=deep),
            pl.BlockSpec((1, d), lambda i, j: (0, 0)),
        ],
        out_specs=[pl.BlockSpec((tm, d), lambda i, j: (i, 0))],
        _explicit_indices=True,
    )

    def outer(ym_hbm, ya_hbm, x_hbm, w1_hbm, w2_hbm, gain_hbm, o_hbm):
        pipeline(ym_hbm, ya_hbm, x_hbm, w1_hbm, w2_hbm, gain_hbm, o_hbm)

    any_space = pl.BlockSpec(memory_space=pl.ANY)
    return pl.pallas_call(
        outer,
        in_specs=[any_space] * 6,
        out_specs=any_space,
        out_shape=jax.ShapeDtypeStruct((n, d), F32),
        compiler_params=pltpu.CompilerParams(vmem_limit_bytes=VMEM_LIMIT_BYTES),
        name="out_proj",
    )(y_m, y_a, x2d, w_out, w_out, gain)


def _rel_bucket(dist):
    n = jnp.maximum(dist, 0)
    nf = jnp.maximum(n, REL_MAX_EXACT).astype(jnp.float32)
    large = REL_MAX_EXACT + (jnp.log(nf / REL_MAX_EXACT) / math.log(REL_MAX_DISTANCE / REL_MAX_EXACT)
                             * (REL_BUCKETS - REL_MAX_EXACT)).astype(jnp.int32)
    large = jnp.minimum(large, REL_BUCKETS - 1)
    return jnp.where(n < REL_MAX_EXACT, n, large)


def _toeplitz_vec(by_dist, base, n_pos, n_neg, lo, hi, shift=None):
    w = n_pos + n_neg
    c = np.arange(w)
    dist = np.where(c < n_pos, base + c, base - (w - c))
    ok = (dist >= lo) & (dist < hi)
    dmax = by_dist.shape[1]

    def run(start, length):
        left, right = max(0, -start), max(0, start + length - dmax)
        ext = jnp.pad(by_dist, ((0, 0), (left, right)), mode="edge")
        return lax.slice_in_dim(ext, start + left, start + left + length, axis=1)

    vals = jnp.concatenate([run(base, n_pos), run(base - n_neg, n_neg)], axis=1)
    if shift is not None:
        vals = vals - shift
    return jnp.where(ok[None], vals * LOG2E, NEG_LOGIT)


def _toeplitz_t(w_row, n_keys, n_q, key_step):
    x = jnp.broadcast_to(w_row, (n_keys, w_row.shape[1]))
    return pltpu.roll(x, 0, 1, stride=key_step, stride_axis=0)[:, 0:n_q]


def _bias_tables_kernel(wc_ref, ws_ref, ww_ref, bc_ref, bs_ref, bw_ref):
    nb, t = bc_ref.shape[2], bc_ref.shape[3]
    bc_ref[0, 0] = _toeplitz_t(wc_ref[0], nb, t, CMP_STRIDE)
    bs_ref[0, 0] = _toeplitz_t(ws_ref[0], bs_ref.shape[2], bs_ref.shape[3], 1)
    bw_ref[0, 0] = _toeplitz_t(ww_ref[0], bw_ref.shape[2], bw_ref.shape[3], 1)


def _bias_tables(rel_bias, t):
    tq, tk = ATT_TQ, ATT_TK
    g_, hpg = NSA_KV_GROUPS, NSA_HPG
    rb = rel_bias.astype(F32)
    dmax = REL_MAX_DISTANCE + 2 * max(SLC_TQ, SLC_TK, WIN_TQ, tq, tk)
    onehot = (_rel_bucket(jnp.arange(dmax, dtype=jnp.int32))[None, :] == jnp.arange(REL_BUCKETS, dtype=jnp.int32)[:, None])
    by_dist = jnp.dot(rb.T, onehot.astype(F32), precision=lax.Precision.HIGHEST)
    far = rb[REL_BUCKETS - 1][:, None]
    big = 1 << 30
    nb = t // CMP_STRIDE
    wc = _toeplitz_vec(by_dist, -(CMP_BLOCK - 1), t, t, 0, big)[:, None, :]
    sq, sk = min(SLC_TQ, t), SLC_TK
    r = sq // sk
    n_s = min(-(-(REL_MAX_DISTANCE + sk - 1) // sk) + r - 1, t // sk)
    wm = (n_s - 1) * sk + sq
    ws = _toeplitz_vec(by_dist, -(r - 1) * sk, wm, sk, 0, big, far)[:, None, :]
    wq = min(WIN_TQ, t)
    rw = wq // tk
    n_w = rw + (WINDOW - 1 + tk - 1) // tk
    wmw = n_w * tk + wq
    ww = _toeplitz_vec(by_dist, -(rw - 1) * tk, wmw, tk, 0, WINDOW)[:, None, :]
    return pl.pallas_call(
        _bias_tables_kernel,
        grid=(NSA_HEADS,),
        in_specs=[
            pl.BlockSpec((1, 1, 2 * t), lambda h: (h, 0, 0)),
            pl.BlockSpec((1, 1, wm + sk), lambda h: (h, 0, 0)),
            pl.BlockSpec((1, 1, wmw + tk), lambda h: (h, 0, 0)),
        ],
        out_specs=[
            pl.BlockSpec((1, 1, nb, t), lambda h: (h // hpg, h % hpg, 0, 0)),
            pl.BlockSpec((1, 1, sk, wm), lambda h: (h // hpg, h % hpg, 0, 0)),
            pl.BlockSpec((1, 1, tk, wmw), lambda h: (h // hpg, h % hpg, 0, 0)),
        ],
        out_shape=[
            jax.ShapeDtypeStruct((g_, hpg, nb, t), F32),
            jax.ShapeDtypeStruct((g_, hpg, sk, wm), F32),
            jax.ShapeDtypeStruct((g_, hpg, tk, wmw), F32),
        ],
        compiler_params=_cparams(("parallel",)),
        name="bias_tables",
    )(wc, ws, ww)


def _cover_t(t):
    nb = t // CMP_STRIDE
    n_cmp = (t - CMP_BLOCK) // CMP_STRIDE + 1
    n_slc = t // SLC_BLOCK
    cs = np.arange(nb) * CMP_STRIDE
    ss = np.arange(n_slc) * SLC_BLOCK
    cover = np.clip(np.minimum(cs[:, None] + CMP_BLOCK, ss[None, :] + SLC_BLOCK)
                    - np.maximum(cs[:, None], ss[None, :]), 0, None) / CMP_BLOCK
    cover[n_cmp:] = 0.0
    return jnp.asarray(cover.T, dtype=BF16)


def kernel(x, norm_gain, w_in, w_conv, b_igate, b_fgate, mlstm_norm_gain, cmp_k_pos, cmp_k_w1, cmp_k_w2,
           cmp_v_pos, cmp_v_w1, cmp_v_w2, rel_bias, w_out, final_norm_gain):
    bsz, t, d = x.shape
    assert d == D_MODEL and t % MLSTM_L == 0 and t % ATT_TQ == 0 and (t // CMP_STRIDE) % LANES == 0
    n = bsz * t
    x2d = x.reshape(n, d)

    w_main, w_gate = _w_prep(w_in.T)
    proj, gates = _in_proj(x2d, norm_gain.reshape(1, d).astype(F32), w_main, w_gate)

    g_rows = gates[:, :2 * MLSTM_HEADS].reshape(bsz, t, 2, MLSTM_HEADS).transpose(0, 3, 2, 1)
    y_m = _mlstm(proj, g_rows, w_conv.astype(F32), b_igate.astype(F32), b_fgate.astype(F32),
                 mlstm_norm_gain.reshape(1, MLSTM_WIDTH).astype(F32), bsz, t)

    dh = NSA_HEAD_DIM
    half = CMP_BLOCK // 2

    def w1cat(w1):
        return jnp.concatenate([w1[:half].reshape(half * dh, dh), w1[half:].reshape(half * dh, dh)], axis=1).astype(BF16)

    k_cmp, v_cmp_t = _compress(
        proj, w1cat(cmp_k_w1), cmp_k_w2.astype(BF16), cmp_k_pos.reshape(2, half * dh).astype(BF16),
        w1cat(cmp_v_w1), cmp_v_w2.astype(BF16), cmp_v_pos.reshape(2, half * dh).astype(BF16), bsz, t)

    bias_c, bias_s, bias_w = _bias_tables(rel_bias, t)
    gates_t = gates.T
    o_cmp, selbias_t = _cmp_attn(proj, gates_t, k_cmp, v_cmp_t, bias_c, _cover_t(t), bsz, t)
    o_slc = _slc_attn(proj, gates_t, selbias_t, bias_s, bsz, t)
    y_a = _win_attn(proj, gates_t, o_cmp, o_slc, bias_w, bsz, t)

    out = _out_proj(y_m, y_a, x2d, w_out.astype(BF16), final_norm_gain.reshape(1, d).astype(F32))
    return out.reshape(bsz, t, d)
```

```python
import functools
import math

import jax
import jax.numpy as jnp
import numpy as np
from jax import lax
from jax.experimental import pallas as pl
from jax.experimental.pallas import tpu as pltpu

F32 = jnp.float32
BF16 = jnp.bfloat16

D_MODEL = 4096
D_MIX = D_MODEL
MLSTM_WIDTH = D_MIX // 2
MLSTM_HEADS = 4
MLSTM_V_DIM = MLSTM_WIDTH // MLSTM_HEADS
MLSTM_QK_DIM = MLSTM_V_DIM // 2
MLSTM_QK_WIDTH = MLSTM_HEADS * MLSTM_QK_DIM
CONV_WIDTH = 4
NSA_WIDTH = D_MIX - MLSTM_WIDTH
NSA_HEAD_DIM = 128
NSA_HEADS = NSA_WIDTH // NSA_HEAD_DIM
NSA_KV_GROUPS = 4
NSA_HPG = NSA_HEADS // NSA_KV_GROUPS
NSA_KV_WIDTH = NSA_KV_GROUPS * NSA_HEAD_DIM
CMP_BLOCK = 32
CMP_STRIDE = 16
SLC_BLOCK = 64
SLC_TOP_N = 16
WINDOW = 512
FORCE_BONUS = 1000.0
REL_BUCKETS = 32
REL_MAX_EXACT = REL_BUCKETS // 2
REL_MAX_DISTANCE = 1024
RMS_EPS = 1e-6
NEG_LOGIT = -1e30

LANES = 128
VMEM_LIMIT_BYTES = 56 * 1024 * 1024

COL_MQ = 0
COL_MK = COL_MQ + MLSTM_QK_WIDTH
COL_MV = COL_MK + MLSTM_QK_WIDTH
COL_MO = COL_MV + MLSTM_WIDTH
COL_MZ = COL_MO + MLSTM_WIDTH
COL_AQ = COL_MZ + MLSTM_WIDTH
COL_CK = COL_AQ + NSA_WIDTH
COL_CV = COL_CK + NSA_KV_WIDTH
COL_SK = COL_CV + NSA_KV_WIDTH
COL_SV = COL_SK + NSA_KV_WIDTH
COL_WK = COL_SV + NSA_KV_WIDTH
COL_WV = COL_WK + NSA_KV_WIDTH
COL_AZ = COL_WV + NSA_KV_WIDTH
MAIN_COLS = COL_AZ + NSA_WIDTH
GATE_COLS = LANES
GATE_I = 0
GATE_F = MLSTM_HEADS
GATE_NSA = 2 * MLSTM_HEADS

INPROJ_TM = 512
INPROJ_TN = 1536
OUTPROJ_TM = 512
OUTPROJ_TN = 1024
OUTPROJ_W_BUFFERS = 3
MLSTM_L = 256
ATT_TQ = 256
ATT_TK = 256
WPREP_TR = 512
SLC_TQ = 512
SLC_TK = 512
CMP_TQ = 512
CMP_GROUPS = 2
WIN_TQ = 256
WIN_GROUPS = 2
MASK_BIG = 2.0 ** 100


def _cparams(sem):
    return pltpu.CompilerParams(dimension_semantics=sem, vmem_limit_bytes=VMEM_LIMIT_BYTES)


def _sigmoid(x):
    return 0.5 * jnp.tanh(0.5 * x) + 0.5


def _silu(x):
    return x * _sigmoid(x)


def _dot(a, b):
    return jnp.dot(a, b, preferred_element_type=F32)


def _dot_nt(a, b):
    return lax.dot_general(a, b, (((1,), (1,)), ((), ())), preferred_element_type=F32)


W_OFF_I = COL_MZ + MLSTM_WIDTH
W_OFF_AQ = W_OFF_I + 2 * MLSTM_HEADS
W_OFF_GATE = W_OFF_AQ + NSA_WIDTH + 6 * NSA_KV_WIDTH
W_OFF_AZ = W_OFF_GATE + 3 * NSA_HEADS


def _wprep_kernel(w_ref, gi_ref, ga_ref, o_ref, g_ref):
    o_ref[...] = w_ref[...].astype(BF16)

    @pl.when(pl.program_id(0) == 0)
    def _():
        pad = jnp.zeros((GATE_COLS - gi_ref.shape[0] - ga_ref.shape[0], g_ref.shape[1]), F32)
        g_ref[...] = jnp.concatenate([gi_ref[...], ga_ref[...], pad], axis=0).astype(BF16)


def _w_prep(w_t):
    rows, d = w_t.shape
    tr = WPREP_TR
    nb_a, nb_b = W_OFF_I // tr, (W_OFF_GATE - W_OFF_AQ) // tr

    def src(m):
        skip = jnp.where(m >= nb_a, W_OFF_AQ - W_OFF_I, 0) + jnp.where(m >= nb_a + nb_b, W_OFF_AZ - W_OFF_GATE, 0)
        return pl.multiple_of(m * tr + skip, 8)

    rows_at = lambda start, size: pl.BlockSpec((pl.Element(size), pl.Element(d)), lambda m: (start, 0))
    return pl.pallas_call(
        _wprep_kernel,
        grid=(MAIN_COLS // tr,),
        in_specs=[
            pl.BlockSpec((pl.Element(tr), pl.Element(d)), lambda m: (src(m), 0)),
            rows_at(W_OFF_I, W_OFF_AQ - W_OFF_I),
            rows_at(W_OFF_GATE, W_OFF_AZ - W_OFF_GATE),
        ],
        out_specs=[pl.BlockSpec((tr, d), lambda m: (m, 0)), pl.BlockSpec((GATE_COLS, d), lambda m: (0, 0))],
        out_shape=[jax.ShapeDtypeStruct((MAIN_COLS, d), BF16), jax.ShapeDtypeStruct((GATE_COLS, d), BF16)],
        compiler_params=_cparams(("arbitrary",)),
        name="w_prep",
    )(w_t, w_t, w_t)


def _inproj_kernel(x_ref, gain_ref, w_hbm, wg_ref, o_hbm, og_ref, h_ref, wbuf, obuf, w_sem, o_sem):
    i = pl.program_id(0)
    ni = pl.num_programs(0)
    tm, tn = obuf.shape[1], obuf.shape[2]
    nj = w_hbm.shape[0] // tn

    def w_copy(j, slot):
        return pltpu.make_async_copy(w_hbm.at[pl.ds(pl.multiple_of(j * tn, tn), tn), :], wbuf.at[slot], w_sem.at[slot])

    def o_copy(j, slot):
        dst = o_hbm.at[pl.ds(pl.multiple_of(i * tm, tm), tm), pl.ds(pl.multiple_of(j * tn, tn), tn)]
        return pltpu.make_async_copy(obuf.at[slot], dst, o_sem.at[slot])

    @pl.when(i == 0)
    def _():
        w_copy(0, 0).start()

    x = x_ref[...]
    ms = jnp.mean(x * x, axis=-1, keepdims=True)
    h = (x * lax.rsqrt(ms + RMS_EPS) * gain_ref[...]).astype(BF16)
    h_ref[...] = h
    og_ref[...] = _dot_nt(h, wg_ref[...])

    def body(j, carry):
        slot = j % 2
        nxt = (j + 1) % nj
        w_copy(j, slot).wait()

        @pl.when((j + 1 < nj) | (i + 1 < ni))
        def _():
            w_copy(nxt, 1 - slot).start()

        @pl.when((j >= 2) | (i > 0))
        def _():
            o_copy(j, slot).wait()

        obuf[slot] = _dot_nt(h_ref[...], wbuf[slot]).astype(BF16)
        o_copy(j, slot).start()
        return carry

    lax.fori_loop(0, nj, body, 0)

    @pl.when(i == ni - 1)
    def _():
        o_copy(nj - 2, (nj - 2) % 2).wait()
        o_copy(nj - 1, (nj - 1) % 2).wait()


def _in_proj(x2d, gain, w_main, w_gate):
    n, d = x2d.shape
    tm, tn = min(INPROJ_TM, n), INPROJ_TN
    assert (MAIN_COLS // tn) % 2 == 0
    return pl.pallas_call(
        _inproj_kernel,
        grid=(n // tm,),
        in_specs=[
            pl.BlockSpec((tm, d), lambda i: (i, 0)),
            pl.BlockSpec((1, d), lambda i: (0, 0)),
            pl.BlockSpec(memory_space=pl.ANY),
            pl.BlockSpec((GATE_COLS, d), lambda i: (0, 0)),
        ],
        out_specs=[
            pl.BlockSpec(memory_space=pl.ANY),
            pl.BlockSpec((tm, GATE_COLS), lambda i: (i, 0)),
        ],
        out_shape=[
            jax.ShapeDtypeStruct((n, MAIN_COLS), BF16),
            jax.ShapeDtypeStruct((n, GATE_COLS), F32),
        ],
        scratch_shapes=[
            pltpu.VMEM((tm, d), BF16),
            pltpu.VMEM((2, tn, d), BF16),
            pltpu.VMEM((2, tm, tn), BF16),
            pltpu.SemaphoreType.DMA((2,)),
            pltpu.SemaphoreType.DMA((2,)),
        ],
        compiler_params=_cparams(("arbitrary",)),
        name="in_proj",
    )(x2d, gain, w_main, w_gate)


def _mlstm_kernel(bi_ref, bf_ref, q_ref, k_ref, v_ref, o_ref, z_ref, g_ref, wq_ref, wk_ref, ng_ref,
                  y_ref, qext, kext, c_st, n_st, m_st):
    L = MLSTM_L
    HIST = 8
    dqk, dv = MLSTM_QK_DIM, MLSTM_V_DIM

    @pl.when(pl.program_id(1) == 0)
    def _():
        qext[:, 0:HIST, :] = jnp.zeros((MLSTM_HEADS, HIST, dqk), F32)
        kext[:, 0:HIST, :] = jnp.zeros((MLSTM_HEADS, HIST, dqk), F32)
        c_st[...] = jnp.zeros_like(c_st)
        n_st[...] = jnp.zeros_like(n_st)
        m_st[...] = jnp.zeros_like(m_st)

    rr = lax.broadcasted_iota(jnp.int32, (L, L), 0)
    cc = lax.broadcasted_iota(jnp.int32, (L, L), 1)
    upper = (rr <= cc).astype(F32)

    for hd in range(MLSTM_HEADS):
        qcols = slice(hd * dqk, (hd + 1) * dqk)
        vcols = slice(hd * dv, (hd + 1) * dv)
        qext[hd, HIST:HIST + L, :] = q_ref[:, qcols].astype(F32)
        kext[hd, HIST:HIST + L, :] = k_ref[:, qcols].astype(F32)

        def conv_silu(ext, w_ref):
            w = w_ref[:, qcols]
            y = ext[hd, pl.ds(HIST, L), :] * w[CONV_WIDTH - 1:CONV_WIDTH, :]
            for s in range(1, CONV_WIDTH):
                y = y + ext[hd, pl.ds(HIST - s, L), :] * w[CONV_WIDTH - 1 - s:CONV_WIDTH - s, :]
            return _silu(y)

        qc = conv_silu(qext, wq_ref)
        kc = conv_silu(kext, wk_ref) * (dqk ** -0.5)
        qext[hd, 0:HIST, :] = qext[hd, L:L + HIST, :]
        kext[hd, 0:HIST, :] = kext[hd, L:L + HIST, :]

        g = g_ref[0, hd]
        i_row = g[0:1, :] + bi_ref[hd]
        f_row = g[1:2, :] + bf_ref[hd]
        lf_row = jnp.minimum(f_row, 0.0) - jnp.log(1.0 + jnp.exp(-jnp.abs(f_row)))

        bcum_row = jnp.dot(jnp.broadcast_to(lf_row, (8, L)), upper, preferred_element_type=F32,
                           precision=lax.Precision.HIGHEST)[0:1, :]
        bcum_col = jnp.sum(jnp.where(rr == cc, bcum_row, 0.0), axis=1, keepdims=True)
        gsum = bcum_row[:, L - 1:L]
        m_prev = m_st[hd]

        dlog = jnp.where(rr >= cc, bcum_col - bcum_row + i_row, -jnp.inf)
        m_inter = bcum_col + m_prev
        m_t = jnp.maximum(m_inter, jnp.max(dlog, axis=1, keepdims=True))
        dmat = jnp.exp(dlog - m_t)
        inter = jnp.exp(m_inter - m_t)

        qb = qc.astype(BF16)
        kct = kc.T
        vb = v_ref[:, vcols]
        s = _dot(qb, kct.astype(BF16)) * dmat
        c_prev = c_st[hd]
        n_prev = n_st[hd]
        num = _dot(s.astype(BF16), vb) + inter * _dot(qb, c_prev.astype(BF16))
        qn = jnp.sum(s, axis=1, keepdims=True) + inter * jnp.sum(qc * n_prev, axis=1, keepdims=True)
        hh = num / jnp.maximum(jnp.abs(qn), jnp.exp(-m_t))

        wlog = gsum - bcum_row + i_row
        m_next = jnp.maximum(gsum + m_prev, jnp.max(wlog, axis=1, keepdims=True))
        wts = jnp.exp(wlog - m_next)
        keep = jnp.exp(gsum + m_prev - m_next)
        c_st[hd] = keep * c_prev + _dot((kct * wts).astype(BF16), vb)
        n_st[hd] = keep * n_prev + _dot(jnp.broadcast_to(wts, (8, L)).astype(BF16), kc.astype(BF16))[0:1, :]
        m_st[hd] = m_next

        hm = _sigmoid(o_ref[:, vcols].astype(F32)) * hh
        hm = hm * lax.rsqrt(jnp.mean(hm * hm, axis=-1, keepdims=True) + RMS_EPS)
        hm = hm * ng_ref[:, vcols]
        y_ref[:, vcols] = (hm * _silu(z_ref[:, vcols].astype(F32))).astype(BF16)


def _mlstm(proj, g_rows, w_conv, b_igate, b_fgate, norm_gain, bsz, t):
    L = MLSTM_L
    nc = t // L
    nh, dqk, dv = MLSTM_HEADS, MLSTM_QK_DIM, MLSTM_V_DIM
    row = lambda b, c: b * nc + c
    smem = pl.BlockSpec(memory_space=pltpu.SMEM)
    return pl.pallas_call(
        _mlstm_kernel,
        grid=(bsz, nc),
        in_specs=[
            smem, smem,
            pl.BlockSpec((L, MLSTM_QK_WIDTH), lambda b, c: (row(b, c), COL_MQ // MLSTM_QK_WIDTH)),
            pl.BlockSpec((L, MLSTM_QK_WIDTH), lambda b, c: (row(b, c), COL_MK // MLSTM_QK_WIDTH)),
            pl.BlockSpec((L, MLSTM_WIDTH), lambda b, c: (row(b, c), COL_MV // MLSTM_WIDTH)),
            pl.BlockSpec((L, MLSTM_WIDTH), lambda b, c: (row(b, c), COL_MO // MLSTM_WIDTH)),
            pl.BlockSpec((L, MLSTM_WIDTH), lambda b, c: (row(b, c), COL_MZ // MLSTM_WIDTH)),
            pl.BlockSpec((1, nh, 2, L), lambda b, c: (b, 0, 0, c)),
            pl.BlockSpec((CONV_WIDTH, MLSTM_QK_WIDTH), lambda b, c: (0, 0)),
            pl.BlockSpec((CONV_WIDTH, MLSTM_QK_WIDTH), lambda b, c: (0, 1)),
            pl.BlockSpec((1, MLSTM_WIDTH), lambda b, c: (0, 0)),
        ],
        out_specs=pl.BlockSpec((L, MLSTM_WIDTH), lambda b, c: (row(b, c), 0)),
        out_shape=jax.ShapeDtypeStruct((bsz * t, MLSTM_WIDTH), BF16),
        scratch_shapes=[
            pltpu.VMEM((nh, L + 8, dqk), F32),
            pltpu.VMEM((nh, L + 8, dqk), F32),
            pltpu.VMEM((nh, dqk, dv), F32),
            pltpu.VMEM((nh, 1, dqk), F32),
            pltpu.VMEM((nh, 1, 1), F32),
        ],
        compiler_params=_cparams(("parallel", "arbitrary")),
        name="mlstm",
    )(b_igate, b_fgate, proj, proj, proj, proj, proj, g_rows, w_conv, w_conv, norm_gain)


def _compress_kernel(ck_ref, cv_ref, w1k_ref, w2k_ref, pk_ref, w1v_ref, w2v_ref, pv_ref,
                     ok_ref, ov_ref, xf, xcat):
    t = ck_ref.shape[0]
    nb = t // CMP_STRIDE
    dh = NSA_HEAD_DIM

    def one(src_ref, w1_ref, w2_ref, pos_ref):
        xf[...] = src_ref[...].astype(F32)
        for l in range(CMP_STRIDE):
            xcat[:, l * dh:(l + 1) * dh] = xf[pl.ds(l, nb, stride=CMP_STRIDE), :].astype(BF16)
        w1 = w1_ref[...]
        ab = _dot(xcat[...], w1)
        pp = _dot(pos_ref[...], w1)
        pos_term = pp[0:1, 0:dh] + pp[1:2, dh:2 * dh]
        second = pltpu.roll(ab[:, dh:2 * dh], nb - 1, 0)
        hid = _silu(ab[:, 0:dh] + second + pos_term)
        return _dot(hid.astype(BF16), w2_ref[...])

    ok_ref[0, 0] = one(ck_ref, w1k_ref, w2k_ref, pk_ref).astype(BF16)
    ov_ref[0, 0] = one(cv_ref, w1v_ref, w2v_ref, pv_ref).T.astype(BF16)


def _compress(proj, w1k, w2k, pk, w1v, w2v, pv, bsz, t):
    g_, dh = NSA_KV_GROUPS, NSA_HEAD_DIM
    nb = t // CMP_STRIDE
    full = lambda a: pl.BlockSpec(a.shape, lambda b, g: (0,) * a.ndim)
    k_spec = pl.BlockSpec((1, 1, nb, dh), lambda b, g: (b, g, 0, 0))
    vt_spec = pl.BlockSpec((1, 1, dh, nb), lambda b, g: (b, g, 0, 0))
    return pl.pallas_call(
        _compress_kernel,
        grid=(bsz, g_),
        in_specs=[
            pl.BlockSpec((t, dh), lambda b, g: (b, COL_CK // dh + g)),
            pl.BlockSpec((t, dh), lambda b, g: (b, COL_CV // dh + g)),
            full(w1k), full(w2k), full(pk), full(w1v), full(w2v), full(pv),
        ],
        out_specs=[k_spec, vt_spec],
        out_shape=[jax.ShapeDtypeStruct((bsz, g_, nb, dh), BF16), jax.ShapeDtypeStruct((bsz, g_, dh, nb), BF16)],
        scratch_shapes=[pltpu.VMEM((t, dh), F32), pltpu.VMEM((nb, CMP_STRIDE * dh), BF16)],
        compiler_params=_cparams(("parallel", "parallel")),
        name="compress",
    )(proj, proj, w1k, w2k, pk, w1v, w2v, pv)


LOG2E = math.log2(math.e)
VROWS = NSA_HEAD_DIM + 16


def _q_scaled(q_ref, hh):
    dh = NSA_HEAD_DIM
    return (q_ref[:, hh * dh:(hh + 1) * dh].astype(F32) * (dh ** -0.5 * LOG2E)).astype(BF16)


def _q_t(q_ref, hh):
    dh = NSA_HEAD_DIM
    return (q_ref[:, hh * dh:(hh + 1) * dh].astype(F32) * (dh ** -0.5 * LOG2E)).T.astype(BF16)


def _gate_row(gt_ref, g, hh, branch):
    row = GATE_NSA + 3 * (g * NSA_HPG + hh) + branch
    return _sigmoid(gt_ref[pl.ds(row, 1), :])


def _fill_vt(vt, v_ref):
    dh = NSA_HEAD_DIM
    vt[0:dh, :] = v_ref[...].astype(F32).T.astype(BF16)
    vt[dh:VROWS, :] = jnp.ones((VROWS - dh, vt.shape[1]), BF16)


def _out_t(acc, gate_row):
    dh = NSA_HEAD_DIM
    return acc[0:dh, :] * (gate_row / acc[dh:dh + 1, :])


def _cmp_attn_kernel(q_ref, kc_ref, vct_ref, bias_ref, cov_ref, gate_ref, oc_ref, sel_ref, score_ref, *, n_sel):
    tq = q_ref.shape[0]
    dh = NSA_HEAD_DIM
    n_slc = cov_ref.shape[0]
    t0 = pl.program_id(1) * tq
    gp = pl.program_id(0)
    heads = [(gl, hh) for gl in range(CMP_GROUPS) for hh in range(NSA_HPG)]
    qk = [_dot_nt(kc_ref[0, gl], _q_scaled(q_ref, gl * NSA_HPG + hh)) for gl, hh in heads]
    ps = []
    for n, (gl, hh) in enumerate(heads):
        logit = qk[n] + bias_ref[gl, hh]
        m = jnp.max(logit, axis=0, keepdims=True)
        e = jnp.exp2(logit - m)
        inv = jnp.where(m > 0.5 * NEG_LOGIT, 1.0 / jnp.sum(e, axis=0, keepdims=True), 0.0)
        ps.append(e * inv)
    ots = [_dot(vct_ref[0, gl], ps[n].astype(BF16)) for n, (gl, hh) in enumerate(heads)]
    for n, (gl, hh) in enumerate(heads):
        oc_ref[0, n] = (ots[n] * _gate_row(gate_ref, gp * CMP_GROUPS + gl, hh, 0)).astype(BF16)

    cov = cov_ref[...]
    jb = lax.broadcasted_iota(jnp.int32, (n_slc, tq), 0)
    cur = (t0 + lax.broadcasted_iota(jnp.int32, (n_slc, tq), 1)) // SLC_BLOCK
    valid = jb <= cur
    forced = (jb == 0) | (jb == cur) | (jb == cur - 1)
    sub = 8
    row_id = lax.broadcasted_iota(jnp.int32, (sub, tq), 0)
    for gl in range(CMP_GROUPS):
        g4 = ps[gl * NSA_HPG:(gl + 1) * NSA_HPG]
        p_sum = (g4[0] + g4[1]) + (g4[2] + g4[3])
        p_hi = p_sum.astype(BF16)
        r1 = p_sum - p_hi.astype(F32)
        p_mid = r1.astype(BF16)
        p_lo = (r1 - p_mid.astype(F32)).astype(BF16)
        st = _dot(cov, p_hi) + _dot(cov, p_mid) + _dot(cov, p_lo)
        score_ref[gl] = jnp.where(valid, st + jnp.where(forced, FORCE_BONUS, 0.0), -1.0)
        groups = [score_ref[gl, r:r + sub, :] for r in range(0, n_slc, sub)]
        ranks = [jnp.zeros((sub, tq), F32) for _ in groups]
        for j2 in range(n_slc):
            row = score_ref[gl, j2:j2 + 1, :]
            for gi, sc in enumerate(groups):
                r0 = gi * sub
                if r0 > j2:
                    inc = jnp.where(row >= sc, 1.0, 0.0)
                elif r0 + sub - 1 <= j2:
                    inc = jnp.where(row > sc, 1.0, 0.0)
                else:
                    inc = jnp.where(row_id > j2 - r0, jnp.where(row >= sc, 1.0, 0.0), jnp.where(row > sc, 1.0, 0.0))
                ranks[gi] = ranks[gi] + inc
        for gi, rk in enumerate(ranks):
            score_ref[gl, gi * sub:(gi + 1) * sub, :] = rk
        sel = valid & (score_ref[gl] < n_sel)
        sel_ref[0, gl, 0:n_slc, :] = jnp.where(sel, 0.0, -MASK_BIG).astype(BF16)
        if n_slc < LANES:
            sel_ref[0, gl, n_slc:LANES, :] = jnp.zeros((LANES - n_slc, tq), BF16)


def _cmp_attn(proj, gates, k_cmp, v_cmp_t, bias_c, cover_t, bsz, t):
    g_, dh, tq = NSA_KV_GROUPS, NSA_HEAD_DIM, CMP_TQ
    nt = t // tq
    nb = t // CMP_STRIDE
    n_slc = t // SLC_BLOCK
    ng = CMP_GROUPS
    gw = ng * NSA_HPG * dh
    return pl.pallas_call(
        functools.partial(_cmp_attn_kernel, n_sel=min(SLC_TOP_N, n_slc)),
        grid=(g_ // ng, nt, bsz),
        in_specs=[
            pl.BlockSpec((tq, gw), lambda g, i, b: (b * nt + i, COL_AQ // gw + g)),
            pl.BlockSpec((1, ng, nb, dh), lambda g, i, b: (b, g, 0, 0)),
            pl.BlockSpec((1, ng, dh, nb), lambda g, i, b: (b, g, 0, 0)),
            pl.BlockSpec((ng, NSA_HPG, nb, tq), lambda g, i, b: (g, 0, 0, i)),
            pl.BlockSpec((n_slc, nb), lambda g, i, b: (0, 0)),
            pl.BlockSpec((GATE_COLS, tq), lambda g, i, b: (0, b * nt + i)),
        ],
        out_specs=[
            pl.BlockSpec((1, ng * NSA_HPG, dh, tq), lambda g, i, b: (b, g, 0, i)),
            pl.BlockSpec((1, ng, LANES, tq), lambda g, i, b: (b, g, 0, i)),
        ],
        out_shape=[
            jax.ShapeDtypeStruct((bsz, NSA_HEADS, dh, t), BF16),
            jax.ShapeDtypeStruct((bsz, g_, LANES, t), BF16),
        ],
        scratch_shapes=[pltpu.VMEM((ng, n_slc, tq), F32)],
        compiler_params=_cparams(("parallel", "parallel", "parallel")),
        name="cmp_attn",
    )(proj, k_cmp, v_cmp_t, bias_c, cover_t, gates)


def _slc_attn_kernel(q_ref, sbt_ref, k_ref, v_ref, bias_ref, gate_ref, os_ref,
                     kaug, vt, qt, s_buf, p_buf, a_buf, m_ref, acc_ref, *, n_near):
    tq, tk, dh = SLC_TQ, SLC_TK, NSA_HEAD_DIM
    r = tq // tk
    t = k_ref.shape[0]
    g = pl.program_id(1)
    i = pl.program_id(2)
    heads = range(NSA_HPG)

    @pl.when(i == 0)
    def _():
        kaug[:, 0:dh] = k_ref[...]
        blk = lax.broadcasted_iota(jnp.int32, (t, LANES), 0) // SLC_BLOCK
        lane = lax.broadcasted_iota(jnp.int32, (t, LANES), 1)
        kaug[:, dh:dh + LANES] = jnp.where(blk == lane, 1.0, 0.0).astype(BF16)
        _fill_vt(vt, v_ref)

    sbt = sbt_ref[0, 0]
    for hh in heads:
        qt[hh, 0:dh, :] = _q_t(q_ref, hh)
        qt[hh, dh:dh + LANES, :] = sbt

    def ktile(j):
        return kaug[pl.ds(pl.multiple_of(j * tk, tk), tk), :]

    def vtile(j):
        return vt[:, pl.ds(pl.multiple_of(j * tk, tk), tk)]

    m_ref[...] = jnp.full(m_ref.shape, NEG_LOGIT, F32)
    acc_ref[...] = jnp.zeros_like(acc_ref)
    k0 = ktile(0)
    for hh in heads:
        s_buf[hh] = _dot(k0, qt[hh])

    last = (i + 1) * r - 1

    def stage(j, bias_fn, with_pv=True, with_qk=True):
        if with_pv:
            v_prev = vtile(j - 1)
            pv = [_dot(v_prev, p_buf[hh]) for hh in heads]
        if with_qk:
            k_next = ktile(j + 1)
            s_next = [_dot(k_next, qt[hh]) for hh in heads]
        if with_pv:
            for hh in heads:
                acc_ref[hh] = a_buf[hh] * acc_ref[hh] + pv[hh]
        for hh in heads:
            s = s_buf[hh]
            if bias_fn is not None:
                s = s + bias_fn(hh)
            m_prev = m_ref[hh]
            m_new = jnp.maximum(m_prev, jnp.max(s, axis=0, keepdims=True))
            a_buf[hh] = jnp.exp2(m_prev - m_new)
            p_buf[hh] = jnp.exp2(s - m_new).astype(BF16)
            m_ref[hh] = m_new
        if with_qk:
            for hh in heads:
                s_buf[hh] = s_next[hh]

    n_far = jnp.maximum(i * r + r - n_near, 0)

    def near_bias(j):
        off = pl.multiple_of((i * r + (r - 1) - j) * tk, tk)
        return lambda hh: bias_ref[0, hh, :, pl.ds(off, tq)]

    @pl.when(n_far > 0)
    def _():
        stage(0, None, with_pv=False)

    @pl.when((n_far == 0) & (last > 0))
    def _():
        stage(0, near_bias(0), with_pv=False)

    @pl.when(last == 0)
    def _():
        stage(0, near_bias(0), with_pv=False, with_qk=False)

    def far_body(j, c):
        stage(j, None)
        return c

    lax.fori_loop(1, n_far, far_body, 0)

    def near_body(j, c):
        stage(j, near_bias(j))
        return c

    lax.fori_loop(jnp.maximum(n_far, 1), last, near_body, 0)

    @pl.when(last > 0)
    def _():
        stage(last, near_bias(last), with_qk=False)

    v_last = vtile(last)
    pv = [_dot(v_last, p_buf[hh]) for hh in heads]
    for hh in heads:
        acc = a_buf[hh] * acc_ref[hh] + pv[hh]
        os_ref[0, hh] = _out_t(acc, _gate_row(gate_ref, g, hh, 1)).astype(BF16)


def _slc_attn(proj, gates, selbias_t, bias_s, bsz, t):
    g_, dh, tq, tk = NSA_KV_GROUPS, NSA_HEAD_DIM, SLC_TQ, SLC_TK
    nt = t // tq
    gw = NSA_HPG * dh
    n_near = (bias_s.shape[3] - tq) // tk + 1
    return pl.pallas_call(
        functools.partial(_slc_attn_kernel, n_near=n_near),
        grid=(bsz, g_, nt),
        in_specs=[
            pl.BlockSpec((tq, gw), lambda b, g, i: (b * nt + i, COL_AQ // gw + g)),
            pl.BlockSpec((1, 1, LANES, tq), lambda b, g, i: (b, g, 0, i)),
            pl.BlockSpec((t, dh), lambda b, g, i: (b, COL_SK // dh + g)),
            pl.BlockSpec((t, dh), lambda b, g, i: (b, COL_SV // dh + g)),
            pl.BlockSpec((1, NSA_HPG, tk, bias_s.shape[3]), lambda b, g, i: (g, 0, 0, 0)),
            pl.BlockSpec((GATE_COLS, tq), lambda b, g, i: (0, b * nt + i)),
        ],
        out_specs=pl.BlockSpec((1, NSA_HPG, dh, tq), lambda b, g, i: (b, g, 0, i)),
        out_shape=jax.ShapeDtypeStruct((bsz, NSA_HEADS, dh, t), BF16),
        scratch_shapes=[
            pltpu.VMEM((t, dh + LANES), BF16),
            pltpu.VMEM((VROWS, t), BF16),
            pltpu.VMEM((NSA_HPG, dh + LANES, tq), BF16),
            pltpu.VMEM((NSA_HPG, tk, tq), F32),
            pltpu.VMEM((NSA_HPG, tk, tq), BF16),
            pltpu.VMEM((NSA_HPG, 1, tq), F32),
            pltpu.VMEM((NSA_HPG, 1, tq), F32),
            pltpu.VMEM((NSA_HPG, VROWS, tq), F32),
        ],
        compiler_params=_cparams(("parallel", "parallel", "arbitrary")),
        name="slc_attn",
    )(proj, selbias_t, proj, proj, bias_s, gates)


def _win_attn_kernel(q_ref, k_ref, v_ref, bias_ref, gate_ref, oc_ref, os_ref, z_ref, ya_ref, vt, *, n_tiles):
    tq, tk, dh = WIN_TQ, ATT_TK, NSA_HEAD_DIM
    r = tq // tk
    gp = pl.program_id(1)
    i = pl.program_id(2)
    heads = [(gl, hh) for gl in range(WIN_GROUPS) for hh in range(NSA_HPG)]

    @pl.when(i == 0)
    def _():
        for gl in range(WIN_GROUPS):
            vt[gl, 0:dh, :] = v_ref[:, gl * dh:(gl + 1) * dh].astype(F32).T.astype(BF16)
            vt[gl, dh:VROWS, :] = jnp.ones((VROWS - dh, vt.shape[2]), BF16)

    js = [i * r + (r - 1) - d for d in range(n_tiles)]
    offs = [pl.multiple_of(jnp.maximum(j, 0) * tk, tk) for j in js]
    boffs = [pl.multiple_of(jnp.where(js[d] >= 0, d, n_tiles) * tk, tk) for d in range(n_tiles)]
    k_tiles = [[k_ref[pl.ds(off, tk), gl * dh:(gl + 1) * dh] for off in offs] for gl in range(WIN_GROUPS)]
    v_tiles = [[vt[gl, :, pl.ds(off, tk)] for off in offs] for gl in range(WIN_GROUPS)]
    qs = [_q_scaled(q_ref, gl * NSA_HPG + hh) for gl, hh in heads]
    ss = [[_dot_nt(k_tiles[gl][d], qs[n]) for d in range(n_tiles)] for n, (gl, hh) in enumerate(heads)]
    ps = []
    for n, (gl, hh) in enumerate(heads):
        s = [ss[n][d] + bias_ref[gl, hh, :, pl.ds(boffs[d], tq)] for d in range(n_tiles)]
        m = functools.reduce(jnp.maximum, [jnp.max(x, axis=0, keepdims=True) for x in s])
        ps.append([jnp.exp2(x - m).astype(BF16) for x in s])
    pvs = [[_dot(v_tiles[gl][d], ps[n][d]) for d in range(n_tiles)] for n, (gl, hh) in enumerate(heads)]
    for n, (gl, hh) in enumerate(heads):
        cols = slice(n * dh, (n + 1) * dh)
        o = _out_t(functools.reduce(lambda x, y: x + y, pvs[n]), _gate_row(gate_ref, gp * WIN_GROUPS + gl, hh, 2))
        o = o + oc_ref[0, n].astype(F32) + os_ref[0, n].astype(F32)
        ya_ref[:, cols] = (o.T * _silu(z_ref[:, cols].astype(F32))).astype(BF16)


def _win_attn(proj, gates, o_cmp, o_slc, bias_w, bsz, t):
    g_, dh, tq, tk = NSA_KV_GROUPS, NSA_HEAD_DIM, WIN_TQ, ATT_TK
    nt = t // tq
    ng = WIN_GROUPS
    gw = ng * NSA_HPG * dh
    kw = ng * dh
    n_tiles = (bias_w.shape[3] - tq) // tk
    return pl.pallas_call(
        functools.partial(_win_attn_kernel, n_tiles=n_tiles),
        grid=(bsz, g_ // ng, nt),
        in_specs=[
            pl.BlockSpec((tq, gw), lambda b, g, i: (b * nt + i, COL_AQ // gw + g)),
            pl.BlockSpec((t, kw), lambda b, g, i: (b, COL_WK // kw + g)),
            pl.BlockSpec((t, kw), lambda b, g, i: (b, COL_WV // kw + g)),
            pl.BlockSpec((ng, NSA_HPG, tk, bias_w.shape[3]), lambda b, g, i: (g, 0, 0, 0)),
            pl.BlockSpec((GATE_COLS, tq), lambda b, g, i: (0, b * nt + i)),
            pl.BlockSpec((1, ng * NSA_HPG, dh, tq), lambda b, g, i: (b, g, 0, i)),
            pl.BlockSpec((1, ng * NSA_HPG, dh, tq), lambda b, g, i: (b, g, 0, i)),
            pl.BlockSpec((tq, gw), lambda b, g, i: (b * nt + i, COL_AZ // gw + g)),
        ],
        out_specs=pl.BlockSpec((tq, gw), lambda b, g, i: (b * nt + i, g)),
        out_shape=jax.ShapeDtypeStruct((bsz * t, NSA_WIDTH), BF16),
        scratch_shapes=[pltpu.VMEM((ng, VROWS, t), BF16)],
        compiler_params=_cparams(("parallel", "parallel", "arbitrary")),
        name="win_attn",
    )(proj, proj, proj, bias_w, gates, o_cmp, o_slc, proj)


def _outproj_step(idx, ym_ref, ya_ref, x_ref, w1_ref, w2_ref, gain_ref, o_ref):
    j = idx[1]
    tn = x_ref.shape[1]
    n_tiles = o_ref.shape[1] // tn
    o_ref[:, pl.ds(pl.multiple_of(j * tn, tn), tn)] = (
        x_ref[...] + _dot(ym_ref[...], w1_ref[...]) + _dot(ya_ref[...], w2_ref[...]))

    @pl.when(j == n_tiles - 1)
    def _():
        ss = None
        for jj in range(n_tiles):
            y = o_ref[:, jj * tn:(jj + 1) * tn]
            part = jnp.sum(y * y, axis=-1, keepdims=True)
            ss = part if ss is None else ss + part
        inv = lax.rsqrt(ss / o_ref.shape[1] + RMS_EPS)
        for jj in range(n_tiles):
            cols = slice(jj * tn, (jj + 1) * tn)
            o_ref[:, cols] = o_ref[:, cols] * inv * gain_ref[:, cols]


def _out_proj(y_m, y_a, x2d, w_out, gain):
    n, d = x2d.shape
    tm, tn = min(OUTPROJ_TM, n), OUTPROJ_TN
    nj = d // tn
    deep = pl.Buffered(OUTPROJ_W_BUFFERS)
    ahead = pl.Buffered(2, use_lookahead=True)
    pipeline = pltpu.emit_pipeline(
        _outproj_step,
        grid=(n // tm, nj),
        in_specs=[
            pl.BlockSpec((tm, MLSTM_WIDTH), lambda i, j: (i, 0), pipeline_mode=ahead),
            pl.BlockSpec((tm, NSA_WIDTH), lambda i, j: (i, 0), pipeline_mode=ahead),
            pl.BlockSpec((tm, tn), lambda i, j: (i, j)),
            pl.BlockSpec((MLSTM_WIDTH, tn), lambda i, j: (0, j), pipeline_mode=deep),
            pl.BlockSpec((NSA_WIDTH, tn), lambda i, j: (MLSTM_WIDTH // NSA_WIDTH, j), pipeline_mode=deep),
            pl.BlockSpec((1, d), lambda i, j: (0, 0)),
        ],
        out_specs=[pl.BlockSpec((tm, d), lambda i, j: (i, 0))],
        _explicit_indices=True,
    )

    def outer(ym_hbm, ya_hbm, x_hbm, w1_hbm, w2_hbm, gain_hbm, o_hbm):
        pipeline(ym_hbm, ya_hbm, x_hbm, w1_hbm, w2_hbm, gain_hbm, o_hbm)

    any_space = pl.BlockSpec(memory_space=pl.ANY)
    return pl.pallas_call(
        outer,
        in_specs=[any_space] * 6,
        out_specs=any_space,
        out_shape=jax.ShapeDtypeStruct((n, d), F32),
        compiler_params=pltpu.CompilerParams(vmem_limit_bytes=VMEM_LIMIT_BYTES),
        name="out_proj",
    )(y_m, y_a, x2d, w_out, w_out, gain)


def _rel_bucket(dist):
    n = jnp.maximum(dist, 0)
    nf = jnp.maximum(n, REL_MAX_EXACT).astype(jnp.float32)
    large = REL_MAX_EXACT + (jnp.log(nf / REL_MAX_EXACT) / math.log(REL_MAX_DISTANCE / REL_MAX_EXACT)
                             * (REL_BUCKETS - REL_MAX_EXACT)).astype(jnp.int32)
    large = jnp.minimum(large, REL_BUCKETS - 1)
    return jnp.where(n < REL_MAX_EXACT, n, large)


def _toeplitz_vec(by_dist, base, n_pos, n_neg, lo, hi, shift=None):
    w = n_pos + n_neg
    c = np.arange(w)
    dist = np.where(c < n_pos, base + c, base - (w - c))
    ok = (dist >= lo) & (dist < hi)
    dmax = by_dist.shape[1]

    def run(start, length):
        left, right = max(0, -start), max(0, start + length - dmax)
        ext = jnp.pad(by_dist, ((0, 0), (left, right)), mode="edge")
        return lax.slice_in_dim(ext, start + left, start + left + length, axis=1)

    vals = jnp.concatenate([run(base, n_pos), run(base - n_neg, n_neg)], axis=1)
    if shift is not None:
        vals = vals - shift
    return jnp.where(ok[None], vals * LOG2E, NEG_LOGIT)


def _toeplitz_t(w_row, n_keys, n_q, key_step):
    x = jnp.broadcast_to(w_row, (n_keys, w_row.shape[1]))
    return pltpu.roll(x, 0, 1, stride=key_step, stride_axis=0)[:, 0:n_q]


def _bias_tables_kernel(wc_ref, ws_ref, ww_ref, bc_ref, bs_ref, bw_ref):
    nb, t = bc_ref.shape[2], bc_ref.shape[3]
    bc_ref[0, 0] = _toeplitz_t(wc_ref[0], nb, t, CMP_STRIDE)
    bs_ref[0, 0] = _toeplitz_t(ws_ref[0], bs_ref.shape[2], bs_ref.shape[3], 1)
    bw_ref[0, 0] = _toeplitz_t(ww_ref[0], bw_ref.shape[2], bw_ref.shape[3], 1)


def _bias_tables(rel_bias, t):
    tq, tk = ATT_TQ, ATT_TK
    g_, hpg = NSA_KV_GROUPS, NSA_HPG
    rb = rel_bias.astype(F32)
    dmax = REL_MAX_DISTANCE + 2 * max(SLC_TQ, SLC_TK, WIN_TQ, tq, tk)
    onehot = (_rel_bucket(jnp.arange(dmax, dtype=jnp.int32))[None, :] == jnp.arange(REL_BUCKETS, dtype=jnp.int32)[:, None])
    by_dist = jnp.dot(rb.T, onehot.astype(F32), precision=lax.Precision.HIGHEST)
    far = rb[REL_BUCKETS - 1][:, None]
    big = 1 << 30
    nb = t // CMP_STRIDE
    wc = _toeplitz_vec(by_dist, -(CMP_BLOCK - 1), t, t, 0, big)[:, None, :]
    sq, sk = min(SLC_TQ, t), SLC_TK
    r = sq // sk
    n_s = min(-(-(REL_MAX_DISTANCE + sk - 1) // sk) + r - 1, t // sk)
    wm = (n_s - 1) * sk + sq
    ws = _toeplitz_vec(by_dist, -(r - 1) * sk, wm, sk, 0, big, far)[:, None, :]
    wq = min(WIN_TQ, t)
    rw = wq // tk
    n_w = rw + (WINDOW - 1 + tk - 1) // tk
    wmw = n_w * tk + wq
    ww = _toeplitz_vec(by_dist, -(rw - 1) * tk, wmw, tk, 0, WINDOW)[:, None, :]
    return pl.pallas_call(
        _bias_tables_kernel,
        grid=(NSA_HEADS,),
        in_specs=[
            pl.BlockSpec((1, 1, 2 * t), lambda h: (h, 0, 0)),
            pl.BlockSpec((1, 1, wm + sk), lambda h: (h, 0, 0)),
            pl.BlockSpec((1, 1, wmw + tk), lambda h: (h, 0, 0)),
        ],
        out_specs=[
            pl.BlockSpec((1, 1, nb, t), lambda h: (h // hpg, h % hpg, 0, 0)),
            pl.BlockSpec((1, 1, sk, wm), lambda h: (h // hpg, h % hpg, 0, 0)),
            pl.BlockSpec((1, 1, tk, wmw), lambda h: (h // hpg, h % hpg, 0, 0)),
        ],
        out_shape=[
            jax.ShapeDtypeStruct((g_, hpg, nb, t), F32),
            jax.ShapeDtypeStruct((g_, hpg, sk, wm), F32),
            jax.ShapeDtypeStruct((g_, hpg, tk, wmw), F32),
        ],
        compiler_params=_cparams(("parallel",)),
        name="bias_tables",
    )(wc, ws, ww)


def _cover_t(t):
    nb = t // CMP_STRIDE
    n_cmp = (t - CMP_BLOCK) // CMP_STRIDE + 1
    n_slc = t // SLC_BLOCK
    cs = np.arange(nb) * CMP_STRIDE
    ss = np.arange(n_slc) * SLC_BLOCK
    cover = np.clip(np.minimum(cs[:, None] + CMP_BLOCK, ss[None, :] + SLC_BLOCK)
                    - np.maximum(cs[:, None], ss[None, :]), 0, None) / CMP_BLOCK
    cover[n_cmp:] = 0.0
    return jnp.asarray(cover.T, dtype=BF16)


def kernel(x, norm_gain, w_in, w_conv, b_igate, b_fgate, mlstm_norm_gain, cmp_k_pos, cmp_k_w1, cmp_k_w2,
           cmp_v_pos, cmp_v_w1, cmp_v_w2, rel_bias, w_out, final_norm_gain):
    bsz, t, d = x.shape
    assert d == D_MODEL and t % MLSTM_L == 0 and t % ATT_TQ == 0 and (t // CMP_STRIDE) % LANES == 0
    n = bsz * t
    x2d = x.reshape(n, d)

    w_main, w_gate = _w_prep(w_in.T)
    proj, gates = _in_proj(x2d, norm_gain.reshape(1, d).astype(F32), w_main, w_gate)

    g_rows = gates[:, :2 * MLSTM_HEADS].reshape(bsz, t, 2, MLSTM_HEADS).transpose(0, 3, 2, 1)
    y_m = _mlstm(proj, g_rows, w_conv.astype(F32), b_igate.astype(F32), b_fgate.astype(F32),
                 mlstm_norm_gain.reshape(1, MLSTM_WIDTH).astype(F32), bsz, t)

    dh = NSA_HEAD_DIM
    half = CMP_BLOCK // 2

    def w1cat(w1):
        return jnp.concatenate([w1[:half].reshape(half * dh, dh), w1[half:].reshape(half * dh, dh)], axis=1).astype(BF16)

    k_cmp, v_cmp_t = _compress(
        proj, w1cat(cmp_k_w1), cmp_k_w2.astype(BF16), cmp_k_pos.reshape(2, half * dh).astype(BF16),
        w1cat(cmp_v_w1), cmp_v_w2.astype(BF16), cmp_v_pos.reshape(2, half * dh).astype(BF16), bsz, t)

    bias_c, bias_s, bias_w = _bias_tables(rel_bias, t)
    gates_t = gates.T
    o_cmp, selbias_t = _cmp_attn(proj, gates_t, k_cmp, v_cmp_t, bias_c, _cover_t(t), bsz, t)
    o_slc = _slc_attn(proj, gates_t, selbias_t, bias_s, bsz, t)
    y_a = _win_attn(proj, gates_t, o_cmp, o_slc, bias_w, bsz, t)

    out = _out_proj(y_m, y_a, x2d, w_out.astype(BF16), final_norm_gain.reshape(1, d).astype(F32))
    return out.reshape(bsz, t, d)
```
